```python
import math
import jax, jax.numpy as jnp
from jax import lax
import numpy as np

D_MODEL = 2048
BATCH = 4
SEQ = 4096
DEPTH = 1

N_Q_HEADS = 16
N_KV_HEADS = 4
HEAD_DIM = 64
WINDOW = 128
ATTN_BLOCK = 128
ATTN_WIDTH = N_Q_HEADS * HEAD_DIM
KV_WIDTH = N_KV_HEADS * HEAD_DIM
REL_BUCKETS = 32
REL_MAX_DIST = 128
HG_HEADS = 8
HG_KEY_DIM = 128
HG_VAL_DIM = 128
HG_KEY_WIDTH = HG_HEADS * HG_KEY_DIM
HG_VAL_WIDTH = HG_HEADS * HG_VAL_DIM
HG_CHUNK = 64
N_GROUPS = 4
EXPERTS_PER_GROUP = 8
N_EXPERTS = N_GROUPS * EXPERTS_PER_GROUP
TOP_K = 2
EXPERT_FF = 512
MOE_BLOCK = 128
RMS_EPS = 1e-6
IN_SECTIONS = (ATTN_WIDTH, KV_WIDTH, KV_WIDTH, HG_KEY_WIDTH, HG_KEY_WIDTH, HG_VAL_WIDTH, HG_VAL_WIDTH, D_MODEL, D_MODEL)
IN_WIDTH = ATTN_WIDTH + 2 * KV_WIDTH + 2 * HG_KEY_WIDTH + 2 * HG_VAL_WIDTH + 2 * D_MODEL

kernel_name = 'hybrid_swa_sink_hgrn2_hiermoe'


def rms_norm(x, g):
    xf = x.astype(jnp.float32)
    y = xf * lax.rsqrt(jnp.mean(xf * xf, axis=-1, keepdims=True) + RMS_EPS)
    return (y * g.astype(jnp.float32)).astype(x.dtype)


def t5_causal_bucket(n):
    max_exact = REL_BUCKETS // 2
    nf = jnp.maximum(n, 1).astype(jnp.float32)
    large = max_exact + (jnp.log(nf / max_exact) / math.log(REL_MAX_DIST / max_exact)
                         * (REL_BUCKETS - max_exact)).astype(jnp.int32)
    large = jnp.minimum(large, REL_BUCKETS - 1)
    return jnp.where(n < max_exact, n, large)


def sliding_window_sink_attention(q, k, v, sinks, rel_table):
    bsz, s_len, _ = q.shape
    nb = s_len // ATTN_BLOCK
    grp = N_Q_HEADS // N_KV_HEADS
    q = q.reshape(bsz, nb, ATTN_BLOCK, N_KV_HEADS, grp, HEAD_DIM)
    k = k.reshape(bsz, nb, ATTN_BLOCK, N_KV_HEADS, HEAD_DIM)
    v = v.reshape(bsz, nb, ATTN_BLOCK, N_KV_HEADS, HEAD_DIM)
    pad_k = jnp.zeros_like(k[:, :1])
    kk = jnp.concatenate([jnp.concatenate([pad_k, k[:, :-1]], axis=1), k], axis=2)
    vv = jnp.concatenate([jnp.concatenate([pad_k, v[:, :-1]], axis=1), v], axis=2)
    s = jnp.einsum('bnqhgd,bnkhd->bnhgqk', q, kk).astype(jnp.float32) * (HEAD_DIM ** -0.5)
    qi = jnp.arange(ATTN_BLOCK)[:, None]
    kj = jnp.arange(2 * ATTN_BLOCK)[None, :]
    dist = qi + ATTN_BLOCK - kj
    bias = rel_table[t5_causal_bucket(jnp.clip(dist, 0))]
    bias = bias.transpose(2, 0, 1).reshape(N_KV_HEADS, grp, ATTN_BLOCK, 2 * ATTN_BLOCK).astype(jnp.float32)
    valid = (dist >= 0) & (dist < WINDOW)
    in_range = (jnp.arange(nb)[:, None, None] > 0) | (kj >= ATTN_BLOCK)[None]
    valid = valid[None] & in_range
    s = jnp.where(valid[None, :, None, None], s + bias, -jnp.inf)
    sink = sinks.astype(jnp.float32).reshape(N_KV_HEADS, grp, 1, 1)
    m = jnp.maximum(jnp.max(s, axis=-1, keepdims=True), sink)
    p = jnp.exp(s - m)
    denom = jnp.sum(p, axis=-1, keepdims=True) + jnp.exp(sink - m)
    o = jnp.einsum('bnhgqk,bnkhd->bnqhgd', (p / denom).astype(vv.dtype), vv)
    return o.reshape(bsz, s_len, ATTN_WIDTH)


def hgrn2(q, f_raw, i_in, g_out, lb, gn):
    bsz, s_len, _ = q.shape
    nc = s_len // HG_CHUNK
    f = lb + (1.0 - lb) * jax.nn.sigmoid(f_raw.astype(jnp.float32))
    logf = jnp.log(f)
    kf = 1.0 - f

    def heads(t, d):
        return t.astype(jnp.float32).reshape(bsz, nc, HG_CHUNK, HG_HEADS, d).transpose(1, 0, 3, 2, 4)

    qc, kc, gc = heads(q, HG_KEY_DIM), heads(kf, HG_KEY_DIM), heads(logf, HG_KEY_DIM)
    vc = heads(i_in, HG_VAL_DIM)
    tri = jnp.tril(jnp.ones((HG_CHUNK, HG_CHUNK), dtype=bool))

    def step(state, inp):
        qt, kt, vt, gt = inp
        b = jnp.cumsum(gt, axis=2)
        o_inter = jnp.einsum('bhtk,bhkv->bhtv', qt * jnp.exp(b), state)
        decay = jnp.exp(jnp.where(tri[None, None, :, :, None], b[:, :, :, None, :] - b[:, :, None, :, :], -jnp.inf))
        att = jnp.einsum('bhtk,bhsk,bhtsk->bhts', qt, kt, decay)
        o = o_inter + jnp.einsum('bhts,bhsv->bhtv', att, vt)
        b_last = b[:, :, -1]
        state = jnp.exp(b_last)[..., None] * state + jnp.einsum('bhsk,bhsv->bhkv', kt * jnp.exp(b_last[:, :, None] - b), vt)
        return state, o

    s0 = jnp.zeros((bsz, HG_HEADS, HG_KEY_DIM, HG_VAL_DIM), jnp.float32)
    _, o = lax.scan(step, s0, (qc, kc, vc, gc))
    o = o.transpose(1, 0, 3, 2, 4).reshape(bsz, s_len, HG_HEADS, HG_VAL_DIM)
    o = o * lax.rsqrt(jnp.mean(o * o, axis=-1, keepdims=True) + RMS_EPS) * gn.astype(jnp.float32)
    o = o * jax.nn.silu(g_out.astype(jnp.float32).reshape(bsz, s_len, HG_HEADS, HG_VAL_DIM))
    return o.reshape(bsz, s_len, HG_VAL_WIDTH).astype(q.dtype)


def hier_moe(h, w_gr, b_gr, w_er, b_er, w_gate, w_up, w_down):
    bsz, s_len, d = h.shape
    n_tok = bsz * s_len
    t = h.reshape(n_tok, d)
    gl = (t @ w_gr).astype(jnp.float32) + b_gr.astype(jnp.float32)
    gp = jax.nn.softmax(gl, axis=-1)
    g_idx = jnp.argmax(gl, axis=-1)
    g_w = jnp.take_along_axis(gp, g_idx[:, None], axis=1)[:, 0]
    el = ((t @ w_er).astype(jnp.float32) + b_er.astype(jnp.float32)).reshape(n_tok, N_GROUPS, EXPERTS_PER_GROUP)
    el_sel = jnp.take_along_axis(el, g_idx[:, None, None], axis=1)[:, 0]
    ep = jax.nn.softmax(el_sel, axis=-1)
    top_p, top_i = lax.top_k(ep, TOP_K)
    wts = g_w[:, None] * top_p / jnp.sum(top_p, axis=-1, keepdims=True)
    e_id = (g_idx[:, None] * EXPERTS_PER_GROUP + top_i).astype(jnp.int32)
    n_asg = n_tok * TOP_K
    e_flat = e_id.reshape(n_asg)
    w_flat = wts.reshape(n_asg)
    tok = jnp.repeat(jnp.arange(n_tok, dtype=jnp.int32), TOP_K)
    order = jnp.argsort(e_flat)
    e_s, tok_s, w_s = e_flat[order], tok[order], w_flat[order]
    counts = jnp.bincount(e_flat, length=N_EXPERTS)
    padded = (counts + MOE_BLOCK - 1) // MOE_BLOCK * MOE_BLOCK
    starts = jnp.cumsum(counts) - counts
    pends = jnp.cumsum(padded)
    pstarts = pends - padded
    dest = pstarts[e_s] + jnp.arange(n_asg, dtype=jnp.int32) - starts[e_s]
    n_blocks = -(-n_asg // MOE_BLOCK) + N_EXPERTS
    n_rows = n_blocks * MOE_BLOCK
    row_tok = jnp.full((n_rows,), n_tok, jnp.int32).at[dest].set(tok_s)
    row_w = jnp.zeros((n_rows,), jnp.float32).at[dest].set(w_s)
    block_e = jnp.minimum(jnp.searchsorted(pends, jnp.arange(n_blocks) * MOE_BLOCK, side='right'), N_EXPERTS - 1)
    t_pad = jnp.concatenate([t, jnp.zeros((1, d), t.dtype)], axis=0)

    def expert_block(args):
        idx, eid = args
        xb = t_pad[idx]
        hb = jax.nn.silu(xb @ w_gate[eid]) * (xb @ w_up[eid])
        return hb @ w_down[eid]

    yb = lax.map(expert_block, (row_tok.reshape(n_blocks, MOE_BLOCK), block_e))
    yb = yb.reshape(n_rows, d) * row_w[:, None].astype(yb.dtype)
    y = jax.ops.segment_sum(yb, row_tok, num_segments=n_tok + 1)[:n_tok]
    return y.reshape(bsz, s_len, d)


def setup_inputs(seed: int = 0) -> dict:
    key = jax.random.key(seed)
    ks = jax.random.split(key, 20)
    f32 = jnp.float32
    nrm = lambda k, shp, sc: jax.random.normal(k, shp, f32) * sc
    return {
        'x': nrm(ks[0], (BATCH, SEQ, D_MODEL), 1.0),
        'norm1_g': 1.0 + nrm(ks[1], (DEPTH, D_MODEL), 0.02),
        'w_in': nrm(ks[2], (DEPTH, D_MODEL, IN_WIDTH), D_MODEL ** -0.5),
        'attn_sinks': nrm(ks[3], (DEPTH, N_Q_HEADS), 0.5),
        'rel_bias': nrm(ks[4], (REL_BUCKETS, N_Q_HEADS), 0.5),
        'hg_lb_logits': nrm(ks[5], (DEPTH + 1, HG_KEY_WIDTH), 0.5),
        'hg_norm_g': 1.0 + nrm(ks[6], (DEPTH, HG_VAL_DIM), 0.02),
        'w_attn_branch': nrm(ks[7], (DEPTH, ATTN_WIDTH, D_MODEL), ATTN_WIDTH ** -0.5),
        'w_hg_branch': nrm(ks[8], (DEPTH, HG_VAL_WIDTH, D_MODEL), HG_VAL_WIDTH ** -0.5),
        'w_out': nrm(ks[9], (DEPTH, D_MODEL, D_MODEL), D_MODEL ** -0.5),
        'norm2_g': 1.0 + nrm(ks[10], (DEPTH, D_MODEL), 0.02),
        'w_group_router': nrm(ks[11], (DEPTH, D_MODEL, N_GROUPS), D_MODEL ** -0.5),
        'b_group_router': nrm(ks[12], (DEPTH, N_GROUPS), 0.01),
        'w_expert_router': nrm(ks[13], (DEPTH, D_MODEL, N_EXPERTS), D_MODEL ** -0.5),
        'b_expert_router': nrm(ks[14], (DEPTH, N_EXPERTS), 0.01),
        'w_gate': nrm(ks[15], (DEPTH, N_EXPERTS, D_MODEL, EXPERT_FF), D_MODEL ** -0.5),
        'w_up': nrm(ks[16], (DEPTH, N_EXPERTS, D_MODEL, EXPERT_FF), D_MODEL ** -0.5),
        'w_down': nrm(ks[17], (DEPTH, N_EXPERTS, EXPERT_FF, D_MODEL), EXPERT_FF ** -0.5),
        'final_g': 1.0 + nrm(ks[18], (D_MODEL,), 0.02),
    }


def reference(x, norm1_g, w_in, attn_sinks, rel_bias, hg_lb_logits, hg_norm_g, w_attn_branch, w_hg_branch,
              w_out, norm2_g, w_group_router, b_group_router, w_expert_router, b_expert_router,
              w_gate, w_up, w_down, final_g):
    lb_all = jnp.cumsum(jax.nn.softmax(hg_lb_logits.astype(jnp.float32), axis=0), axis=0)
    offsets = [int(v) for v in np.cumsum(IN_SECTIONS)[:-1]]
    for l in range(DEPTH):
        h = rms_norm(x, norm1_g[l])
        proj = h @ w_in[l]
        q_a, k_a, v_a, q_h, f_h, i_h, g_h, gate_a, gate_h = jnp.split(proj, offsets, axis=-1)
        a = sliding_window_sink_attention(q_a, k_a, v_a, attn_sinks[l], rel_bias)
        r = hgrn2(q_h, f_h, i_h, g_h, lb_all[l], hg_norm_g[l])
        mixed = jax.nn.sigmoid(gate_a) * (a @ w_attn_branch[l]) + jax.nn.sigmoid(gate_h) * (r @ w_hg_branch[l])
        x = x + mixed @ w_out[l]
        h2 = rms_norm(x, norm2_g[l])
        x = x + hier_moe(h2, w_group_router[l], b_group_router[l], w_expert_router[l], b_expert_router[l],
                         w_gate[l], w_up[l], w_down[l])
    return rms_norm(x, final_g)
```

```python
import functools
import math

import numpy as np
import jax
import jax.numpy as jnp
from jax import lax
from jax.experimental import pallas as pl
from jax.experimental.pallas import tpu as pltpu

D_MODEL = 2048
N_Q_HEADS = 16
N_KV_HEADS = 4
HEAD_DIM = 64
WINDOW = 128
ATTN_BLOCK = 128
ATTN_WIDTH = N_Q_HEADS * HEAD_DIM
KV_WIDTH = N_KV_HEADS * HEAD_DIM
REL_BUCKETS = 32
REL_MAX_DIST = 128
HG_HEADS = 8
HG_DIM = 128
HG_WIDTH = HG_HEADS * HG_DIM
HG_CHUNK = 64
HG_SUB = 16
N_GROUPS = 4
EXPERTS_PER_GROUP = 8
N_EXPERTS = N_GROUPS * EXPERTS_PER_GROUP
TOP_K = 2
EXPERT_FF = 512
RMS_EPS = 1e-6

OFF_GATE_A = 0
OFF_GATE_H = D_MODEL
OFF_QA = 2 * D_MODEL
OFF_KA = OFF_QA + ATTN_WIDTH
OFF_VA = OFF_KA + KV_WIDTH
OFF_QH = OFF_VA + KV_WIDTH
OFF_FH = OFF_QH + HG_WIDTH
OFF_IH = OFF_FH + HG_WIDTH
OFF_GH = OFF_IH + HG_WIDTH
IN_WIDTH = OFF_GH + HG_WIDTH

HG_SAFE_DECAY = 60.0

IN_TM = 1024
IN_TN = 512
HG_TC = 256
MIX_TM = 256
ROUTE_LANES = 128
MOE_TM = 256
FIN_TM = 512

F32 = jnp.float32
BF16 = jnp.bfloat16
NEG_INF = float("-inf")


def _vmem(mib):
    return mib * 1024 * 1024


def _sigmoid(x):
    return 1.0 / (1.0 + jnp.exp(-x))


def _in_proj_kernel(x_ref, g_ref, w_ref, o_ref, h_ref):
    @pl.when(pl.program_id(1) == 0)
    def _():
        x = x_ref[...]
        ms = jnp.mean(x * x, axis=-1, keepdims=True)
        h_ref[...] = (x * lax.rsqrt(ms + RMS_EPS) * g_ref[...]).astype(BF16)

    o_ref[...] = jnp.dot(h_ref[...], w_ref[...], preferred_element_type=F32).astype(o_ref.dtype)


def _in_proj(x2d, g, w):
    n = x2d.shape[0]
    return pl.pallas_call(
        _in_proj_kernel,
        grid=(n // IN_TM, IN_WIDTH // IN_TN),
        in_specs=[
            pl.BlockSpec((IN_TM, D_MODEL), lambda i, j: (i, 0)),
            pl.BlockSpec((1, D_MODEL), lambda i, j: (0, 0)),
            pl.BlockSpec((D_MODEL, IN_TN), lambda i, j: (0, j)),
        ],
        out_specs=pl.BlockSpec((IN_TM, IN_TN), lambda i, j: (i, j)),
        out_shape=jax.ShapeDtypeStruct((n, IN_WIDTH), BF16),
        scratch_shapes=[pltpu.VMEM((IN_TM, D_MODEL), BF16)],
        compiler_params=pltpu.CompilerParams(
            dimension_semantics=("parallel", "arbitrary"), vmem_limit_bytes=_vmem(48)),
        name="in_proj",
    )(x2d, g, w)


def _swa_kernel(sink_ref, q_ref, kp_ref, kc_ref, vp_ref, vc_ref, bias_ref, o_ref):
    has_prev = pl.program_id(1) > 0
    q = q_ref[0]
    kk = jnp.concatenate([kp_ref[0], kc_ref[0]], axis=0)
    vv = jnp.concatenate([vp_ref[0], vc_ref[0]], axis=0)
    col = lax.broadcasted_iota(jnp.int32, (ATTN_BLOCK, 2 * ATTN_BLOCK), 1)
    key_ok = jnp.logical_or(col >= ATTN_BLOCK, has_prev)
    scale = HEAD_DIM ** -0.5
    grp = N_Q_HEADS // N_KV_HEADS
    outs = []
    for hq in range(N_Q_HEADS):
        hk = hq // grp
        qh = q[:, hq * HEAD_DIM:(hq + 1) * HEAD_DIM]
        kh = kk[:, hk * HEAD_DIM:(hk + 1) * HEAD_DIM]
        vh = vv[:, hk * HEAD_DIM:(hk + 1) * HEAD_DIM]
        s = lax.dot_general(qh, kh, (((1,), (1,)), ((), ())), preferred_element_type=F32) * scale
        s = jnp.where(key_ok, s + bias_ref[hq], NEG_INF)
        sink = sink_ref[hq]
        m = jnp.maximum(jnp.max(s, axis=-1, keepdims=True), sink)
        p = jnp.exp(s - m)
        denom = jnp.sum(p, axis=-1, keepdims=True) + jnp.exp(sink - m)
        o = jnp.dot(p.astype(BF16), vh, preferred_element_type=F32)
        outs.append(o / denom)
    o_ref[0] = jnp.concatenate(outs, axis=-1).astype(o_ref.dtype)


def _swa(proj3, sinks, bias):
    bsz, s_len, _ = proj3.shape
    nb = s_len // ATTN_BLOCK
    kblk = OFF_KA // KV_WIDTH
    vblk = OFF_VA // KV_WIDTH
    prev = lambda n: jnp.maximum(n - 1, 0)
    return pl.pallas_call(
        _swa_kernel,
        grid=(bsz, nb),
        in_specs=[
            pl.BlockSpec(memory_space=pltpu.SMEM),
            pl.BlockSpec((1, ATTN_BLOCK, ATTN_WIDTH), lambda b, n: (b, n, OFF_QA // ATTN_WIDTH)),
            pl.BlockSpec((1, ATTN_BLOCK, KV_WIDTH), lambda b, n: (b, prev(n), kblk)),
            pl.BlockSpec((1, ATTN_BLOCK, KV_WIDTH), lambda b, n: (b, n, kblk)),
            pl.BlockSpec((1, ATTN_BLOCK, KV_WIDTH), lambda b, n: (b, prev(n), vblk)),
            pl.BlockSpec((1, ATTN_BLOCK, KV_WIDTH), lambda b, n: (b, n, vblk)),
            pl.BlockSpec((N_Q_HEADS, ATTN_BLOCK, 2 * ATTN_BLOCK), lambda b, n: (0, 0, 0)),
        ],
        out_specs=pl.BlockSpec((1, ATTN_BLOCK, ATTN_WIDTH), lambda b, n: (b, n, 0)),
        out_shape=jax.ShapeDtypeStruct((bsz, s_len, ATTN_WIDTH), BF16),
        compiler_params=pltpu.CompilerParams(
            dimension_semantics=("parallel", "arbitrary"), vmem_limit_bytes=_vmem(32)),
        name="swa",
    )(sinks, proj3, proj3, proj3, proj3, proj3, bias)


def _t5_causal_bucket(n):
    max_exact = REL_BUCKETS // 2
    nf = jnp.maximum(n, 1).astype(F32)
    large = max_exact + (jnp.log(nf / max_exact) / math.log(REL_MAX_DIST / max_exact)
                         * (REL_BUCKETS - max_exact)).astype(jnp.int32)
    large = jnp.minimum(large, REL_BUCKETS - 1)
    return jnp.where(n < max_exact, n, large)


def _attn_bias_table(rel_bias):
    qi = jnp.arange(ATTN_BLOCK)[:, None]
    kj = jnp.arange(2 * ATTN_BLOCK)[None, :]
    dist = qi + ATTN_BLOCK - kj
    bias = rel_bias.astype(F32)[_t5_causal_bucket(jnp.clip(dist, 0))]
    valid = (dist >= 0) & (dist < WINDOW)
    bias = jnp.where(valid[:, :, None], bias, NEG_INF)
    return bias.transpose(2, 0, 1)


def _cumsum_rows(tri_bf16, g):
    g1 = g.astype(BF16)
    r1 = g - g1.astype(F32)
    g2 = r1.astype(BF16)
    g3 = (r1 - g2.astype(F32)).astype(BF16)
    acc = jnp.dot(tri_bf16, g3, preferred_element_type=F32)
    acc = acc + jnp.dot(tri_bf16, g2, preferred_element_type=F32)
    return acc + jnp.dot(tri_bf16, g1, preferred_element_type=F32)


def _hgrn_chunk_fast(q, kf, v, b, bend, st, tri_mask):
    qp = (q * jnp.exp(b)).astype(BF16)
    kp = kf * jnp.exp(-b)
    o = lax.dot_general(qp, st.astype(BF16), (((1,), (1,)), ((), ())), preferred_element_type=F32)
    att = lax.dot_general(qp, kp.astype(BF16), (((1,), (1,)), ((), ())), preferred_element_type=F32)
    att = jnp.where(tri_mask, att, 0.0)
    o = o + jnp.dot(att.astype(BF16), v, preferred_element_type=F32)
    eb = jnp.exp(bend)
    kd = (kp * eb).astype(BF16)
    st_new = st * eb + lax.dot_general(v, kd, (((0,), (0,)), ((), ())), preferred_element_type=F32)
    return o, st_new


def _hgrn_chunk_safe(q, kf, v, b, st):
    rows = lax.broadcasted_iota(jnp.int32, (HG_SUB, 1), 0)
    outs = []
    prev_end = jnp.zeros((1, HG_DIM), F32)
    for i in range(HG_CHUNK // HG_SUB):
        sl = slice(i * HG_SUB, (i + 1) * HG_SUB)
        bl = b[sl] - prev_end
        qi, ki, vi = q[sl], kf[sl], v[sl]
        vif = vi.astype(F32)
        blast = bl[HG_SUB - 1:HG_SUB]
        qs = (qi * jnp.exp(bl)).astype(BF16)
        o = lax.dot_general(qs, st.astype(BF16), (((1,), (1,)), ((), ())), preferred_element_type=F32)
        for s in range(HG_SUB):
            d = jnp.exp(jnp.minimum(bl - bl[s:s + 1], 0.0))
            a = jnp.sum(qi * ki[s:s + 1] * d, axis=-1, keepdims=True)
            o = o + jnp.where(rows >= s, a, 0.0) * vif[s:s + 1]
        kd = (ki * jnp.exp(blast - bl)).astype(BF16)
        st = st * jnp.exp(blast) + lax.dot_general(vi, kd, (((0,), (0,)), ((), ())), preferred_element_type=F32)
        prev_end = b[(i + 1) * HG_SUB - 1:(i + 1) * HG_SUB]
        outs.append(o)
    return jnp.concatenate(outs, axis=0), st


def _hgrn_kernel(q_ref, f_ref, i_ref, g_ref, lbl_ref, gn_ref, o_ref, st_ref):
    @pl.when(pl.program_id(2) == 0)
    def _():
        st_ref[...] = jnp.zeros_like(st_ref)

    lg2 = lbl_ref[...].astype(F32)
    e = jnp.exp(lg2 - jnp.max(lg2, axis=0, keepdims=True))
    lb = e[0:1] / jnp.sum(e, axis=0, keepdims=True)
    gn = gn_ref[...].astype(F32)
    r = lax.broadcasted_iota(jnp.int32, (HG_CHUNK, HG_CHUNK), 0)
    c = lax.broadcasted_iota(jnp.int32, (HG_CHUNK, HG_CHUNK), 1)
    tri_mask = r >= c
    tri = jnp.where(tri_mask, 1.0, 0.0).astype(BF16)

    def chunk(ci, carry):
        start = pl.multiple_of(ci * HG_CHUNK, HG_CHUNK)
        sl = pl.ds(start, HG_CHUNK)
        q = q_ref[0, sl, :].astype(F32)
        fr = f_ref[0, sl, :].astype(F32)
        v = i_ref[0, sl, :]
        f = lb + (1.0 - lb) * _sigmoid(fr)
        kf = 1.0 - f
        b = _cumsum_rows(tri, jnp.log(f))
        bend = b[HG_CHUNK - 1:HG_CHUNK]
        st = st_ref[...]
        o, st_new = lax.cond(
            jnp.min(bend) >= -HG_SAFE_DECAY,
            lambda: _hgrn_chunk_fast(q, kf, v, b, bend, st, tri_mask),
            lambda: _hgrn_chunk_safe(q, kf, v, b, st),
        )
        st_ref[...] = st_new
        o = o * lax.rsqrt(jnp.mean(o * o, axis=-1, keepdims=True) + RMS_EPS) * gn
        go = g_ref[0, sl, :].astype(F32)
        o_ref[0, sl, :] = (o * (go * _sigmoid(go))).astype(o_ref.dtype)
        return carry

    lax.fori_loop(0, HG_TC // HG_CHUNK, chunk, 0)


def _hgrn2(proj3, lb_logits, gn):
    bsz, s_len, _ = proj3.shape
    blk = lambda off: (lambda b, h, t: (b, t, off // HG_DIM + h))
    return pl.pallas_call(
        _hgrn_kernel,
        grid=(bsz, HG_HEADS, s_len // HG_TC),
        in_specs=[
            pl.BlockSpec((1, HG_TC, HG_DIM), blk(OFF_QH)),
            pl.BlockSpec((1, HG_TC, HG_DIM), blk(OFF_FH)),
            pl.BlockSpec((1, HG_TC, HG_DIM), blk(OFF_IH)),
            pl.BlockSpec((1, HG_TC, HG_DIM), blk(OFF_GH)),
            pl.BlockSpec((lb_logits.shape[0], HG_DIM), lambda b, h, t: (0, h)),
            pl.BlockSpec((1, HG_DIM), lambda b, h, t: (0, 0)),
        ],
        out_specs=pl.BlockSpec((1, HG_TC, HG_DIM), lambda b, h, t: (b, t, h)),
        out_shape=jax.ShapeDtypeStruct((bsz, s_len, HG_WIDTH), BF16),
        scratch_shapes=[pltpu.VMEM((HG_DIM, HG_DIM), F32)],
        compiler_params=pltpu.CompilerParams(
            dimension_semantics=("parallel", "parallel", "arbitrary"), vmem_limit_bytes=_vmem(32)),
        name="hgrn2",
    )(proj3, proj3, proj3, proj3, lb_logits, gn)


def _mix_kernel(a_ref, r_ref, ga_ref, gh_ref, x_ref, wa_ref, wr_ref, wo_ref, g2_ref, whi_ref, wlo_ref, br_ref,
                x1_ref, h2_ref, route_ref):
    am = jnp.dot(a_ref[...], wa_ref[...], preferred_element_type=F32)
    rm = jnp.dot(r_ref[...], wr_ref[...], preferred_element_type=F32)
    mixed = _sigmoid(ga_ref[...].astype(F32)) * am + _sigmoid(gh_ref[...].astype(F32)) * rm
    x1 = x_ref[...] + jnp.dot(mixed.astype(BF16), wo_ref[...], preferred_element_type=F32)
    x1_ref[...] = x1
    h2 = x1 * lax.rsqrt(jnp.mean(x1 * x1, axis=-1, keepdims=True) + RMS_EPS) * g2_ref[...]
    h2_ref[...] = h2

    hi = h2.astype(BF16)
    lo = (h2 - hi.astype(F32)).astype(BF16)
    logits = jnp.dot(lo, whi_ref[...], preferred_element_type=F32)
    logits = logits + jnp.dot(hi, wlo_ref[...], preferred_element_type=F32)
    logits = logits + jnp.dot(hi, whi_ref[...], preferred_element_type=F32) + br_ref[...]

    lane = lax.broadcasted_iota(jnp.int32, logits.shape, 1)
    lane_f = lane.astype(F32)
    big = float(ROUTE_LANES)
    gl = jnp.where(lane < N_GROUPS, logits, NEG_INF)
    gmax = jnp.max(gl, axis=-1, keepdims=True)
    g_idx = jnp.min(jnp.where(gl == gmax, lane_f, big), axis=-1, keepdims=True)
    g_w = 1.0 / jnp.sum(jnp.exp(gl - gmax), axis=-1, keepdims=True)
    e_lane = lane - N_GROUPS
    in_group = jnp.logical_and(e_lane >= 0, e_lane < N_EXPERTS)
    lane_group = jnp.right_shift(e_lane, 3).astype(F32)
    in_group = jnp.logical_and(in_group, lane_group == g_idx)
    el = jnp.where(in_group, logits, NEG_INF)
    m1 = jnp.max(el, axis=-1, keepdims=True)
    i1 = jnp.min(jnp.where(el == m1, lane_f, big), axis=-1, keepdims=True)
    el2 = jnp.where(lane_f == i1, NEG_INF, el)
    m2 = jnp.max(el2, axis=-1, keepdims=True)
    i2 = jnp.min(jnp.where(el2 == m2, lane_f, big), axis=-1, keepdims=True)
    p2 = jnp.exp(m2 - m1)
    w1 = g_w / (1.0 + p2)
    w2 = g_w * p2 / (1.0 + p2)
    e1 = i1 - N_GROUPS
    e2 = i2 - N_GROUPS
    route = jnp.where(lane == 0, w1, jnp.where(lane == 1, w2, jnp.where(lane == 2, e1, jnp.where(lane == 3, e2, 0.0))))
    route_ref[...] = route


def _mix(a2d, r2d, proj, x2d, wa, wr, wo, g2, whi, wlo, br):
    n = x2d.shape[0]
    row = lambda i: (i, 0)
    const = lambda i: (0, 0)
    return pl.pallas_call(
        _mix_kernel,
        grid=(n // MIX_TM,),
        in_specs=[
            pl.BlockSpec((MIX_TM, ATTN_WIDTH), row),
            pl.BlockSpec((MIX_TM, HG_WIDTH), row),
            pl.BlockSpec((MIX_TM, D_MODEL), lambda i: (i, OFF_GATE_A // D_MODEL)),
            pl.BlockSpec((MIX_TM, D_MODEL), lambda i: (i, OFF_GATE_H // D_MODEL)),
            pl.BlockSpec((MIX_TM, D_MODEL), row),
            pl.BlockSpec((ATTN_WIDTH, D_MODEL), const),
            pl.BlockSpec((HG_WIDTH, D_MODEL), const),
            pl.BlockSpec((D_MODEL, D_MODEL), const),
            pl.BlockSpec((1, D_MODEL), const),
            pl.BlockSpec((D_MODEL, ROUTE_LANES), const),
            pl.BlockSpec((D_MODEL, ROUTE_LANES), const),
            pl.BlockSpec((1, ROUTE_LANES), const),
        ],
        out_specs=[
            pl.BlockSpec((MIX_TM, D_MODEL), row),
            pl.BlockSpec((MIX_TM, D_MODEL), row),
            pl.BlockSpec((MIX_TM, ROUTE_LANES), row),
        ],
        out_shape=[
            jax.ShapeDtypeStruct((n, D_MODEL), F32),
            jax.ShapeDtypeStruct((n, D_MODEL), F32),
            jax.ShapeDtypeStruct((n, ROUTE_LANES), F32),
        ],
        compiler_params=pltpu.CompilerParams(dimension_semantics=("parallel",), vmem_limit_bytes=_vmem(56)),
        name="mix",
    )(a2d, r2d, proj, proj, x2d, wa, wr, wo, g2, whi, wlo, br)


def _moe_kernel(be_ref, nused_ref, src_ref, srcn_ref, roww_ref, h2_hbm, wg_ref, wu_ref, wd_ref, y_hbm,
                xbuf, ybuf, gsem, ssem):
    b = pl.program_id(0)
    n_used = nused_ref[0]
    slot = lax.rem(b, 2)

    def gather_copy(src, s, r):
        tok = jnp.maximum(src, 0) >> 1
        return pltpu.make_async_copy(h2_hbm.at[pl.ds(tok, 1)], xbuf.at[s, pl.ds(r, 1)], gsem.at[s])

    def start_gather(idx_ref, s):
        def body(r, c):
            gather_copy(idx_ref[0, 0, r], s, r).start()
            return c
        lax.fori_loop(0, MOE_TM, body, 0)

    def wait_gather(s):
        def body(r, c):
            gather_copy(0, s, r).wait()
            return c
        lax.fori_loop(0, MOE_TM, body, 0)

    @pl.when(jnp.logical_and(b == 0, n_used > 0))
    def _():
        start_gather(src_ref, 0)

    @pl.when(b + 1 < n_used)
    def _():
        start_gather(srcn_ref, 1 - slot)

    @pl.when(b < n_used)
    def _():
        wait_gather(slot)
        xb = xbuf[slot].astype(BF16)
        hg = jnp.dot(xb, wg_ref[0], preferred_element_type=F32)
        hu = jnp.dot(xb, wu_ref[0], preferred_element_type=F32)
        hb = (hg * _sigmoid(hg) * hu).astype(BF16)
        y = jnp.dot(hb, wd_ref[0], preferred_element_type=F32)
        ybuf[...] = y * roww_ref[0]

        def scatter_copy(src, r):
            return pltpu.make_async_copy(ybuf.at[pl.ds(r, 1)], y_hbm.at[pl.ds(src, 1)], ssem.at[0])

        def s_start(r, c):
            src = src_ref[0, 0, r]

            @pl.when(src >= 0)
            def _():
                scatter_copy(src, r).start()
            return c

        def s_wait(r, c):
            src = src_ref[0, 0, r]

            @pl.when(src >= 0)
            def _():
                scatter_copy(src, r).wait()
            return c

        lax.fori_loop(0, MOE_TM, s_start, 0)
        lax.fori_loop(0, MOE_TM, s_wait, 0)


def _moe(block_e, n_used, row_src3, row_w3, h2, wg, wu, wd, n_asg):
    n_blocks = row_src3.shape[0]
    wspec = lambda shp: pl.BlockSpec((1,) + shp, lambda b, be, nu: (be[b], 0, 0))
    grid_spec = pltpu.PrefetchScalarGridSpec(
        num_scalar_prefetch=2,
        grid=(n_blocks,),
        in_specs=[
            pl.BlockSpec((1, 1, MOE_TM), lambda b, be, nu: (b, 0, 0), memory_space=pltpu.SMEM),
            pl.BlockSpec((1, 1, MOE_TM), lambda b, be, nu: (jnp.minimum(b + 1, n_blocks - 1), 0, 0),
                         memory_space=pltpu.SMEM),
            pl.BlockSpec((1, MOE_TM, 1), lambda b, be, nu: (b, 0, 0)),
            pl.BlockSpec(memory_space=pl.ANY),
            wspec((D_MODEL, EXPERT_FF)),
            wspec((D_MODEL, EXPERT_FF)),
            wspec((EXPERT_FF, D_MODEL)),
        ],
        out_specs=pl.BlockSpec(memory_space=pl.ANY),
        scratch_shapes=[
            pltpu.VMEM((2, MOE_TM, D_MODEL), F32),
            pltpu.VMEM((MOE_TM, D_MODEL), F32),
            pltpu.SemaphoreType.DMA((2,)),
            pltpu.SemaphoreType.DMA((1,)),
        ],
    )
    return pl.pallas_call(
        _moe_kernel,
        grid_spec=grid_spec,
        out_shape=jax.ShapeDtypeStruct((n_asg, D_MODEL), F32),
        compiler_params=pltpu.CompilerParams(dimension_semantics=("arbitrary",), vmem_limit_bytes=_vmem(48)),
        name="moe",
    )(block_e, n_used, row_src3, row_src3, row_w3, h2, wg, wu, wd)


def _dispatch(route, n_tok):
    n_asg = n_tok * TOP_K
    e_flat = route[:, 2:2 + TOP_K].astype(jnp.int32).reshape(n_asg)
    w_flat = route[:, 0:TOP_K].reshape(n_asg)
    onehot = (e_flat[:, None] == jnp.arange(N_EXPERTS, dtype=jnp.int32)[None, :]).astype(jnp.int32)
    csum = jnp.cumsum(onehot, axis=0)
    rank = jnp.sum(csum * onehot, axis=1) - 1
    counts = csum[-1]
    padded = (counts + MOE_TM - 1) // MOE_TM * MOE_TM
    pends = jnp.cumsum(padded)
    pstarts = pends - padded
    dest = pstarts[e_flat] + rank
    n_blocks = n_asg // MOE_TM + N_EXPERTS
    n_rows = n_blocks * MOE_TM
    row_src = jnp.full((n_rows,), -1, jnp.int32).at[dest].set(jnp.arange(n_asg, dtype=jnp.int32))
    row_w = jnp.zeros((n_rows,), F32).at[dest].set(w_flat)
    block_e = jnp.minimum(
        jnp.searchsorted(pends, jnp.arange(n_blocks, dtype=jnp.int32) * MOE_TM, side="right"), N_EXPERTS - 1
    ).astype(jnp.int32)
    n_used = (pends[-1] // MOE_TM).astype(jnp.int32).reshape(1)
    return block_e, n_used, row_src.reshape(n_blocks, 1, MOE_TM), row_w.reshape(n_blocks, MOE_TM, 1)


def _final_kernel(x1_ref, y0_ref, y1_ref, g_ref, o_ref):
    x = x1_ref[...] + (y0_ref[...] + y1_ref[...])
    o_ref[...] = x * lax.rsqrt(jnp.mean(x * x, axis=-1, keepdims=True) + RMS_EPS) * g_ref[...]


def _final(x1, y2, g):
    n = x1.shape[0]
    return pl.pallas_call(
        _final_kernel,
        grid=(n // FIN_TM,),
        in_specs=[
            pl.BlockSpec((FIN_TM, D_MODEL), lambda i: (i, 0)),
            pl.BlockSpec((FIN_TM, D_MODEL), lambda i: (i, 0)),
            pl.BlockSpec((FIN_TM, D_MODEL), lambda i: (i, 1)),
            pl.BlockSpec((1, D_MODEL), lambda i: (0, 0)),
        ],
        out_specs=pl.BlockSpec((FIN_TM, D_MODEL), lambda i: (i, 0)),
        out_shape=jax.ShapeDtypeStruct((n, D_MODEL), F32),
        compiler_params=pltpu.CompilerParams(dimension_semantics=("parallel",), vmem_limit_bytes=_vmem(48)),
        name="final",
    )(x1, y2, y2, g)


def kernel(x, norm1_g, w_in, attn_sinks, rel_bias, hg_lb_logits, hg_norm_g, w_attn_branch, w_hg_branch, w_out,
           norm2_g, w_group_router, b_group_router, w_expert_router, b_expert_router, w_gate, w_up, w_down, final_g):
    bsz, s_len, d = x.shape
    n_tok = bsz * s_len
    x2d = x.reshape(n_tok, d)

    w = w_in[0]
    ref_off = np.cumsum([0, ATTN_WIDTH, KV_WIDTH, KV_WIDTH, HG_WIDTH, HG_WIDTH, HG_WIDTH, HG_WIDTH, D_MODEL, D_MODEL])
    sec = [w[:, int(ref_off[k]):int(ref_off[k + 1])] for k in range(9)]
    w_perm = jnp.concatenate([sec[7], sec[8]] + sec[:7], axis=1).astype(BF16)
    w_route = jnp.concatenate([w_group_router[0], w_expert_router[0]], axis=1).astype(F32)
    w_route = jnp.pad(w_route, ((0, 0), (0, ROUTE_LANES - w_route.shape[1])))
    whi = w_route.astype(BF16)
    wlo = (w_route - whi.astype(F32)).astype(BF16)
    b_route = jnp.concatenate([b_group_router[0], b_expert_router[0]]).astype(F32)
    b_route = jnp.pad(b_route, (0, ROUTE_LANES - b_route.shape[0])).reshape(1, ROUTE_LANES)
    bias = _attn_bias_table(rel_bias)

    proj = _in_proj(x2d, norm1_g[0].reshape(1, d).astype(F32), w_perm)
    proj3 = proj.reshape(bsz, s_len, IN_WIDTH)
    a = _swa(proj3, attn_sinks[0].astype(F32), bias)
    r = _hgrn2(proj3, hg_lb_logits.astype(F32), hg_norm_g[0].reshape(1, HG_DIM).astype(F32))
    x1, h2, route = _mix(
        a.reshape(n_tok, ATTN_WIDTH), r.reshape(n_tok, HG_WIDTH), proj, x2d,
        w_attn_branch[0].astype(BF16), w_hg_branch[0].astype(BF16), w_out[0].astype(BF16),
        norm2_g[0].reshape(1, d).astype(F32), whi, wlo, b_route)
    block_e, n_used, row_src3, row_w3 = _dispatch(route, n_tok)
    y = _moe(block_e, n_used, row_src3, row_w3, h2,
             w_gate[0].astype(BF16), w_up[0].astype(BF16), w_down[0].astype(BF16), n_tok * TOP_K)
    out = _final(x1, y.reshape(n_tok, TOP_K * d), final_g.reshape(1, d).astype(F32))
    return out.reshape(bsz, s_len, d)
```

```python
import math

import numpy as np
import jax
import jax.numpy as jnp
from jax import lax
from jax.experimental import pallas as pl
from jax.experimental.pallas import tpu as pltpu

D_MODEL = 2048
N_Q_HEADS = 16
N_KV_HEADS = 4
HEAD_DIM = 64
WINDOW = 128
ATTN_BLOCK = 128
ATTN_WIDTH = N_Q_HEADS * HEAD_DIM
KV_WIDTH = N_KV_HEADS * HEAD_DIM
REL_BUCKETS = 32
REL_MAX_DIST = 128
HG_HEADS = 8
HG_DIM = 128
HG_WIDTH = HG_HEADS * HG_DIM
HG_CHUNK = 64
HG_SUB = 16
N_GROUPS = 4
EXPERTS_PER_GROUP = 8
N_EXPERTS = N_GROUPS * EXPERTS_PER_GROUP
TOP_K = 2
EXPERT_FF = 512
RMS_EPS = 1e-6

OFF_GATE_A = 0
OFF_GATE_H = D_MODEL
OFF_QA = 2 * D_MODEL
OFF_QH = OFF_QA + ATTN_WIDTH
OFF_FH = OFF_QH + HG_WIDTH
OFF_IH = OFF_FH + HG_WIDTH
OFF_GH = OFF_IH + HG_WIDTH
OFF_KA = OFF_GH + HG_WIDTH
OFF_VA = OFF_KA + KV_WIDTH
IN_WIDTH = OFF_VA + KV_WIDTH
REF_OFF_QA = 0
REF_OFF_KA = ATTN_WIDTH
REF_OFF_QH = ATTN_WIDTH + 2 * KV_WIDTH
REF_OFF_GATE_A = REF_OFF_QH + 4 * HG_WIDTH

HG_SAFE_DECAY = 60.0

IN_TM = 1024
IN_TN = 512
HG_TC = 256
MIX_TM = 256
ROUTE_LANES = 128
DISPATCH_ROWS = 1024
DMA_UNROLL = 8
MOE_TM = 256
FIN_TM = 256

F32 = jnp.float32
BF16 = jnp.bfloat16
U32 = jnp.uint32
NEG_INF = float("-inf")


def _vmem(mib):
    return mib * 1024 * 1024


def _sigmoid(x):
    return 1.0 / (1.0 + jnp.exp(-x))


def _dot_nt(a, b):
    return lax.dot_general(a, b, (((1,), (1,)), ((), ())), preferred_element_type=F32)


def _dot_tn(a, b):
    return lax.dot_general(a, b, (((0,), (0,)), ((), ())), preferred_element_type=F32)


def _in_proj_kernel(x_ref, g_ref, w_ref, o_ref, h_ref):
    @pl.when(pl.program_id(1) == 0)
    def _():
        x = x_ref[...]
        ms = jnp.mean(x * x, axis=-1, keepdims=True)
        h_ref[...] = (x * lax.rsqrt(ms + RMS_EPS) * g_ref[...]).astype(BF16)

    o_ref[...] = jnp.dot(h_ref[...], w_ref[0].astype(BF16), preferred_element_type=F32).astype(o_ref.dtype)


def _in_proj(x2d, g, w_in):
    n = x2d.shape[0]
    n_col = IN_WIDTH // IN_TN

    def out_block(j):
        gates = j - REF_OFF_GATE_A // IN_TN + OFF_GATE_A // IN_TN
        q_a = j - REF_OFF_QA // IN_TN + OFF_QA // IN_TN
        kv = j - REF_OFF_KA // IN_TN + OFF_KA // IN_TN
        hg = j - REF_OFF_QH // IN_TN + OFF_QH // IN_TN
        return jnp.where(j >= REF_OFF_GATE_A // IN_TN, gates,
                         jnp.where(j >= REF_OFF_QH // IN_TN, hg, jnp.where(j >= REF_OFF_KA // IN_TN, kv, q_a)))

    return pl.pallas_call(
        _in_proj_kernel,
        grid=(n // IN_TM, n_col),
        in_specs=[
            pl.BlockSpec((IN_TM, D_MODEL), lambda i, j: (i, 0)),
            pl.BlockSpec((1, D_MODEL), lambda i, j: (0, 0)),
            pl.BlockSpec((1, D_MODEL, IN_TN), lambda i, j: (0, 0, j)),
        ],
        out_specs=pl.BlockSpec((IN_TM, IN_TN), lambda i, j: (i, out_block(j))),
        out_shape=jax.ShapeDtypeStruct((n, IN_WIDTH), BF16),
        scratch_shapes=[pltpu.VMEM((IN_TM, D_MODEL), BF16)],
        compiler_params=pltpu.CompilerParams(
            dimension_semantics=("parallel", "arbitrary"), vmem_limit_bytes=_vmem(48)),
        name="in_proj",
    )(x2d, g, w_in)


def _swa_kernel(sink_ref, q_ref, kp_ref, kc_ref, vp_ref, vc_ref, bias_ref, o_ref):
    has_prev = pl.program_id(1) > 0
    q = q_ref[0]
    kk = jnp.concatenate([kp_ref[0], kc_ref[0]], axis=0)
    vv = jnp.concatenate([vp_ref[0], vc_ref[0]], axis=0)
    col = lax.broadcasted_iota(jnp.int32, (ATTN_BLOCK, 2 * ATTN_BLOCK), 1)
    key_ok = jnp.logical_or(col >= ATTN_BLOCK, has_prev)
    scale = HEAD_DIM ** -0.5
    grp = N_Q_HEADS // N_KV_HEADS
    outs = []
    for hq in range(N_Q_HEADS):
        hk = hq // grp
        qh = q[:, hq * HEAD_DIM:(hq + 1) * HEAD_DIM]
        kh = kk[:, hk * HEAD_DIM:(hk + 1) * HEAD_DIM]
        vh = vv[:, hk * HEAD_DIM:(hk + 1) * HEAD_DIM]
        s = _dot_nt(qh, kh) * scale
        s = jnp.where(key_ok, s + bias_ref[hq], NEG_INF)
        sink = sink_ref[hq]
        m = jnp.maximum(jnp.max(s, axis=-1, keepdims=True), sink)
        p = jnp.exp(s - m)
        denom = jnp.sum(p, axis=-1, keepdims=True) + jnp.exp(sink - m)
        o = jnp.dot(p.astype(BF16), vh, preferred_element_type=F32)
        outs.append(o / denom)
    o_ref[0] = jnp.concatenate(outs, axis=-1).astype(o_ref.dtype)


def _swa(proj3, sinks, bias):
    bsz, s_len, _ = proj3.shape
    nb = s_len // ATTN_BLOCK
    kblk = OFF_KA // KV_WIDTH
    vblk = OFF_VA // KV_WIDTH
    prev = lambda n: jnp.maximum(n - 1, 0)
    return pl.pallas_call(
        _swa_kernel,
        grid=(bsz, nb),
        in_specs=[
            pl.BlockSpec(memory_space=pltpu.SMEM),
            pl.BlockSpec((1, ATTN_BLOCK, ATTN_WIDTH), lambda b, n: (b, n, OFF_QA // ATTN_WIDTH)),
            pl.BlockSpec((1, ATTN_BLOCK, KV_WIDTH), lambda b, n: (b, prev(n), kblk)),
            pl.BlockSpec((1, ATTN_BLOCK, KV_WIDTH), lambda b, n: (b, n, kblk)),
            pl.BlockSpec((1, ATTN_BLOCK, KV_WIDTH), lambda b, n: (b, prev(n), vblk)),
            pl.BlockSpec((1, ATTN_BLOCK, KV_WIDTH), lambda b, n: (b, n, vblk)),
            pl.BlockSpec((N_Q_HEADS, ATTN_BLOCK, 2 * ATTN_BLOCK), lambda b, n: (0, 0, 0)),
        ],
        out_specs=pl.BlockSpec((1, ATTN_BLOCK, ATTN_WIDTH), lambda b, n: (b, n, 0)),
        out_shape=jax.ShapeDtypeStruct((bsz, s_len, ATTN_WIDTH), BF16),
        compiler_params=pltpu.CompilerParams(
            dimension_semantics=("parallel", "arbitrary"), vmem_limit_bytes=_vmem(32)),
        name="swa",
    )(sinks, proj3, proj3, proj3, proj3, proj3, bias)


def _t5_causal_bucket(n):
    max_exact = REL_BUCKETS // 2
    nf = jnp.maximum(n, 1).astype(F32)
    large = max_exact + (jnp.log(nf / max_exact) / math.log(REL_MAX_DIST / max_exact)
                         * (REL_BUCKETS - max_exact)).astype(jnp.int32)
    large = jnp.minimum(large, REL_BUCKETS - 1)
    return jnp.where(n < max_exact, n, large)


def _attn_bias_table(rel_bias):
    qi = jnp.arange(ATTN_BLOCK)[:, None]
    kj = jnp.arange(2 * ATTN_BLOCK)[None, :]
    dist = qi + ATTN_BLOCK - kj
    bias = rel_bias.astype(F32)[_t5_causal_bucket(jnp.clip(dist, 0))]
    valid = (dist >= 0) & (dist < WINDOW)
    bias = jnp.where(valid[:, :, None], bias, NEG_INF)
    return bias.transpose(2, 0, 1)


def _cumsum_rows(tri_bf16, g):
    width = g.shape[1]
    g1 = g.astype(BF16)
    r1 = g - g1.astype(F32)
    g2 = r1.astype(BF16)
    g3 = (r1 - g2.astype(F32)).astype(BF16)
    acc = jnp.dot(tri_bf16, jnp.concatenate([g3, g2, g1], axis=1), preferred_element_type=F32)
    return (acc[:, :width] + acc[:, width:2 * width]) + acc[:, 2 * width:]


def _hgrn_head_safe(q, kf, v, b, st):
    rows = lax.broadcasted_iota(jnp.int32, (HG_SUB, 1), 0)
    outs = []
    prev_end = jnp.zeros((1, HG_DIM), F32)
    for i in range(HG_CHUNK // HG_SUB):
        sl = slice(i * HG_SUB, (i + 1) * HG_SUB)
        bl = b[sl] - prev_end
        qi, ki, vi = q[sl], kf[sl], v[sl]
        vif = vi.astype(F32)
        blast = bl[HG_SUB - 1:HG_SUB]
        o = _dot_nt((qi * jnp.exp(bl)).astype(BF16), st.astype(BF16))
        for s in range(HG_SUB):
            d = jnp.exp(jnp.minimum(bl - bl[s:s + 1], 0.0))
            a = jnp.sum(qi * ki[s:s + 1] * d, axis=-1, keepdims=True)
            o = o + jnp.where(rows >= s, a, 0.0) * vif[s:s + 1]
        kd = (ki * jnp.exp(blast - bl)).astype(BF16)
        st = st * jnp.exp(blast) + _dot_tn(vi, kd)
        prev_end = b[(i + 1) * HG_SUB - 1:(i + 1) * HG_SUB]
        outs.append(o)
    return jnp.concatenate(outs, axis=0), st


def _hgrn_kernel(q_ref, f_ref, i_ref, g_ref, lbl_ref, gn_ref, o_ref, st_ref):
    @pl.when(pl.program_id(1) == 0)
    def _():
        st_ref[...] = jnp.zeros_like(st_ref)

    lg2 = lbl_ref[...].astype(F32)
    e = jnp.exp(lg2 - jnp.max(lg2, axis=0, keepdims=True))
    lb = e[0:1] / jnp.sum(e, axis=0, keepdims=True)
    gn = gn_ref[...].astype(F32)
    r = lax.broadcasted_iota(jnp.int32, (HG_CHUNK, HG_CHUNK), 0)
    c = lax.broadcasted_iota(jnp.int32, (HG_CHUNK, HG_CHUNK), 1)
    tri_mask = r >= c
    tri = jnp.where(tri_mask, 1.0, 0.0).astype(BF16)

    def chunk(ci, carry):
        start = pl.multiple_of(ci * HG_CHUNK, HG_CHUNK)
        sl = pl.ds(start, HG_CHUNK)
        q = q_ref[0, sl, :].astype(F32)
        fr = f_ref[0, sl, :].astype(F32)
        f = lb + (1.0 - lb) * _sigmoid(fr)
        kf = 1.0 - f
        b = _cumsum_rows(tri, jnp.log(f))
        bend = b[HG_CHUNK - 1:HG_CHUNK]

        def finish(h, o):
            hs = slice(h * HG_DIM, (h + 1) * HG_DIM)
            o = o * lax.rsqrt(jnp.mean(o * o, axis=-1, keepdims=True) + RMS_EPS) * gn
            go = g_ref[0, sl, hs].astype(F32)
            o_ref[0, sl, hs] = (o * (go * _sigmoid(go))).astype(o_ref.dtype)

        def fast():
            qp = (q * jnp.exp(b)).astype(BF16)
            kp = kf * jnp.exp(-b)
            kpb = kp.astype(BF16)
            eb = jnp.exp(bend)
            kd = (kp * eb).astype(BF16)
            for h in range(HG_HEADS):
                hs = slice(h * HG_DIM, (h + 1) * HG_DIM)
                st = st_ref[h]
                v = i_ref[0, sl, hs]
                att = jnp.where(tri_mask, _dot_nt(qp[:, hs], kpb[:, hs]), 0.0)
                o = _dot_nt(qp[:, hs], st.astype(BF16)) + jnp.dot(att.astype(BF16), v, preferred_element_type=F32)
                st_ref[h] = st * eb[:, hs] + _dot_tn(v, kd[:, hs])
                finish(h, o)

        def safe():
            for h in range(HG_HEADS):
                hs = slice(h * HG_DIM, (h + 1) * HG_DIM)
                o, st_new = _hgrn_head_safe(q[:, hs], kf[:, hs], i_ref[0, sl, hs], b[:, hs], st_ref[h])
                st_ref[h] = st_new
                finish(h, o)

        lax.cond(jnp.min(bend) >= -HG_SAFE_DECAY, fast, safe)
        return carry

    lax.fori_loop(0, HG_TC // HG_CHUNK, chunk, 0)


def _hgrn2(proj3, lb_logits, gn):
    bsz, s_len, _ = proj3.shape
    blk = lambda off: (lambda b, t: (b, t, off // HG_WIDTH))
    return pl.pallas_call(
        _hgrn_kernel,
        grid=(bsz, s_len // HG_TC),
        in_specs=[
            pl.BlockSpec((1, HG_TC, HG_WIDTH), blk(OFF_QH)),
            pl.BlockSpec((1, HG_TC, HG_WIDTH), blk(OFF_FH)),
            pl.BlockSpec((1, HG_TC, HG_WIDTH), blk(OFF_IH)),
            pl.BlockSpec((1, HG_TC, HG_WIDTH), blk(OFF_GH)),
            pl.BlockSpec((lb_logits.shape[0], HG_WIDTH), lambda b, t: (0, 0)),
            pl.BlockSpec((1, HG_DIM), lambda b, t: (0, 0)),
        ],
        out_specs=pl.BlockSpec((1, HG_TC, HG_WIDTH), lambda b, t: (b, t, 0)),
        out_shape=jax.ShapeDtypeStruct((bsz, s_len, HG_WIDTH), BF16),
        scratch_shapes=[pltpu.VMEM((HG_HEADS, HG_DIM, HG_DIM), F32)],
        compiler_params=pltpu.CompilerParams(
            dimension_semantics=("parallel", "arbitrary"), vmem_limit_bytes=_vmem(32)),
        name="hgrn2",
    )(proj3, proj3, proj3, proj3, lb_logits, gn)


def _mix_kernel(a_ref, r_ref, ga_ref, gh_ref, x_ref, wa_ref, wr_ref, wo_ref, g2_ref, whi_ref, wlo_ref, br_ref,
                x1_ref, h2p_ref, route_ref, cnt_out_ref, cnt_ref):
    @pl.when(pl.program_id(0) == 0)
    def _():
        cnt_ref[...] = jnp.zeros_like(cnt_ref)

    am = jnp.dot(a_ref[...], wa_ref[...], preferred_element_type=F32)
    rm = jnp.dot(r_ref[...], wr_ref[...], preferred_element_type=F32)
    mixed = _sigmoid(ga_ref[...].astype(F32)) * am + _sigmoid(gh_ref[...].astype(F32)) * rm
    x1 = x_ref[...] + jnp.dot(mixed.astype(BF16), wo_ref[...], preferred_element_type=F32)
    x1_ref[...] = x1
    h2 = x1 * lax.rsqrt(jnp.mean(x1 * x1, axis=-1, keepdims=True) + RMS_EPS) * g2_ref[...]

    hi = h2.astype(BF16)
    bits = pltpu.bitcast(hi.astype(F32), U32)
    half = D_MODEL // 2
    h2p_ref[...] = jnp.bitwise_or(jnp.right_shift(bits[:, :half], jnp.uint32(16)), bits[:, half:])

    lo = (h2 - hi.astype(F32)).astype(BF16)
    logits = jnp.dot(lo, whi_ref[...], preferred_element_type=F32)
    logits = logits + jnp.dot(hi, wlo_ref[...], preferred_element_type=F32)
    logits = logits + jnp.dot(hi, whi_ref[...], preferred_element_type=F32) + br_ref[...]

    lane = lax.broadcasted_iota(jnp.int32, logits.shape, 1)
    lane_f = lane.astype(F32)
    big = float(ROUTE_LANES)
    gl = jnp.where(lane < N_GROUPS, logits, NEG_INF)
    gmax = jnp.max(gl, axis=-1, keepdims=True)
    g_idx = jnp.min(jnp.where(gl == gmax, lane_f, big), axis=-1, keepdims=True)
    g_w = 1.0 / jnp.sum(jnp.exp(gl - gmax), axis=-1, keepdims=True)
    e_lane = lane - N_GROUPS
    in_group = jnp.logical_and(e_lane >= 0, e_lane < N_EXPERTS)
    lane_group = jnp.right_shift(e_lane, 3).astype(F32)
    in_group = jnp.logical_and(in_group, lane_group == g_idx)
    el = jnp.where(in_group, logits, NEG_INF)
    m1 = jnp.max(el, axis=-1, keepdims=True)
    i1 = jnp.min(jnp.where(el == m1, lane_f, big), axis=-1, keepdims=True)
    el2 = jnp.where(lane_f == i1, NEG_INF, el)
    m2 = jnp.max(el2, axis=-1, keepdims=True)
    i2 = jnp.min(jnp.where(el2 == m2, lane_f, big), axis=-1, keepdims=True)
    p2 = jnp.exp(m2 - m1)
    w1 = g_w / (1.0 + p2)
    w2 = g_w * p2 / (1.0 + p2)
    e1 = i1 - N_GROUPS
    e2 = i2 - N_GROUPS

    tm = logits.shape[0]
    rr = lax.broadcasted_iota(jnp.int32, (tm, tm), 0)
    cc = lax.broadcasted_iota(jnp.int32, (tm, tm), 1)
    ltri = jnp.where(rr > cc, 1.0, 0.0).astype(BF16)
    sel1 = lane_f == e1
    sel2 = lane_f == e2
    oh1 = jnp.where(sel1, 1.0, 0.0)
    oh2 = jnp.where(sel2, 1.0, 0.0)
    base = cnt_ref[...]
    c1 = jnp.sum(oh1, axis=0, keepdims=True)
    c2 = jnp.sum(oh2, axis=0, keepdims=True)
    pre1 = jnp.dot(ltri, oh1.astype(BF16), preferred_element_type=F32) + base
    pre2 = jnp.dot(ltri, oh2.astype(BF16), preferred_element_type=F32) + (base + c1)
    rank1 = jnp.sum(jnp.where(sel1, pre1, 0.0), axis=-1, keepdims=True)
    rank2 = jnp.sum(jnp.where(sel2, pre2, 0.0), axis=-1, keepdims=True)
    total = base + c1 + c2
    cnt_ref[...] = total
    cnt_out_ref[...] = total

    route = jnp.zeros_like(logits)
    for k, val in enumerate((w1, w2, e1, e2, rank1, rank2)):
        route = jnp.where(lane == k, val, route)
    route_ref[...] = route


def _mix(a2d, r2d, proj, x2d, wa, wr, wo, g2, whi, wlo, br):
    n = x2d.shape[0]
    row = lambda i: (i, 0)
    const = lambda i: (0, 0)
    return pl.pallas_call(
        _mix_kernel,
        grid=(n // MIX_TM,),
        in_specs=[
            pl.BlockSpec((MIX_TM, ATTN_WIDTH), row),
            pl.BlockSpec((MIX_TM, HG_WIDTH), row),
            pl.BlockSpec((MIX_TM, D_MODEL), lambda i: (i, OFF_GATE_A // D_MODEL)),
            pl.BlockSpec((MIX_TM, D_MODEL), lambda i: (i, OFF_GATE_H // D_MODEL)),
            pl.BlockSpec((MIX_TM, D_MODEL), row),
            pl.BlockSpec((ATTN_WIDTH, D_MODEL), const),
            pl.BlockSpec((HG_WIDTH, D_MODEL), const),
            pl.BlockSpec((D_MODEL, D_MODEL), const),
            pl.BlockSpec((1, D_MODEL), const),
            pl.BlockSpec((D_MODEL, ROUTE_LANES), const),
            pl.BlockSpec((D_MODEL, ROUTE_LANES), const),
            pl.BlockSpec((1, ROUTE_LANES), const),
        ],
        out_specs=[
            pl.BlockSpec((MIX_TM, D_MODEL), row),
            pl.BlockSpec((MIX_TM, D_MODEL // 2), row),
            pl.BlockSpec((MIX_TM, ROUTE_LANES), row),
            pl.BlockSpec((1, ROUTE_LANES), const),
        ],
        out_shape=[
            jax.ShapeDtypeStruct((n, D_MODEL), F32),
            jax.ShapeDtypeStruct((n, D_MODEL // 2), U32),
            jax.ShapeDtypeStruct((n, ROUTE_LANES), F32),
            jax.ShapeDtypeStruct((1, ROUTE_LANES), F32),
        ],
        scratch_shapes=[pltpu.VMEM((1, ROUTE_LANES), F32)],
        compiler_params=pltpu.CompilerParams(dimension_semantics=("arbitrary",), vmem_limit_bytes=_vmem(56)),
        name="mix",
    )(a2d, r2d, proj, proj, x2d, wa, wr, wo, g2, whi, wlo, br)


def _dispatch_kernel(dest_ref, src_hbm, zeros_hbm, dst_hbm, sem):
    del zeros_hbm
    step = pl.program_id(0)
    base = step * DISPATCH_ROWS

    def issue(r, c):
        tok = jnp.right_shift(base + r, 1)
        pltpu.make_async_copy(src_hbm.at[pl.ds(tok, 1)], dst_hbm.at[pl.ds(dest_ref[0, 0, r], 1)], sem.at[0]).start()
        return c

    lax.fori_loop(0, DISPATCH_ROWS, issue, 0, unroll=DMA_UNROLL)

    def wait_one_step():
        pltpu.make_async_copy(src_hbm.at[pl.ds(0, DISPATCH_ROWS)], dst_hbm.at[pl.ds(0, DISPATCH_ROWS)],
                              sem.at[0]).wait()

    @pl.when(step > 0)
    def _():
        wait_one_step()

    @pl.when(step == pl.num_programs(0) - 1)
    def _():
        wait_one_step()


def _dispatch(dest_flat, h2p, n_rows):
    n_asg = dest_flat.shape[0]
    steps = n_asg // DISPATCH_ROWS
    return pl.pallas_call(
        _dispatch_kernel,
        grid=(steps,),
        in_specs=[
            pl.BlockSpec((1, 1, DISPATCH_ROWS), lambda s: (s, 0, 0), memory_space=pltpu.SMEM),
            pl.BlockSpec(memory_space=pl.ANY),
            pl.BlockSpec(memory_space=pl.ANY),
        ],
        out_specs=pl.BlockSpec(memory_space=pl.ANY),
        out_shape=jax.ShapeDtypeStruct((n_rows, D_MODEL // 2), U32),
        input_output_aliases={2: 0},
        scratch_shapes=[pltpu.SemaphoreType.DMA((1,))],
        compiler_params=pltpu.CompilerParams(dimension_semantics=("arbitrary",)),
        name="dispatch",
    )(dest_flat.reshape(steps, 1, DISPATCH_ROWS), h2p, jnp.zeros((n_rows, D_MODEL // 2), U32))


def _moe_kernel(be_ref, nused_ref, xs_ref, wg_ref, wu_ref, wd_ref, y_ref, wg_s, wu_s, wd_s):
    b = pl.program_id(0)
    live = b < nused_ref[0]
    new_expert = jnp.logical_or(b == 0, be_ref[b] != be_ref[jnp.maximum(b - 1, 0)])

    @pl.when(jnp.logical_and(live, new_expert))
    def _():
        wg_s[...] = wg_ref[0, 0].astype(BF16)
        wu_s[...] = wu_ref[0, 0].astype(BF16)
        wd_s[...] = wd_ref[0, 0].astype(BF16)

    @pl.when(jnp.logical_not(live))
    def _():
        y_ref[...] = jnp.zeros_like(y_ref)

    @pl.when(live)
    def _():
        half = D_MODEL // 2
        x = xs_ref[...]
        xlo = pltpu.bitcast(jnp.left_shift(x, jnp.uint32(16)), F32).astype(BF16)
        xhi = pltpu.bitcast(jnp.bitwise_and(x, jnp.uint32(0xFFFF0000)), F32).astype(BF16)
        hg = jnp.dot(xlo, wg_s[:half], preferred_element_type=F32) + jnp.dot(xhi, wg_s[half:], preferred_element_type=F32)
        hu = jnp.dot(xlo, wu_s[:half], preferred_element_type=F32) + jnp.dot(xhi, wu_s[half:], preferred_element_type=F32)
        hb = (hg * _sigmoid(hg) * hu).astype(BF16)
        y_ref[...] = jnp.dot(hb, wd_s[...], preferred_element_type=F32)


def _moe(block_e, n_used, xs, w_gate, w_up, w_down):
    n_rows = xs.shape[0]
    n_blocks = n_rows // MOE_TM
    wspec = lambda shp: pl.BlockSpec((1, 1) + shp, lambda b, be, nu: (0, be[b], 0, 0))
    grid_spec = pltpu.PrefetchScalarGridSpec(
        num_scalar_prefetch=2,
        grid=(n_blocks,),
        in_specs=[
            pl.BlockSpec((MOE_TM, D_MODEL // 2), lambda b, be, nu: (jnp.minimum(b, nu[0] - 1), 0)),
            wspec((D_MODEL, EXPERT_FF)),
            wspec((D_MODEL, EXPERT_FF)),
            wspec((EXPERT_FF, D_MODEL)),
        ],
        out_specs=pl.BlockSpec((MOE_TM, D_MODEL), lambda b, be, nu: (b, 0)),
        scratch_shapes=[
            pltpu.VMEM((D_MODEL, EXPERT_FF), BF16),
            pltpu.VMEM((D_MODEL, EXPERT_FF), BF16),
            pltpu.VMEM((EXPERT_FF, D_MODEL), BF16),
        ],
    )
    return pl.pallas_call(
        _moe_kernel,
        grid_spec=grid_spec,
        out_shape=jax.ShapeDtypeStruct((n_rows, D_MODEL), F32),
        compiler_params=pltpu.CompilerParams(dimension_semantics=("arbitrary",), vmem_limit_bytes=_vmem(52)),
        name="moe",
    )(block_e, n_used, xs, w_gate, w_up, w_down)


def _moe_layout(route, counts_f):
    counts = counts_f[0, :N_EXPERTS].astype(jnp.int32)
    padded = (counts + MOE_TM - 1) // MOE_TM * MOE_TM
    pends = jnp.cumsum(padded)
    pstarts = pends - padded
    n_tok = route.shape[0]
    n_blocks = n_tok * TOP_K // MOE_TM + N_EXPERTS
    blk_start = jnp.arange(n_blocks, dtype=jnp.int32) * MOE_TM
    block_e = jnp.minimum(jnp.sum((blk_start[:, None] >= pends[None, :]).astype(jnp.int32), axis=1), N_EXPERTS - 1)
    n_used = (pends[-1] // MOE_TM).astype(jnp.int32).reshape(1)
    e_id = route[:, 2:2 + TOP_K].astype(jnp.int32)
    rank = route[:, 2 + TOP_K:2 + 2 * TOP_K].astype(jnp.int32)
    onehot_e = e_id[:, :, None] == jnp.arange(N_EXPERTS, dtype=jnp.int32)[None, None, :]
    dest = jnp.sum(jnp.where(onehot_e, pstarts[None, None, :], 0), axis=2) + rank
    return block_e.astype(jnp.int32), n_used, dest, n_blocks * MOE_TM


def _final_kernel(dest_ref, destn_ref, x1_ref, route_ref, g_ref, y_hbm, o_ref, ybuf, sem):
    i = pl.program_id(0)
    n_steps = pl.num_programs(0)
    slot = lax.rem(i, 2)
    n_rows = TOP_K * FIN_TM

    def start_gather(idx_ref, s):
        def issue(r, c):
            pltpu.make_async_copy(y_hbm.at[pl.ds(idx_ref[0, 0, r], 1)], ybuf.at[s, pl.ds(r, 1)], sem.at[s]).start()
            return c
        lax.fori_loop(0, n_rows, issue, 0, unroll=DMA_UNROLL)

    @pl.when(i == 0)
    def _():
        start_gather(dest_ref, 0)

    @pl.when(i + 1 < n_steps)
    def _():
        start_gather(destn_ref, 1 - slot)

    pltpu.make_async_copy(y_hbm.at[pl.ds(0, n_rows)], ybuf.at[slot], sem.at[slot]).wait()
    route = route_ref[...]
    x = x1_ref[...]
    for k in range(TOP_K):
        x = x + route[:, k:k + 1] * ybuf[slot, k * FIN_TM:(k + 1) * FIN_TM, :]
    o_ref[...] = x * lax.rsqrt(jnp.mean(x * x, axis=-1, keepdims=True) + RMS_EPS) * g_ref[...]


def _final(dest, x1, route, g, y_sorted):
    n = x1.shape[0]
    steps = n // FIN_TM
    dest3 = dest.reshape(steps, FIN_TM, TOP_K).transpose(0, 2, 1).reshape(steps, 1, TOP_K * FIN_TM)
    return pl.pallas_call(
        _final_kernel,
        grid=(steps,),
        in_specs=[
            pl.BlockSpec((1, 1, TOP_K * FIN_TM), lambda i: (i, 0, 0), memory_space=pltpu.SMEM),
            pl.BlockSpec((1, 1, TOP_K * FIN_TM), lambda i: (jnp.minimum(i + 1, steps - 1), 0, 0),
                         memory_space=pltpu.SMEM),
            pl.BlockSpec((FIN_TM, D_MODEL), lambda i: (i, 0)),
            pl.BlockSpec((FIN_TM, ROUTE_LANES), lambda i: (i, 0)),
            pl.BlockSpec((1, D_MODEL), lambda i: (0, 0)),
            pl.BlockSpec(memory_space=pl.ANY),
        ],
        out_specs=pl.BlockSpec((FIN_TM, D_MODEL), lambda i: (i, 0)),
        out_shape=jax.ShapeDtypeStruct((n, D_MODEL), F32),
        scratch_shapes=[
            pltpu.VMEM((2, TOP_K * FIN_TM, D_MODEL), F32),
            pltpu.SemaphoreType.DMA((2,)),
        ],
        compiler_params=pltpu.CompilerParams(dimension_semantics=("arbitrary",), vmem_limit_bytes=_vmem(40)),
        name="final",
    )(dest3, dest3, x1, route, g, y_sorted)


def kernel(x, norm1_g, w_in, attn_sinks, rel_bias, hg_lb_logits, hg_norm_g, w_attn_branch, w_hg_branch, w_out,
           norm2_g, w_group_router, b_group_router, w_expert_router, b_expert_router, w_gate, w_up, w_down, final_g):
    bsz, s_len, d = x.shape
    n_tok = bsz * s_len
    x2d = x.reshape(n_tok, d)

    w_route = jnp.concatenate([w_group_router[0], w_expert_router[0]], axis=1).astype(F32)
    w_route = jnp.pad(w_route, ((0, 0), (0, ROUTE_LANES - w_route.shape[1])))
    whi = w_route.astype(BF16)
    wlo = (w_route - whi.astype(F32)).astype(BF16)
    b_route = jnp.concatenate([b_group_router[0], b_expert_router[0]]).astype(F32)
    b_route = jnp.pad(b_route, (0, ROUTE_LANES - b_route.shape[0])).reshape(1, ROUTE_LANES)
    bias = _attn_bias_table(rel_bias)

    proj = _in_proj(x2d, norm1_g[0].reshape(1, d).astype(F32), w_in)
    proj3 = proj.reshape(bsz, s_len, IN_WIDTH)
    a = _swa(proj3, attn_sinks[0].astype(F32), bias)
    r = _hgrn2(proj3, hg_lb_logits.astype(F32), hg_norm_g[0].reshape(1, HG_DIM).astype(F32))
    x1, h2p, route, counts = _mix(
        a.reshape(n_tok, ATTN_WIDTH), r.reshape(n_tok, HG_WIDTH), proj, x2d,
        w_attn_branch[0].astype(BF16), w_hg_branch[0].astype(BF16), w_out[0].astype(BF16),
        norm2_g[0].reshape(1, d).astype(F32), whi, wlo, b_route)
    block_e, n_used, dest, n_rows = _moe_layout(route, counts)
    xs = _dispatch(dest.reshape(n_tok * TOP_K), h2p, n_rows)
    y_sorted = _moe(block_e, n_used, xs, w_gate, w_up, w_down)
    out = _final(dest, x1, route, final_g.reshape(1, d).astype(F32), y_sorted)
    return out.reshape(bsz, s_len, d)
```

```python
import math

import numpy as np
import jax
import jax.numpy as jnp
from jax import lax
from jax.experimental import pallas as pl
from jax.experimental.pallas import tpu as pltpu

D_MODEL = 2048
N_Q_HEADS = 16
N_KV_HEADS = 4
HEAD_DIM = 64
WINDOW = 128
ATTN_BLOCK = 128
ATTN_WIDTH = N_Q_HEADS * HEAD_DIM
KV_WIDTH = N_KV_HEADS * HEAD_DIM
REL_BUCKETS = 32
REL_MAX_DIST = 128
HG_HEADS = 8
HG_DIM = 128
HG_WIDTH = HG_HEADS * HG_DIM
HG_CHUNK = 64
HG_SUB = 16
N_GROUPS = 4
EXPERTS_PER_GROUP = 8
N_EXPERTS = N_GROUPS * EXPERTS_PER_GROUP
TOP_K = 2
EXPERT_FF = 512
RMS_EPS = 1e-6

OFF_GATE_A = 0
OFF_GATE_H = D_MODEL
OFF_QA = 2 * D_MODEL
OFF_QH = OFF_QA + ATTN_WIDTH
OFF_FH = OFF_QH + HG_WIDTH
OFF_IH = OFF_FH + HG_WIDTH
OFF_GH = OFF_IH + HG_WIDTH
OFF_KA = OFF_GH + HG_WIDTH
OFF_VA = OFF_KA + KV_WIDTH
IN_WIDTH = OFF_VA + KV_WIDTH
REF_OFF_QA = 0
REF_OFF_KA = ATTN_WIDTH
REF_OFF_QH = ATTN_WIDTH + 2 * KV_WIDTH
REF_OFF_GATE_A = REF_OFF_QH + 4 * HG_WIDTH

HG_SAFE_DECAY = 60.0

IN_TM = 1024
IN_TN = 512
HG_TC = 256
MIX_TM = 256
ROUTE_LANES = 128
DISPATCH_ROWS = 1024
DMA_UNROLL = 8
MOE_TM = 256
FIN_TM = 256

F32 = jnp.float32
BF16 = jnp.bfloat16
U32 = jnp.uint32
NEG_INF = float("-inf")


def _vmem(mib):
    return mib * 1024 * 1024


def _sigmoid(x):
    return 1.0 / (1.0 + jnp.exp(-x))


def _dot_nt(a, b):
    return lax.dot_general(a, b, (((1,), (1,)), ((), ())), preferred_element_type=F32)


def _dot_tn(a, b):
    return lax.dot_general(a, b, (((0,), (0,)), ((), ())), preferred_element_type=F32)


def _in_proj_kernel(x_ref, g_ref, w_ref, o_ref, h_ref):
    @pl.when(pl.program_id(1) == 0)
    def _():
        x = x_ref[...]
        ms = jnp.mean(x * x, axis=-1, keepdims=True)
        h_ref[...] = (x * lax.rsqrt(ms + RMS_EPS) * g_ref[...]).astype(BF16)

    o_ref[...] = jnp.dot(h_ref[...], w_ref[0].astype(BF16), preferred_element_type=F32).astype(o_ref.dtype)


def _in_proj(x2d, g, w_in):
    n = x2d.shape[0]
    n_col = IN_WIDTH // IN_TN

    def out_block(j):
        gates = j - REF_OFF_GATE_A // IN_TN + OFF_GATE_A // IN_TN
        q_a = j - REF_OFF_QA // IN_TN + OFF_QA // IN_TN
        kv = j - REF_OFF_KA // IN_TN + OFF_KA // IN_TN
        hg = j - REF_OFF_QH // IN_TN + OFF_QH // IN_TN
        return jnp.where(j >= REF_OFF_GATE_A // IN_TN, gates,
                         jnp.where(j >= REF_OFF_QH // IN_TN, hg, jnp.where(j >= REF_OFF_KA // IN_TN, kv, q_a)))

    return pl.pallas_call(
        _in_proj_kernel,
        grid=(n // IN_TM, n_col),
        in_specs=[
            pl.BlockSpec((IN_TM, D_MODEL), lambda i, j: (i, 0)),
            pl.BlockSpec((1, D_MODEL), lambda i, j: (0, 0)),
            pl.BlockSpec((1, D_MODEL, IN_TN), lambda i, j: (0, 0, j)),
        ],
        out_specs=pl.BlockSpec((IN_TM, IN_TN), lambda i, j: (i, out_block(j))),
        out_shape=jax.ShapeDtypeStruct((n, IN_WIDTH), BF16),
        scratch_shapes=[pltpu.VMEM((IN_TM, D_MODEL), BF16)],
        compiler_params=pltpu.CompilerParams(
            dimension_semantics=("parallel", "arbitrary"), vmem_limit_bytes=_vmem(48)),
        name="in_proj",
    )(x2d, g, w_in)


def _swa_kernel(sink_ref, q_ref, kp_ref, kc_ref, vp_ref, vc_ref, bias_ref, o_ref):
    has_prev = pl.program_id(1) > 0
    q = q_ref[0]
    kk = jnp.concatenate([kp_ref[0], kc_ref[0]], axis=0)
    vv = jnp.concatenate([vp_ref[0], vc_ref[0]], axis=0)
    col = lax.broadcasted_iota(jnp.int32, (ATTN_BLOCK, 2 * ATTN_BLOCK), 1)
    key_ok = jnp.logical_or(col >= ATTN_BLOCK, has_prev)
    scale = HEAD_DIM ** -0.5
    grp = N_Q_HEADS // N_KV_HEADS
    outs = []
    for hq in range(N_Q_HEADS):
        hk = hq // grp
        qh = q[:, hq * HEAD_DIM:(hq + 1) * HEAD_DIM]
        kh = kk[:, hk * HEAD_DIM:(hk + 1) * HEAD_DIM]
        vh = vv[:, hk * HEAD_DIM:(hk + 1) * HEAD_DIM]
        s = _dot_nt(qh, kh) * scale
        s = jnp.where(key_ok, s + bias_ref[hq], NEG_INF)
        sink = sink_ref[hq]
        m = jnp.maximum(jnp.max(s, axis=-1, keepdims=True), sink)
        p = jnp.exp(s - m)
        denom = jnp.sum(p, axis=-1, keepdims=True) + jnp.exp(sink - m)
        o = jnp.dot(p.astype(BF16), vh, preferred_element_type=F32)
        outs.append(o / denom)
    o_ref[0] = jnp.concatenate(outs, axis=-1).astype(o_ref.dtype)


def _swa(proj3, sinks, bias):
    bsz, s_len, _ = proj3.shape
    nb = s_len // ATTN_BLOCK
    kblk = OFF_KA // KV_WIDTH
    vblk = OFF_VA // KV_WIDTH
    prev = lambda n: jnp.maximum(n - 1, 0)
    return pl.pallas_call(
        _swa_kernel,
        grid=(bsz, nb),
        in_specs=[
            pl.BlockSpec(memory_space=pltpu.SMEM),
            pl.BlockSpec((1, ATTN_BLOCK, ATTN_WIDTH), lambda b, n: (b, n, OFF_QA // ATTN_WIDTH)),
            pl.BlockSpec((1, ATTN_BLOCK, KV_WIDTH), lambda b, n: (b, prev(n), kblk)),
            pl.BlockSpec((1, ATTN_BLOCK, KV_WIDTH), lambda b, n: (b, n, kblk)),
            pl.BlockSpec((1, ATTN_BLOCK, KV_WIDTH), lambda b, n: (b, prev(n), vblk)),
            pl.BlockSpec((1, ATTN_BLOCK, KV_WIDTH), lambda b, n: (b, n, vblk)),
            pl.BlockSpec((N_Q_HEADS, ATTN_BLOCK, 2 * ATTN_BLOCK), lambda b, n: (0, 0, 0)),
        ],
        out_specs=pl.BlockSpec((1, ATTN_BLOCK, ATTN_WIDTH), lambda b, n: (b, n, 0)),
        out_shape=jax.ShapeDtypeStruct((bsz, s_len, ATTN_WIDTH), BF16),
        compiler_params=pltpu.CompilerParams(
            dimension_semantics=("parallel", "arbitrary"), vmem_limit_bytes=_vmem(32)),
        name="swa",
    )(sinks, proj3, proj3, proj3, proj3, proj3, bias)


def _t5_causal_bucket(n):
    max_exact = REL_BUCKETS // 2
    nf = jnp.maximum(n, 1).astype(F32)
    large = max_exact + (jnp.log(nf / max_exact) / math.log(REL_MAX_DIST / max_exact)
                         * (REL_BUCKETS - max_exact)).astype(jnp.int32)
    large = jnp.minimum(large, REL_BUCKETS - 1)
    return jnp.where(n < max_exact, n, large)


def _attn_bias_table(rel_bias):
    qi = jnp.arange(ATTN_BLOCK)[:, None]
    kj = jnp.arange(2 * ATTN_BLOCK)[None, :]
    dist = qi + ATTN_BLOCK - kj
    bucket = _t5_causal_bucket(jnp.clip(dist, 0))
    onehot = (bucket[None, :, :] == jnp.arange(REL_BUCKETS)[:, None, None]).astype(F32)
    bias = jnp.einsum("bh,bqk->hqk", rel_bias.astype(F32), onehot, precision=lax.Precision.HIGHEST)
    valid = (dist >= 0) & (dist < WINDOW)
    return jnp.where(valid[None], bias, NEG_INF)


def _cumsum_rows(tri_bf16, g):
    width = g.shape[1]
    g1 = g.astype(BF16)
    r1 = g - g1.astype(F32)
    g2 = r1.astype(BF16)
    g3 = (r1 - g2.astype(F32)).astype(BF16)
    acc = jnp.dot(tri_bf16, jnp.concatenate([g3, g2, g1], axis=1), preferred_element_type=F32)
    return (acc[:, :width] + acc[:, width:2 * width]) + acc[:, 2 * width:]


def _hgrn_head_safe(q, kf, v, b, st):
    rows = lax.broadcasted_iota(jnp.int32, (HG_SUB, 1), 0)
    outs = []
    prev_end = jnp.zeros((1, HG_DIM), F32)
    for i in range(HG_CHUNK // HG_SUB):
        sl = slice(i * HG_SUB, (i + 1) * HG_SUB)
        bl = b[sl] - prev_end
        qi, ki, vi = q[sl], kf[sl], v[sl]
        vif = vi.astype(F32)
        blast = bl[HG_SUB - 1:HG_SUB]
        o = _dot_nt((qi * jnp.exp(bl)).astype(BF16), st.astype(BF16))
        for s in range(HG_SUB):
            d = jnp.exp(jnp.minimum(bl - bl[s:s + 1], 0.0))
            a = jnp.sum(qi * ki[s:s + 1] * d, axis=-1, keepdims=True)
            o = o + jnp.where(rows >= s, a, 0.0) * vif[s:s + 1]
        kd = (ki * jnp.exp(blast - bl)).astype(BF16)
        st = st * jnp.exp(blast) + _dot_tn(vi, kd)
        prev_end = b[(i + 1) * HG_SUB - 1:(i + 1) * HG_SUB]
        outs.append(o)
    return jnp.concatenate(outs, axis=0), st


def _hgrn_kernel(q_ref, f_ref, i_ref, g_ref, lbl_ref, gn_ref, o_ref, st_ref):
    @pl.when(pl.program_id(1) == 0)
    def _():
        st_ref[...] = jnp.zeros_like(st_ref)

    lg2 = lbl_ref[...].astype(F32)
    e = jnp.exp(lg2 - jnp.max(lg2, axis=0, keepdims=True))
    lb = e[0:1] / jnp.sum(e, axis=0, keepdims=True)
    gn = gn_ref[...].astype(F32)
    r = lax.broadcasted_iota(jnp.int32, (HG_CHUNK, HG_CHUNK), 0)
    c = lax.broadcasted_iota(jnp.int32, (HG_CHUNK, HG_CHUNK), 1)
    tri_mask = r >= c
    tri = jnp.where(tri_mask, 1.0, 0.0).astype(BF16)

    def chunk(ci, carry):
        start = pl.multiple_of(ci * HG_CHUNK, HG_CHUNK)
        sl = pl.ds(start, HG_CHUNK)
        q = q_ref[0, sl, :].astype(F32)
        fr = f_ref[0, sl, :].astype(F32)
        f = lb + (1.0 - lb) * _sigmoid(fr)
        kf = 1.0 - f
        b = _cumsum_rows(tri, jnp.log(f))
        bend = b[HG_CHUNK - 1:HG_CHUNK]

        def finish(h, o):
            hs = slice(h * HG_DIM, (h + 1) * HG_DIM)
            o = o * lax.rsqrt(jnp.mean(o * o, axis=-1, keepdims=True) + RMS_EPS) * gn
            go = g_ref[0, sl, hs].astype(F32)
            o_ref[0, sl, hs] = (o * (go * _sigmoid(go))).astype(o_ref.dtype)

        def fast():
            qp = (q * jnp.exp(b)).astype(BF16)
            kp = kf * jnp.exp(-b)
            kpb = kp.astype(BF16)
            eb = jnp.exp(bend)
            kd = (kp * eb).astype(BF16)
            for h in range(HG_HEADS):
                hs = slice(h * HG_DIM, (h + 1) * HG_DIM)
                st = st_ref[h]
                v = i_ref[0, sl, hs]
                att = jnp.where(tri_mask, _dot_nt(qp[:, hs], kpb[:, hs]), 0.0)
                o = _dot_nt(qp[:, hs], st.astype(BF16)) + jnp.dot(att.astype(BF16), v, preferred_element_type=F32)
                st_ref[h] = st * eb[:, hs] + _dot_tn(v, kd[:, hs])
                finish(h, o)

        def safe():
            for h in range(HG_HEADS):
                hs = slice(h * HG_DIM, (h + 1) * HG_DIM)
                o, st_new = _hgrn_head_safe(q[:, hs], kf[:, hs], i_ref[0, sl, hs], b[:, hs], st_ref[h])
                st_ref[h] = st_new
                finish(h, o)

        lax.cond(jnp.min(bend) >= -HG_SAFE_DECAY, fast, safe)
        return carry

    lax.fori_loop(0, HG_TC // HG_CHUNK, chunk, 0)


def _hgrn2(proj3, lb_logits, gn):
    bsz, s_len, _ = proj3.shape
    blk = lambda off: (lambda b, t: (b, t, off // HG_WIDTH))
    return pl.pallas_call(
        _hgrn_kernel,
        grid=(bsz, s_len // HG_TC),
        in_specs=[
            pl.BlockSpec((1, HG_TC, HG_WIDTH), blk(OFF_QH)),
            pl.BlockSpec((1, HG_TC, HG_WIDTH), blk(OFF_FH)),
            pl.BlockSpec((1, HG_TC, HG_WIDTH), blk(OFF_IH)),
            pl.BlockSpec((1, HG_TC, HG_WIDTH), blk(OFF_GH)),
            pl.BlockSpec((lb_logits.shape[0], HG_WIDTH), lambda b, t: (0, 0)),
            pl.BlockSpec((1, HG_DIM), lambda b, t: (0, 0)),
        ],
        out_specs=pl.BlockSpec((1, HG_TC, HG_WIDTH), lambda b, t: (b, t, 0)),
        out_shape=jax.ShapeDtypeStruct((bsz, s_len, HG_WIDTH), BF16),
        scratch_shapes=[pltpu.VMEM((HG_HEADS, HG_DIM, HG_DIM), F32)],
        compiler_params=pltpu.CompilerParams(
            dimension_semantics=("parallel", "arbitrary"), vmem_limit_bytes=_vmem(32)),
        name="hgrn2",
    )(proj3, proj3, proj3, proj3, lb_logits, gn)


def _mix_kernel(a_ref, r_ref, ga_ref, gh_ref, x_ref, wa_ref, wr_ref, wo_ref, g2_ref, whi_ref, wlo_ref, br_ref,
                x1_ref, h2p_ref, route_ref, cnt_out_ref, cnt_ref):
    @pl.when(pl.program_id(0) == 0)
    def _():
        cnt_ref[...] = jnp.zeros_like(cnt_ref)

    am = jnp.dot(a_ref[...], wa_ref[...], preferred_element_type=F32)
    rm = jnp.dot(r_ref[...], wr_ref[...], preferred_element_type=F32)
    mixed = _sigmoid(ga_ref[...].astype(F32)) * am + _sigmoid(gh_ref[...].astype(F32)) * rm
    x1 = x_ref[...] + jnp.dot(mixed.astype(BF16), wo_ref[...], preferred_element_type=F32)
    x1_ref[...] = x1
    h2 = x1 * lax.rsqrt(jnp.mean(x1 * x1, axis=-1, keepdims=True) + RMS_EPS) * g2_ref[...]

    hi = h2.astype(BF16)
    bits = pltpu.bitcast(hi.astype(F32), U32)
    half = D_MODEL // 2
    h2p_ref[...] = jnp.bitwise_or(jnp.right_shift(bits[:, :half], jnp.uint32(16)), bits[:, half:])

    lo = (h2 - hi.astype(F32)).astype(BF16)
    logits = jnp.dot(lo, whi_ref[...], preferred_element_type=F32)
    logits = logits + jnp.dot(hi, wlo_ref[...], preferred_element_type=F32)
    logits = logits + jnp.dot(hi, whi_ref[...], preferred_element_type=F32) + br_ref[...]

    lane = lax.broadcasted_iota(jnp.int32, logits.shape, 1)
    lane_f = lane.astype(F32)
    big = float(ROUTE_LANES)
    gl = jnp.where(lane < N_GROUPS, logits, NEG_INF)
    gmax = jnp.max(gl, axis=-1, keepdims=True)
    g_idx = jnp.min(jnp.where(gl == gmax, lane_f, big), axis=-1, keepdims=True)
    g_w = 1.0 / jnp.sum(jnp.exp(gl - gmax), axis=-1, keepdims=True)
    e_lane = lane - N_GROUPS
    in_group = jnp.logical_and(e_lane >= 0, e_lane < N_EXPERTS)
    lane_group = jnp.right_shift(e_lane, 3).astype(F32)
    in_group = jnp.logical_and(in_group, lane_group == g_idx)
    el = jnp.where(in_group, logits, NEG_INF)
    m1 = jnp.max(el, axis=-1, keepdims=True)
    i1 = jnp.min(jnp.where(el == m1, lane_f, big), axis=-1, keepdims=True)
    el2 = jnp.where(lane_f == i1, NEG_INF, el)
    m2 = jnp.max(el2, axis=-1, keepdims=True)
    i2 = jnp.min(jnp.where(el2 == m2, lane_f, big), axis=-1, keepdims=True)
    p2 = jnp.exp(m2 - m1)
    w1 = g_w / (1.0 + p2)
    w2 = g_w * p2 / (1.0 + p2)
    e1 = i1 - N_GROUPS
    e2 = i2 - N_GROUPS

    tm = logits.shape[0]
    rr = lax.broadcasted_iota(jnp.int32, (tm, tm), 0)
    cc = lax.broadcasted_iota(jnp.int32, (tm, tm), 1)
    ltri = jnp.where(rr > cc, 1.0, 0.0).astype(BF16)
    sel1 = lane_f == e1
    sel2 = lane_f == e2
    oh1 = jnp.where(sel1, 1.0, 0.0)
    oh2 = jnp.where(sel2, 1.0, 0.0)
    base = cnt_ref[...]
    c1 = jnp.sum(oh1, axis=0, keepdims=True)
    c2 = jnp.sum(oh2, axis=0, keepdims=True)
    pre1 = jnp.dot(ltri, oh1.astype(BF16), preferred_element_type=F32) + base
    pre2 = jnp.dot(ltri, oh2.astype(BF16), preferred_element_type=F32) + (base + c1)
    rank1 = jnp.sum(jnp.where(sel1, pre1, 0.0), axis=-1, keepdims=True)
    rank2 = jnp.sum(jnp.where(sel2, pre2, 0.0), axis=-1, keepdims=True)
    total = base + c1 + c2
    cnt_ref[...] = total
    cnt_out_ref[...] = total

    route = jnp.zeros_like(logits)
    for k, val in enumerate((w1, w2, e1, e2, rank1, rank2)):
        route = jnp.where(lane == k, val, route)
    route_ref[...] = route


def _mix(a2d, r2d, proj, x2d, wa, wr, wo, g2, whi, wlo, br):
    n = x2d.shape[0]
    row = lambda i: (i, 0)
    const = lambda i: (0, 0)
    return pl.pallas_call(
        _mix_kernel,
        grid=(n // MIX_TM,),
        in_specs=[
            pl.BlockSpec((MIX_TM, ATTN_WIDTH), row),
            pl.BlockSpec((MIX_TM, HG_WIDTH), row),
            pl.BlockSpec((MIX_TM, D_MODEL), lambda i: (i, OFF_GATE_A // D_MODEL)),
            pl.BlockSpec((MIX_TM, D_MODEL), lambda i: (i, OFF_GATE_H // D_MODEL)),
            pl.BlockSpec((MIX_TM, D_MODEL), row),
            pl.BlockSpec((ATTN_WIDTH, D_MODEL), const),
            pl.BlockSpec((HG_WIDTH, D_MODEL), const),
            pl.BlockSpec((D_MODEL, D_MODEL), const),
            pl.BlockSpec((1, D_MODEL), const),
            pl.BlockSpec((D_MODEL, ROUTE_LANES), const),
            pl.BlockSpec((D_MODEL, ROUTE_LANES), const),
            pl.BlockSpec((1, ROUTE_LANES), const),
        ],
        out_specs=[
            pl.BlockSpec((MIX_TM, D_MODEL), row),
            pl.BlockSpec((MIX_TM, D_MODEL // 2), row),
            pl.BlockSpec((MIX_TM, ROUTE_LANES), row),
            pl.BlockSpec((1, ROUTE_LANES), const),
        ],
        out_shape=[
            jax.ShapeDtypeStruct((n, D_MODEL), F32),
            jax.ShapeDtypeStruct((n, D_MODEL // 2), U32),
            jax.ShapeDtypeStruct((n, ROUTE_LANES), F32),
            jax.ShapeDtypeStruct((1, ROUTE_LANES), F32),
        ],
        scratch_shapes=[pltpu.VMEM((1, ROUTE_LANES), F32)],
        compiler_params=pltpu.CompilerParams(dimension_semantics=("arbitrary",), vmem_limit_bytes=_vmem(56)),
        name="mix",
    )(a2d, r2d, proj, proj, x2d, wa, wr, wo, g2, whi, wlo, br)


def _dispatch_kernel(dest_ref, src_ref, zeros_hbm, dst_hbm, sem):
    del zeros_hbm

    def issue(t, c):
        for k in range(TOP_K):
            pltpu.make_async_copy(src_ref.at[pl.ds(t, 1)], dst_hbm.at[pl.ds(dest_ref[0, 0, TOP_K * t + k], 1)],
                                  sem.at[0]).start()
        return c

    lax.fori_loop(0, DISPATCH_ROWS // TOP_K, issue, 0, unroll=DMA_UNROLL // TOP_K)
    for _ in range(TOP_K):
        pltpu.make_async_copy(src_ref, dst_hbm.at[pl.ds(0, DISPATCH_ROWS // TOP_K)], sem.at[0]).wait()


def _dispatch(dest_flat, h2p, n_rows):
    n_asg = dest_flat.shape[0]
    steps = n_asg // DISPATCH_ROWS
    return pl.pallas_call(
        _dispatch_kernel,
        grid=(steps,),
        in_specs=[
            pl.BlockSpec((1, 1, DISPATCH_ROWS), lambda s: (s, 0, 0), memory_space=pltpu.SMEM),
            pl.BlockSpec((DISPATCH_ROWS // TOP_K, D_MODEL // 2), lambda s: (s, 0)),
            pl.BlockSpec(memory_space=pl.ANY),
        ],
        out_specs=pl.BlockSpec(memory_space=pl.ANY),
        out_shape=jax.ShapeDtypeStruct((n_rows, D_MODEL // 2), U32),
        input_output_aliases={2: 0},
        scratch_shapes=[pltpu.SemaphoreType.DMA((1,))],
        compiler_params=pltpu.CompilerParams(dimension_semantics=("arbitrary",)),
        name="dispatch",
    )(dest_flat.reshape(steps, 1, DISPATCH_ROWS), h2p, jnp.zeros((n_rows, D_MODEL // 2), U32))


def _moe_kernel(be_ref, nused_ref, xs_ref, wg_ref, wu_ref, wd_ref, y_ref, wg_s, wu_s, wd_s):
    b = pl.program_id(0)
    live = b < nused_ref[0]
    new_expert = jnp.logical_or(b == 0, be_ref[b] != be_ref[jnp.maximum(b - 1, 0)])

    @pl.when(jnp.logical_and(live, new_expert))
    def _():
        wg_s[...] = wg_ref[0, 0].astype(BF16)
        wu_s[...] = wu_ref[0, 0].astype(BF16)
        wd_s[...] = wd_ref[0, 0].astype(BF16)

    @pl.when(jnp.logical_not(live))
    def _():
        y_ref[...] = jnp.zeros_like(y_ref)

    @pl.when(live)
    def _():
        half = D_MODEL // 2
        x = xs_ref[...]
        xlo = pltpu.bitcast(jnp.left_shift(x, jnp.uint32(16)), F32).astype(BF16)
        xhi = pltpu.bitcast(jnp.bitwise_and(x, jnp.uint32(0xFFFF0000)), F32).astype(BF16)
        hg = jnp.dot(xlo, wg_s[:half], preferred_element_type=F32) + jnp.dot(xhi, wg_s[half:], preferred_element_type=F32)
        hu = jnp.dot(xlo, wu_s[:half], preferred_element_type=F32) + jnp.dot(xhi, wu_s[half:], preferred_element_type=F32)
        hb = (hg * _sigmoid(hg) * hu).astype(BF16)
        y_ref[...] = jnp.dot(hb, wd_s[...], preferred_element_type=F32)


def _moe(block_e, n_used, xs, w_gate, w_up, w_down):
    n_rows = xs.shape[0]
    n_blocks = n_rows // MOE_TM
    wspec = lambda shp: pl.BlockSpec((1, 1) + shp, lambda b, be, nu: (0, be[b], 0, 0))
    grid_spec = pltpu.PrefetchScalarGridSpec(
        num_scalar_prefetch=2,
        grid=(n_blocks,),
        in_specs=[
            pl.BlockSpec((MOE_TM, D_MODEL // 2), lambda b, be, nu: (jnp.minimum(b, nu[0] - 1), 0)),
            wspec((D_MODEL, EXPERT_FF)),
            wspec((D_MODEL, EXPERT_FF)),
            wspec((EXPERT_FF, D_MODEL)),
        ],
        out_specs=pl.BlockSpec((MOE_TM, D_MODEL), lambda b, be, nu: (b, 0)),
        scratch_shapes=[
            pltpu.VMEM((D_MODEL, EXPERT_FF), BF16),
            pltpu.VMEM((D_MODEL, EXPERT_FF), BF16),
            pltpu.VMEM((EXPERT_FF, D_MODEL), BF16),
        ],
    )
    return pl.pallas_call(
        _moe_kernel,
        grid_spec=grid_spec,
        out_shape=jax.ShapeDtypeStruct((n_rows, D_MODEL), F32),
        compiler_params=pltpu.CompilerParams(dimension_semantics=("arbitrary",), vmem_limit_bytes=_vmem(52)),
        name="moe",
    )(block_e, n_used, xs, w_gate, w_up, w_down)


def _moe_layout(route, counts_f):
    counts = counts_f[0, :N_EXPERTS].astype(jnp.int32)
    padded = (counts + MOE_TM - 1) // MOE_TM * MOE_TM
    pends = jnp.cumsum(padded)
    pstarts = pends - padded
    n_tok = route.shape[0]
    n_blocks = n_tok * TOP_K // MOE_TM + N_EXPERTS
    blk_start = jnp.arange(n_blocks, dtype=jnp.int32) * MOE_TM
    block_e = jnp.minimum(jnp.sum((blk_start[:, None] >= pends[None, :]).astype(jnp.int32), axis=1), N_EXPERTS - 1)
    n_used = (pends[-1] // MOE_TM).astype(jnp.int32).reshape(1)
    e_id = route[:, 2:2 + TOP_K].astype(jnp.int32)
    rank = route[:, 2 + TOP_K:2 + 2 * TOP_K].astype(jnp.int32)
    onehot_e = e_id[:, :, None] == jnp.arange(N_EXPERTS, dtype=jnp.int32)[None, None, :]
    dest = jnp.sum(jnp.where(onehot_e, pstarts[None, None, :], 0), axis=2) + rank
    return block_e.astype(jnp.int32), n_used, dest, n_blocks * MOE_TM


def _final_kernel(dest_ref, destn_ref, x1_ref, route_ref, g_ref, y_hbm, o_ref, ybuf, sem):
    i = pl.program_id(0)
    n_steps = pl.num_programs(0)
    slot = lax.rem(i, 2)
    n_rows = TOP_K * FIN_TM

    def start_gather(idx_ref, s):
        def issue(r, c):
            pltpu.make_async_copy(y_hbm.at[pl.ds(idx_ref[0, 0, r], 1)], ybuf.at[s, pl.ds(r, 1)], sem.at[s]).start()
            return c
        lax.fori_loop(0, n_rows, issue, 0, unroll=DMA_UNROLL)

    @pl.when(i == 0)
    def _():
        start_gather(dest_ref, 0)

    @pl.when(i + 1 < n_steps)
    def _():
        start_gather(destn_ref, 1 - slot)

    pltpu.make_async_copy(y_hbm.at[pl.ds(0, n_rows)], ybuf.at[slot], sem.at[slot]).wait()
    route = route_ref[...]
    x = x1_ref[...]
    for k in range(TOP_K):
        x = x + route[:, k:k + 1] * ybuf[slot, k * FIN_TM:(k + 1) * FIN_TM, :]
    o_ref[...] = x * lax.rsqrt(jnp.mean(x * x, axis=-1, keepdims=True) + RMS_EPS) * g_ref[...]


def _final(dest, x1, route, g, y_sorted):
    n = x1.shape[0]
    steps = n // FIN_TM
    dest3 = dest.reshape(steps, FIN_TM, TOP_K).transpose(0, 2, 1).reshape(steps, 1, TOP_K * FIN_TM)
    return pl.pallas_call(
        _final_kernel,
        grid=(steps,),
        in_specs=[
            pl.BlockSpec((1, 1, TOP_K * FIN_TM), lambda i: (i, 0, 0), memory_space=pltpu.SMEM),
            pl.BlockSpec((1, 1, TOP_K * FIN_TM), lambda i: (jnp.minimum(i + 1, steps - 1), 0, 0),
                         memory_space=pltpu.SMEM),
            pl.BlockSpec((FIN_TM, D_MODEL), lambda i: (i, 0)),
            pl.BlockSpec((FIN_TM, ROUTE_LANES), lambda i: (i, 0)),
            pl.BlockSpec((1, D_MODEL), lambda i: (0, 0)),
            pl.BlockSpec(memory_space=pl.ANY),
        ],
        out_specs=pl.BlockSpec((FIN_TM, D_MODEL), lambda i: (i, 0)),
        out_shape=jax.ShapeDtypeStruct((n, D_MODEL), F32),
        scratch_shapes=[
            pltpu.VMEM((2, TOP_K * FIN_TM, D_MODEL), F32),
            pltpu.SemaphoreType.DMA((2,)),
        ],
        compiler_params=pltpu.CompilerParams(dimension_semantics=("arbitrary",), vmem_limit_bytes=_vmem(40)),
        name="final",
    )(dest3, dest3, x1, route, g, y_sorted)


def kernel(x, norm1_g, w_in, attn_sinks, rel_bias, hg_lb_logits, hg_norm_g, w_attn_branch, w_hg_branch, w_out,
           norm2_g, w_group_router, b_group_router, w_expert_router, b_expert_router, w_gate, w_up, w_down, final_g):
    bsz, s_len, d = x.shape
    n_tok = bsz * s_len
    x2d = x.reshape(n_tok, d)

    w_route = jnp.concatenate([w_group_router[0], w_expert_router[0]], axis=1).astype(F32)
    w_route = jnp.pad(w_route, ((0, 0), (0, ROUTE_LANES - w_route.shape[1])))
    whi = w_route.astype(BF16)
    wlo = (w_route - whi.astype(F32)).astype(BF16)
    b_route = jnp.concatenate([b_group_router[0], b_expert_router[0]]).astype(F32)
    b_route = jnp.pad(b_route, (0, ROUTE_LANES - b_route.shape[0])).reshape(1, ROUTE_LANES)
    bias = _attn_bias_table(rel_bias)

    proj = _in_proj(x2d, norm1_g[0].reshape(1, d).astype(F32), w_in)
    proj3 = proj.reshape(bsz, s_len, IN_WIDTH)
    a = _swa(proj3, attn_sinks[0].astype(F32), bias)
    r = _hgrn2(proj3, hg_lb_logits.astype(F32), hg_norm_g[0].reshape(1, HG_DIM).astype(F32))
    x1, h2p, route, counts = _mix(
        a.reshape(n_tok, ATTN_WIDTH), r.reshape(n_tok, HG_WIDTH), proj, x2d,
        w_attn_branch[0].astype(BF16), w_hg_branch[0].astype(BF16), w_out[0].astype(BF16),
        norm2_g[0].reshape(1, d).astype(F32), whi, wlo, b_route)
    block_e, n_used, dest, n_rows = _moe_layout(route, counts)
    xs = _dispatch(dest.reshape(n_tok * TOP_K), h2p, n_rows)
    y_sorted = _moe(block_e, n_used, xs, w_gate, w_up, w_down)
    out = _final(dest, x1, route, final_g.reshape(1, d).astype(F32), y_sorted)
    return out.reshape(bsz, s_len, d)
```

```python
import math

import numpy as np
import jax
import jax.numpy as jnp
from jax import lax
from jax.experimental import pallas as pl
from jax.experimental.pallas import tpu as pltpu

D_MODEL = 2048
N_Q_HEADS = 16
N_KV_HEADS = 4
HEAD_DIM = 64
WINDOW = 128
ATTN_BLOCK = 128
ATTN_WIDTH = N_Q_HEADS * HEAD_DIM
KV_WIDTH = N_KV_HEADS * HEAD_DIM
REL_BUCKETS = 32
REL_MAX_DIST = 128
HG_HEADS = 8
HG_DIM = 128
HG_WIDTH = HG_HEADS * HG_DIM
HG_CHUNK = 64
HG_SUB = 16
N_GROUPS = 4
EXPERTS_PER_GROUP = 8
N_EXPERTS = N_GROUPS * EXPERTS_PER_GROUP
TOP_K = 2
EXPERT_FF = 512
RMS_EPS = 1e-6

OFF_GATE_A = 0
OFF_GATE_H = D_MODEL
OFF_QA = 2 * D_MODEL
OFF_QH = OFF_QA + ATTN_WIDTH
OFF_FH = OFF_QH + HG_WIDTH
OFF_IH = OFF_FH + HG_WIDTH
OFF_GH = OFF_IH + HG_WIDTH
OFF_KA = OFF_GH + HG_WIDTH
OFF_VA = OFF_KA + KV_WIDTH
IN_WIDTH = OFF_VA + KV_WIDTH
REF_OFF_QA = 0
REF_OFF_KA = ATTN_WIDTH
REF_OFF_QH = ATTN_WIDTH + 2 * KV_WIDTH
REF_OFF_GATE_A = REF_OFF_QH + 4 * HG_WIDTH

HG_SAFE_DECAY = 60.0

IN_TM = 1024
IN_TN = 512
HG_TC = 256
MIX_TM = 256
ROUTE_LANES = 128
DISPATCH_ROWS = 1024
MOE_TM = 256
FIN_TM = 256

F32 = jnp.float32
BF16 = jnp.bfloat16
U32 = jnp.uint32
NEG_INF = float("-inf")


def _vmem(mib):
    return mib * 1024 * 1024


def _sigmoid(x):
    return 1.0 / (1.0 + jnp.exp(-x))


def _dot_nt(a, b):
    return lax.dot_general(a, b, (((1,), (1,)), ((), ())), preferred_element_type=F32)


def _dot_tn(a, b):
    return lax.dot_general(a, b, (((0,), (0,)), ((), ())), preferred_element_type=F32)


def _in_proj_kernel(x_ref, g_ref, w_ref, o_ref, h_ref):
    @pl.when(pl.program_id(1) == 0)
    def _():
        x = x_ref[...]
        ms = jnp.mean(x * x, axis=-1, keepdims=True)
        h_ref[...] = (x * lax.rsqrt(ms + RMS_EPS) * g_ref[...]).astype(BF16)

    o_ref[...] = jnp.dot(h_ref[...], w_ref[0].astype(BF16), preferred_element_type=F32).astype(o_ref.dtype)


def _in_proj(x2d, g, w_in):
    n = x2d.shape[0]
    n_col = IN_WIDTH // IN_TN

    def out_block(j):
        gates = j - REF_OFF_GATE_A // IN_TN + OFF_GATE_A // IN_TN
        q_a = j - REF_OFF_QA // IN_TN + OFF_QA // IN_TN
        kv = j - REF_OFF_KA // IN_TN + OFF_KA // IN_TN
        hg = j - REF_OFF_QH // IN_TN + OFF_QH // IN_TN
        return jnp.where(j >= REF_OFF_GATE_A // IN_TN, gates,
                         jnp.where(j >= REF_OFF_QH // IN_TN, hg, jnp.where(j >= REF_OFF_KA // IN_TN, kv, q_a)))

    return pl.pallas_call(
        _in_proj_kernel,
        grid=(n // IN_TM, n_col),
        in_specs=[
            pl.BlockSpec((IN_TM, D_MODEL), lambda i, j: (i, 0)),
            pl.BlockSpec((1, D_MODEL), lambda i, j: (0, 0)),
            pl.BlockSpec((1, D_MODEL, IN_TN), lambda i, j: (0, 0, j)),
        ],
        out_specs=pl.BlockSpec((IN_TM, IN_TN), lambda i, j: (i, out_block(j))),
        out_shape=jax.ShapeDtypeStruct((n, IN_WIDTH), BF16),
        scratch_shapes=[pltpu.VMEM((IN_TM, D_MODEL), BF16)],
        compiler_params=pltpu.CompilerParams(
            dimension_semantics=("parallel", "arbitrary"), vmem_limit_bytes=_vmem(48)),
        name="in_proj",
    )(x2d, g, w_in)


def _swa_kernel(sink_ref, q_ref, kp_ref, kc_ref, vp_ref, vc_ref, bias_ref, o_ref):
    has_prev = pl.program_id(1) > 0
    q = q_ref[0] * jnp.asarray(HEAD_DIM ** -0.5, BF16)
    kk = jnp.concatenate([kp_ref[0], kc_ref[0]], axis=0)
    vv = jnp.concatenate([vp_ref[0], vc_ref[0]], axis=0)
    row = lax.broadcasted_iota(jnp.int32, (ATTN_BLOCK, ATTN_BLOCK), 0)
    col = lax.broadcasted_iota(jnp.int32, (ATTN_BLOCK, ATTN_BLOCK), 1)
    cur = col <= row
    key_ok = jnp.logical_or(cur, has_prev)
    grp = N_Q_HEADS // N_KV_HEADS
    outs = []
    for hq in range(N_Q_HEADS):
        hk = hq // grp
        qh = q[:, hq * HEAD_DIM:(hq + 1) * HEAD_DIM]
        kh = kk[:, hk * HEAD_DIM:(hk + 1) * HEAD_DIM]
        vh = vv[:, hk * HEAD_DIM:(hk + 1) * HEAD_DIM]
        s2 = _dot_nt(qh, kh)
        s = jnp.where(cur, s2[:, ATTN_BLOCK:], s2[:, :ATTN_BLOCK]) + bias_ref[hq]
        s = jnp.where(key_ok, s, NEG_INF)
        sink = sink_ref[hq]
        m = jnp.maximum(jnp.max(s, axis=-1, keepdims=True), sink)
        p = jnp.exp(s - m)
        denom = jnp.sum(p, axis=-1, keepdims=True) + jnp.exp(sink - m)
        p2 = jnp.concatenate([jnp.where(cur, 0.0, p), jnp.where(cur, p, 0.0)], axis=1).astype(BF16)
        o = jnp.dot(p2, vh, preferred_element_type=F32)
        outs.append(o / denom)
    o_ref[0] = jnp.concatenate(outs, axis=-1).astype(o_ref.dtype)


def _swa(proj3, sinks, bias):
    bsz, s_len, _ = proj3.shape
    nb = s_len // ATTN_BLOCK
    kblk = OFF_KA // KV_WIDTH
    vblk = OFF_VA // KV_WIDTH
    prev = lambda n: jnp.maximum(n - 1, 0)
    return pl.pallas_call(
        _swa_kernel,
        grid=(bsz, nb),
        in_specs=[
            pl.BlockSpec(memory_space=pltpu.SMEM),
            pl.BlockSpec((1, ATTN_BLOCK, ATTN_WIDTH), lambda b, n: (b, n, OFF_QA // ATTN_WIDTH)),
            pl.BlockSpec((1, ATTN_BLOCK, KV_WIDTH), lambda b, n: (b, prev(n), kblk)),
            pl.BlockSpec((1, ATTN_BLOCK, KV_WIDTH), lambda b, n: (b, n, kblk)),
            pl.BlockSpec((1, ATTN_BLOCK, KV_WIDTH), lambda b, n: (b, prev(n), vblk)),
            pl.BlockSpec((1, ATTN_BLOCK, KV_WIDTH), lambda b, n: (b, n, vblk)),
            pl.BlockSpec((N_Q_HEADS, ATTN_BLOCK, ATTN_BLOCK), lambda b, n: (0, 0, 0)),
        ],
        out_specs=pl.BlockSpec((1, ATTN_BLOCK, ATTN_WIDTH), lambda b, n: (b, n, 0)),
        out_shape=jax.ShapeDtypeStruct((bsz, s_len, ATTN_WIDTH), BF16),
        compiler_params=pltpu.CompilerParams(
            dimension_semantics=("parallel", "arbitrary"), vmem_limit_bytes=_vmem(32)),
        name="swa",
    )(sinks, proj3, proj3, proj3, proj3, proj3, bias)


def _t5_causal_bucket(n):
    max_exact = REL_BUCKETS // 2
    nf = jnp.maximum(n, 1).astype(F32)
    large = max_exact + (jnp.log(nf / max_exact) / math.log(REL_MAX_DIST / max_exact)
                         * (REL_BUCKETS - max_exact)).astype(jnp.int32)
    large = jnp.minimum(large, REL_BUCKETS - 1)
    return jnp.where(n < max_exact, n, large)


def _attn_bias_table(rel_bias):
    assert WINDOW == ATTN_BLOCK
    r = jnp.arange(ATTN_BLOCK)[:, None]
    c = jnp.arange(ATTN_BLOCK)[None, :]
    dist = jnp.where(c <= r, r - c, r + ATTN_BLOCK - c)
    bucket = _t5_causal_bucket(dist)
    onehot = (bucket[None, :, :] == jnp.arange(REL_BUCKETS)[:, None, None]).astype(F32)
    return jnp.einsum("bh,bqk->hqk", rel_bias.astype(F32), onehot, precision=lax.Precision.HIGHEST)


def _cumsum_rows(tri_bf16, g):
    width = g.shape[1]
    g1 = g.astype(BF16)
    r1 = g - g1.astype(F32)
    g2 = r1.astype(BF16)
    g3 = (r1 - g2.astype(F32)).astype(BF16)
    acc = jnp.dot(tri_bf16, jnp.concatenate([g3, g2, g1], axis=1), preferred_element_type=F32)
    return (acc[:, :width] + acc[:, width:2 * width]) + acc[:, 2 * width:]


def _hgrn_head_safe(q, kf, v, b, st):
    rows = lax.broadcasted_iota(jnp.int32, (HG_SUB, 1), 0)
    outs = []
    prev_end = jnp.zeros((1, HG_DIM), F32)
    for i in range(HG_CHUNK // HG_SUB):
        sl = slice(i * HG_SUB, (i + 1) * HG_SUB)
        bl = b[sl] - prev_end
        qi, ki, vi = q[sl], kf[sl], v[sl]
        vif = vi.astype(F32)
        blast = bl[HG_SUB - 1:HG_SUB]
        o = _dot_nt((qi * jnp.exp(bl)).astype(BF16), st.astype(BF16))
        for s in range(HG_SUB):
            d = jnp.exp(jnp.minimum(bl - bl[s:s + 1], 0.0))
            a = jnp.sum(qi * ki[s:s + 1] * d, axis=-1, keepdims=True)
            o = o + jnp.where(rows >= s, a, 0.0) * vif[s:s + 1]
        kd = (ki * jnp.exp(blast - bl)).astype(BF16)
        st = st * jnp.exp(blast) + _dot_tn(vi, kd)
        prev_end = b[(i + 1) * HG_SUB - 1:(i + 1) * HG_SUB]
        outs.append(o)
    return jnp.concatenate(outs, axis=0), st


def _hgrn_kernel(q_ref, f_ref, i_ref, g_ref, lbl_ref, gn_ref, o_ref, st_ref, kf_ref, b_ref):
    @pl.when(pl.program_id(1) == 0)
    def _():
        st_ref[...] = jnp.zeros_like(st_ref)

    lg2 = lbl_ref[...].astype(F32)
    e = jnp.exp(lg2 - jnp.max(lg2, axis=0, keepdims=True))
    lb = e[0:1] / jnp.sum(e, axis=0, keepdims=True)
    gn = gn_ref[...].astype(F32)
    r = lax.broadcasted_iota(jnp.int32, (HG_CHUNK, HG_CHUNK), 0)
    c = lax.broadcasted_iota(jnp.int32, (HG_CHUNK, HG_CHUNK), 1)
    tri_mask = r >= c
    tri = jnp.where(tri_mask, 1.0, 0.0).astype(BF16)

    n_chunks = HG_TC // HG_CHUNK

    f = lb + (1.0 - lb) * _sigmoid(f_ref[0].astype(F32))
    kf_ref[...] = 1.0 - f
    logf = jnp.log(f)
    worst = None
    for ci in range(n_chunks):
        sl = slice(ci * HG_CHUNK, (ci + 1) * HG_CHUNK)
        b = _cumsum_rows(tri, logf[sl])
        b_ref[sl, :] = b
        bend = b[HG_CHUNK - 1:HG_CHUNK]
        worst = bend if worst is None else jnp.minimum(worst, bend)

    def finish(h, sl, o):
        hs = slice(h * HG_DIM, (h + 1) * HG_DIM)
        o = o * lax.rsqrt(jnp.mean(o * o, axis=-1, keepdims=True) + RMS_EPS) * gn
        go = g_ref[0, sl, hs].astype(F32)
        o_ref[0, sl, hs] = (o * (go * _sigmoid(go))).astype(o_ref.dtype)

    def fast():
        states = [st_ref[h] for h in range(HG_HEADS)]
        for ci in range(n_chunks):
            sl = slice(ci * HG_CHUNK, (ci + 1) * HG_CHUNK)
            b = b_ref[sl, :]
            qp = (q_ref[0, sl, :].astype(F32) * jnp.exp(b)).astype(BF16)
            kp = kf_ref[sl, :] * jnp.exp(-b)
            kpb = kp.astype(BF16)
            eb = jnp.exp(b[HG_CHUNK - 1:HG_CHUNK])
            kd = (kp * eb).astype(BF16)
            for h in range(HG_HEADS):
                hs = slice(h * HG_DIM, (h + 1) * HG_DIM)
                st = states[h]
                v = i_ref[0, sl, hs]
                att = jnp.where(tri_mask, _dot_nt(qp[:, hs], kpb[:, hs]), 0.0)
                o = _dot_nt(qp[:, hs], st.astype(BF16)) + jnp.dot(att.astype(BF16), v, preferred_element_type=F32)
                states[h] = st * eb[:, hs] + _dot_tn(v, kd[:, hs])
                finish(h, sl, o)
        for h in range(HG_HEADS):
            st_ref[h] = states[h]

    def safe():
        def chunk(ci, carry):
            sl = pl.ds(pl.multiple_of(ci * HG_CHUNK, HG_CHUNK), HG_CHUNK)
            for h in range(HG_HEADS):
                hs = slice(h * HG_DIM, (h + 1) * HG_DIM)
                o, st_new = _hgrn_head_safe(q_ref[0, sl, hs].astype(F32), kf_ref[sl, hs], i_ref[0, sl, hs],
                                            b_ref[sl, hs], st_ref[h])
                st_ref[h] = st_new
                finish(h, sl, o)
            return carry

        lax.fori_loop(0, n_chunks, chunk, 0)

    lax.cond(jnp.min(worst) >= -HG_SAFE_DECAY, fast, safe)


def _hgrn2(proj3, lb_logits, gn):
    bsz, s_len, _ = proj3.shape
    blk = lambda off: (lambda b, t: (b, t, off // HG_WIDTH))
    return pl.pallas_call(
        _hgrn_kernel,
        grid=(bsz, s_len // HG_TC),
        in_specs=[
            pl.BlockSpec((1, HG_TC, HG_WIDTH), blk(OFF_QH)),
            pl.BlockSpec((1, HG_TC, HG_WIDTH), blk(OFF_FH)),
            pl.BlockSpec((1, HG_TC, HG_WIDTH), blk(OFF_IH)),
            pl.BlockSpec((1, HG_TC, HG_WIDTH), blk(OFF_GH)),
            pl.BlockSpec((lb_logits.shape[0], HG_WIDTH), lambda b, t: (0, 0)),
            pl.BlockSpec((1, HG_DIM), lambda b, t: (0, 0)),
        ],
        out_specs=pl.BlockSpec((1, HG_TC, HG_WIDTH), lambda b, t: (b, t, 0)),
        out_shape=jax.ShapeDtypeStruct((bsz, s_len, HG_WIDTH), BF16),
        scratch_shapes=[
            pltpu.VMEM((HG_HEADS, HG_DIM, HG_DIM), F32),
            pltpu.VMEM((HG_TC, HG_WIDTH), F32),
            pltpu.VMEM((HG_TC, HG_WIDTH), F32),
        ],
        compiler_params=pltpu.CompilerParams(
            dimension_semantics=("parallel", "arbitrary"), vmem_limit_bytes=_vmem(32)),
        name="hgrn2",
    )(proj3, proj3, proj3, proj3, lb_logits, gn)


def _mix_kernel(a_ref, r_ref, ga_ref, gh_ref, x_ref, wa_ref, wr_ref, wo_ref, g2_ref, whi_ref, wlo_ref, br_ref,
                x1_ref, h2p_ref, route_ref, cnt_out_ref, cnt_ref):
    @pl.when(pl.program_id(0) == 0)
    def _():
        cnt_ref[...] = jnp.zeros_like(cnt_ref)

    am = jnp.dot(a_ref[...], wa_ref[...], preferred_element_type=F32)
    rm = jnp.dot(r_ref[...], wr_ref[...], preferred_element_type=F32)
    mixed = _sigmoid(ga_ref[...].astype(F32)) * am + _sigmoid(gh_ref[...].astype(F32)) * rm
    x1 = x_ref[...] + jnp.dot(mixed.astype(BF16), wo_ref[...], preferred_element_type=F32)
    x1_ref[...] = x1
    h2 = x1 * lax.rsqrt(jnp.mean(x1 * x1, axis=-1, keepdims=True) + RMS_EPS) * g2_ref[...]

    hi = h2.astype(BF16)
    bits = pltpu.bitcast(hi.astype(F32), U32)
    half = D_MODEL // 2
    h2p_ref[...] = jnp.bitwise_or(jnp.right_shift(bits[:, :half], jnp.uint32(16)), bits[:, half:])

    lo = (h2 - hi.astype(F32)).astype(BF16)
    logits = jnp.dot(lo, whi_ref[...], preferred_element_type=F32)
    logits = logits + jnp.dot(hi, wlo_ref[...], preferred_element_type=F32)
    logits = logits + jnp.dot(hi, whi_ref[...], preferred_element_type=F32) + br_ref[...]

    lane = lax.broadcasted_iota(jnp.int32, logits.shape, 1)
    lane_f = lane.astype(F32)
    big = float(ROUTE_LANES)
    gl = jnp.where(lane < N_GROUPS, logits, NEG_INF)
    gmax = jnp.max(gl, axis=-1, keepdims=True)
    g_idx = jnp.min(jnp.where(gl == gmax, lane_f, big), axis=-1, keepdims=True)
    g_w = 1.0 / jnp.sum(jnp.exp(gl - gmax), axis=-1, keepdims=True)
    e_lane = lane - N_GROUPS
    in_group = jnp.logical_and(e_lane >= 0, e_lane < N_EXPERTS)
    lane_group = jnp.right_shift(e_lane, 3).astype(F32)
    in_group = jnp.logical_and(in_group, lane_group == g_idx)
    el = jnp.where(in_group, logits, NEG_INF)
    m1 = jnp.max(el, axis=-1, keepdims=True)
    i1 = jnp.min(jnp.where(el == m1, lane_f, big), axis=-1, keepdims=True)
    el2 = jnp.where(lane_f == i1, NEG_INF, el)
    m2 = jnp.max(el2, axis=-1, keepdims=True)
    i2 = jnp.min(jnp.where(el2 == m2, lane_f, big), axis=-1, keepdims=True)
    p2 = jnp.exp(m2 - m1)
    w1 = g_w / (1.0 + p2)
    w2 = g_w * p2 / (1.0 + p2)
    e1 = i1 - N_GROUPS
    e2 = i2 - N_GROUPS

    tm = logits.shape[0]
    rr = lax.broadcasted_iota(jnp.int32, (tm, tm), 0)
    cc = lax.broadcasted_iota(jnp.int32, (tm, tm), 1)
    ltri = jnp.where(rr > cc, 1.0, 0.0).astype(BF16)
    sel1 = lane_f == e1
    sel2 = lane_f == e2
    oh1 = jnp.where(sel1, 1.0, 0.0)
    oh2 = jnp.where(sel2, 1.0, 0.0)
    base = cnt_ref[...]
    c1 = jnp.sum(oh1, axis=0, keepdims=True)
    c2 = jnp.sum(oh2, axis=0, keepdims=True)
    pre1 = jnp.dot(ltri, oh1.astype(BF16), preferred_element_type=F32) + base
    pre2 = jnp.dot(ltri, oh2.astype(BF16), preferred_element_type=F32) + (base + c1)
    rank1 = jnp.sum(jnp.where(sel1, pre1, 0.0), axis=-1, keepdims=True)
    rank2 = jnp.sum(jnp.where(sel2, pre2, 0.0), axis=-1, keepdims=True)
    total = base + c1 + c2
    cnt_ref[...] = total
    cnt_out_ref[...] = total

    route = jnp.zeros_like(logits)
    for k, val in enumerate((w1, w2, e1, e2, rank1, rank2)):
        route = jnp.where(lane == k, val, route)
    route_ref[...] = route


def _mix(a2d, r2d, proj, x2d, wa, wr, wo, g2, whi, wlo, br):
    n = x2d.shape[0]
    row = lambda i: (i, 0)
    const = lambda i: (0, 0)
    return pl.pallas_call(
        _mix_kernel,
        grid=(n // MIX_TM,),
        in_specs=[
            pl.BlockSpec((MIX_TM, ATTN_WIDTH), row),
            pl.BlockSpec((MIX_TM, HG_WIDTH), row),
            pl.BlockSpec((MIX_TM, D_MODEL), lambda i: (i, OFF_GATE_A // D_MODEL)),
            pl.BlockSpec((MIX_TM, D_MODEL), lambda i: (i, OFF_GATE_H // D_MODEL)),
            pl.BlockSpec((MIX_TM, D_MODEL), row),
            pl.BlockSpec((ATTN_WIDTH, D_MODEL), const),
            pl.BlockSpec((HG_WIDTH, D_MODEL), const),
            pl.BlockSpec((D_MODEL, D_MODEL), const),
            pl.BlockSpec((1, D_MODEL), const),
            pl.BlockSpec((D_MODEL, ROUTE_LANES), const),
            pl.BlockSpec((D_MODEL, ROUTE_LANES), const),
            pl.BlockSpec((1, ROUTE_LANES), const),
        ],
        out_specs=[
            pl.BlockSpec((MIX_TM, D_MODEL), row),
            pl.BlockSpec((MIX_TM, D_MODEL // 2), row),
            pl.BlockSpec((MIX_TM, ROUTE_LANES), row),
            pl.BlockSpec((1, ROUTE_LANES), const),
        ],
        out_shape=[
            jax.ShapeDtypeStruct((n, D_MODEL), F32),
            jax.ShapeDtypeStruct((n, D_MODEL // 2), U32),
            jax.ShapeDtypeStruct((n, ROUTE_LANES), F32),
            jax.ShapeDtypeStruct((1, ROUTE_LANES), F32),
        ],
        scratch_shapes=[pltpu.VMEM((1, ROUTE_LANES), F32)],
        compiler_params=pltpu.CompilerParams(dimension_semantics=("arbitrary",), vmem_limit_bytes=_vmem(56)),
        name="mix",
    )(a2d, r2d, proj, proj, x2d, wa, wr, wo, g2, whi, wlo, br)


def _dispatch_kernel(dest_ref, src_ref, zeros_hbm, dst_hbm, sem):
    del zeros_hbm

    for a in range(DISPATCH_ROWS):
        pltpu.make_async_copy(src_ref.at[pl.ds(a // TOP_K, 1)], dst_hbm.at[pl.ds(dest_ref[0, 0, a], 1)],
                              sem.at[0]).start(priority=a % 2)
    for _ in range(TOP_K):
        pltpu.make_async_copy(src_ref, dst_hbm.at[pl.ds(0, DISPATCH_ROWS // TOP_K)], sem.at[0]).wait()


def _dispatch(dest_flat, h2p, n_rows):
    n_asg = dest_flat.shape[0]
    steps = n_asg // DISPATCH_ROWS
    return pl.pallas_call(
        _dispatch_kernel,
        grid=(steps,),
        in_specs=[
            pl.BlockSpec((1, 1, DISPATCH_ROWS), lambda s: (s, 0, 0), memory_space=pltpu.SMEM),
            pl.BlockSpec((DISPATCH_ROWS // TOP_K, D_MODEL // 2), lambda s: (s, 0)),
            pl.BlockSpec(memory_space=pl.ANY),
        ],
        out_specs=pl.BlockSpec(memory_space=pl.ANY),
        out_shape=jax.ShapeDtypeStruct((n_rows, D_MODEL // 2), U32),
        input_output_aliases={2: 0},
        scratch_shapes=[pltpu.SemaphoreType.DMA((1,))],
        compiler_params=pltpu.CompilerParams(dimension_semantics=("arbitrary",)),
        name="dispatch",
    )(dest_flat.reshape(steps, 1, DISPATCH_ROWS), h2p, jnp.zeros((n_rows, D_MODEL // 2), U32))


def _moe_kernel(be_ref, nused_ref, xs_ref, wg_ref, wu_ref, wd_ref, y_ref, wg_s, wu_s, wd_s):
    b = pl.program_id(0)
    live = b < nused_ref[0]
    new_expert = jnp.logical_or(b == 0, be_ref[b] != be_ref[jnp.maximum(b - 1, 0)])

    @pl.when(jnp.logical_and(live, new_expert))
    def _():
        wg_s[...] = wg_ref[0, 0].astype(BF16)
        wu_s[...] = wu_ref[0, 0].astype(BF16)
        wd_s[...] = wd_ref[0, 0].astype(BF16)

    @pl.when(jnp.logical_not(live))
    def _():
        y_ref[...] = jnp.zeros_like(y_ref)

    @pl.when(live)
    def _():
        x = xs_ref[...]
        xlo = pltpu.bitcast(jnp.left_shift(x, jnp.uint32(16)), F32).astype(BF16)
        xhi = pltpu.bitcast(jnp.bitwise_and(x, jnp.uint32(0xFFFF0000)), F32).astype(BF16)
        xb = jnp.concatenate([xlo, xhi], axis=1)
        hg = jnp.dot(xb, wg_s[...], preferred_element_type=F32)
        hu = jnp.dot(xb, wu_s[...], preferred_element_type=F32)
        hb = (hg * _sigmoid(hg) * hu).astype(BF16)
        y_ref[...] = jnp.dot(hb, wd_s[...], preferred_element_type=F32)


def _moe(block_e, n_used, xs, w_gate, w_up, w_down):
    n_rows = xs.shape[0]
    n_blocks = n_rows // MOE_TM
    wspec = lambda shp: pl.BlockSpec((1, 1) + shp, lambda b, be, nu: (0, be[b], 0, 0))
    grid_spec = pltpu.PrefetchScalarGridSpec(
        num_scalar_prefetch=2,
        grid=(n_blocks,),
        in_specs=[
            pl.BlockSpec((MOE_TM, D_MODEL // 2), lambda b, be, nu: (jnp.minimum(b, nu[0] - 1), 0)),
            wspec((D_MODEL, EXPERT_FF)),
            wspec((D_MODEL, EXPERT_FF)),
            wspec((EXPERT_FF, D_MODEL)),
        ],
        out_specs=pl.BlockSpec((MOE_TM, D_MODEL), lambda b, be, nu: (b, 0)),
        scratch_shapes=[
            pltpu.VMEM((D_MODEL, EXPERT_FF), BF16),
            pltpu.VMEM((D_MODEL, EXPERT_FF), BF16),
            pltpu.VMEM((EXPERT_FF, D_MODEL), BF16),
        ],
    )
    return pl.pallas_call(
        _moe_kernel,
        grid_spec=grid_spec,
        out_shape=jax.ShapeDtypeStruct((n_rows, D_MODEL), F32),
        compiler_params=pltpu.CompilerParams(dimension_semantics=("arbitrary",), vmem_limit_bytes=_vmem(52)),
        name="moe",
    )(block_e, n_used, xs, w_gate, w_up, w_down)


def _moe_layout(route, counts_f):
    counts = counts_f[0, :N_EXPERTS].astype(jnp.int32)
    padded = (counts + MOE_TM - 1) // MOE_TM * MOE_TM
    pends = jnp.cumsum(padded)
    pstarts = pends - padded
    n_tok = route.shape[0]
    n_blocks = n_tok * TOP_K // MOE_TM + N_EXPERTS
    blk_start = jnp.arange(n_blocks, dtype=jnp.int32) * MOE_TM
    block_e = jnp.minimum(jnp.sum((blk_start[:, None] >= pends[None, :]).astype(jnp.int32), axis=1), N_EXPERTS - 1)
    n_used = (pends[-1] // MOE_TM).astype(jnp.int32).reshape(1)
    e_id = route[:, 2:2 + TOP_K].astype(jnp.int32)
    rank = route[:, 2 + TOP_K:2 + 2 * TOP_K].astype(jnp.int32)
    onehot_e = e_id[:, :, None] == jnp.arange(N_EXPERTS, dtype=jnp.int32)[None, None, :]
    dest = jnp.sum(jnp.where(onehot_e, pstarts[None, None, :], 0), axis=2) + rank
    return block_e.astype(jnp.int32), n_used, dest, n_blocks * MOE_TM


def _final_kernel(dest_ref, destn_ref, x1_ref, route_ref, g_ref, y_hbm, o_ref, ybuf, sem):
    i = pl.program_id(0)
    n_steps = pl.num_programs(0)
    slot = lax.rem(i, 2)
    n_rows = TOP_K * FIN_TM

    def start_gather(idx_ref, s):
        for r in range(n_rows):
            pltpu.make_async_copy(y_hbm.at[pl.ds(idx_ref[0, 0, r], 1)], ybuf.at[s, pl.ds(r, 1)],
                                  sem.at[s]).start(priority=r % 2)

    @pl.when(i == 0)
    def _():
        start_gather(dest_ref, 0)

    for s in range(2):
        @pl.when(jnp.logical_and(i + 1 < n_steps, 1 - slot == s))
        def _():
            start_gather(destn_ref, s)

    pltpu.make_async_copy(y_hbm.at[pl.ds(0, n_rows)], ybuf.at[slot], sem.at[slot]).wait()
    route = route_ref[...]
    x = x1_ref[...]
    for k in range(TOP_K):
        x = x + route[:, k:k + 1] * ybuf[slot, k * FIN_TM:(k + 1) * FIN_TM, :]
    o_ref[...] = x * lax.rsqrt(jnp.mean(x * x, axis=-1, keepdims=True) + RMS_EPS) * g_ref[...]


def _final(dest, x1, route, g, y_sorted):
    n = x1.shape[0]
    steps = n // FIN_TM
    dest3 = dest.reshape(steps, FIN_TM, TOP_K).transpose(0, 2, 1).reshape(steps, 1, TOP_K * FIN_TM)
    return pl.pallas_call(
        _final_kernel,
        grid=(steps,),
        in_specs=[
            pl.BlockSpec((1, 1, TOP_K * FIN_TM), lambda i: (i, 0, 0), memory_space=pltpu.SMEM),
            pl.BlockSpec((1, 1, TOP_K * FIN_TM), lambda i: (jnp.minimum(i + 1, steps - 1), 0, 0),
                         memory_space=pltpu.SMEM),
            pl.BlockSpec((FIN_TM, D_MODEL), lambda i: (i, 0)),
            pl.BlockSpec((FIN_TM, ROUTE_LANES), lambda i: (i, 0)),
            pl.BlockSpec((1, D_MODEL), lambda i: (0, 0)),
            pl.BlockSpec(memory_space=pl.ANY),
        ],
        out_specs=pl.BlockSpec((FIN_TM, D_MODEL), lambda i: (i, 0)),
        out_shape=jax.ShapeDtypeStruct((n, D_MODEL), F32),
        scratch_shapes=[
            pltpu.VMEM((2, TOP_K * FIN_TM, D_MODEL), F32),
            pltpu.SemaphoreType.DMA((2,)),
        ],
        compiler_params=pltpu.CompilerParams(dimension_semantics=("arbitrary",), vmem_limit_bytes=_vmem(40)),
        name="final",
    )(dest3, dest3, x1, route, g, y_sorted)


def kernel(x, norm1_g, w_in, attn_sinks, rel_bias, hg_lb_logits, hg_norm_g, w_attn_branch, w_hg_branch, w_out,
           norm2_g, w_group_router, b_group_router, w_expert_router, b_expert_router, w_gate, w_up, w_down, final_g):
    bsz, s_len, d = x.shape
    n_tok = bsz * s_len
    x2d = x.reshape(n_tok, d)

    w_route = jnp.concatenate([w_group_router[0], w_expert_router[0]], axis=1).astype(F32)
    w_route = jnp.pad(w_route, ((0, 0), (0, ROUTE_LANES - w_route.shape[1])))
    whi = w_route.astype(BF16)
    wlo = (w_route - whi.astype(F32)).astype(BF16)
    b_route = jnp.concatenate([b_group_router[0], b_expert_router[0]]).astype(F32)
    b_route = jnp.pad(b_route, (0, ROUTE_LANES - b_route.shape[0])).reshape(1, ROUTE_LANES)
    bias = _attn_bias_table(rel_bias)

    proj = _in_proj(x2d, norm1_g[0].reshape(1, d).astype(F32), w_in)
    proj3 = proj.reshape(bsz, s_len, IN_WIDTH)
    a = _swa(proj3, attn_sinks[0].astype(F32), bias)
    r = _hgrn2(proj3, hg_lb_logits.astype(F32), hg_norm_g[0].reshape(1, HG_DIM).astype(F32))
    x1, h2p, route, counts = _mix(
        a.reshape(n_tok, ATTN_WIDTH), r.reshape(n_tok, HG_WIDTH), proj, x2d,
        w_attn_branch[0].astype(BF16), w_hg_branch[0].astype(BF16), w_out[0].astype(BF16),
        norm2_g[0].reshape(1, d).astype(F32), whi, wlo, b_route)
    block_e, n_used, dest, n_rows = _moe_layout(route, counts)
    xs = _dispatch(dest.reshape(n_tok * TOP_K), h2p, n_rows)
    y_sorted = _moe(block_e, n_used, xs, w_gate, w_up, w_down)
    out = _final(dest, x1, route, final_g.reshape(1, d).astype(F32), y_sorted)
    return out.reshape(bsz, s_len, d)
```

```python
import math

import numpy as np
import jax
import jax.numpy as jnp
from jax import lax
from jax.experimental import pallas as pl
from jax.experimental.pallas import tpu as pltpu

D_MODEL = 2048
N_Q_HEADS = 16
N_KV_HEADS = 4
HEAD_DIM = 64
WINDOW = 128
ATTN_BLOCK = 128
ATTN_WIDTH = N_Q_HEADS * HEAD_DIM
KV_WIDTH = N_KV_HEADS * HEAD_DIM
REL_BUCKETS = 32
REL_MAX_DIST = 128
HG_HEADS = 8
HG_DIM = 128
HG_WIDTH = HG_HEADS * HG_DIM
HG_CHUNK = 64
HG_SUB = 16
N_GROUPS = 4
EXPERTS_PER_GROUP = 8
N_EXPERTS = N_GROUPS * EXPERTS_PER_GROUP
TOP_K = 2
EXPERT_FF = 512
RMS_EPS = 1e-6

OFF_GATE_A = 0
OFF_GATE_H = D_MODEL
OFF_QA = 2 * D_MODEL
OFF_QH = OFF_QA + ATTN_WIDTH
OFF_FH = OFF_QH + HG_WIDTH
OFF_IH = OFF_FH + HG_WIDTH
OFF_GH = OFF_IH + HG_WIDTH
OFF_KA = OFF_GH + HG_WIDTH
OFF_VA = OFF_KA + KV_WIDTH
IN_WIDTH = OFF_VA + KV_WIDTH
REF_OFF_QA = 0
REF_OFF_KA = ATTN_WIDTH
REF_OFF_QH = ATTN_WIDTH + 2 * KV_WIDTH
REF_OFF_GATE_A = REF_OFF_QH + 4 * HG_WIDTH

HG_SAFE_DECAY = 60.0

ROW_TILE = 8
LANES = 128
assert D_MODEL // 2 == ROW_TILE * LANES

IN_TM = 1024
IN_TN = 512
HG_TC = 256
MIX_TM = 256
ROUTE_LANES = 128
DISPATCH_ROWS = 1024
MOE_TM = 256
FIN_TM = 256

F32 = jnp.float32
BF16 = jnp.bfloat16
U32 = jnp.uint32
NEG_INF = float("-inf")


def _vmem(mib):
    return mib * 1024 * 1024


def _sigmoid(x):
    return 1.0 / (1.0 + jnp.exp(-x))


def _dot_nt(a, b):
    return lax.dot_general(a, b, (((1,), (1,)), ((), ())), preferred_element_type=F32)


def _dot_tn(a, b):
    return lax.dot_general(a, b, (((0,), (0,)), ((), ())), preferred_element_type=F32)


def _in_proj_kernel(x_ref, g_ref, w_ref, o_ref, h_ref):
    @pl.when(pl.program_id(1) == 0)
    def _():
        x = x_ref[...]
        ms = jnp.mean(x * x, axis=-1, keepdims=True)
        h_ref[...] = (x * lax.rsqrt(ms + RMS_EPS) * g_ref[...]).astype(BF16)

    o_ref[...] = jnp.dot(h_ref[...], w_ref[0].astype(BF16), preferred_element_type=F32).astype(o_ref.dtype)


def _in_proj(x2d, g, w_in):
    n = x2d.shape[0]
    n_col = IN_WIDTH // IN_TN

    def out_block(j):
        gates = j - REF_OFF_GATE_A // IN_TN + OFF_GATE_A // IN_TN
        q_a = j - REF_OFF_QA // IN_TN + OFF_QA // IN_TN
        kv = j - REF_OFF_KA // IN_TN + OFF_KA // IN_TN
        hg = j - REF_OFF_QH // IN_TN + OFF_QH // IN_TN
        return jnp.where(j >= REF_OFF_GATE_A // IN_TN, gates,
                         jnp.where(j >= REF_OFF_QH // IN_TN, hg, jnp.where(j >= REF_OFF_KA // IN_TN, kv, q_a)))

    return pl.pallas_call(
        _in_proj_kernel,
        grid=(n // IN_TM, n_col),
        in_specs=[
            pl.BlockSpec((IN_TM, D_MODEL), lambda i, j: (i, 0)),
            pl.BlockSpec((1, D_MODEL), lambda i, j: (0, 0)),
            pl.BlockSpec((1, D_MODEL, IN_TN), lambda i, j: (0, 0, j)),
        ],
        out_specs=pl.BlockSpec((IN_TM, IN_TN), lambda i, j: (i, out_block(j))),
        out_shape=jax.ShapeDtypeStruct((n, IN_WIDTH), BF16),
        scratch_shapes=[pltpu.VMEM((IN_TM, D_MODEL), BF16)],
        compiler_params=pltpu.CompilerParams(
            dimension_semantics=("parallel", "arbitrary"), vmem_limit_bytes=_vmem(48)),
        name="in_proj",
    )(x2d, g, w_in)


def _swa_kernel(sink_ref, q_ref, kp_ref, kc_ref, vp_ref, vc_ref, bias_ref, o_ref):
    has_prev = pl.program_id(1) > 0
    q = q_ref[0] * jnp.asarray(HEAD_DIM ** -0.5, BF16)
    kk = jnp.concatenate([kp_ref[0], kc_ref[0]], axis=0)
    vv = jnp.concatenate([vp_ref[0], vc_ref[0]], axis=0)
    row = lax.broadcasted_iota(jnp.int32, (ATTN_BLOCK, ATTN_BLOCK), 0)
    col = lax.broadcasted_iota(jnp.int32, (ATTN_BLOCK, ATTN_BLOCK), 1)
    cur = col <= row
    key_ok = jnp.logical_or(cur, has_prev)
    grp = N_Q_HEADS // N_KV_HEADS
    outs = []
    for hq in range(N_Q_HEADS):
        hk = hq // grp
        qh = q[:, hq * HEAD_DIM:(hq + 1) * HEAD_DIM]
        kh = kk[:, hk * HEAD_DIM:(hk + 1) * HEAD_DIM]
        vh = vv[:, hk * HEAD_DIM:(hk + 1) * HEAD_DIM]
        s2 = _dot_nt(qh, kh)
        s = jnp.where(cur, s2[:, ATTN_BLOCK:], s2[:, :ATTN_BLOCK]) + bias_ref[hq]
        s = jnp.where(key_ok, s, NEG_INF)
        sink = sink_ref[hq]
        m = jnp.maximum(jnp.max(s, axis=-1, keepdims=True), sink)
        p = jnp.exp(s - m)
        denom = jnp.sum(p, axis=-1, keepdims=True) + jnp.exp(sink - m)
        p2 = jnp.concatenate([jnp.where(cur, 0.0, p), jnp.where(cur, p, 0.0)], axis=1).astype(BF16)
        o = jnp.dot(p2, vh, preferred_element_type=F32)
        outs.append(o / denom)
    o_ref[0] = jnp.concatenate(outs, axis=-1).astype(o_ref.dtype)


def _swa(proj3, sinks, bias):
    bsz, s_len, _ = proj3.shape
    nb = s_len // ATTN_BLOCK
    kblk = OFF_KA // KV_WIDTH
    vblk = OFF_VA // KV_WIDTH
    prev = lambda n: jnp.maximum(n - 1, 0)
    return pl.pallas_call(
        _swa_kernel,
        grid=(bsz, nb),
        in_specs=[
            pl.BlockSpec(memory_space=pltpu.SMEM),
            pl.BlockSpec((1, ATTN_BLOCK, ATTN_WIDTH), lambda b, n: (b, n, OFF_QA // ATTN_WIDTH)),
            pl.BlockSpec((1, ATTN_BLOCK, KV_WIDTH), lambda b, n: (b, prev(n), kblk)),
            pl.BlockSpec((1, ATTN_BLOCK, KV_WIDTH), lambda b, n: (b, n, kblk)),
            pl.BlockSpec((1, ATTN_BLOCK, KV_WIDTH), lambda b, n: (b, prev(n), vblk)),
            pl.BlockSpec((1, ATTN_BLOCK, KV_WIDTH), lambda b, n: (b, n, vblk)),
            pl.BlockSpec((N_Q_HEADS, ATTN_BLOCK, ATTN_BLOCK), lambda b, n: (0, 0, 0)),
        ],
        out_specs=pl.BlockSpec((1, ATTN_BLOCK, ATTN_WIDTH), lambda b, n: (b, n, 0)),
        out_shape=jax.ShapeDtypeStruct((bsz, s_len, ATTN_WIDTH), BF16),
        compiler_params=pltpu.CompilerParams(
            dimension_semantics=("parallel", "arbitrary"), vmem_limit_bytes=_vmem(32)),
        name="swa",
    )(sinks, proj3, proj3, proj3, proj3, proj3, bias)


def _t5_causal_bucket(n):
    max_exact = REL_BUCKETS // 2
    nf = jnp.maximum(n, 1).astype(F32)
    large = max_exact + (jnp.log(nf / max_exact) / math.log(REL_MAX_DIST / max_exact)
                         * (REL_BUCKETS - max_exact)).astype(jnp.int32)
    large = jnp.minimum(large, REL_BUCKETS - 1)
    return jnp.where(n < max_exact, n, large)


def _attn_bias_table(rel_bias):
    assert WINDOW == ATTN_BLOCK
    r = jnp.arange(ATTN_BLOCK)[:, None]
    c = jnp.arange(ATTN_BLOCK)[None, :]
    dist = jnp.where(c <= r, r - c, r + ATTN_BLOCK - c)
    bucket = _t5_causal_bucket(dist)
    onehot = (bucket[None, :, :] == jnp.arange(REL_BUCKETS)[:, None, None]).astype(F32)
    return jnp.einsum("bh,bqk->hqk", rel_bias.astype(F32), onehot, precision=lax.Precision.HIGHEST)


def _cumsum_rows(tri_bf16, g):
    width = g.shape[1]
    g1 = g.astype(BF16)
    r1 = g - g1.astype(F32)
    g2 = r1.astype(BF16)
    g3 = (r1 - g2.astype(F32)).astype(BF16)
    acc = jnp.dot(tri_bf16, jnp.concatenate([g3, g2, g1], axis=1), preferred_element_type=F32)
    return (acc[:, :width] + acc[:, width:2 * width]) + acc[:, 2 * width:]


def _hgrn_head_safe(q, kf, v, b, st):
    rows = lax.broadcasted_iota(jnp.int32, (HG_SUB, 1), 0)
    outs = []
    prev_end = jnp.zeros((1, HG_DIM), F32)
    for i in range(HG_CHUNK // HG_SUB):
        sl = slice(i * HG_SUB, (i + 1) * HG_SUB)
        bl = b[sl] - prev_end
        qi, ki, vi = q[sl], kf[sl], v[sl]
        vif = vi.astype(F32)
        blast = bl[HG_SUB - 1:HG_SUB]
        o = _dot_nt((qi * jnp.exp(bl)).astype(BF16), st.astype(BF16))
        for s in range(HG_SUB):
            d = jnp.exp(jnp.minimum(bl - bl[s:s + 1], 0.0))
            a = jnp.sum(qi * ki[s:s + 1] * d, axis=-1, keepdims=True)
            o = o + jnp.where(rows >= s, a, 0.0) * vif[s:s + 1]
        kd = (ki * jnp.exp(blast - bl)).astype(BF16)
        st = st * jnp.exp(blast) + _dot_tn(vi, kd)
        prev_end = b[(i + 1) * HG_SUB - 1:(i + 1) * HG_SUB]
        outs.append(o)
    return jnp.concatenate(outs, axis=0), st


def _hgrn_kernel(q_ref, f_ref, i_ref, g_ref, lbl_ref, gn_ref, o_ref, st_ref, kf_ref, b_ref):
    @pl.when(pl.program_id(1) == 0)
    def _():
        st_ref[...] = jnp.zeros_like(st_ref)

    lg2 = lbl_ref[...].astype(F32)
    e = jnp.exp(lg2 - jnp.max(lg2, axis=0, keepdims=True))
    lb = e[0:1] / jnp.sum(e, axis=0, keepdims=True)
    gn = gn_ref[...].astype(F32)
    r = lax.broadcasted_iota(jnp.int32, (HG_CHUNK, HG_CHUNK), 0)
    c = lax.broadcasted_iota(jnp.int32, (HG_CHUNK, HG_CHUNK), 1)
    tri_mask = r >= c
    tri = jnp.where(tri_mask, 1.0, 0.0).astype(BF16)

    n_chunks = HG_TC // HG_CHUNK

    f = lb + (1.0 - lb) * _sigmoid(f_ref[0].astype(F32))
    kf_ref[...] = 1.0 - f
    logf = jnp.log(f)
    worst = None
    for ci in range(n_chunks):
        sl = slice(ci * HG_CHUNK, (ci + 1) * HG_CHUNK)
        b = _cumsum_rows(tri, logf[sl])
        b_ref[sl, :] = b
        bend = b[HG_CHUNK - 1:HG_CHUNK]
        worst = bend if worst is None else jnp.minimum(worst, bend)

    def finish(h, sl, o):
        hs = slice(h * HG_DIM, (h + 1) * HG_DIM)
        o = o * lax.rsqrt(jnp.mean(o * o, axis=-1, keepdims=True) + RMS_EPS) * gn
        go = g_ref[0, sl, hs].astype(F32)
        o_ref[0, sl, hs] = (o * (go * _sigmoid(go))).astype(o_ref.dtype)

    def fast():
        states = [st_ref[h] for h in range(HG_HEADS)]
        for ci in range(n_chunks):
            sl = slice(ci * HG_CHUNK, (ci + 1) * HG_CHUNK)
            b = b_ref[sl, :]
            qp = (q_ref[0, sl, :].astype(F32) * jnp.exp(b)).astype(BF16)
            kp = kf_ref[sl, :] * jnp.exp(-b)
            kpb = kp.astype(BF16)
            eb = jnp.exp(b[HG_CHUNK - 1:HG_CHUNK])
            kd = (kp * eb).astype(BF16)
            for h in range(HG_HEADS):
                hs = slice(h * HG_DIM, (h + 1) * HG_DIM)
                st = states[h]
                v = i_ref[0, sl, hs]
                att = jnp.where(tri_mask, _dot_nt(qp[:, hs], kpb[:, hs]), 0.0)
                o = _dot_nt(qp[:, hs], st.astype(BF16)) + jnp.dot(att.astype(BF16), v, preferred_element_type=F32)
                states[h] = st * eb[:, hs] + _dot_tn(v, kd[:, hs])
                finish(h, sl, o)
        for h in range(HG_HEADS):
            st_ref[h] = states[h]

    def safe():
        def chunk(ci, carry):
            sl = pl.ds(pl.multiple_of(ci * HG_CHUNK, HG_CHUNK), HG_CHUNK)
            for h in range(HG_HEADS):
                hs = slice(h * HG_DIM, (h + 1) * HG_DIM)
                o, st_new = _hgrn_head_safe(q_ref[0, sl, hs].astype(F32), kf_ref[sl, hs], i_ref[0, sl, hs],
                                            b_ref[sl, hs], st_ref[h])
                st_ref[h] = st_new
                finish(h, sl, o)
            return carry

        lax.fori_loop(0, n_chunks, chunk, 0)

    lax.cond(jnp.min(worst) >= -HG_SAFE_DECAY, fast, safe)


def _hgrn2(proj3, lb_logits, gn):
    bsz, s_len, _ = proj3.shape
    blk = lambda off: (lambda b, t: (b, t, off // HG_WIDTH))
    return pl.pallas_call(
        _hgrn_kernel,
        grid=(bsz, s_len // HG_TC),
        in_specs=[
            pl.BlockSpec((1, HG_TC, HG_WIDTH), blk(OFF_QH)),
            pl.BlockSpec((1, HG_TC, HG_WIDTH), blk(OFF_FH)),
            pl.BlockSpec((1, HG_TC, HG_WIDTH), blk(OFF_IH)),
            pl.BlockSpec((1, HG_TC, HG_WIDTH), blk(OFF_GH)),
            pl.BlockSpec((lb_logits.shape[0], HG_WIDTH), lambda b, t: (0, 0)),
            pl.BlockSpec((1, HG_DIM), lambda b, t: (0, 0)),
        ],
        out_specs=pl.BlockSpec((1, HG_TC, HG_WIDTH), lambda b, t: (b, t, 0)),
        out_shape=jax.ShapeDtypeStruct((bsz, s_len, HG_WIDTH), BF16),
        scratch_shapes=[
            pltpu.VMEM((HG_HEADS, HG_DIM, HG_DIM), F32),
            pltpu.VMEM((HG_TC, HG_WIDTH), F32),
            pltpu.VMEM((HG_TC, HG_WIDTH), F32),
        ],
        compiler_params=pltpu.CompilerParams(
            dimension_semantics=("parallel", "arbitrary"), vmem_limit_bytes=_vmem(32)),
        name="hgrn2",
    )(proj3, proj3, proj3, proj3, lb_logits, gn)


def _store_rows_as_tiles(ref, val):
    rows = val.shape[0]
    for s in range(ROW_TILE):
        ref[pl.ds(s, rows, stride=ROW_TILE), :] = val[:, s * LANES:(s + 1) * LANES]


def _load_rows_from_tiles(ref, start, rows):
    return jnp.concatenate(
        [ref[pl.ds(start * ROW_TILE + s, rows, stride=ROW_TILE), :] for s in range(ROW_TILE)], axis=1)


def _pack_bf16_pairs(x):
    c = x.shape[1] // 2
    bits = pltpu.bitcast(x.astype(BF16).astype(F32), U32)
    return jnp.bitwise_or(jnp.right_shift(bits[:, :c], jnp.uint32(16)), bits[:, c:])


def _unpack_bf16_pairs(p):
    lo = pltpu.bitcast(jnp.left_shift(p, jnp.uint32(16)), F32)
    hi = pltpu.bitcast(jnp.bitwise_and(p, jnp.uint32(0xFFFF0000)), F32)
    return lo, hi


def _mix_kernel(a_ref, r_ref, ga_ref, gh_ref, x_ref, wa_ref, wr_ref, wo_ref, g2_ref, wr2_ref, br_ref,
                x1_ref, h2p_ref, route_ref, cnt_out_ref, cnt_ref):
    @pl.when(pl.program_id(0) == 0)
    def _():
        cnt_ref[...] = jnp.zeros_like(cnt_ref)

    am = jnp.dot(a_ref[...], wa_ref[...], preferred_element_type=F32)
    rm = jnp.dot(r_ref[...], wr_ref[...], preferred_element_type=F32)
    mixed = _sigmoid(ga_ref[...].astype(F32)) * am + _sigmoid(gh_ref[...].astype(F32)) * rm
    x1 = x_ref[...] + jnp.dot(mixed.astype(BF16), wo_ref[...], preferred_element_type=F32)
    x1_ref[...] = x1
    h2 = x1 * lax.rsqrt(jnp.mean(x1 * x1, axis=-1, keepdims=True) + RMS_EPS) * g2_ref[...]

    _store_rows_as_tiles(h2p_ref, _pack_bf16_pairs(h2))

    hi = h2.astype(BF16)
    lo = (h2 - hi.astype(F32)).astype(BF16)
    both = jnp.dot(hi, wr2_ref[...], preferred_element_type=F32)
    logits = jnp.dot(lo, wr2_ref[:, :ROUTE_LANES], preferred_element_type=F32) + both[:, ROUTE_LANES:]
    logits = logits + both[:, :ROUTE_LANES] + br_ref[...]

    lane = lax.broadcasted_iota(jnp.int32, logits.shape, 1)
    lane_f = lane.astype(F32)
    big = float(ROUTE_LANES)
    gl = jnp.where(lane < N_GROUPS, logits, NEG_INF)
    gmax = jnp.max(gl, axis=-1, keepdims=True)
    g_idx = jnp.min(jnp.where(gl == gmax, lane_f, big), axis=-1, keepdims=True)
    g_w = 1.0 / jnp.sum(jnp.exp(gl - gmax), axis=-1, keepdims=True)
    e_lane = lane - N_GROUPS
    in_group = jnp.logical_and(e_lane >= 0, e_lane < N_EXPERTS)
    lane_group = jnp.right_shift(e_lane, 3).astype(F32)
    in_group = jnp.logical_and(in_group, lane_group == g_idx)
    el = jnp.where(in_group, logits, NEG_INF)
    m1 = jnp.max(el, axis=-1, keepdims=True)
    i1 = jnp.min(jnp.where(el == m1, lane_f, big), axis=-1, keepdims=True)
    el2 = jnp.where(lane_f == i1, NEG_INF, el)
    m2 = jnp.max(el2, axis=-1, keepdims=True)
    i2 = jnp.min(jnp.where(el2 == m2, lane_f, big), axis=-1, keepdims=True)
    p2 = jnp.exp(m2 - m1)
    w1 = g_w / (1.0 + p2)
    w2 = g_w * p2 / (1.0 + p2)
    e1 = i1 - N_GROUPS
    e2 = i2 - N_GROUPS

    tm = logits.shape[0]
    rr = lax.broadcasted_iota(jnp.int32, (tm, tm), 0)
    cc = lax.broadcasted_iota(jnp.int32, (tm, tm), 1)
    ltri = jnp.where(rr > cc, 1.0, 0.0).astype(BF16)
    sel1 = lane_f == e1
    sel2 = lane_f == e2
    oh1 = jnp.where(sel1, 1.0, 0.0)
    oh2 = jnp.where(sel2, 1.0, 0.0)
    base = cnt_ref[...]
    c1 = jnp.sum(oh1, axis=0, keepdims=True)
    c2 = jnp.sum(oh2, axis=0, keepdims=True)
    pre1 = jnp.dot(ltri, oh1.astype(BF16), preferred_element_type=F32) + base
    pre2 = jnp.dot(ltri, oh2.astype(BF16), preferred_element_type=F32) + (base + c1)
    rank1 = jnp.sum(jnp.where(sel1, pre1, 0.0), axis=-1, keepdims=True)
    rank2 = jnp.sum(jnp.where(sel2, pre2, 0.0), axis=-1, keepdims=True)
    total = base + c1 + c2
    cnt_ref[...] = total
    cnt_out_ref[...] = total

    route = jnp.zeros_like(logits)
    for k, val in enumerate((w1, w2, e1, e2, rank1, rank2)):
        route = jnp.where(lane == k, val, route)
    route_ref[...] = route


def _mix(a2d, r2d, proj, x2d, wa, wr, wo, g2, wr2, br):
    n = x2d.shape[0]
    row = lambda i: (i, 0)
    const = lambda i: (0, 0)
    return pl.pallas_call(
        _mix_kernel,
        grid=(n // MIX_TM,),
        in_specs=[
            pl.BlockSpec((MIX_TM, ATTN_WIDTH), row),
            pl.BlockSpec((MIX_TM, HG_WIDTH), row),
            pl.BlockSpec((MIX_TM, D_MODEL), lambda i: (i, OFF_GATE_A // D_MODEL)),
            pl.BlockSpec((MIX_TM, D_MODEL), lambda i: (i, OFF_GATE_H // D_MODEL)),
            pl.BlockSpec((MIX_TM, D_MODEL), row),
            pl.BlockSpec((ATTN_WIDTH, D_MODEL), const),
            pl.BlockSpec((HG_WIDTH, D_MODEL), const),
            pl.BlockSpec((D_MODEL, D_MODEL), const),
            pl.BlockSpec((1, D_MODEL), const),
            pl.BlockSpec((D_MODEL, 2 * ROUTE_LANES), const),
            pl.BlockSpec((1, ROUTE_LANES), const),
        ],
        out_specs=[
            pl.BlockSpec((MIX_TM, D_MODEL), row),
            pl.BlockSpec((MIX_TM * ROW_TILE, LANES), row),
            pl.BlockSpec((MIX_TM, ROUTE_LANES), row),
            pl.BlockSpec((1, ROUTE_LANES), const),
        ],
        out_shape=[
            jax.ShapeDtypeStruct((n, D_MODEL), F32),
            jax.ShapeDtypeStruct((n * ROW_TILE, LANES), U32),
            jax.ShapeDtypeStruct((n, ROUTE_LANES), F32),
            jax.ShapeDtypeStruct((1, ROUTE_LANES), F32),
        ],
        scratch_shapes=[pltpu.VMEM((1, ROUTE_LANES), F32)],
        compiler_params=pltpu.CompilerParams(dimension_semantics=("arbitrary",), vmem_limit_bytes=_vmem(56)),
        name="mix",
    )(a2d, r2d, proj, proj, x2d, wa, wr, wo, g2, wr2, br)


def _dispatch_kernel(dest_ref, src_ref, zeros_hbm, dst_hbm, sem):
    del zeros_hbm

    for a in range(DISPATCH_ROWS):
        pltpu.make_async_copy(src_ref.at[a // TOP_K], dst_hbm.at[dest_ref[0, 0, a]], sem.at[0]).start(priority=a % 2)
    for _ in range(TOP_K):
        pltpu.make_async_copy(src_ref, dst_hbm.at[pl.ds(0, DISPATCH_ROWS // TOP_K)], sem.at[0]).wait()


def _dispatch(dest_flat, h2p_tiles, n_rows):
    n_asg = dest_flat.shape[0]
    steps = n_asg // DISPATCH_ROWS
    return pl.pallas_call(
        _dispatch_kernel,
        grid=(steps,),
        in_specs=[
            pl.BlockSpec((1, 1, DISPATCH_ROWS), lambda s: (s, 0, 0), memory_space=pltpu.SMEM),
            pl.BlockSpec((DISPATCH_ROWS // TOP_K, ROW_TILE, LANES), lambda s: (s, 0, 0)),
            pl.BlockSpec(memory_space=pl.ANY),
        ],
        out_specs=pl.BlockSpec(memory_space=pl.ANY),
        out_shape=jax.ShapeDtypeStruct((n_rows, ROW_TILE, LANES), U32),
        input_output_aliases={2: 0},
        scratch_shapes=[pltpu.SemaphoreType.DMA((1,))],
        compiler_params=pltpu.CompilerParams(dimension_semantics=("arbitrary",)),
        name="dispatch",
    )(dest_flat.reshape(steps, 1, DISPATCH_ROWS), h2p_tiles, jnp.zeros((n_rows, ROW_TILE, LANES), U32))


def _moe_kernel(be_ref, nused_ref, xs_ref, wg_ref, wu_ref, wd_ref, y_ref, wg_s, wu_s, wd_s):
    b = pl.program_id(0)
    live = b < nused_ref[0]
    new_expert = jnp.logical_or(b == 0, be_ref[b] != be_ref[jnp.maximum(b - 1, 0)])

    @pl.when(jnp.logical_and(live, new_expert))
    def _():
        wg_s[...] = wg_ref[0, 0].astype(BF16)
        wu_s[...] = wu_ref[0, 0].astype(BF16)
        wd_s[...] = wd_ref[0, 0].astype(BF16)

    @pl.when(jnp.logical_not(live))
    def _():
        y_ref[...] = jnp.zeros_like(y_ref)

    @pl.when(live)
    def _():
        xlo, xhi = _unpack_bf16_pairs(_load_rows_from_tiles(xs_ref, 0, MOE_TM))
        xb = jnp.concatenate([xlo.astype(BF16), xhi.astype(BF16)], axis=1)
        hg = jnp.dot(xb, wg_s[...], preferred_element_type=F32)
        hu = jnp.dot(xb, wu_s[...], preferred_element_type=F32)
        hb = (hg * _sigmoid(hg) * hu).astype(BF16)
        y = jnp.dot(hb, wd_s[...], preferred_element_type=F32)
        _store_rows_as_tiles(y_ref, _pack_bf16_pairs(y))


def _moe(block_e, n_used, xs, w_gate, w_up, w_down):
    n_rows = xs.shape[0] // ROW_TILE
    n_blocks = n_rows // MOE_TM
    wspec = lambda shp: pl.BlockSpec((1, 1) + shp, lambda b, be, nu: (0, be[b], 0, 0))
    grid_spec = pltpu.PrefetchScalarGridSpec(
        num_scalar_prefetch=2,
        grid=(n_blocks,),
        in_specs=[
            pl.BlockSpec((MOE_TM * ROW_TILE, LANES), lambda b, be, nu: (jnp.minimum(b, nu[0] - 1), 0)),
            wspec((D_MODEL, EXPERT_FF)),
            wspec((D_MODEL, EXPERT_FF)),
            wspec((EXPERT_FF, D_MODEL)),
        ],
        out_specs=pl.BlockSpec((MOE_TM * ROW_TILE, LANES), lambda b, be, nu: (b, 0)),
        scratch_shapes=[
            pltpu.VMEM((D_MODEL, EXPERT_FF), BF16),
            pltpu.VMEM((D_MODEL, EXPERT_FF), BF16),
            pltpu.VMEM((EXPERT_FF, D_MODEL), BF16),
        ],
    )
    return pl.pallas_call(
        _moe_kernel,
        grid_spec=grid_spec,
        out_shape=jax.ShapeDtypeStruct((n_rows * ROW_TILE, LANES), U32),
        compiler_params=pltpu.CompilerParams(dimension_semantics=("arbitrary",), vmem_limit_bytes=_vmem(52)),
        name="moe",
    )(block_e, n_used, xs, w_gate, w_up, w_down)


def _moe_layout(route, counts_f):
    counts = counts_f[0, :N_EXPERTS].astype(jnp.int32)
    padded = (counts + MOE_TM - 1) // MOE_TM * MOE_TM
    pends = jnp.cumsum(padded)
    pstarts = pends - padded
    n_tok = route.shape[0]
    n_blocks = n_tok * TOP_K // MOE_TM + N_EXPERTS
    blk_start = jnp.arange(n_blocks, dtype=jnp.int32) * MOE_TM
    block_e = jnp.minimum(jnp.sum((blk_start[:, None] >= pends[None, :]).astype(jnp.int32), axis=1), N_EXPERTS - 1)
    n_used = (pends[-1] // MOE_TM).astype(jnp.int32).reshape(1)
    e_id = route[:, 2:2 + TOP_K].astype(jnp.int32)
    rank = route[:, 2 + TOP_K:2 + 2 * TOP_K].astype(jnp.int32)
    onehot_e = e_id[:, :, None] == jnp.arange(N_EXPERTS, dtype=jnp.int32)[None, None, :]
    dest = jnp.sum(jnp.where(onehot_e, pstarts[None, None, :], 0), axis=2) + rank
    return block_e.astype(jnp.int32), n_used, dest, n_blocks * MOE_TM


def _final_kernel(dest_ref, destn_ref, x1_ref, route_ref, g_ref, y_hbm, y_flat_hbm, o_ref, ybuf, sem):
    i = pl.program_id(0)
    n_steps = pl.num_programs(0)
    slot = lax.rem(i, 2)
    n_rows = TOP_K * FIN_TM

    def start_gather(idx_ref, s):
        for r in range(n_rows):
            pltpu.make_async_copy(y_hbm.at[idx_ref[0, 0, r]], ybuf.at[pl.ds((s * n_rows + r) * ROW_TILE, ROW_TILE)],
                                  sem.at[s]).start(priority=r % 2)

    @pl.when(i == 0)
    def _():
        start_gather(dest_ref, 0)

    for s in range(2):
        @pl.when(jnp.logical_and(i + 1 < n_steps, 1 - slot == s))
        def _():
            start_gather(destn_ref, s)

    slot_rows = n_rows * ROW_TILE
    slot_start = pl.multiple_of(slot * slot_rows, slot_rows)
    pltpu.make_async_copy(y_flat_hbm.at[pl.ds(0, slot_rows)], ybuf.at[pl.ds(slot_start, slot_rows)],
                          sem.at[slot]).wait()
    route = route_ref[...]
    x = x1_ref[...]
    for k in range(TOP_K):
        ylo, yhi = _unpack_bf16_pairs(_load_rows_from_tiles(ybuf, slot * n_rows + k * FIN_TM, FIN_TM))
        x = x + route[:, k:k + 1] * jnp.concatenate([ylo, yhi], axis=1)
    o_ref[...] = x * lax.rsqrt(jnp.mean(x * x, axis=-1, keepdims=True) + RMS_EPS) * g_ref[...]


def _final(dest, x1, route, g, y_sorted):
    n = x1.shape[0]
    steps = n // FIN_TM
    dest3 = dest.reshape(steps, FIN_TM, TOP_K).transpose(0, 2, 1).reshape(steps, 1, TOP_K * FIN_TM)
    return pl.pallas_call(
        _final_kernel,
        grid=(steps,),
        in_specs=[
            pl.BlockSpec((1, 1, TOP_K * FIN_TM), lambda i: (i, 0, 0), memory_space=pltpu.SMEM),
            pl.BlockSpec((1, 1, TOP_K * FIN_TM), lambda i: (jnp.minimum(i + 1, steps - 1), 0, 0),
                         memory_space=pltpu.SMEM),
            pl.BlockSpec((FIN_TM, D_MODEL), lambda i: (i, 0)),
            pl.BlockSpec((FIN_TM, ROUTE_LANES), lambda i: (i, 0)),
            pl.BlockSpec((1, D_MODEL), lambda i: (0, 0)),
            pl.BlockSpec(memory_space=pl.ANY),
            pl.BlockSpec(memory_space=pl.ANY),
        ],
        out_specs=pl.BlockSpec((FIN_TM, D_MODEL), lambda i: (i, 0)),
        out_shape=jax.ShapeDtypeStruct((n, D_MODEL), F32),
        scratch_shapes=[
            pltpu.VMEM((2 * TOP_K * FIN_TM * ROW_TILE, LANES), U32),
            pltpu.SemaphoreType.DMA((2,)),
        ],
        compiler_params=pltpu.CompilerParams(dimension_semantics=("arbitrary",), vmem_limit_bytes=_vmem(40)),
        name="final",
    )(dest3, dest3, x1, route, g, y_sorted.reshape(-1, ROW_TILE, LANES), y_sorted)


def kernel(x, norm1_g, w_in, attn_sinks, rel_bias, hg_lb_logits, hg_norm_g, w_attn_branch, w_hg_branch, w_out,
           norm2_g, w_group_router, b_group_router, w_expert_router, b_expert_router, w_gate, w_up, w_down, final_g):
    bsz, s_len, d = x.shape
    n_tok = bsz * s_len
    x2d = x.reshape(n_tok, d)

    w_route = jnp.concatenate([w_group_router[0], w_expert_router[0]], axis=1).astype(F32)
    w_route = jnp.pad(w_route, ((0, 0), (0, ROUTE_LANES - w_route.shape[1])))
    whi = w_route.astype(BF16)
    wlo = (w_route - whi.astype(F32)).astype(BF16)
    wr2 = jnp.concatenate([whi, wlo], axis=1)
    b_route = jnp.concatenate([b_group_router[0], b_expert_router[0]]).astype(F32)
    b_route = jnp.pad(b_route, (0, ROUTE_LANES - b_route.shape[0])).reshape(1, ROUTE_LANES)
    bias = _attn_bias_table(rel_bias)

    proj = _in_proj(x2d, norm1_g[0].reshape(1, d).astype(F32), w_in)
    proj3 = proj.reshape(bsz, s_len, IN_WIDTH)
    a = _swa(proj3, attn_sinks[0].astype(F32), bias)
    r = _hgrn2(proj3, hg_lb_logits.astype(F32), hg_norm_g[0].reshape(1, HG_DIM).astype(F32))
    x1, h2p, route, counts = _mix(
        a.reshape(n_tok, ATTN_WIDTH), r.reshape(n_tok, HG_WIDTH), proj, x2d,
        w_attn_branch[0].astype(BF16), w_hg_branch[0].astype(BF16), w_out[0].astype(BF16),
        norm2_g[0].reshape(1, d).astype(F32), wr2, b_route)
    block_e, n_used, dest, n_rows = _moe_layout(route, counts)
    xs = _dispatch(dest.reshape(n_tok * TOP_K), h2p.reshape(n_tok, ROW_TILE, LANES), n_rows)
    y_sorted = _moe(block_e, n_used, xs.reshape(n_rows * ROW_TILE, LANES), w_gate, w_up, w_down)
    out = _final(dest, x1, route, final_g.reshape(1, d).astype(F32), y_sorted)
    return out.reshape(bsz, s_len, d)
```

```python
import math

import numpy as np
import jax
import jax.numpy as jnp
from jax import lax
from jax.experimental import pallas as pl
from jax.experimental.pallas import tpu as pltpu

D_MODEL = 2048
N_Q_HEADS = 16
N_KV_HEADS = 4
HEAD_DIM = 64
WINDOW = 128
ATTN_BLOCK = 128
ATTN_WIDTH = N_Q_HEADS * HEAD_DIM
KV_WIDTH = N_KV_HEADS * HEAD_DIM
REL_BUCKETS = 32
REL_MAX_DIST = 128
HG_HEADS = 8
HG_DIM = 128
HG_WIDTH = HG_HEADS * HG_DIM
HG_CHUNK = 64
HG_SUB = 16
N_GROUPS = 4
EXPERTS_PER_GROUP = 8
N_EXPERTS = N_GROUPS * EXPERTS_PER_GROUP
TOP_K = 2
EXPERT_FF = 512
RMS_EPS = 1e-6

OFF_GATE_A = 0
OFF_GATE_H = D_MODEL
OFF_QA = 2 * D_MODEL
OFF_QH = OFF_QA + ATTN_WIDTH
OFF_FH = OFF_QH + HG_WIDTH
OFF_IH = OFF_FH + HG_WIDTH
OFF_GH = OFF_IH + HG_WIDTH
OFF_KA = OFF_GH + HG_WIDTH
OFF_VA = OFF_KA + KV_WIDTH
IN_WIDTH = OFF_VA + KV_WIDTH
REF_OFF_QA = 0
REF_OFF_KA = ATTN_WIDTH
REF_OFF_QH = ATTN_WIDTH + 2 * KV_WIDTH
REF_OFF_GATE_A = REF_OFF_QH + 4 * HG_WIDTH

HG_SAFE_DECAY = 60.0

ROW_TILE = 8
LANES = 128
assert D_MODEL // 2 == ROW_TILE * LANES

IN_TM = 2048
IN_TN = 512
HG_TC = 256
MIX_TM = 256
ROUTE_LANES = 128
DISPATCH_ROWS = 1024
MOE_TM = 256
FIN_TM = 256

F32 = jnp.float32
BF16 = jnp.bfloat16
U32 = jnp.uint32
NEG_INF = float("-inf")


def _vmem(mib):
    return mib * 1024 * 1024


def _sigmoid(x):
    return 1.0 / (1.0 + jnp.exp(-x))


def _dot_nt(a, b):
    return lax.dot_general(a, b, (((1,), (1,)), ((), ())), preferred_element_type=F32)


def _dot_tn(a, b):
    return lax.dot_general(a, b, (((0,), (0,)), ((), ())), preferred_element_type=F32)


def _in_proj_kernel(x_ref, g_ref, w_ref, o_ref, h_ref):
    @pl.when(pl.program_id(1) == 0)
    def _():
        x = x_ref[...]
        ms = jnp.mean(x * x, axis=-1, keepdims=True)
        h_ref[...] = (x * lax.rsqrt(ms + RMS_EPS) * g_ref[...]).astype(BF16)

    o_ref[...] = jnp.dot(h_ref[...], w_ref[0].astype(BF16), preferred_element_type=F32).astype(o_ref.dtype)


def _in_proj(x2d, g, w_in):
    n = x2d.shape[0]
    n_col = IN_WIDTH // IN_TN

    def out_block(j):
        gates = j - REF_OFF_GATE_A // IN_TN + OFF_GATE_A // IN_TN
        q_a = j - REF_OFF_QA // IN_TN + OFF_QA // IN_TN
        kv = j - REF_OFF_KA // IN_TN + OFF_KA // IN_TN
        hg = j - REF_OFF_QH // IN_TN + OFF_QH // IN_TN
        return jnp.where(j >= REF_OFF_GATE_A // IN_TN, gates,
                         jnp.where(j >= REF_OFF_QH // IN_TN, hg, jnp.where(j >= REF_OFF_KA // IN_TN, kv, q_a)))

    return pl.pallas_call(
        _in_proj_kernel,
        grid=(n // IN_TM, n_col),
        in_specs=[
            pl.BlockSpec((IN_TM, D_MODEL), lambda i, j: (i, 0)),
            pl.BlockSpec((1, D_MODEL), lambda i, j: (0, 0)),
            pl.BlockSpec((1, D_MODEL, IN_TN), lambda i, j: (0, 0, j)),
        ],
        out_specs=pl.BlockSpec((IN_TM, IN_TN), lambda i, j: (i, out_block(j))),
        out_shape=jax.ShapeDtypeStruct((n, IN_WIDTH), BF16),
        scratch_shapes=[pltpu.VMEM((IN_TM, D_MODEL), BF16)],
        compiler_params=pltpu.CompilerParams(
            dimension_semantics=("parallel", "arbitrary"), vmem_limit_bytes=_vmem(58)),
        name="in_proj",
    )(x2d, g, w_in)


def _swa_kernel(sink_ref, q_ref, kp_ref, kc_ref, vp_ref, vc_ref, bias_ref, o_ref):
    has_prev = pl.program_id(1) > 0
    q = q_ref[0] * jnp.asarray(HEAD_DIM ** -0.5, BF16)
    kk = jnp.concatenate([kp_ref[0], kc_ref[0]], axis=0)
    vv = jnp.concatenate([vp_ref[0], vc_ref[0]], axis=0)
    row = lax.broadcasted_iota(jnp.int32, (ATTN_BLOCK, ATTN_BLOCK), 0)
    col = lax.broadcasted_iota(jnp.int32, (ATTN_BLOCK, ATTN_BLOCK), 1)
    cur = col <= row
    key_ok = jnp.logical_or(cur, has_prev)
    grp = N_Q_HEADS // N_KV_HEADS
    outs = []
    for hq in range(N_Q_HEADS):
        hk = hq // grp
        qh = q[:, hq * HEAD_DIM:(hq + 1) * HEAD_DIM]
        kh = kk[:, hk * HEAD_DIM:(hk + 1) * HEAD_DIM]
        vh = vv[:, hk * HEAD_DIM:(hk + 1) * HEAD_DIM]
        s2 = _dot_nt(qh, kh)
        s = jnp.where(cur, s2[:, ATTN_BLOCK:], s2[:, :ATTN_BLOCK]) + bias_ref[hq]
        s = jnp.where(key_ok, s, NEG_INF)
        sink = sink_ref[hq]
        m = jnp.maximum(jnp.max(s, axis=-1, keepdims=True), sink)
        p = jnp.exp(s - m)
        denom = jnp.sum(p, axis=-1, keepdims=True) + jnp.exp(sink - m)
        p2 = jnp.concatenate([jnp.where(cur, 0.0, p), jnp.where(cur, p, 0.0)], axis=1).astype(BF16)
        o = jnp.dot(p2, vh, preferred_element_type=F32)
        outs.append(o / denom)
    o_ref[0] = jnp.concatenate(outs, axis=-1).astype(o_ref.dtype)


def _swa(proj3, sinks, bias):
    bsz, s_len, _ = proj3.shape
    nb = s_len // ATTN_BLOCK
    kblk = OFF_KA // KV_WIDTH
    vblk = OFF_VA // KV_WIDTH
    prev = lambda n: jnp.maximum(n - 1, 0)
    return pl.pallas_call(
        _swa_kernel,
        grid=(bsz, nb),
        in_specs=[
            pl.BlockSpec(memory_space=pltpu.SMEM),
            pl.BlockSpec((1, ATTN_BLOCK, ATTN_WIDTH), lambda b, n: (b, n, OFF_QA // ATTN_WIDTH)),
            pl.BlockSpec((1, ATTN_BLOCK, KV_WIDTH), lambda b, n: (b, prev(n), kblk)),
            pl.BlockSpec((1, ATTN_BLOCK, KV_WIDTH), lambda b, n: (b, n, kblk)),
            pl.BlockSpec((1, ATTN_BLOCK, KV_WIDTH), lambda b, n: (b, prev(n), vblk)),
            pl.BlockSpec((1, ATTN_BLOCK, KV_WIDTH), lambda b, n: (b, n, vblk)),
            pl.BlockSpec((N_Q_HEADS, ATTN_BLOCK, ATTN_BLOCK), lambda b, n: (0, 0, 0)),
        ],
        out_specs=pl.BlockSpec((1, ATTN_BLOCK, ATTN_WIDTH), lambda b, n: (b, n, 0)),
        out_shape=jax.ShapeDtypeStruct((bsz, s_len, ATTN_WIDTH), BF16),
        compiler_params=pltpu.CompilerParams(
            dimension_semantics=("parallel", "arbitrary"), vmem_limit_bytes=_vmem(32)),
        name="swa",
    )(sinks, proj3, proj3, proj3, proj3, proj3, bias)


def _t5_causal_bucket(n):
    max_exact = REL_BUCKETS // 2
    nf = jnp.maximum(n, 1).astype(F32)
    large = max_exact + (jnp.log(nf / max_exact) / math.log(REL_MAX_DIST / max_exact)
                         * (REL_BUCKETS - max_exact)).astype(jnp.int32)
    large = jnp.minimum(large, REL_BUCKETS - 1)
    return jnp.where(n < max_exact, n, large)


def _attn_bias_table(rel_bias):
    assert WINDOW == ATTN_BLOCK
    r = jnp.arange(ATTN_BLOCK)[:, None]
    c = jnp.arange(ATTN_BLOCK)[None, :]
    dist = jnp.where(c <= r, r - c, r + ATTN_BLOCK - c)
    bucket = _t5_causal_bucket(dist)
    onehot = (bucket[None, :, :] == jnp.arange(REL_BUCKETS)[:, None, None]).astype(F32)
    return jnp.einsum("bh,bqk->hqk", rel_bias.astype(F32), onehot, precision=lax.Precision.HIGHEST)


def _cumsum_rows(tri_bf16, g):
    width = g.shape[1]
    g1 = g.astype(BF16)
    r1 = g - g1.astype(F32)
    g2 = r1.astype(BF16)
    g3 = (r1 - g2.astype(F32)).astype(BF16)
    acc = jnp.dot(tri_bf16, jnp.concatenate([g3, g2, g1], axis=1), preferred_element_type=F32)
    return (acc[:, :width] + acc[:, width:2 * width]) + acc[:, 2 * width:]


def _hgrn_head_safe(q, kf, v, b, st):
    rows = lax.broadcasted_iota(jnp.int32, (HG_SUB, 1), 0)
    outs = []
    prev_end = jnp.zeros((1, HG_DIM), F32)
    for i in range(HG_CHUNK // HG_SUB):
        sl = slice(i * HG_SUB, (i + 1) * HG_SUB)
        bl = b[sl] - prev_end
        qi, ki, vi = q[sl], kf[sl], v[sl]
        vif = vi.astype(F32)
        blast = bl[HG_SUB - 1:HG_SUB]
        o = _dot_nt((qi * jnp.exp(bl)).astype(BF16), st.astype(BF16))
        for s in range(HG_SUB):
            d = jnp.exp(jnp.minimum(bl - bl[s:s + 1], 0.0))
            a = jnp.sum(qi * ki[s:s + 1] * d, axis=-1, keepdims=True)
            o = o + jnp.where(rows >= s, a, 0.0) * vif[s:s + 1]
        kd = (ki * jnp.exp(blast - bl)).astype(BF16)
        st = st * jnp.exp(blast) + _dot_tn(vi, kd)
        prev_end = b[(i + 1) * HG_SUB - 1:(i + 1) * HG_SUB]
        outs.append(o)
    return jnp.concatenate(outs, axis=0), st


def _hgrn_kernel(q_ref, f_ref, i_ref, g_ref, lbl_ref, gn_ref, o_ref, st_ref, kf_ref, b_ref):
    @pl.when(pl.program_id(1) == 0)
    def _():
        st_ref[...] = jnp.zeros_like(st_ref)

    lg2 = lbl_ref[...].astype(F32)
    e = jnp.exp(lg2 - jnp.max(lg2, axis=0, keepdims=True))
    lb = e[0:1] / jnp.sum(e, axis=0, keepdims=True)
    gn = gn_ref[...].astype(F32)
    r = lax.broadcasted_iota(jnp.int32, (HG_CHUNK, HG_CHUNK), 0)
    c = lax.broadcasted_iota(jnp.int32, (HG_CHUNK, HG_CHUNK), 1)
    tri_mask = r >= c
    tri = jnp.where(tri_mask, 1.0, 0.0).astype(BF16)

    n_chunks = HG_TC // HG_CHUNK

    f = lb + (1.0 - lb) * _sigmoid(f_ref[0].astype(F32))
    kf_ref[...] = 1.0 - f
    logf = jnp.log(f)
    worst = None
    for ci in range(n_chunks):
        sl = slice(ci * HG_CHUNK, (ci + 1) * HG_CHUNK)
        b = _cumsum_rows(tri, logf[sl])
        b_ref[sl, :] = b
        bend = b[HG_CHUNK - 1:HG_CHUNK]
        worst = bend if worst is None else jnp.minimum(worst, bend)

    def finish(h, sl, o):
        hs = slice(h * HG_DIM, (h + 1) * HG_DIM)
        o = o * lax.rsqrt(jnp.mean(o * o, axis=-1, keepdims=True) + RMS_EPS) * gn
        go = g_ref[0, sl, hs].astype(F32)
        o_ref[0, sl, hs] = (o * (go * _sigmoid(go))).astype(o_ref.dtype)

    def fast():
        states = [st_ref[h] for h in range(HG_HEADS)]
        for ci in range(n_chunks):
            sl = slice(ci * HG_CHUNK, (ci + 1) * HG_CHUNK)
            b = b_ref[sl, :]
            qp = (q_ref[0, sl, :].astype(F32) * jnp.exp(b)).astype(BF16)
            kp = kf_ref[sl, :] * jnp.exp(-b)
            kpb = kp.astype(BF16)
            eb = jnp.exp(b[HG_CHUNK - 1:HG_CHUNK])
            kd = (kp * eb).astype(BF16)
            for h in range(HG_HEADS):
                hs = slice(h * HG_DIM, (h + 1) * HG_DIM)
                st = states[h]
                v = i_ref[0, sl, hs]
                att = jnp.where(tri_mask, _dot_nt(qp[:, hs], kpb[:, hs]), 0.0)
                o = _dot_nt(qp[:, hs], st.astype(BF16)) + jnp.dot(att.astype(BF16), v, preferred_element_type=F32)
                states[h] = st * eb[:, hs] + _dot_tn(v, kd[:, hs])
                finish(h, sl, o)
        for h in range(HG_HEADS):
            st_ref[h] = states[h]

    def safe():
        def chunk(ci, carry):
            sl = pl.ds(pl.multiple_of(ci * HG_CHUNK, HG_CHUNK), HG_CHUNK)
            for h in range(HG_HEADS):
                hs = slice(h * HG_DIM, (h + 1) * HG_DIM)
                o, st_new = _hgrn_head_safe(q_ref[0, sl, hs].astype(F32), kf_ref[sl, hs], i_ref[0, sl, hs],
                                            b_ref[sl, hs], st_ref[h])
                st_ref[h] = st_new
                finish(h, sl, o)
            return carry

        lax.fori_loop(0, n_chunks, chunk, 0)

    lax.cond(jnp.min(worst) >= -HG_SAFE_DECAY, fast, safe)


def _hgrn2(proj3, lb_logits, gn):
    bsz, s_len, _ = proj3.shape
    blk = lambda off: (lambda b, t: (b, t, off // HG_WIDTH))
    return pl.pallas_call(
        _hgrn_kernel,
        grid=(bsz, s_len // HG_TC),
        in_specs=[
            pl.BlockSpec((1, HG_TC, HG_WIDTH), blk(OFF_QH)),
            pl.BlockSpec((1, HG_TC, HG_WIDTH), blk(OFF_FH)),
            pl.BlockSpec((1, HG_TC, HG_WIDTH), blk(OFF_IH)),
            pl.BlockSpec((1, HG_TC, HG_WIDTH), blk(OFF_GH)),
            pl.BlockSpec((lb_logits.shape[0], HG_WIDTH), lambda b, t: (0, 0)),
            pl.BlockSpec((1, HG_DIM), lambda b, t: (0, 0)),
        ],
        out_specs=pl.BlockSpec((1, HG_TC, HG_WIDTH), lambda b, t: (b, t, 0)),
        out_shape=jax.ShapeDtypeStruct((bsz, s_len, HG_WIDTH), BF16),
        scratch_shapes=[
            pltpu.VMEM((HG_HEADS, HG_DIM, HG_DIM), F32),
            pltpu.VMEM((HG_TC, HG_WIDTH), F32),
            pltpu.VMEM((HG_TC, HG_WIDTH), F32),
        ],
        compiler_params=pltpu.CompilerParams(
            dimension_semantics=("parallel", "arbitrary"), vmem_limit_bytes=_vmem(32)),
        name="hgrn2",
    )(proj3, proj3, proj3, proj3, lb_logits, gn)


def _store_rows_as_tiles(ref, val):
    rows = val.shape[0]
    for s in range(ROW_TILE):
        ref[pl.ds(s, rows, stride=ROW_TILE), :] = val[:, s * LANES:(s + 1) * LANES]


def _load_rows_from_tiles(ref, start, rows):
    return jnp.concatenate(
        [ref[pl.ds(start * ROW_TILE + s, rows, stride=ROW_TILE), :] for s in range(ROW_TILE)], axis=1)


def _pack_bf16_pairs(x):
    c = x.shape[1] // 2
    bits = pltpu.bitcast(x.astype(BF16).astype(F32), U32)
    return jnp.bitwise_or(jnp.right_shift(bits[:, :c], jnp.uint32(16)), bits[:, c:])


def _unpack_bf16_pairs(p):
    lo = pltpu.bitcast(jnp.left_shift(p, jnp.uint32(16)), F32)
    hi = pltpu.bitcast(jnp.bitwise_and(p, jnp.uint32(0xFFFF0000)), F32)
    return lo, hi


def _mix_kernel(a_ref, r_ref, ga_ref, gh_ref, x_ref, wa_ref, wr_ref, wo_ref, g2_ref, wr2_ref, br_ref,
                x1_ref, h2p_ref, route_ref, cnt_out_ref, cnt_ref):
    @pl.when(pl.program_id(0) == 0)
    def _():
        cnt_ref[...] = jnp.zeros_like(cnt_ref)

    am = jnp.dot(a_ref[...], wa_ref[...], preferred_element_type=F32)
    rm = jnp.dot(r_ref[...], wr_ref[...], preferred_element_type=F32)
    mixed = _sigmoid(ga_ref[...].astype(F32)) * am + _sigmoid(gh_ref[...].astype(F32)) * rm
    x1 = x_ref[...] + jnp.dot(mixed.astype(BF16), wo_ref[...], preferred_element_type=F32)
    x1_ref[...] = x1
    h2 = x1 * lax.rsqrt(jnp.mean(x1 * x1, axis=-1, keepdims=True) + RMS_EPS) * g2_ref[...]

    _store_rows_as_tiles(h2p_ref, _pack_bf16_pairs(h2))

    hi = h2.astype(BF16)
    lo = (h2 - hi.astype(F32)).astype(BF16)
    both = jnp.dot(hi, wr2_ref[...], preferred_element_type=F32)
    logits = jnp.dot(lo, wr2_ref[:, :ROUTE_LANES], preferred_element_type=F32) + both[:, ROUTE_LANES:]
    logits = logits + both[:, :ROUTE_LANES] + br_ref[...]

    lane = lax.broadcasted_iota(jnp.int32, logits.shape, 1)
    lane_f = lane.astype(F32)
    big = float(ROUTE_LANES)
    gl = jnp.where(lane < N_GROUPS, logits, NEG_INF)
    gmax = jnp.max(gl, axis=-1, keepdims=True)
    g_idx = jnp.min(jnp.where(gl == gmax, lane_f, big), axis=-1, keepdims=True)
    g_w = 1.0 / jnp.sum(jnp.exp(gl - gmax), axis=-1, keepdims=True)
    e_lane = lane - N_GROUPS
    in_group = jnp.logical_and(e_lane >= 0, e_lane < N_EXPERTS)
    lane_group = jnp.right_shift(e_lane, 3).astype(F32)
    in_group = jnp.logical_and(in_group, lane_group == g_idx)
    el = jnp.where(in_group, logits, NEG_INF)
    m1 = jnp.max(el, axis=-1, keepdims=True)
    i1 = jnp.min(jnp.where(el == m1, lane_f, big), axis=-1, keepdims=True)
    el2 = jnp.where(lane_f == i1, NEG_INF, el)
    m2 = jnp.max(el2, axis=-1, keepdims=True)
    i2 = jnp.min(jnp.where(el2 == m2, lane_f, big), axis=-1, keepdims=True)
    p2 = jnp.exp(m2 - m1)
    w1 = g_w / (1.0 + p2)
    w2 = g_w * p2 / (1.0 + p2)
    e1 = i1 - N_GROUPS
    e2 = i2 - N_GROUPS

    tm = logits.shape[0]
    rr = lax.broadcasted_iota(jnp.int32, (tm, tm), 0)
    cc = lax.broadcasted_iota(jnp.int32, (tm, tm), 1)
    ltri = jnp.where(rr > cc, 1.0, 0.0).astype(BF16)
    sel1 = lane_f == e1
    sel2 = lane_f == e2
    oh1 = jnp.where(sel1, 1.0, 0.0)
    oh2 = jnp.where(sel2, 1.0, 0.0)
    base = cnt_ref[...]
    c1 = jnp.sum(oh1, axis=0, keepdims=True)
    c2 = jnp.sum(oh2, axis=0, keepdims=True)
    pre1 = jnp.dot(ltri, oh1.astype(BF16), preferred_element_type=F32) + base
    pre2 = jnp.dot(ltri, oh2.astype(BF16), preferred_element_type=F32) + (base + c1)
    rank1 = jnp.sum(jnp.where(sel1, pre1, 0.0), axis=-1, keepdims=True)
    rank2 = jnp.sum(jnp.where(sel2, pre2, 0.0), axis=-1, keepdims=True)
    total = base + c1 + c2
    cnt_ref[...] = total
    cnt_out_ref[...] = total

    route = jnp.zeros_like(logits)
    for k, val in enumerate((w1, w2, e1, e2, rank1, rank2)):
        route = jnp.where(lane == k, val, route)
    route_ref[...] = route


def _mix(a2d, r2d, proj, x2d, wa, wr, wo, g2, wr2, br):
    n = x2d.shape[0]
    row = lambda i: (i, 0)
    const = lambda i: (0, 0)
    return pl.pallas_call(
        _mix_kernel,
        grid=(n // MIX_TM,),
        in_specs=[
            pl.BlockSpec((MIX_TM, ATTN_WIDTH), row),
            pl.BlockSpec((MIX_TM, HG_WIDTH), row),
            pl.BlockSpec((MIX_TM, D_MODEL), lambda i: (i, OFF_GATE_A // D_MODEL)),
            pl.BlockSpec((MIX_TM, D_MODEL), lambda i: (i, OFF_GATE_H // D_MODEL)),
            pl.BlockSpec((MIX_TM, D_MODEL), row),
            pl.BlockSpec((ATTN_WIDTH, D_MODEL), const),
            pl.BlockSpec((HG_WIDTH, D_MODEL), const),
            pl.BlockSpec((D_MODEL, D_MODEL), const),
            pl.BlockSpec((1, D_MODEL), const),
            pl.BlockSpec((D_MODEL, 2 * ROUTE_LANES), const),
            pl.BlockSpec((1, ROUTE_LANES), const),
        ],
        out_specs=[
            pl.BlockSpec((MIX_TM, D_MODEL), row),
            pl.BlockSpec((MIX_TM * ROW_TILE, LANES), row),
            pl.BlockSpec((MIX_TM, ROUTE_LANES), row),
            pl.BlockSpec((1, ROUTE_LANES), const),
        ],
        out_shape=[
            jax.ShapeDtypeStruct((n, D_MODEL), F32),
            jax.ShapeDtypeStruct((n * ROW_TILE, LANES), U32),
            jax.ShapeDtypeStruct((n, ROUTE_LANES), F32),
            jax.ShapeDtypeStruct((1, ROUTE_LANES), F32),
        ],
        scratch_shapes=[pltpu.VMEM((1, ROUTE_LANES), F32)],
        compiler_params=pltpu.CompilerParams(dimension_semantics=("arbitrary",), vmem_limit_bytes=_vmem(56)),
        name="mix",
    )(a2d, r2d, proj, proj, x2d, wa, wr, wo, g2, wr2, br)


def _dispatch_kernel(dest_ref, src_ref, zeros_hbm, dst_hbm, sem):
    del zeros_hbm

    for a in range(DISPATCH_ROWS):
        pltpu.make_async_copy(src_ref.at[a // TOP_K], dst_hbm.at[dest_ref[0, 0, a]], sem.at[0]).start(priority=a % 2)
    for _ in range(TOP_K):
        pltpu.make_async_copy(src_ref, dst_hbm.at[pl.ds(0, DISPATCH_ROWS // TOP_K)], sem.at[0]).wait()


def _dispatch(dest_flat, h2p_tiles, n_rows):
    n_asg = dest_flat.shape[0]
    steps = n_asg // DISPATCH_ROWS
    return pl.pallas_call(
        _dispatch_kernel,
        grid=(steps,),
        in_specs=[
            pl.BlockSpec((1, 1, DISPATCH_ROWS), lambda s: (s, 0, 0), memory_space=pltpu.SMEM),
            pl.BlockSpec((DISPATCH_ROWS // TOP_K, ROW_TILE, LANES), lambda s: (s, 0, 0)),
            pl.BlockSpec(memory_space=pl.ANY),
        ],
        out_specs=pl.BlockSpec(memory_space=pl.ANY),
        out_shape=jax.ShapeDtypeStruct((n_rows, ROW_TILE, LANES), U32),
        input_output_aliases={2: 0},
        scratch_shapes=[pltpu.SemaphoreType.DMA((1,))],
        compiler_params=pltpu.CompilerParams(dimension_semantics=("arbitrary",)),
        name="dispatch",
    )(dest_flat.reshape(steps, 1, DISPATCH_ROWS), h2p_tiles, jnp.zeros((n_rows, ROW_TILE, LANES), U32))


def _moe_kernel(be_ref, nused_ref, first_ref, slot_ref, next_ref, xs_ref, wg_hbm, wu_hbm, wd_hbm, y_ref,
                wg_buf, wu_buf, wd_buf, wg_s, wu_s, wd_s, sem):
    b = pl.program_id(0)
    live = b < nused_ref[0]

    def weight_copies(e, s):
        return (pltpu.make_async_copy(wg_hbm.at[0, e], wg_buf.at[s], sem.at[s, 0]),
                pltpu.make_async_copy(wu_hbm.at[0, e], wu_buf.at[s], sem.at[s, 1]),
                pltpu.make_async_copy(wd_hbm.at[0, e], wd_buf.at[s], sem.at[s, 2]))

    @pl.when(b == 0)
    def _():
        for c in weight_copies(be_ref[0], 0):
            c.start()

    @pl.when(jnp.logical_and(live, first_ref[b] == 1))
    def _():
        s = slot_ref[b]
        for c in weight_copies(be_ref[b], s):
            c.wait()
        wg_s[...] = wg_buf[s].astype(BF16)
        wu_s[...] = wu_buf[s].astype(BF16)
        wd_s[...] = wd_buf[s].astype(BF16)

        @pl.when(next_ref[b] >= 0)
        def _():
            for c in weight_copies(next_ref[b], 1 - s):
                c.start()

    @pl.when(jnp.logical_not(live))
    def _():
        y_ref[...] = jnp.zeros_like(y_ref)

    @pl.when(live)
    def _():
        xlo, xhi = _unpack_bf16_pairs(_load_rows_from_tiles(xs_ref, 0, MOE_TM))
        xb = jnp.concatenate([xlo.astype(BF16), xhi.astype(BF16)], axis=1)
        hg = jnp.dot(xb, wg_s[...], preferred_element_type=F32)
        hu = jnp.dot(xb, wu_s[...], preferred_element_type=F32)
        hb = (hg * _sigmoid(hg) * hu).astype(BF16)
        y = jnp.dot(hb, wd_s[...], preferred_element_type=F32)
        _store_rows_as_tiles(y_ref, _pack_bf16_pairs(y))


def _moe(block_e, n_used, first, slot, next_e, xs, w_gate, w_up, w_down):
    n_rows = xs.shape[0] // ROW_TILE
    n_blocks = n_rows // MOE_TM
    grid_spec = pltpu.PrefetchScalarGridSpec(
        num_scalar_prefetch=5,
        grid=(n_blocks,),
        in_specs=[
            pl.BlockSpec((MOE_TM * ROW_TILE, LANES), lambda b, be, nu, *_: (jnp.minimum(b, nu[0] - 1), 0)),
            pl.BlockSpec(memory_space=pl.ANY),
            pl.BlockSpec(memory_space=pl.ANY),
            pl.BlockSpec(memory_space=pl.ANY),
        ],
        out_specs=pl.BlockSpec((MOE_TM * ROW_TILE, LANES), lambda b, *_: (b, 0)),
        scratch_shapes=[
            pltpu.VMEM((2, D_MODEL, EXPERT_FF), F32),
            pltpu.VMEM((2, D_MODEL, EXPERT_FF), F32),
            pltpu.VMEM((2, EXPERT_FF, D_MODEL), F32),
            pltpu.VMEM((D_MODEL, EXPERT_FF), BF16),
            pltpu.VMEM((D_MODEL, EXPERT_FF), BF16),
            pltpu.VMEM((EXPERT_FF, D_MODEL), BF16),
            pltpu.SemaphoreType.DMA((2, 3)),
        ],
    )
    return pl.pallas_call(
        _moe_kernel,
        grid_spec=grid_spec,
        out_shape=jax.ShapeDtypeStruct((n_rows * ROW_TILE, LANES), U32),
        compiler_params=pltpu.CompilerParams(dimension_semantics=("arbitrary",), vmem_limit_bytes=_vmem(52)),
        name="moe",
    )(block_e, n_used, first, slot, next_e, xs, w_gate, w_up, w_down)


def _moe_layout(route, counts_f):
    counts = counts_f[0, :N_EXPERTS].astype(jnp.int32)
    padded = (counts + MOE_TM - 1) // MOE_TM * MOE_TM
    pends = jnp.cumsum(padded)
    pstarts = pends - padded
    n_tok = route.shape[0]
    n_blocks = n_tok * TOP_K // MOE_TM + N_EXPERTS
    blk_start = jnp.arange(n_blocks, dtype=jnp.int32) * MOE_TM
    block_e = jnp.minimum(jnp.sum((blk_start[:, None] >= pends[None, :]).astype(jnp.int32), axis=1), N_EXPERTS - 1)
    n_used = (pends[-1] // MOE_TM).astype(jnp.int32).reshape(1)
    blk = jnp.arange(n_blocks, dtype=jnp.int32)
    first = jnp.logical_and(blk < n_used[0], jnp.logical_or(blk == 0, block_e != jnp.roll(block_e, 1)))
    slot = jnp.bitwise_and(jnp.cumsum(first.astype(jnp.int32)) - 1, 1)
    ex = jnp.arange(N_EXPERTS, dtype=jnp.int32)
    later = jnp.logical_and(counts[None, :] > 0, ex[None, :] > ex[:, None])
    next_of_expert = jnp.min(jnp.where(later, ex[None, :], N_EXPERTS), axis=1)
    next_of_expert = jnp.where(next_of_expert == N_EXPERTS, -1, next_of_expert)
    next_e = jnp.sum(jnp.where(block_e[:, None] == ex[None, :], next_of_expert[None, :], 0), axis=1)
    e_id = route[:, 2:2 + TOP_K].astype(jnp.int32)
    rank = route[:, 2 + TOP_K:2 + 2 * TOP_K].astype(jnp.int32)
    onehot_e = e_id[:, :, None] == jnp.arange(N_EXPERTS, dtype=jnp.int32)[None, None, :]
    dest = jnp.sum(jnp.where(onehot_e, pstarts[None, None, :], 0), axis=2) + rank
    tables = (block_e.astype(jnp.int32), n_used, first.astype(jnp.int32), slot.astype(jnp.int32),
              next_e.astype(jnp.int32))
    return tables, dest, n_blocks * MOE_TM


def _final_kernel(dest_ref, destn_ref, x1_ref, route_ref, g_ref, y_hbm, y_flat_hbm, o_ref, ybuf, sem):
    i = pl.program_id(0)
    n_steps = pl.num_programs(0)
    slot = lax.rem(i, 2)
    n_rows = TOP_K * FIN_TM

    def start_gather(idx_ref, s):
        for r in range(n_rows):
            pltpu.make_async_copy(y_hbm.at[idx_ref[0, 0, r]], ybuf.at[pl.ds((s * n_rows + r) * ROW_TILE, ROW_TILE)],
                                  sem.at[s]).start(priority=r % 2)

    @pl.when(i == 0)
    def _():
        start_gather(dest_ref, 0)

    for s in range(2):
        @pl.when(jnp.logical_and(i + 1 < n_steps, 1 - slot == s))
        def _():
            start_gather(destn_ref, s)

    slot_rows = n_rows * ROW_TILE
    slot_start = pl.multiple_of(slot * slot_rows, slot_rows)
    pltpu.make_async_copy(y_flat_hbm.at[pl.ds(0, slot_rows)], ybuf.at[pl.ds(slot_start, slot_rows)],
                          sem.at[slot]).wait()
    route = route_ref[...]
    x = x1_ref[...]
    for k in range(TOP_K):
        ylo, yhi = _unpack_bf16_pairs(_load_rows_from_tiles(ybuf, slot * n_rows + k * FIN_TM, FIN_TM))
        x = x + route[:, k:k + 1] * jnp.concatenate([ylo, yhi], axis=1)
    o_ref[...] = x * lax.rsqrt(jnp.mean(x * x, axis=-1, keepdims=True) + RMS_EPS) * g_ref[...]


def _final(dest, x1, route, g, y_sorted):
    n = x1.shape[0]
    steps = n // FIN_TM
    dest3 = dest.reshape(steps, FIN_TM, TOP_K).transpose(0, 2, 1).reshape(steps, 1, TOP_K * FIN_TM)
    return pl.pallas_call(
        _final_kernel,
        grid=(steps,),
        in_specs=[
            pl.BlockSpec((1, 1, TOP_K * FIN_TM), lambda i: (i, 0, 0), memory_space=pltpu.SMEM),
            pl.BlockSpec((1, 1, TOP_K * FIN_TM), lambda i: (jnp.minimum(i + 1, steps - 1), 0, 0),
                         memory_space=pltpu.SMEM),
            pl.BlockSpec((FIN_TM, D_MODEL), lambda i: (i, 0)),
            pl.BlockSpec((FIN_TM, ROUTE_LANES), lambda i: (i, 0)),
            pl.BlockSpec((1, D_MODEL), lambda i: (0, 0)),
            pl.BlockSpec(memory_space=pl.ANY),
            pl.BlockSpec(memory_space=pl.ANY),
        ],
        out_specs=pl.BlockSpec((FIN_TM, D_MODEL), lambda i: (i, 0)),
        out_shape=jax.ShapeDtypeStruct((n, D_MODEL), F32),
        scratch_shapes=[
            pltpu.VMEM((2 * TOP_K * FIN_TM * ROW_TILE, LANES), U32),
            pltpu.SemaphoreType.DMA((2,)),
        ],
        compiler_params=pltpu.CompilerParams(dimension_semantics=("arbitrary",), vmem_limit_bytes=_vmem(40)),
        name="final",
    )(dest3, dest3, x1, route, g, y_sorted.reshape(-1, ROW_TILE, LANES), y_sorted)


def kernel(x, norm1_g, w_in, attn_sinks, rel_bias, hg_lb_logits, hg_norm_g, w_attn_branch, w_hg_branch, w_out,
           norm2_g, w_group_router, b_group_router, w_expert_router, b_expert_router, w_gate, w_up, w_down, final_g):
    bsz, s_len, d = x.shape
    n_tok = bsz * s_len
    x2d = x.reshape(n_tok, d)

    w_route = jnp.concatenate([w_group_router[0], w_expert_router[0]], axis=1).astype(F32)
    w_route = jnp.pad(w_route, ((0, 0), (0, ROUTE_LANES - w_route.shape[1])))
    whi = w_route.astype(BF16)
    wlo = (w_route - whi.astype(F32)).astype(BF16)
    wr2 = jnp.concatenate([whi, wlo], axis=1)
    b_route = jnp.concatenate([b_group_router[0], b_expert_router[0]]).astype(F32)
    b_route = jnp.pad(b_route, (0, ROUTE_LANES - b_route.shape[0])).reshape(1, ROUTE_LANES)
    bias = _attn_bias_table(rel_bias)

    proj = _in_proj(x2d, norm1_g[0].reshape(1, d).astype(F32), w_in)
    proj3 = proj.reshape(bsz, s_len, IN_WIDTH)
    a = _swa(proj3, attn_sinks[0].astype(F32), bias)
    r = _hgrn2(proj3, hg_lb_logits.astype(F32), hg_norm_g[0].reshape(1, HG_DIM).astype(F32))
    x1, h2p, route, counts = _mix(
        a.reshape(n_tok, ATTN_WIDTH), r.reshape(n_tok, HG_WIDTH), proj, x2d,
        w_attn_branch[0].astype(BF16), w_hg_branch[0].astype(BF16), w_out[0].astype(BF16),
        norm2_g[0].reshape(1, d).astype(F32), wr2, b_route)
    tables, dest, n_rows = _moe_layout(route, counts)
    xs = _dispatch(dest.reshape(n_tok * TOP_K), h2p.reshape(n_tok, ROW_TILE, LANES), n_rows)
    y_sorted = _moe(*tables, xs.reshape(n_rows * ROW_TILE, LANES), w_gate, w_up, w_down)
    out = _final(dest, x1, route, final_g.reshape(1, d).astype(F32), y_sorted)
    return out.reshape(bsz, s_len, d)
```

```python
import math

import numpy as np
import jax
import jax.numpy as jnp
from jax import lax
from jax.experimental import pallas as pl
from jax.experimental.pallas import tpu as pltpu

D_MODEL = 2048
N_Q_HEADS = 16
N_KV_HEADS = 4
HEAD_DIM = 64
WINDOW = 128
ATTN_BLOCK = 128
ATTN_WIDTH = N_Q_HEADS * HEAD_DIM
KV_WIDTH = N_KV_HEADS * HEAD_DIM
REL_BUCKETS = 32
REL_MAX_DIST = 128
HG_HEADS = 8
HG_DIM = 128
HG_WIDTH = HG_HEADS * HG_DIM
HG_CHUNK = 64
HG_SUB = 16
N_GROUPS = 4
EXPERTS_PER_GROUP = 8
N_EXPERTS = N_GROUPS * EXPERTS_PER_GROUP
TOP_K = 2
EXPERT_FF = 512
RMS_EPS = 1e-6

OFF_GATE_A = 0
OFF_GATE_H = D_MODEL
OFF_QA = 2 * D_MODEL
OFF_QH = OFF_QA + ATTN_WIDTH
OFF_FH = OFF_QH + HG_WIDTH
OFF_IH = OFF_FH + HG_WIDTH
OFF_GH = OFF_IH + HG_WIDTH
OFF_KA = OFF_GH + HG_WIDTH
OFF_VA = OFF_KA + KV_WIDTH
IN_WIDTH = OFF_VA + KV_WIDTH
REF_OFF_QA = 0
REF_OFF_KA = ATTN_WIDTH
REF_OFF_QH = ATTN_WIDTH + 2 * KV_WIDTH
REF_OFF_GATE_A = REF_OFF_QH + 4 * HG_WIDTH

HG_SAFE_DECAY = 60.0

ROW_TILE = 8
LANES = 128
assert D_MODEL // 2 == ROW_TILE * LANES

IN_TM = 2048
IN_TN = 512
HG_TC = 256
MIX_TM = 256
ROUTE_LANES = 128
DISPATCH_ROWS = 1024
MOE_TM = 256
FIN_TM = 256

F32 = jnp.float32
BF16 = jnp.bfloat16
U32 = jnp.uint32
NEG_INF = float("-inf")


def _vmem(mib):
    return mib * 1024 * 1024


def _sigmoid(x):
    return 1.0 / (1.0 + jnp.exp(-x))


def _dot_nt(a, b):
    return lax.dot_general(a, b, (((1,), (1,)), ((), ())), preferred_element_type=F32)


def _dot_tn(a, b):
    return lax.dot_general(a, b, (((0,), (0,)), ((), ())), preferred_element_type=F32)


def _in_proj_kernel(x_ref, g_ref, w_ref, o_ref, h_ref):
    @pl.when(pl.program_id(1) == 0)
    def _():
        x = x_ref[...]
        ms = jnp.mean(x * x, axis=-1, keepdims=True)
        h_ref[...] = (x * lax.rsqrt(ms + RMS_EPS) * g_ref[...]).astype(BF16)

    o_ref[...] = jnp.dot(h_ref[...], w_ref[0].astype(BF16), preferred_element_type=F32).astype(o_ref.dtype)


def _in_proj(x2d, g, w_in):
    n = x2d.shape[0]
    n_col = IN_WIDTH // IN_TN

    def out_block(j):
        gates = j - REF_OFF_GATE_A // IN_TN + OFF_GATE_A // IN_TN
        q_a = j - REF_OFF_QA // IN_TN + OFF_QA // IN_TN
        kv = j - REF_OFF_KA // IN_TN + OFF_KA // IN_TN
        hg = j - REF_OFF_QH // IN_TN + OFF_QH // IN_TN
        return jnp.where(j >= REF_OFF_GATE_A // IN_TN, gates,
                         jnp.where(j >= REF_OFF_QH // IN_TN, hg, jnp.where(j >= REF_OFF_KA // IN_TN, kv, q_a)))

    return pl.pallas_call(
        _in_proj_kernel,
        grid=(n // IN_TM, n_col),
        in_specs=[
            pl.BlockSpec((IN_TM, D_MODEL), lambda i, j: (i, 0)),
            pl.BlockSpec((1, D_MODEL), lambda i, j: (0, 0)),
            pl.BlockSpec((1, D_MODEL, IN_TN), lambda i, j: (0, 0, j)),
        ],
        out_specs=pl.BlockSpec((IN_TM, IN_TN), lambda i, j: (i, out_block(j))),
        out_shape=jax.ShapeDtypeStruct((n, IN_WIDTH), BF16),
        scratch_shapes=[pltpu.VMEM((IN_TM, D_MODEL), BF16)],
        compiler_params=pltpu.CompilerParams(
            dimension_semantics=("parallel", "arbitrary"), vmem_limit_bytes=_vmem(58)),
        name="in_proj",
    )(x2d, g, w_in)


def _swa_kernel(sink_ref, q_ref, kp_ref, kc_ref, vp_ref, vc_ref, bias_ref, o_ref):
    has_prev = pl.program_id(1) > 0
    q = q_ref[0] * jnp.asarray(HEAD_DIM ** -0.5, BF16)
    kk = jnp.concatenate([kp_ref[0], kc_ref[0]], axis=0)
    vv = jnp.concatenate([vp_ref[0], vc_ref[0]], axis=0)
    key = lax.broadcasted_iota(jnp.int32, (ATTN_BLOCK, ATTN_BLOCK), 0)
    qry = lax.broadcasted_iota(jnp.int32, (ATTN_BLOCK, ATTN_BLOCK), 1)
    cur = key <= qry
    key_ok = jnp.logical_or(cur, has_prev)
    grp = N_Q_HEADS // N_KV_HEADS
    head = lambda t, h: t[:, h * HEAD_DIM:(h + 1) * HEAD_DIM]
    scores = [_dot_nt(head(kk, hq // grp), head(q, hq)) for hq in range(N_Q_HEADS)]
    probs, denoms = [], []
    for hq in range(N_Q_HEADS):
        s2 = scores[hq]
        s = jnp.where(cur, s2[ATTN_BLOCK:], s2[:ATTN_BLOCK]) + bias_ref[hq]
        s = jnp.where(key_ok, s, NEG_INF)
        sink = sink_ref[hq]
        m = jnp.maximum(jnp.max(s, axis=0, keepdims=True), sink)
        p = jnp.exp(s - m)
        denoms.append(jnp.sum(p, axis=0, keepdims=True) + jnp.exp(sink - m))
        probs.append(jnp.concatenate([jnp.where(cur, 0.0, p), jnp.where(cur, p, 0.0)], axis=0).astype(BF16))
    outs = [_dot_tn(head(vv, hq // grp), probs[hq]) / denoms[hq] for hq in range(N_Q_HEADS)]
    o_ref[...] = jnp.concatenate(outs, axis=0).astype(o_ref.dtype)


def _swa(proj3, sinks, bias):
    bsz, s_len, _ = proj3.shape
    nb = s_len // ATTN_BLOCK
    kblk = OFF_KA // KV_WIDTH
    vblk = OFF_VA // KV_WIDTH
    prev = lambda n: jnp.maximum(n - 1, 0)
    return pl.pallas_call(
        _swa_kernel,
        grid=(bsz, nb),
        in_specs=[
            pl.BlockSpec(memory_space=pltpu.SMEM),
            pl.BlockSpec((1, ATTN_BLOCK, ATTN_WIDTH), lambda b, n: (b, n, OFF_QA // ATTN_WIDTH)),
            pl.BlockSpec((1, ATTN_BLOCK, KV_WIDTH), lambda b, n: (b, prev(n), kblk)),
            pl.BlockSpec((1, ATTN_BLOCK, KV_WIDTH), lambda b, n: (b, n, kblk)),
            pl.BlockSpec((1, ATTN_BLOCK, KV_WIDTH), lambda b, n: (b, prev(n), vblk)),
            pl.BlockSpec((1, ATTN_BLOCK, KV_WIDTH), lambda b, n: (b, n, vblk)),
            pl.BlockSpec((N_Q_HEADS, ATTN_BLOCK, ATTN_BLOCK), lambda b, n: (0, 0, 0)),
        ],
        out_specs=pl.BlockSpec((ATTN_WIDTH, ATTN_BLOCK), lambda b, n: (0, b * nb + n)),
        out_shape=jax.ShapeDtypeStruct((ATTN_WIDTH, bsz * s_len), BF16),
        compiler_params=pltpu.CompilerParams(
            dimension_semantics=("parallel", "arbitrary"), vmem_limit_bytes=_vmem(32)),
        name="swa",
    )(sinks, proj3, proj3, proj3, proj3, proj3, bias)


def _t5_causal_bucket(n):
    max_exact = REL_BUCKETS // 2
    nf = jnp.maximum(n, 1).astype(F32)
    large = max_exact + (jnp.log(nf / max_exact) / math.log(REL_MAX_DIST / max_exact)
                         * (REL_BUCKETS - max_exact)).astype(jnp.int32)
    large = jnp.minimum(large, REL_BUCKETS - 1)
    return jnp.where(n < max_exact, n, large)


def _attn_bias_table(rel_bias):
    assert WINDOW == ATTN_BLOCK
    r = jnp.arange(ATTN_BLOCK)[None, :]
    c = jnp.arange(ATTN_BLOCK)[:, None]
    dist = jnp.where(c <= r, r - c, r + ATTN_BLOCK - c)
    bucket = _t5_causal_bucket(dist)
    onehot = (bucket[None, :, :] == jnp.arange(REL_BUCKETS)[:, None, None]).astype(F32)
    return jnp.einsum("bh,bqk->hqk", rel_bias.astype(F32), onehot, precision=lax.Precision.HIGHEST)


def _cumsum_rows(tri_bf16, g):
    width = g.shape[1]
    g1 = g.astype(BF16)
    r1 = g - g1.astype(F32)
    g2 = r1.astype(BF16)
    g3 = (r1 - g2.astype(F32)).astype(BF16)
    acc = jnp.dot(tri_bf16, jnp.concatenate([g3, g2, g1], axis=1), preferred_element_type=F32)
    return (acc[:, :width] + acc[:, width:2 * width]) + acc[:, 2 * width:]


def _hgrn_head_safe(q, kf, v, b, st):
    rows = lax.broadcasted_iota(jnp.int32, (HG_SUB, 1), 0)
    outs = []
    prev_end = jnp.zeros((1, HG_DIM), F32)
    for i in range(HG_CHUNK // HG_SUB):
        sl = slice(i * HG_SUB, (i + 1) * HG_SUB)
        bl = b[sl] - prev_end
        qi, ki, vi = q[sl], kf[sl], v[sl]
        vif = vi.astype(F32)
        blast = bl[HG_SUB - 1:HG_SUB]
        o = _dot_nt((qi * jnp.exp(bl)).astype(BF16), st.astype(BF16))
        for s in range(HG_SUB):
            d = jnp.exp(jnp.minimum(bl - bl[s:s + 1], 0.0))
            a = jnp.sum(qi * ki[s:s + 1] * d, axis=-1, keepdims=True)
            o = o + jnp.where(rows >= s, a, 0.0) * vif[s:s + 1]
        kd = (ki * jnp.exp(blast - bl)).astype(BF16)
        st = st * jnp.exp(blast) + _dot_tn(vi, kd)
        prev_end = b[(i + 1) * HG_SUB - 1:(i + 1) * HG_SUB]
        outs.append(o)
    return jnp.concatenate(outs, axis=0), st


def _hgrn_kernel(q_ref, f_ref, i_ref, g_ref, lbl_ref, gn_ref, o_ref, st_ref, kf_ref, b_ref):
    @pl.when(pl.program_id(1) == 0)
    def _():
        st_ref[...] = jnp.zeros_like(st_ref)

    lg2 = lbl_ref[...].astype(F32)
    e = jnp.exp(lg2 - jnp.max(lg2, axis=0, keepdims=True))
    lb = e[0:1] / jnp.sum(e, axis=0, keepdims=True)
    gn = gn_ref[...].astype(F32)
    r = lax.broadcasted_iota(jnp.int32, (HG_CHUNK, HG_CHUNK), 0)
    c = lax.broadcasted_iota(jnp.int32, (HG_CHUNK, HG_CHUNK), 1)
    tri_mask = r >= c
    tri = jnp.where(tri_mask, 1.0, 0.0).astype(BF16)

    n_chunks = HG_TC // HG_CHUNK

    f = lb + (1.0 - lb) * _sigmoid(f_ref[0].astype(F32))
    kf_ref[...] = 1.0 - f
    logf = jnp.log(f)
    worst = None
    for ci in range(n_chunks):
        sl = slice(ci * HG_CHUNK, (ci + 1) * HG_CHUNK)
        b = _cumsum_rows(tri, logf[sl])
        b_ref[sl, :] = b
        bend = b[HG_CHUNK - 1:HG_CHUNK]
        worst = bend if worst is None else jnp.minimum(worst, bend)

    heads = range(HG_HEADS)
    head = lambda t, h: t[:, h * HG_DIM:(h + 1) * HG_DIM]

    def finish_all(sl, outs):
        scale = [lax.rsqrt(jnp.mean(o * o, axis=-1, keepdims=True) + RMS_EPS) for o in outs]
        normed = jnp.concatenate([outs[h] * scale[h] * gn for h in heads], axis=1)
        go = g_ref[0, sl, :].astype(F32)
        o_ref[0, sl, :] = (normed * (go * _sigmoid(go))).astype(o_ref.dtype)

    def fast():
        states = [st_ref[h] for h in heads]
        for ci in range(n_chunks):
            sl = slice(ci * HG_CHUNK, (ci + 1) * HG_CHUNK)
            b = b_ref[sl, :]
            qp = (q_ref[0, sl, :].astype(F32) * jnp.exp(b)).astype(BF16)
            kp = kf_ref[sl, :] * jnp.exp(-b)
            kpb = kp.astype(BF16)
            eb = jnp.exp(b[HG_CHUNK - 1:HG_CHUNK])
            kd = (kp * eb).astype(BF16)
            v = i_ref[0, sl, :]
            att = [_dot_nt(head(qp, h), head(kpb, h)) for h in heads]
            inter = [_dot_nt(head(qp, h), states[h].astype(BF16)) for h in heads]
            update = [_dot_tn(head(v, h), head(kd, h)) for h in heads]
            att = [jnp.where(tri_mask, a, 0.0).astype(BF16) for a in att]
            outs = [inter[h] + jnp.dot(att[h], head(v, h), preferred_element_type=F32) for h in heads]
            states = [states[h] * head(eb, h) + update[h] for h in heads]
            finish_all(sl, outs)
        for h in heads:
            st_ref[h] = states[h]

    def safe():
        def chunk(ci, carry):
            sl = pl.ds(pl.multiple_of(ci * HG_CHUNK, HG_CHUNK), HG_CHUNK)
            outs = []
            for h in heads:
                hs = slice(h * HG_DIM, (h + 1) * HG_DIM)
                o, st_new = _hgrn_head_safe(q_ref[0, sl, hs].astype(F32), kf_ref[sl, hs], i_ref[0, sl, hs],
                                            b_ref[sl, hs], st_ref[h])
                st_ref[h] = st_new
                outs.append(o)
            finish_all(sl, outs)
            return carry

        lax.fori_loop(0, n_chunks, chunk, 0)

    lax.cond(jnp.min(worst) >= -HG_SAFE_DECAY, fast, safe)


def _hgrn2(proj3, lb_logits, gn):
    bsz, s_len, _ = proj3.shape
    blk = lambda off: (lambda b, t: (b, t, off // HG_WIDTH))
    return pl.pallas_call(
        _hgrn_kernel,
        grid=(bsz, s_len // HG_TC),
        in_specs=[
            pl.BlockSpec((1, HG_TC, HG_WIDTH), blk(OFF_QH)),
            pl.BlockSpec((1, HG_TC, HG_WIDTH), blk(OFF_FH)),
            pl.BlockSpec((1, HG_TC, HG_WIDTH), blk(OFF_IH)),
            pl.BlockSpec((1, HG_TC, HG_WIDTH), blk(OFF_GH)),
            pl.BlockSpec((lb_logits.shape[0], HG_WIDTH), lambda b, t: (0, 0)),
            pl.BlockSpec((1, HG_DIM), lambda b, t: (0, 0)),
        ],
        out_specs=pl.BlockSpec((1, HG_TC, HG_WIDTH), lambda b, t: (b, t, 0)),
        out_shape=jax.ShapeDtypeStruct((bsz, s_len, HG_WIDTH), BF16),
        scratch_shapes=[
            pltpu.VMEM((HG_HEADS, HG_DIM, HG_DIM), F32),
            pltpu.VMEM((HG_TC, HG_WIDTH), F32),
            pltpu.VMEM((HG_TC, HG_WIDTH), F32),
        ],
        compiler_params=pltpu.CompilerParams(
            dimension_semantics=("parallel", "arbitrary"), vmem_limit_bytes=_vmem(32)),
        name="hgrn2",
    )(proj3, proj3, proj3, proj3, lb_logits, gn)


def _store_rows_as_tiles(ref, val):
    rows = val.shape[0]
    for s in range(ROW_TILE):
        ref[pl.ds(s, rows, stride=ROW_TILE), :] = val[:, s * LANES:(s + 1) * LANES]


def _load_rows_from_tiles(ref, start, rows):
    return jnp.concatenate(
        [ref[pl.ds(start * ROW_TILE + s, rows, stride=ROW_TILE), :] for s in range(ROW_TILE)], axis=1)


def _pack_bf16_pairs(x):
    c = x.shape[1] // 2
    bits = pltpu.bitcast(x.astype(BF16).astype(F32), U32)
    return jnp.bitwise_or(jnp.right_shift(bits[:, :c], jnp.uint32(16)), bits[:, c:])


def _unpack_bf16_pairs(p):
    lo = pltpu.bitcast(jnp.left_shift(p, jnp.uint32(16)), F32)
    hi = pltpu.bitcast(jnp.bitwise_and(p, jnp.uint32(0xFFFF0000)), F32)
    return lo, hi


def _mix_kernel(a_ref, r_ref, ga_ref, gh_ref, x_ref, wa_ref, wr_ref, wo_ref, g2_ref, wr2_ref, br_ref,
                x1_ref, h2p_ref, route_ref, cnt_out_ref, cnt_ref):
    @pl.when(pl.program_id(0) == 0)
    def _():
        cnt_ref[...] = jnp.zeros_like(cnt_ref)

    am = _dot_tn(a_ref[...], wa_ref[...])
    rm = jnp.dot(r_ref[...], wr_ref[...], preferred_element_type=F32)
    mixed = _sigmoid(ga_ref[...].astype(F32)) * am + _sigmoid(gh_ref[...].astype(F32)) * rm
    x1 = x_ref[...] + jnp.dot(mixed.astype(BF16), wo_ref[...], preferred_element_type=F32)
    x1_ref[...] = x1
    h2 = x1 * lax.rsqrt(jnp.mean(x1 * x1, axis=-1, keepdims=True) + RMS_EPS) * g2_ref[...]

    _store_rows_as_tiles(h2p_ref, _pack_bf16_pairs(h2))

    hi = h2.astype(BF16)
    lo = (h2 - hi.astype(F32)).astype(BF16)
    both = jnp.dot(hi, wr2_ref[...], preferred_element_type=F32)
    logits = jnp.dot(lo, wr2_ref[:, :ROUTE_LANES], preferred_element_type=F32) + both[:, ROUTE_LANES:]
    logits = logits + both[:, :ROUTE_LANES] + br_ref[...]

    lane = lax.broadcasted_iota(jnp.int32, logits.shape, 1)
    lane_f = lane.astype(F32)
    big = float(ROUTE_LANES)
    gl = jnp.where(lane < N_GROUPS, logits, NEG_INF)
    gmax = jnp.max(gl, axis=-1, keepdims=True)
    g_idx = jnp.min(jnp.where(gl == gmax, lane_f, big), axis=-1, keepdims=True)
    g_w = 1.0 / jnp.sum(jnp.exp(gl - gmax), axis=-1, keepdims=True)
    e_lane = lane - N_GROUPS
    in_group = jnp.logical_and(e_lane >= 0, e_lane < N_EXPERTS)
    lane_group = jnp.right_shift(e_lane, 3).astype(F32)
    in_group = jnp.logical_and(in_group, lane_group == g_idx)
    el = jnp.where(in_group, logits, NEG_INF)
    m1 = jnp.max(el, axis=-1, keepdims=True)
    i1 = jnp.min(jnp.where(el == m1, lane_f, big), axis=-1, keepdims=True)
    el2 = jnp.where(lane_f == i1, NEG_INF, el)
    m2 = jnp.max(el2, axis=-1, keepdims=True)
    i2 = jnp.min(jnp.where(el2 == m2, lane_f, big), axis=-1, keepdims=True)
    p2 = jnp.exp(m2 - m1)
    w1 = g_w / (1.0 + p2)
    w2 = g_w * p2 / (1.0 + p2)
    e1 = i1 - N_GROUPS
    e2 = i2 - N_GROUPS

    tm = logits.shape[0]
    rr = lax.broadcasted_iota(jnp.int32, (tm, tm), 0)
    cc = lax.broadcasted_iota(jnp.int32, (tm, tm), 1)
    ltri = jnp.where(rr > cc, 1.0, 0.0).astype(BF16)
    sel1 = lane_f == e1
    sel2 = lane_f == e2
    oh1 = jnp.where(sel1, 1.0, 0.0)
    oh2 = jnp.where(sel2, 1.0, 0.0)
    base = cnt_ref[...]
    c1 = jnp.sum(oh1, axis=0, keepdims=True)
    c2 = jnp.sum(oh2, axis=0, keepdims=True)
    pre1 = jnp.dot(ltri, oh1.astype(BF16), preferred_element_type=F32) + base
    pre2 = jnp.dot(ltri, oh2.astype(BF16), preferred_element_type=F32) + (base + c1)
    rank1 = jnp.sum(jnp.where(sel1, pre1, 0.0), axis=-1, keepdims=True)
    rank2 = jnp.sum(jnp.where(sel2, pre2, 0.0), axis=-1, keepdims=True)
    total = base + c1 + c2
    cnt_ref[...] = total
    cnt_out_ref[...] = total

    route = jnp.zeros_like(logits)
    for k, val in enumerate((w1, w2, e1, e2, rank1, rank2)):
        route = jnp.where(lane == k, val, route)
    route_ref[...] = route


def _mix(a_t, r2d, proj, x2d, wa, wr, wo, g2, wr2, br):
    n = x2d.shape[0]
    row = lambda i: (i, 0)
    const = lambda i: (0, 0)
    return pl.pallas_call(
        _mix_kernel,
        grid=(n // MIX_TM,),
        in_specs=[
            pl.BlockSpec((ATTN_WIDTH, MIX_TM), lambda i: (0, i)),
            pl.BlockSpec((MIX_TM, HG_WIDTH), row),
            pl.BlockSpec((MIX_TM, D_MODEL), lambda i: (i, OFF_GATE_A // D_MODEL)),
            pl.BlockSpec((MIX_TM, D_MODEL), lambda i: (i, OFF_GATE_H // D_MODEL)),
            pl.BlockSpec((MIX_TM, D_MODEL), row),
            pl.BlockSpec((ATTN_WIDTH, D_MODEL), const),
            pl.BlockSpec((HG_WIDTH, D_MODEL), const),
            pl.BlockSpec((D_MODEL, D_MODEL), const),
            pl.BlockSpec((1, D_MODEL), const),
            pl.BlockSpec((D_MODEL, 2 * ROUTE_LANES), const),
            pl.BlockSpec((1, ROUTE_LANES), const),
        ],
        out_specs=[
            pl.BlockSpec((MIX_TM, D_MODEL), row),
            pl.BlockSpec((MIX_TM * ROW_TILE, LANES), row),
            pl.BlockSpec((MIX_TM, ROUTE_LANES), row),
            pl.BlockSpec((1, ROUTE_LANES), const),
        ],
        out_shape=[
            jax.ShapeDtypeStruct((n, D_MODEL), F32),
            jax.ShapeDtypeStruct((n * ROW_TILE, LANES), U32),
            jax.ShapeDtypeStruct((n, ROUTE_LANES), F32),
            jax.ShapeDtypeStruct((1, ROUTE_LANES), F32),
        ],
        scratch_shapes=[pltpu.VMEM((1, ROUTE_LANES), F32)],
        compiler_params=pltpu.CompilerParams(dimension_semantics=("arbitrary",), vmem_limit_bytes=_vmem(56)),
        name="mix",
    )(a_t, r2d, proj, proj, x2d, wa, wr, wo, g2, wr2, br)


def _dispatch_kernel(dest_ref, src_ref, zeros_hbm, dst_hbm, sem):
    del zeros_hbm

    for a in range(DISPATCH_ROWS):
        pltpu.make_async_copy(src_ref.at[a // TOP_K], dst_hbm.at[dest_ref[0, 0, a]], sem.at[0]).start(priority=a % 2)
    for _ in range(TOP_K):
        pltpu.make_async_copy(src_ref, dst_hbm.at[pl.ds(0, DISPATCH_ROWS // TOP_K)], sem.at[0]).wait()


def _dispatch(dest_flat, h2p_tiles, n_rows):
    n_asg = dest_flat.shape[0]
    steps = n_asg // DISPATCH_ROWS
    return pl.pallas_call(
        _dispatch_kernel,
        grid=(steps,),
        in_specs=[
            pl.BlockSpec((1, 1, DISPATCH_ROWS), lambda s: (s, 0, 0), memory_space=pltpu.SMEM),
            pl.BlockSpec((DISPATCH_ROWS // TOP_K, ROW_TILE, LANES), lambda s: (s, 0, 0)),
            pl.BlockSpec(memory_space=pl.ANY),
        ],
        out_specs=pl.BlockSpec(memory_space=pl.ANY),
        out_shape=jax.ShapeDtypeStruct((n_rows, ROW_TILE, LANES), U32),
        input_output_aliases={2: 0},
        scratch_shapes=[pltpu.SemaphoreType.DMA((1,))],
        compiler_params=pltpu.CompilerParams(dimension_semantics=("arbitrary",)),
        name="dispatch",
    )(dest_flat.reshape(steps, 1, DISPATCH_ROWS), h2p_tiles, jnp.zeros((n_rows, ROW_TILE, LANES), U32))


def _moe_kernel(be_ref, nused_ref, first_ref, slot_ref, next_ref, xs_ref, wg_hbm, wu_hbm, wd_hbm, y_ref,
                wg_buf, wu_buf, wd_buf, wg_s, wu_s, wd_s, sem):
    b = pl.program_id(0)
    live = b < nused_ref[0]

    def weight_copies(e, s):
        return (pltpu.make_async_copy(wg_hbm.at[0, e], wg_buf.at[s], sem.at[s, 0]),
                pltpu.make_async_copy(wu_hbm.at[0, e], wu_buf.at[s], sem.at[s, 1]),
                pltpu.make_async_copy(wd_hbm.at[0, e], wd_buf.at[s], sem.at[s, 2]))

    @pl.when(b == 0)
    def _():
        for c in weight_copies(be_ref[0], 0):
            c.start()

    @pl.when(jnp.logical_and(live, first_ref[b] == 1))
    def _():
        s = slot_ref[b]
        for c in weight_copies(be_ref[b], s):
            c.wait()
        wg_s[...] = wg_buf[s].astype(BF16)
        wu_s[...] = wu_buf[s].astype(BF16)
        wd_s[...] = wd_buf[s].astype(BF16)

        @pl.when(next_ref[b] >= 0)
        def _():
            for c in weight_copies(next_ref[b], 1 - s):
                c.start()

    @pl.when(jnp.logical_not(live))
    def _():
        y_ref[...] = jnp.zeros_like(y_ref)

    @pl.when(live)
    def _():
        xlo, xhi = _unpack_bf16_pairs(_load_rows_from_tiles(xs_ref, 0, MOE_TM))
        xb = jnp.concatenate([xlo.astype(BF16), xhi.astype(BF16)], axis=1)
        hg = jnp.dot(xb, wg_s[...], preferred_element_type=F32)
        hu = jnp.dot(xb, wu_s[...], preferred_element_type=F32)
        hb = (hg * _sigmoid(hg) * hu).astype(BF16)
        y = jnp.dot(hb, wd_s[...], preferred_element_type=F32)
        _store_rows_as_tiles(y_ref, _pack_bf16_pairs(y))


def _moe(block_e, n_used, first, slot, next_e, xs, w_gate, w_up, w_down):
    n_rows = xs.shape[0] // ROW_TILE
    n_blocks = n_rows // MOE_TM
    grid_spec = pltpu.PrefetchScalarGridSpec(
        num_scalar_prefetch=5,
        grid=(n_blocks,),
        in_specs=[
            pl.BlockSpec((MOE_TM * ROW_TILE, LANES), lambda b, be, nu, *_: (jnp.minimum(b, nu[0] - 1), 0)),
            pl.BlockSpec(memory_space=pl.ANY),
            pl.BlockSpec(memory_space=pl.ANY),
            pl.BlockSpec(memory_space=pl.ANY),
        ],
        out_specs=pl.BlockSpec((MOE_TM * ROW_TILE, LANES), lambda b, *_: (b, 0)),
        scratch_shapes=[
            pltpu.VMEM((2, D_MODEL, EXPERT_FF), F32),
            pltpu.VMEM((2, D_MODEL, EXPERT_FF), F32),
            pltpu.VMEM((2, EXPERT_FF, D_MODEL), F32),
            pltpu.VMEM((D_MODEL, EXPERT_FF), BF16),
            pltpu.VMEM((D_MODEL, EXPERT_FF), BF16),
            pltpu.VMEM((EXPERT_FF, D_MODEL), BF16),
            pltpu.SemaphoreType.DMA((2, 3)),
        ],
    )
    return pl.pallas_call(
        _moe_kernel,
        grid_spec=grid_spec,
        out_shape=jax.ShapeDtypeStruct((n_rows * ROW_TILE, LANES), U32),
        compiler_params=pltpu.CompilerParams(dimension_semantics=("arbitrary",), vmem_limit_bytes=_vmem(52)),
        name="moe",
    )(block_e, n_used, first, slot, next_e, xs, w_gate, w_up, w_down)


def _moe_layout(route, counts_f):
    counts = counts_f[0, :N_EXPERTS].astype(jnp.int32)
    padded = (counts + MOE_TM - 1) // MOE_TM * MOE_TM
    pends = jnp.cumsum(padded)
    pstarts = pends - padded
    n_tok = route.shape[0]
    n_blocks = n_tok * TOP_K // MOE_TM + N_EXPERTS
    blk_start = jnp.arange(n_blocks, dtype=jnp.int32) * MOE_TM
    block_e = jnp.minimum(jnp.sum((blk_start[:, None] >= pends[None, :]).astype(jnp.int32), axis=1), N_EXPERTS - 1)
    n_used = (pends[-1] // MOE_TM).astype(jnp.int32).reshape(1)
    blk = jnp.arange(n_blocks, dtype=jnp.int32)
    first = jnp.logical_and(blk < n_used[0], jnp.logical_or(blk == 0, block_e != jnp.roll(block_e, 1)))
    slot = jnp.bitwise_and(jnp.cumsum(first.astype(jnp.int32)) - 1, 1)
    ex = jnp.arange(N_EXPERTS, dtype=jnp.int32)
    later = jnp.logical_and(counts[None, :] > 0, ex[None, :] > ex[:, None])
    next_of_expert = jnp.min(jnp.where(later, ex[None, :], N_EXPERTS), axis=1)
    next_of_expert = jnp.where(next_of_expert == N_EXPERTS, -1, next_of_expert)
    next_e = jnp.sum(jnp.where(block_e[:, None] == ex[None, :], next_of_expert[None, :], 0), axis=1)
    e_id = route[:, 2:2 + TOP_K].astype(jnp.int32)
    rank = route[:, 2 + TOP_K:2 + 2 * TOP_K].astype(jnp.int32)
    onehot_e = e_id[:, :, None] == jnp.arange(N_EXPERTS, dtype=jnp.int32)[None, None, :]
    dest = jnp.sum(jnp.where(onehot_e, pstarts[None, None, :], 0), axis=2) + rank
    tables = (block_e.astype(jnp.int32), n_used, first.astype(jnp.int32), slot.astype(jnp.int32),
              next_e.astype(jnp.int32))
    return tables, dest, n_blocks * MOE_TM


def _final_kernel(dest_ref, destn_ref, x1_ref, route_ref, g_ref, y_hbm, y_flat_hbm, o_ref, ybuf, sem):
    i = pl.program_id(0)
    n_steps = pl.num_programs(0)
    slot = lax.rem(i, 2)
    n_rows = TOP_K * FIN_TM

    def start_gather(idx_ref, s):
        for r in range(n_rows):
            pltpu.make_async_copy(y_hbm.at[idx_ref[0, 0, r]], ybuf.at[pl.ds((s * n_rows + r) * ROW_TILE, ROW_TILE)],
                                  sem.at[s]).start(priority=r % 2)

    @pl.when(i == 0)
    def _():
        start_gather(dest_ref, 0)

    for s in range(2):
        @pl.when(jnp.logical_and(i + 1 < n_steps, 1 - slot == s))
        def _():
            start_gather(destn_ref, s)

    slot_rows = n_rows * ROW_TILE
    slot_start = pl.multiple_of(slot * slot_rows, slot_rows)
    pltpu.make_async_copy(y_flat_hbm.at[pl.ds(0, slot_rows)], ybuf.at[pl.ds(slot_start, slot_rows)],
                          sem.at[slot]).wait()
    route = route_ref[...]
    x = x1_ref[...]
    for k in range(TOP_K):
        ylo, yhi = _unpack_bf16_pairs(_load_rows_from_tiles(ybuf, slot * n_rows + k * FIN_TM, FIN_TM))
        x = x + route[:, k:k + 1] * jnp.concatenate([ylo, yhi], axis=1)
    o_ref[...] = x * lax.rsqrt(jnp.mean(x * x, axis=-1, keepdims=True) + RMS_EPS) * g_ref[...]


def _final(dest, x1, route, g, y_sorted):
    n = x1.shape[0]
    steps = n // FIN_TM
    dest3 = dest.reshape(steps, FIN_TM, TOP_K).transpose(0, 2, 1).reshape(steps, 1, TOP_K * FIN_TM)
    return pl.pallas_call(
        _final_kernel,
        grid=(steps,),
        in_specs=[
            pl.BlockSpec((1, 1, TOP_K * FIN_TM), lambda i: (i, 0, 0), memory_space=pltpu.SMEM),
            pl.BlockSpec((1, 1, TOP_K * FIN_TM), lambda i: (jnp.minimum(i + 1, steps - 1), 0, 0),
                         memory_space=pltpu.SMEM),
            pl.BlockSpec((FIN_TM, D_MODEL), lambda i: (i, 0)),
            pl.BlockSpec((FIN_TM, ROUTE_LANES), lambda i: (i, 0)),
            pl.BlockSpec((1, D_MODEL), lambda i: (0, 0)),
            pl.BlockSpec(memory_space=pl.ANY),
            pl.BlockSpec(memory_space=pl.ANY),
        ],
        out_specs=pl.BlockSpec((FIN_TM, D_MODEL), lambda i: (i, 0)),
        out_shape=jax.ShapeDtypeStruct((n, D_MODEL), F32),
        scratch_shapes=[
            pltpu.VMEM((2 * TOP_K * FIN_TM * ROW_TILE, LANES), U32),
            pltpu.SemaphoreType.DMA((2,)),
        ],
        compiler_params=pltpu.CompilerParams(dimension_semantics=("arbitrary",), vmem_limit_bytes=_vmem(40)),
        name="final",
    )(dest3, dest3, x1, route, g, y_sorted.reshape(-1, ROW_TILE, LANES), y_sorted)


def kernel(x, norm1_g, w_in, attn_sinks, rel_bias, hg_lb_logits, hg_norm_g, w_attn_branch, w_hg_branch, w_out,
           norm2_g, w_group_router, b_group_router, w_expert_router, b_expert_router, w_gate, w_up, w_down, final_g):
    bsz, s_len, d = x.shape
    n_tok = bsz * s_len
    x2d = x.reshape(n_tok, d)

    w_route = jnp.concatenate([w_group_router[0], w_expert_router[0]], axis=1).astype(F32)
    w_route = jnp.pad(w_route, ((0, 0), (0, ROUTE_LANES - w_route.shape[1])))
    whi = w_route.astype(BF16)
    wlo = (w_route - whi.astype(F32)).astype(BF16)
    wr2 = jnp.concatenate([whi, wlo], axis=1)
    b_route = jnp.concatenate([b_group_router[0], b_expert_router[0]]).astype(F32)
    b_route = jnp.pad(b_route, (0, ROUTE_LANES - b_route.shape[0])).reshape(1, ROUTE_LANES)
    bias = _attn_bias_table(rel_bias)

    proj = _in_proj(x2d, norm1_g[0].reshape(1, d).astype(F32), w_in)
    proj3 = proj.reshape(bsz, s_len, IN_WIDTH)
    a = _swa(proj3, attn_sinks[0].astype(F32), bias)
    r = _hgrn2(proj3, hg_lb_logits.astype(F32), hg_norm_g[0].reshape(1, HG_DIM).astype(F32))
    x1, h2p, route, counts = _mix(
        a, r.reshape(n_tok, HG_WIDTH), proj, x2d,
        w_attn_branch[0].astype(BF16), w_hg_branch[0].astype(BF16), w_out[0].astype(BF16),
        norm2_g[0].reshape(1, d).astype(F32), wr2, b_route)
    tables, dest, n_rows = _moe_layout(route, counts)
    xs = _dispatch(dest.reshape(n_tok * TOP_K), h2p.reshape(n_tok, ROW_TILE, LANES), n_rows)
    y_sorted = _moe(*tables, xs.reshape(n_rows * ROW_TILE, LANES), w_gate, w_up, w_down)
    out = _final(dest, x1, route, final_g.reshape(1, d).astype(F32), y_sorted)
    return out.reshape(bsz, s_len, d)
```

```python
import functools
import math

import numpy as np
import jax
import jax.numpy as jnp
from jax import lax
from jax.experimental import pallas as pl
from jax.experimental.pallas import tpu as pltpu

D_MODEL = 2048
N_Q_HEADS = 16
N_KV_HEADS = 4
HEAD_DIM = 64
WINDOW = 128
ATTN_BLOCK = 128
ATTN_WIDTH = N_Q_HEADS * HEAD_DIM
KV_WIDTH = N_KV_HEADS * HEAD_DIM
REL_BUCKETS = 32
REL_MAX_DIST = 128
HG_HEADS = 8
HG_DIM = 128
HG_WIDTH = HG_HEADS * HG_DIM
HG_CHUNK = 64
HG_SUB = 16
N_GROUPS = 4
EXPERTS_PER_GROUP = 8
N_EXPERTS = N_GROUPS * EXPERTS_PER_GROUP
TOP_K = 2
EXPERT_FF = 512
RMS_EPS = 1e-6

OFF_GATE_A = 0
OFF_GATE_H = D_MODEL
OFF_QA = 2 * D_MODEL
OFF_QH = OFF_QA + ATTN_WIDTH
OFF_FH = OFF_QH + HG_WIDTH
OFF_IH = OFF_FH + HG_WIDTH
OFF_GH = OFF_IH + HG_WIDTH
OFF_KA = OFF_GH + HG_WIDTH
OFF_VA = OFF_KA + KV_WIDTH
IN_WIDTH = OFF_VA + KV_WIDTH
REF_OFF_QA = 0
REF_OFF_KA = ATTN_WIDTH
REF_OFF_QH = ATTN_WIDTH + 2 * KV_WIDTH
REF_OFF_GATE_A = REF_OFF_QH + 4 * HG_WIDTH

HG_SAFE_DECAY = 60.0

ROW_TILE = 8
LANES = 128
assert D_MODEL // 2 == ROW_TILE * LANES

IN_TM = 2048
IN_TN = 512
HG_TC = 256
MIX_TM = 256
ROUTE_LANES = 128
DISPATCH_ROWS = 1024
MOE_TM = 256
FIN_TM = 256

F32 = jnp.float32
BF16 = jnp.bfloat16
U32 = jnp.uint32
NEG_INF = float("-inf")


def _vmem(mib):
    return mib * 1024 * 1024


def _sigmoid(x):
    return 1.0 / (1.0 + jnp.exp(-x))


def _dot_nt(a, b):
    return lax.dot_general(a, b, (((1,), (1,)), ((), ())), preferred_element_type=F32)


def _dot_tn(a, b):
    return lax.dot_general(a, b, (((0,), (0,)), ((), ())), preferred_element_type=F32)


def _in_proj_kernel(x_ref, g_ref, w_ref, o_ref, h_ref):
    @pl.when(pl.program_id(1) == 0)
    def _():
        x = x_ref[...]
        ms = jnp.mean(x * x, axis=-1, keepdims=True)
        h_ref[...] = (x * lax.rsqrt(ms + RMS_EPS) * g_ref[...]).astype(BF16)

    o_ref[...] = jnp.dot(h_ref[...], w_ref[0].astype(BF16), preferred_element_type=F32).astype(o_ref.dtype)


def _in_proj(x2d, g, w_in):
    n = x2d.shape[0]
    n_col = IN_WIDTH // IN_TN

    def out_block(j):
        gates = j - REF_OFF_GATE_A // IN_TN + OFF_GATE_A // IN_TN
        q_a = j - REF_OFF_QA // IN_TN + OFF_QA // IN_TN
        kv = j - REF_OFF_KA // IN_TN + OFF_KA // IN_TN
        hg = j - REF_OFF_QH // IN_TN + OFF_QH // IN_TN
        return jnp.where(j >= REF_OFF_GATE_A // IN_TN, gates,
                         jnp.where(j >= REF_OFF_QH // IN_TN, hg, jnp.where(j >= REF_OFF_KA // IN_TN, kv, q_a)))

    return pl.pallas_call(
        _in_proj_kernel,
        grid=(n // IN_TM, n_col),
        in_specs=[
            pl.BlockSpec((IN_TM, D_MODEL), lambda i, j: (i, 0)),
            pl.BlockSpec((1, D_MODEL), lambda i, j: (0, 0)),
            pl.BlockSpec((1, D_MODEL, IN_TN), lambda i, j: (0, 0, j)),
        ],
        out_specs=pl.BlockSpec((IN_TM, IN_TN), lambda i, j: (i, out_block(j))),
        out_shape=jax.ShapeDtypeStruct((n, IN_WIDTH), BF16),
        scratch_shapes=[pltpu.VMEM((IN_TM, D_MODEL), BF16)],
        compiler_params=pltpu.CompilerParams(
            dimension_semantics=("parallel", "arbitrary"), vmem_limit_bytes=_vmem(58)),
        name="in_proj",
    )(x2d, g, w_in)


def _swa_block(q, kk, vv, has_prev, bias_ref, sink_ref):
    q = q * jnp.asarray(HEAD_DIM ** -0.5, BF16)
    key = lax.broadcasted_iota(jnp.int32, (ATTN_BLOCK, ATTN_BLOCK), 0)
    qry = lax.broadcasted_iota(jnp.int32, (ATTN_BLOCK, ATTN_BLOCK), 1)
    cur = key <= qry
    key_ok = jnp.logical_or(cur, has_prev)
    grp = N_Q_HEADS // N_KV_HEADS
    head = lambda t, h: t[:, h * HEAD_DIM:(h + 1) * HEAD_DIM]
    lanes = lambda t, g: t[:, g * ATTN_BLOCK:(g + 1) * ATTN_BLOCK]
    scores = []
    for hk in range(N_KV_HEADS):
        q_grp = jnp.concatenate([head(q, hk * grp + g) for g in range(grp)], axis=0)
        scores.append(_dot_nt(head(kk, hk), q_grp))
    probs, denoms = [], []
    for hq in range(N_Q_HEADS):
        s2 = lanes(scores[hq // grp], hq % grp)
        s = jnp.where(cur, s2[ATTN_BLOCK:], s2[:ATTN_BLOCK]) + bias_ref[hq]
        s = jnp.where(key_ok, s, NEG_INF)
        sink = sink_ref[hq]
        m = jnp.maximum(jnp.max(s, axis=0, keepdims=True), sink)
        p = jnp.exp(s - m)
        denoms.append(jnp.sum(p, axis=0, keepdims=True) + jnp.exp(sink - m))
        probs.append(jnp.concatenate([jnp.where(cur, 0.0, p), jnp.where(cur, p, 0.0)], axis=0).astype(BF16))
    outs = []
    for hk in range(N_KV_HEADS):
        p_grp = jnp.concatenate(probs[hk * grp:(hk + 1) * grp], axis=1)
        o_grp = _dot_tn(head(vv, hk), p_grp)
        outs += [lanes(o_grp, g) / denoms[hk * grp + g] for g in range(grp)]
    return jnp.concatenate(outs, axis=0)


def _t5_causal_bucket(n):
    max_exact = REL_BUCKETS // 2
    nf = jnp.maximum(n, 1).astype(F32)
    large = max_exact + (jnp.log(nf / max_exact) / math.log(REL_MAX_DIST / max_exact)
                         * (REL_BUCKETS - max_exact)).astype(jnp.int32)
    large = jnp.minimum(large, REL_BUCKETS - 1)
    return jnp.where(n < max_exact, n, large)


def _attn_bias_table(rel_bias):
    assert WINDOW == ATTN_BLOCK
    r = jnp.arange(ATTN_BLOCK)[None, :]
    c = jnp.arange(ATTN_BLOCK)[:, None]
    dist = jnp.where(c <= r, r - c, r + ATTN_BLOCK - c)
    bucket = _t5_causal_bucket(dist)
    onehot = (bucket[None, :, :] == jnp.arange(REL_BUCKETS)[:, None, None]).astype(F32)
    return jnp.einsum("bh,bqk->hqk", rel_bias.astype(F32), onehot, precision=lax.Precision.HIGHEST)


def _cumsum_rows(tri_bf16, g):
    width = g.shape[1]
    g1 = g.astype(BF16)
    r1 = g - g1.astype(F32)
    g2 = r1.astype(BF16)
    g3 = (r1 - g2.astype(F32)).astype(BF16)
    acc = jnp.dot(tri_bf16, jnp.concatenate([g3, g2, g1], axis=1), preferred_element_type=F32)
    return (acc[:, :width] + acc[:, width:2 * width]) + acc[:, 2 * width:]


def _hgrn_head_safe(q, kf, v, b, st):
    rows = lax.broadcasted_iota(jnp.int32, (HG_SUB, 1), 0)
    outs = []
    prev_end = jnp.zeros((1, HG_DIM), F32)
    for i in range(HG_CHUNK // HG_SUB):
        sl = slice(i * HG_SUB, (i + 1) * HG_SUB)
        bl = b[sl] - prev_end
        qi, ki, vi = q[sl], kf[sl], v[sl]
        vif = vi.astype(F32)
        blast = bl[HG_SUB - 1:HG_SUB]
        o = _dot_nt((qi * jnp.exp(bl)).astype(BF16), st.astype(BF16))
        for s in range(HG_SUB):
            d = jnp.exp(jnp.minimum(bl - bl[s:s + 1], 0.0))
            a = jnp.sum(qi * ki[s:s + 1] * d, axis=-1, keepdims=True)
            o = o + jnp.where(rows >= s, a, 0.0) * vif[s:s + 1]
        kd = (ki * jnp.exp(blast - bl)).astype(BF16)
        st = st * jnp.exp(blast) + _dot_tn(vi, kd)
        prev_end = b[(i + 1) * HG_SUB - 1:(i + 1) * HG_SUB]
        outs.append(o)
    return jnp.concatenate(outs, axis=0), st


def _hgrn_kernel(q_ref, f_ref, i_ref, g_ref, lbl_ref, gn_ref, o_ref, st_ref, kf_ref, b_ref):
    @pl.when(pl.program_id(1) == 0)
    def _():
        st_ref[...] = jnp.zeros_like(st_ref)

    lg2 = lbl_ref[...].astype(F32)
    e = jnp.exp(lg2 - jnp.max(lg2, axis=0, keepdims=True))
    lb = e[0:1] / jnp.sum(e, axis=0, keepdims=True)
    gn = gn_ref[...].astype(F32)
    r = lax.broadcasted_iota(jnp.int32, (HG_CHUNK, HG_CHUNK), 0)
    c = lax.broadcasted_iota(jnp.int32, (HG_CHUNK, HG_CHUNK), 1)
    tri_mask = r >= c
    tri = jnp.where(tri_mask, 1.0, 0.0).astype(BF16)

    n_chunks = HG_TC // HG_CHUNK

    f = lb + (1.0 - lb) * _sigmoid(f_ref[0].astype(F32))
    kf_ref[...] = 1.0 - f
    logf = jnp.log(f)
    worst = None
    for ci in range(n_chunks):
        sl = slice(ci * HG_CHUNK, (ci + 1) * HG_CHUNK)
        b = _cumsum_rows(tri, logf[sl])
        b_ref[sl, :] = b
        bend = b[HG_CHUNK - 1:HG_CHUNK]
        worst = bend if worst is None else jnp.minimum(worst, bend)

    heads = range(HG_HEADS)
    head = lambda t, h: t[:, h * HG_DIM:(h + 1) * HG_DIM]

    def finish_all(sl, outs):
        scale = [lax.rsqrt(jnp.mean(o * o, axis=-1, keepdims=True) + RMS_EPS) for o in outs]
        normed = jnp.concatenate([outs[h] * scale[h] * gn for h in heads], axis=1)
        go = g_ref[0, sl, :].astype(F32)
        o_ref[0, sl, :] = (normed * (go * _sigmoid(go))).astype(o_ref.dtype)

    def fast():
        states = [st_ref[h] for h in heads]
        for ci in range(n_chunks):
            sl = slice(ci * HG_CHUNK, (ci + 1) * HG_CHUNK)
            b = b_ref[sl, :]
            qp = (q_ref[0, sl, :].astype(F32) * jnp.exp(b)).astype(BF16)
            kp = kf_ref[sl, :] * jnp.exp(-b)
            kpb = kp.astype(BF16)
            eb = jnp.exp(b[HG_CHUNK - 1:HG_CHUNK])
            kd = (kp * eb).astype(BF16)
            v = i_ref[0, sl, :]
            att = [_dot_nt(head(qp, h), head(kpb, h)) for h in heads]
            inter = [_dot_nt(head(qp, h), states[h].astype(BF16)) for h in heads]
            update = [_dot_tn(head(v, h), head(kd, h)) for h in heads]
            att = [jnp.where(tri_mask, a, 0.0).astype(BF16) for a in att]
            outs = [inter[h] + jnp.dot(att[h], head(v, h), preferred_element_type=F32) for h in heads]
            states = [states[h] * head(eb, h) + update[h] for h in heads]
            finish_all(sl, outs)
        for h in heads:
            st_ref[h] = states[h]

    def safe():
        def chunk(ci, carry):
            sl = pl.ds(pl.multiple_of(ci * HG_CHUNK, HG_CHUNK), HG_CHUNK)
            outs = []
            for h in heads:
                hs = slice(h * HG_DIM, (h + 1) * HG_DIM)
                o, st_new = _hgrn_head_safe(q_ref[0, sl, hs].astype(F32), kf_ref[sl, hs], i_ref[0, sl, hs],
                                            b_ref[sl, hs], st_ref[h])
                st_ref[h] = st_new
                outs.append(o)
            finish_all(sl, outs)
            return carry

        lax.fori_loop(0, n_chunks, chunk, 0)

    lax.cond(jnp.min(worst) >= -HG_SAFE_DECAY, fast, safe)


def _hgrn2(proj3, lb_logits, gn):
    bsz, s_len, _ = proj3.shape
    blk = lambda off: (lambda b, t: (b, t, off // HG_WIDTH))
    return pl.pallas_call(
        _hgrn_kernel,
        grid=(bsz, s_len // HG_TC),
        in_specs=[
            pl.BlockSpec((1, HG_TC, HG_WIDTH), blk(OFF_QH)),
            pl.BlockSpec((1, HG_TC, HG_WIDTH), blk(OFF_FH)),
            pl.BlockSpec((1, HG_TC, HG_WIDTH), blk(OFF_IH)),
            pl.BlockSpec((1, HG_TC, HG_WIDTH), blk(OFF_GH)),
            pl.BlockSpec((lb_logits.shape[0], HG_WIDTH), lambda b, t: (0, 0)),
            pl.BlockSpec((1, HG_DIM), lambda b, t: (0, 0)),
        ],
        out_specs=pl.BlockSpec((1, HG_TC, HG_WIDTH), lambda b, t: (b, t, 0)),
        out_shape=jax.ShapeDtypeStruct((bsz, s_len, HG_WIDTH), BF16),
        scratch_shapes=[
            pltpu.VMEM((HG_HEADS, HG_DIM, HG_DIM), F32),
            pltpu.VMEM((HG_TC, HG_WIDTH), F32),
            pltpu.VMEM((HG_TC, HG_WIDTH), F32),
        ],
        compiler_params=pltpu.CompilerParams(
            dimension_semantics=("parallel", "arbitrary"), vmem_limit_bytes=_vmem(32)),
        name="hgrn2",
    )(proj3, proj3, proj3, proj3, lb_logits, gn)


def _store_rows_as_tiles(ref, val):
    rows = val.shape[0]
    for s in range(ROW_TILE):
        ref[pl.ds(s, rows, stride=ROW_TILE), :] = val[:, s * LANES:(s + 1) * LANES]


def _load_rows_from_tiles(ref, start, rows):
    return jnp.concatenate(
        [ref[pl.ds(start * ROW_TILE + s, rows, stride=ROW_TILE), :] for s in range(ROW_TILE)], axis=1)


def _pack_bf16_pairs(x):
    c = x.shape[1] // 2
    bits = pltpu.bitcast(x.astype(BF16).astype(F32), U32)
    return jnp.bitwise_or(jnp.right_shift(bits[:, :c], jnp.uint32(16)), bits[:, c:])


def _unpack_bf16_pairs(p):
    lo = pltpu.bitcast(jnp.left_shift(p, jnp.uint32(16)), F32)
    hi = pltpu.bitcast(jnp.bitwise_and(p, jnp.uint32(0xFFFF0000)), F32)
    return lo, hi


def _mix_kernel(sink_ref, q_ref, kp_ref, kc_ref, vp_ref, vc_ref, bias_ref,
                r_ref, ga_ref, gh_ref, x_ref, wa_ref, wr_ref, wo_ref, g2_ref, wr2_ref, br_ref,
                x1_ref, h2p_ref, route_ref, cnt_out_ref, cnt_ref, *, tiles_per_seq):
    i = pl.program_id(0)

    @pl.when(i == 0)
    def _():
        cnt_ref[...] = jnp.zeros_like(cnt_ref)

    rm = jnp.dot(r_ref[...], wr_ref[...], preferred_element_type=F32)

    k_rows = jnp.concatenate([kp_ref[...], kc_ref[...]], axis=0)
    v_rows = jnp.concatenate([vp_ref[...], vc_ref[...]], axis=0)
    first_in_seq = lax.rem(i, tiles_per_seq) == 0
    blocks = []
    for blk in range(MIX_TM // ATTN_BLOCK):
        rows = slice(blk * ATTN_BLOCK, (blk + 1) * ATTN_BLOCK)
        window = slice(blk * ATTN_BLOCK, (blk + 2) * ATTN_BLOCK)
        has_prev = jnp.logical_not(first_in_seq) if blk == 0 else True
        blocks.append(_swa_block(q_ref[rows, :], k_rows[window], v_rows[window], has_prev, bias_ref, sink_ref))
    a_t = jnp.concatenate(blocks, axis=1).astype(BF16)
    am = _dot_tn(a_t, wa_ref[...])
    mixed = _sigmoid(ga_ref[...].astype(F32)) * am + _sigmoid(gh_ref[...].astype(F32)) * rm
    x1 = x_ref[...] + jnp.dot(mixed.astype(BF16), wo_ref[...], preferred_element_type=F32)
    x1_ref[...] = x1
    h2 = x1 * lax.rsqrt(jnp.mean(x1 * x1, axis=-1, keepdims=True) + RMS_EPS) * g2_ref[...]

    _store_rows_as_tiles(h2p_ref, _pack_bf16_pairs(h2))

    hi = h2.astype(BF16)
    lo = (h2 - hi.astype(F32)).astype(BF16)
    both = jnp.dot(hi, wr2_ref[...], preferred_element_type=F32)
    logits = jnp.dot(lo, wr2_ref[:, :ROUTE_LANES], preferred_element_type=F32) + both[:, ROUTE_LANES:]
    logits = logits + both[:, :ROUTE_LANES] + br_ref[...]

    lane = lax.broadcasted_iota(jnp.int32, logits.shape, 1)
    lane_f = lane.astype(F32)
    big = float(ROUTE_LANES)
    gl = jnp.where(lane < N_GROUPS, logits, NEG_INF)
    gmax = jnp.max(gl, axis=-1, keepdims=True)
    g_idx = jnp.min(jnp.where(gl == gmax, lane_f, big), axis=-1, keepdims=True)
    g_w = 1.0 / jnp.sum(jnp.exp(gl - gmax), axis=-1, keepdims=True)
    e_lane = lane - N_GROUPS
    in_group = jnp.logical_and(e_lane >= 0, e_lane < N_EXPERTS)
    lane_group = jnp.right_shift(e_lane, 3).astype(F32)
    in_group = jnp.logical_and(in_group, lane_group == g_idx)
    el = jnp.where(in_group, logits, NEG_INF)
    m1 = jnp.max(el, axis=-1, keepdims=True)
    i1 = jnp.min(jnp.where(el == m1, lane_f, big), axis=-1, keepdims=True)
    el2 = jnp.where(lane_f == i1, NEG_INF, el)
    m2 = jnp.max(el2, axis=-1, keepdims=True)
    i2 = jnp.min(jnp.where(el2 == m2, lane_f, big), axis=-1, keepdims=True)
    p2 = jnp.exp(m2 - m1)
    w1 = g_w / (1.0 + p2)
    w2 = g_w * p2 / (1.0 + p2)
    e1 = i1 - N_GROUPS
    e2 = i2 - N_GROUPS

    tm = logits.shape[0]
    rr = lax.broadcasted_iota(jnp.int32, (tm, tm), 0)
    cc = lax.broadcasted_iota(jnp.int32, (tm, tm), 1)
    ltri = jnp.where(rr > cc, 1.0, 0.0).astype(BF16)
    sel1 = lane_f == e1
    sel2 = lane_f == e2
    oh1 = jnp.where(sel1, 1.0, 0.0)
    oh2 = jnp.where(sel2, 1.0, 0.0)
    base = cnt_ref[...]
    c1 = jnp.sum(oh1, axis=0, keepdims=True)
    c2 = jnp.sum(oh2, axis=0, keepdims=True)
    pre1 = jnp.dot(ltri, oh1.astype(BF16), preferred_element_type=F32) + base
    pre2 = jnp.dot(ltri, oh2.astype(BF16), preferred_element_type=F32) + (base + c1)
    rank1 = jnp.sum(jnp.where(sel1, pre1, 0.0), axis=-1, keepdims=True)
    rank2 = jnp.sum(jnp.where(sel2, pre2, 0.0), axis=-1, keepdims=True)
    total = base + c1 + c2
    cnt_ref[...] = total
    cnt_out_ref[...] = total

    route = jnp.zeros_like(logits)
    for k, val in enumerate((w1, w2, e1, e2, rank1, rank2)):
        route = jnp.where(lane == k, val, route)
    route_ref[...] = route


def _mix(sinks, bias, r2d, proj, x2d, wa, wr, wo, g2, wr2, br, seq_len):
    n = x2d.shape[0]
    row = lambda i: (i, 0)
    const = lambda i: (0, 0)
    blocks_per_tile = MIX_TM // ATTN_BLOCK
    prev_block = lambda i: jnp.maximum(blocks_per_tile * i - 1, 0)
    return pl.pallas_call(
        functools.partial(_mix_kernel, tiles_per_seq=seq_len // MIX_TM),
        grid=(n // MIX_TM,),
        in_specs=[
            pl.BlockSpec(memory_space=pltpu.SMEM),
            pl.BlockSpec((MIX_TM, ATTN_WIDTH), lambda i: (i, OFF_QA // ATTN_WIDTH)),
            pl.BlockSpec((ATTN_BLOCK, KV_WIDTH), lambda i: (prev_block(i), OFF_KA // KV_WIDTH)),
            pl.BlockSpec((MIX_TM, KV_WIDTH), lambda i: (i, OFF_KA // KV_WIDTH)),
            pl.BlockSpec((ATTN_BLOCK, KV_WIDTH), lambda i: (prev_block(i), OFF_VA // KV_WIDTH)),
            pl.BlockSpec((MIX_TM, KV_WIDTH), lambda i: (i, OFF_VA // KV_WIDTH)),
            pl.BlockSpec((N_Q_HEADS, ATTN_BLOCK, ATTN_BLOCK), lambda i: (0, 0, 0)),
            pl.BlockSpec((MIX_TM, HG_WIDTH), row),
            pl.BlockSpec((MIX_TM, D_MODEL), lambda i: (i, OFF_GATE_A // D_MODEL)),
            pl.BlockSpec((MIX_TM, D_MODEL), lambda i: (i, OFF_GATE_H // D_MODEL)),
            pl.BlockSpec((MIX_TM, D_MODEL), row),
            pl.BlockSpec((ATTN_WIDTH, D_MODEL), const),
            pl.BlockSpec((HG_WIDTH, D_MODEL), const),
            pl.BlockSpec((D_MODEL, D_MODEL), const),
            pl.BlockSpec((1, D_MODEL), const),
            pl.BlockSpec((D_MODEL, 2 * ROUTE_LANES), const),
            pl.BlockSpec((1, ROUTE_LANES), const),
        ],
        out_specs=[
            pl.BlockSpec((MIX_TM, D_MODEL), row),
            pl.BlockSpec((MIX_TM * ROW_TILE, LANES), row),
            pl.BlockSpec((MIX_TM, ROUTE_LANES), row),
            pl.BlockSpec((1, ROUTE_LANES), const),
        ],
        out_shape=[
            jax.ShapeDtypeStruct((n, D_MODEL), F32),
            jax.ShapeDtypeStruct((n * ROW_TILE, LANES), U32),
            jax.ShapeDtypeStruct((n, ROUTE_LANES), F32),
            jax.ShapeDtypeStruct((1, ROUTE_LANES), F32),
        ],
        scratch_shapes=[pltpu.VMEM((1, ROUTE_LANES), F32)],
        compiler_params=pltpu.CompilerParams(dimension_semantics=("arbitrary",), vmem_limit_bytes=_vmem(56)),
        name="mix",
    )(sinks, proj, proj, proj, proj, proj, bias, r2d, proj, proj, x2d, wa, wr, wo, g2, wr2, br)


def _dispatch_kernel(dest_ref, src_ref, zeros_hbm, dst_hbm, sem):
    del zeros_hbm

    for a in range(DISPATCH_ROWS):
        pltpu.make_async_copy(src_ref.at[a // TOP_K], dst_hbm.at[dest_ref[0, 0, a]], sem.at[0]).start(priority=a % 2)
    for _ in range(TOP_K):
        pltpu.make_async_copy(src_ref, dst_hbm.at[pl.ds(0, DISPATCH_ROWS // TOP_K)], sem.at[0]).wait()


def _dispatch(dest_flat, h2p_tiles, n_rows):
    n_asg = dest_flat.shape[0]
    steps = n_asg // DISPATCH_ROWS
    return pl.pallas_call(
        _dispatch_kernel,
        grid=(steps,),
        in_specs=[
            pl.BlockSpec((1, 1, DISPATCH_ROWS), lambda s: (s, 0, 0), memory_space=pltpu.SMEM),
            pl.BlockSpec((DISPATCH_ROWS // TOP_K, ROW_TILE, LANES), lambda s: (s, 0, 0)),
            pl.BlockSpec(memory_space=pl.ANY),
        ],
        out_specs=pl.BlockSpec(memory_space=pl.ANY),
        out_shape=jax.ShapeDtypeStruct((n_rows, ROW_TILE, LANES), U32),
        input_output_aliases={2: 0},
        scratch_shapes=[pltpu.SemaphoreType.DMA((1,))],
        compiler_params=pltpu.CompilerParams(dimension_semantics=("arbitrary",)),
        name="dispatch",
    )(dest_flat.reshape(steps, 1, DISPATCH_ROWS), h2p_tiles, jnp.zeros((n_rows, ROW_TILE, LANES), U32))


def _moe_kernel(be_ref, nused_ref, first_ref, slot_ref, next_ref, xs_ref, wg_hbm, wu_hbm, wd_hbm, y_ref,
                wg_buf, wu_buf, wd_buf, wg_s, wu_s, wd_s, sem):
    b = pl.program_id(0)
    live = b < nused_ref[0]

    def weight_copies(e, s):
        return (pltpu.make_async_copy(wg_hbm.at[0, e], wg_buf.at[s], sem.at[s, 0]),
                pltpu.make_async_copy(wu_hbm.at[0, e], wu_buf.at[s], sem.at[s, 1]),
                pltpu.make_async_copy(wd_hbm.at[0, e], wd_buf.at[s], sem.at[s, 2]))

    @pl.when(b == 0)
    def _():
        for c in weight_copies(be_ref[0], 0):
            c.start()

    @pl.when(jnp.logical_and(live, first_ref[b] == 1))
    def _():
        s = slot_ref[b]
        for c in weight_copies(be_ref[b], s):
            c.wait()
        wg_s[...] = wg_buf[s].astype(BF16)
        wu_s[...] = wu_buf[s].astype(BF16)
        wd_s[...] = wd_buf[s].astype(BF16)

        @pl.when(next_ref[b] >= 0)
        def _():
            for c in weight_copies(next_ref[b], 1 - s):
                c.start()

    @pl.when(jnp.logical_not(live))
    def _():
        y_ref[...] = jnp.zeros_like(y_ref)

    @pl.when(live)
    def _():
        xlo, xhi = _unpack_bf16_pairs(_load_rows_from_tiles(xs_ref, 0, MOE_TM))
        xb = jnp.concatenate([xlo.astype(BF16), xhi.astype(BF16)], axis=1)
        hg = jnp.dot(xb, wg_s[...], preferred_element_type=F32)
        hu = jnp.dot(xb, wu_s[...], preferred_element_type=F32)
        hb = (hg * _sigmoid(hg) * hu).astype(BF16)
        y = jnp.dot(hb, wd_s[...], preferred_element_type=F32)
        _store_rows_as_tiles(y_ref, _pack_bf16_pairs(y))


def _moe(block_e, n_used, first, slot, next_e, xs, w_gate, w_up, w_down):
    n_rows = xs.shape[0] // ROW_TILE
    n_blocks = n_rows // MOE_TM
    grid_spec = pltpu.PrefetchScalarGridSpec(
        num_scalar_prefetch=5,
        grid=(n_blocks,),
        in_specs=[
            pl.BlockSpec((MOE_TM * ROW_TILE, LANES), lambda b, be, nu, *_: (jnp.minimum(b, nu[0] - 1), 0)),
            pl.BlockSpec(memory_space=pl.ANY),
            pl.BlockSpec(memory_space=pl.ANY),
            pl.BlockSpec(memory_space=pl.ANY),
        ],
        out_specs=pl.BlockSpec((MOE_TM * ROW_TILE, LANES), lambda b, *_: (b, 0)),
        scratch_shapes=[
            pltpu.VMEM((2, D_MODEL, EXPERT_FF), F32),
            pltpu.VMEM((2, D_MODEL, EXPERT_FF), F32),
            pltpu.VMEM((2, EXPERT_FF, D_MODEL), F32),
            pltpu.VMEM((D_MODEL, EXPERT_FF), BF16),
            pltpu.VMEM((D_MODEL, EXPERT_FF), BF16),
            pltpu.VMEM((EXPERT_FF, D_MODEL), BF16),
            pltpu.SemaphoreType.DMA((2, 3)),
        ],
    )
    return pl.pallas_call(
        _moe_kernel,
        grid_spec=grid_spec,
        out_shape=jax.ShapeDtypeStruct((n_rows * ROW_TILE, LANES), U32),
        compiler_params=pltpu.CompilerParams(dimension_semantics=("arbitrary",), vmem_limit_bytes=_vmem(52)),
        name="moe",
    )(block_e, n_used, first, slot, next_e, xs, w_gate, w_up, w_down)


def _moe_layout(route, counts_f):
    counts = counts_f[0, :N_EXPERTS].astype(jnp.int32)
    padded = (counts + MOE_TM - 1) // MOE_TM * MOE_TM
    pends = jnp.cumsum(padded)
    pstarts = pends - padded
    n_tok = route.shape[0]
    n_blocks = n_tok * TOP_K // MOE_TM + N_EXPERTS
    blk_start = jnp.arange(n_blocks, dtype=jnp.int32) * MOE_TM
    block_e = jnp.minimum(jnp.sum((blk_start[:, None] >= pends[None, :]).astype(jnp.int32), axis=1), N_EXPERTS - 1)
    n_used = (pends[-1] // MOE_TM).astype(jnp.int32).reshape(1)
    blk = jnp.arange(n_blocks, dtype=jnp.int32)
    first = jnp.logical_and(blk < n_used[0], jnp.logical_or(blk == 0, block_e != jnp.roll(block_e, 1)))
    slot = jnp.bitwise_and(jnp.cumsum(first.astype(jnp.int32)) - 1, 1)
    ex = jnp.arange(N_EXPERTS, dtype=jnp.int32)
    later = jnp.logical_and(counts[None, :] > 0, ex[None, :] > ex[:, None])
    next_of_expert = jnp.min(jnp.where(later, ex[None, :], N_EXPERTS), axis=1)
    next_of_expert = jnp.where(next_of_expert == N_EXPERTS, -1, next_of_expert)
    next_e = jnp.sum(jnp.where(block_e[:, None] == ex[None, :], next_of_expert[None, :], 0), axis=1)
    e_id = route[:, 2:2 + TOP_K].astype(jnp.int32)
    rank = route[:, 2 + TOP_K:2 + 2 * TOP_K].astype(jnp.int32)
    onehot_e = e_id[:, :, None] == jnp.arange(N_EXPERTS, dtype=jnp.int32)[None, None, :]
    dest = jnp.sum(jnp.where(onehot_e, pstarts[None, None, :], 0), axis=2) + rank
    tables = (block_e.astype(jnp.int32), n_used, first.astype(jnp.int32), slot.astype(jnp.int32),
              next_e.astype(jnp.int32))
    return tables, dest, n_blocks * MOE_TM


def _final_kernel(dest_ref, destn_ref, x1_ref, route_ref, g_ref, y_hbm, y_flat_hbm, o_ref, ybuf, sem):
    i = pl.program_id(0)
    n_steps = pl.num_programs(0)
    slot = lax.rem(i, 2)
    n_rows = TOP_K * FIN_TM

    def start_gather(idx_ref, s):
        for r in range(n_rows):
            pltpu.make_async_copy(y_hbm.at[idx_ref[0, 0, r]], ybuf.at[pl.ds((s * n_rows + r) * ROW_TILE, ROW_TILE)],
                                  sem.at[s]).start(priority=r % 2)

    @pl.when(i == 0)
    def _():
        start_gather(dest_ref, 0)

    for s in range(2):
        @pl.when(jnp.logical_and(i + 1 < n_steps, 1 - slot == s))
        def _():
            start_gather(destn_ref, s)

    slot_rows = n_rows * ROW_TILE
    slot_start = pl.multiple_of(slot * slot_rows, slot_rows)
    pltpu.make_async_copy(y_flat_hbm.at[pl.ds(0, slot_rows)], ybuf.at[pl.ds(slot_start, slot_rows)],
                          sem.at[slot]).wait()
    route = route_ref[...]
    x = x1_ref[...]
    for k in range(TOP_K):
        ylo, yhi = _unpack_bf16_pairs(_load_rows_from_tiles(ybuf, slot * n_rows + k * FIN_TM, FIN_TM))
        x = x + route[:, k:k + 1] * jnp.concatenate([ylo, yhi], axis=1)
    o_ref[...] = x * lax.rsqrt(jnp.mean(x * x, axis=-1, keepdims=True) + RMS_EPS) * g_ref[...]


def _final(dest, x1, route, g, y_sorted):
    n = x1.shape[0]
    steps = n // FIN_TM
    dest3 = dest.reshape(steps, FIN_TM, TOP_K).transpose(0, 2, 1).reshape(steps, 1, TOP_K * FIN_TM)
    return pl.pallas_call(
        _final_kernel,
        grid=(steps,),
        in_specs=[
            pl.BlockSpec((1, 1, TOP_K * FIN_TM), lambda i: (i, 0, 0), memory_space=pltpu.SMEM),
            pl.BlockSpec((1, 1, TOP_K * FIN_TM), lambda i: (jnp.minimum(i + 1, steps - 1), 0, 0),
                         memory_space=pltpu.SMEM),
            pl.BlockSpec((FIN_TM, D_MODEL), lambda i: (i, 0)),
            pl.BlockSpec((FIN_TM, ROUTE_LANES), lambda i: (i, 0)),
            pl.BlockSpec((1, D_MODEL), lambda i: (0, 0)),
            pl.BlockSpec(memory_space=pl.ANY),
            pl.BlockSpec(memory_space=pl.ANY),
        ],
        out_specs=pl.BlockSpec((FIN_TM, D_MODEL), lambda i: (i, 0)),
        out_shape=jax.ShapeDtypeStruct((n, D_MODEL), F32),
        scratch_shapes=[
            pltpu.VMEM((2 * TOP_K * FIN_TM * ROW_TILE, LANES), U32),
            pltpu.SemaphoreType.DMA((2,)),
        ],
        compiler_params=pltpu.CompilerParams(dimension_semantics=("arbitrary",), vmem_limit_bytes=_vmem(40)),
        name="final",
    )(dest3, dest3, x1, route, g, y_sorted.reshape(-1, ROW_TILE, LANES), y_sorted)


def kernel(x, norm1_g, w_in, attn_sinks, rel_bias, hg_lb_logits, hg_norm_g, w_attn_branch, w_hg_branch, w_out,
           norm2_g, w_group_router, b_group_router, w_expert_router, b_expert_router, w_gate, w_up, w_down, final_g):
    bsz, s_len, d = x.shape
    n_tok = bsz * s_len
    x2d = x.reshape(n_tok, d)

    w_route = jnp.concatenate([w_group_router[0], w_expert_router[0]], axis=1).astype(F32)
    w_route = jnp.pad(w_route, ((0, 0), (0, ROUTE_LANES - w_route.shape[1])))
    whi = w_route.astype(BF16)
    wlo = (w_route - whi.astype(F32)).astype(BF16)
    wr2 = jnp.concatenate([whi, wlo], axis=1)
    b_route = jnp.concatenate([b_group_router[0], b_expert_router[0]]).astype(F32)
    b_route = jnp.pad(b_route, (0, ROUTE_LANES - b_route.shape[0])).reshape(1, ROUTE_LANES)
    bias = _attn_bias_table(rel_bias)

    proj = _in_proj(x2d, norm1_g[0].reshape(1, d).astype(F32), w_in)
    proj3 = proj.reshape(bsz, s_len, IN_WIDTH)
    r = _hgrn2(proj3, hg_lb_logits.astype(F32), hg_norm_g[0].reshape(1, HG_DIM).astype(F32))
    x1, h2p, route, counts = _mix(
        attn_sinks[0].astype(F32), bias, r.reshape(n_tok, HG_WIDTH), proj, x2d,
        w_attn_branch[0].astype(BF16), w_hg_branch[0].astype(BF16), w_out[0].astype(BF16),
        norm2_g[0].reshape(1, d).astype(F32), wr2, b_route, s_len)
    tables, dest, n_rows = _moe_layout(route, counts)
    xs = _dispatch(dest.reshape(n_tok * TOP_K), h2p.reshape(n_tok, ROW_TILE, LANES), n_rows)
    y_sorted = _moe(*tables, xs.reshape(n_rows * ROW_TILE, LANES), w_gate, w_up, w_down)
    out = _final(dest, x1, route, final_g.reshape(1, d).astype(F32), y_sorted)
    return out.reshape(bsz, s_len, d)
```

```python
import functools
import math

import numpy as np
import jax
import jax.numpy as jnp
from jax import lax
from jax.experimental import pallas as pl
from jax.experimental.pallas import tpu as pltpu

D_MODEL = 2048
N_Q_HEADS = 16
N_KV_HEADS = 4
HEAD_DIM = 64
WINDOW = 128
ATTN_BLOCK = 128
ATTN_WIDTH = N_Q_HEADS * HEAD_DIM
KV_WIDTH = N_KV_HEADS * HEAD_DIM
REL_BUCKETS = 32
REL_MAX_DIST = 128
HG_HEADS = 8
HG_DIM = 128
HG_WIDTH = HG_HEADS * HG_DIM
HG_CHUNK = 64
HG_SUB = 16
N_GROUPS = 4
EXPERTS_PER_GROUP = 8
N_EXPERTS = N_GROUPS * EXPERTS_PER_GROUP
TOP_K = 2
EXPERT_FF = 512
RMS_EPS = 1e-6

OFF_GATE_A = 0
OFF_GATE_H = D_MODEL
OFF_QA = 2 * D_MODEL
OFF_QH = OFF_QA + ATTN_WIDTH
OFF_FH = OFF_QH + HG_WIDTH
OFF_IH = OFF_FH + HG_WIDTH
OFF_GH = OFF_IH + HG_WIDTH
OFF_KA = OFF_GH + HG_WIDTH
OFF_VA = OFF_KA + KV_WIDTH
IN_WIDTH = OFF_VA + KV_WIDTH
REF_OFF_QA = 0
REF_OFF_KA = ATTN_WIDTH
REF_OFF_QH = ATTN_WIDTH + 2 * KV_WIDTH
REF_OFF_GATE_A = REF_OFF_QH + 4 * HG_WIDTH

HG_SAFE_DECAY = 60.0

ROW_TILE = 8
LANES = 128
assert D_MODEL // 2 == ROW_TILE * LANES

IN_TM = 2048
IN_TN = 512
HG_TC = 256
MIX_TM = 256
ROUTE_LANES = 128
DISPATCH_ROWS = 1024
MOE_TM = 256
FIN_TM = 256

F32 = jnp.float32
BF16 = jnp.bfloat16
U32 = jnp.uint32
NEG_INF = float("-inf")


def _vmem(mib):
    return mib * 1024 * 1024


def _sigmoid(x):
    return 1.0 / (1.0 + jnp.exp(-x))


def _dot_nt(a, b):
    return lax.dot_general(a, b, (((1,), (1,)), ((), ())), preferred_element_type=F32)


def _dot_tn(a, b):
    return lax.dot_general(a, b, (((0,), (0,)), ((), ())), preferred_element_type=F32)


def _in_proj_kernel(x_ref, g_ref, w_ref, o_ref, h_ref):
    @pl.when(pl.program_id(1) == 0)
    def _():
        x = x_ref[...]
        ms = jnp.mean(x * x, axis=-1, keepdims=True)
        h_ref[...] = (x * lax.rsqrt(ms + RMS_EPS) * g_ref[...]).astype(BF16)

    o_ref[...] = jnp.dot(h_ref[...], w_ref[0].astype(BF16), preferred_element_type=F32).astype(o_ref.dtype)


def _in_proj(x2d, g, w_in):
    n = x2d.shape[0]
    n_col = IN_WIDTH // IN_TN

    def out_block(j):
        gates = j - REF_OFF_GATE_A // IN_TN + OFF_GATE_A // IN_TN
        q_a = j - REF_OFF_QA // IN_TN + OFF_QA // IN_TN
        kv = j - REF_OFF_KA // IN_TN + OFF_KA // IN_TN
        hg = j - REF_OFF_QH // IN_TN + OFF_QH // IN_TN
        return jnp.where(j >= REF_OFF_GATE_A // IN_TN, gates,
                         jnp.where(j >= REF_OFF_QH // IN_TN, hg, jnp.where(j >= REF_OFF_KA // IN_TN, kv, q_a)))

    return pl.pallas_call(
        _in_proj_kernel,
        grid=(n // IN_TM, n_col),
        in_specs=[
            pl.BlockSpec((IN_TM, D_MODEL), lambda i, j: (i, 0)),
            pl.BlockSpec((1, D_MODEL), lambda i, j: (0, 0)),
            pl.BlockSpec((1, D_MODEL, IN_TN), lambda i, j: (0, 0, j)),
        ],
        out_specs=pl.BlockSpec((IN_TM, IN_TN), lambda i, j: (i, out_block(j))),
        out_shape=jax.ShapeDtypeStruct((n, IN_WIDTH), BF16),
        scratch_shapes=[pltpu.VMEM((IN_TM, D_MODEL), BF16)],
        compiler_params=pltpu.CompilerParams(
            dimension_semantics=("parallel", "arbitrary"), vmem_limit_bytes=_vmem(58)),
        name="in_proj",
    )(x2d, g, w_in)


def _swa_block(q, kk, vv, has_prev, bias_ref, sink_ref):
    q = q * jnp.asarray(HEAD_DIM ** -0.5, BF16)
    key = lax.broadcasted_iota(jnp.int32, (ATTN_BLOCK, ATTN_BLOCK), 0)
    qry = lax.broadcasted_iota(jnp.int32, (ATTN_BLOCK, ATTN_BLOCK), 1)
    cur = key <= qry
    key_ok = jnp.logical_or(cur, has_prev)
    grp = N_Q_HEADS // N_KV_HEADS
    head = lambda t, h: t[:, h * HEAD_DIM:(h + 1) * HEAD_DIM]
    lanes = lambda t, g: t[:, g * ATTN_BLOCK:(g + 1) * ATTN_BLOCK]
    scores = []
    for hk in range(N_KV_HEADS):
        q_grp = jnp.concatenate([head(q, hk * grp + g) for g in range(grp)], axis=0)
        scores.append(_dot_nt(head(kk, hk), q_grp))
    probs, denoms = [], []
    for hq in range(N_Q_HEADS):
        s2 = lanes(scores[hq // grp], hq % grp)
        s = jnp.where(cur, s2[ATTN_BLOCK:], s2[:ATTN_BLOCK]) + bias_ref[hq]
        s = jnp.where(key_ok, s, NEG_INF)
        sink = sink_ref[hq]
        m = jnp.maximum(jnp.max(s, axis=0, keepdims=True), sink)
        p = jnp.exp(s - m)
        denoms.append(jnp.sum(p, axis=0, keepdims=True) + jnp.exp(sink - m))
        probs.append(jnp.concatenate([jnp.where(cur, 0.0, p), jnp.where(cur, p, 0.0)], axis=0).astype(BF16))
    outs = []
    for hk in range(N_KV_HEADS):
        p_grp = jnp.concatenate(probs[hk * grp:(hk + 1) * grp], axis=1)
        o_grp = _dot_tn(head(vv, hk), p_grp)
        outs += [lanes(o_grp, g) / denoms[hk * grp + g] for g in range(grp)]
    return jnp.concatenate(outs, axis=0)


def _t5_causal_bucket(n):
    max_exact = REL_BUCKETS // 2
    nf = jnp.maximum(n, 1).astype(F32)
    large = max_exact + (jnp.log(nf / max_exact) / math.log(REL_MAX_DIST / max_exact)
                         * (REL_BUCKETS - max_exact)).astype(jnp.int32)
    large = jnp.minimum(large, REL_BUCKETS - 1)
    return jnp.where(n < max_exact, n, large)


def _attn_bias_table(rel_bias):
    assert WINDOW == ATTN_BLOCK
    r = jnp.arange(ATTN_BLOCK)[None, :]
    c = jnp.arange(ATTN_BLOCK)[:, None]
    dist = jnp.where(c <= r, r - c, r + ATTN_BLOCK - c)
    bucket = _t5_causal_bucket(dist)
    onehot = (bucket[None, :, :] == jnp.arange(REL_BUCKETS)[:, None, None]).astype(F32)
    return jnp.einsum("bh,bqk->hqk", rel_bias.astype(F32), onehot, precision=lax.Precision.HIGHEST)


def _cumsum_rows(tri_bf16, g):
    width = g.shape[1]
    g1 = g.astype(BF16)
    r1 = g - g1.astype(F32)
    g2 = r1.astype(BF16)
    g3 = (r1 - g2.astype(F32)).astype(BF16)
    acc = jnp.dot(tri_bf16, jnp.concatenate([g3, g2, g1], axis=1), preferred_element_type=F32)
    return (acc[:, :width] + acc[:, width:2 * width]) + acc[:, 2 * width:]


def _hgrn_head_safe(q, kf, v, b, st):
    rows = lax.broadcasted_iota(jnp.int32, (HG_SUB, 1), 0)
    outs = []
    prev_end = jnp.zeros((1, HG_DIM), F32)
    for i in range(HG_CHUNK // HG_SUB):
        sl = slice(i * HG_SUB, (i + 1) * HG_SUB)
        bl = b[sl] - prev_end
        qi, ki, vi = q[sl], kf[sl], v[sl]
        vif = vi.astype(F32)
        blast = bl[HG_SUB - 1:HG_SUB]
        o = _dot_nt((qi * jnp.exp(bl)).astype(BF16), st.astype(BF16))
        for s in range(HG_SUB):
            d = jnp.exp(jnp.minimum(bl - bl[s:s + 1], 0.0))
            a = jnp.sum(qi * ki[s:s + 1] * d, axis=-1, keepdims=True)
            o = o + jnp.where(rows >= s, a, 0.0) * vif[s:s + 1]
        kd = (ki * jnp.exp(blast - bl)).astype(BF16)
        st = st * jnp.exp(blast) + _dot_tn(vi, kd)
        prev_end = b[(i + 1) * HG_SUB - 1:(i + 1) * HG_SUB]
        outs.append(o)
    return jnp.concatenate(outs, axis=0), st


def _hgrn_kernel(q_ref, f_ref, i_ref, g_ref, lbl_ref, gn_ref, o_ref, st_ref, kf_ref, b_ref):
    @pl.when(pl.program_id(1) == 0)
    def _():
        st_ref[...] = jnp.zeros_like(st_ref)

    lg2 = lbl_ref[...].astype(F32)
    e = jnp.exp(lg2 - jnp.max(lg2, axis=0, keepdims=True))
    lb = e[0:1] / jnp.sum(e, axis=0, keepdims=True)
    gn = gn_ref[...].astype(F32)
    r = lax.broadcasted_iota(jnp.int32, (HG_CHUNK, HG_CHUNK), 0)
    c = lax.broadcasted_iota(jnp.int32, (HG_CHUNK, HG_CHUNK), 1)
    tri_mask = r >= c
    tri = jnp.where(tri_mask, 1.0, 0.0).astype(BF16)

    n_chunks = HG_TC // HG_CHUNK

    f = lb + (1.0 - lb) * _sigmoid(f_ref[0].astype(F32))
    kf_ref[...] = 1.0 - f
    logf = jnp.log(f)
    worst = None
    for ci in range(n_chunks):
        sl = slice(ci * HG_CHUNK, (ci + 1) * HG_CHUNK)
        b = _cumsum_rows(tri, logf[sl])
        b_ref[sl, :] = b
        bend = b[HG_CHUNK - 1:HG_CHUNK]
        worst = bend if worst is None else jnp.minimum(worst, bend)

    heads = range(HG_HEADS)
    head = lambda t, h: t[:, h * HG_DIM:(h + 1) * HG_DIM]

    def finish_all(sl, outs):
        scale = [lax.rsqrt(jnp.mean(o * o, axis=-1, keepdims=True) + RMS_EPS) for o in outs]
        normed = jnp.concatenate([outs[h] * scale[h] * gn for h in heads], axis=1)
        go = g_ref[0, sl, :].astype(F32)
        o_ref[0, sl, :] = (normed * (go * _sigmoid(go))).astype(o_ref.dtype)

    def fast():
        states = [st_ref[h] for h in heads]
        for ci in range(n_chunks):
            sl = slice(ci * HG_CHUNK, (ci + 1) * HG_CHUNK)
            b = b_ref[sl, :]
            qp = (q_ref[0, sl, :].astype(F32) * jnp.exp(b)).astype(BF16)
            kp = kf_ref[sl, :] * jnp.exp(-b)
            kpb = kp.astype(BF16)
            eb = jnp.exp(b[HG_CHUNK - 1:HG_CHUNK])
            kd = (kp * eb).astype(BF16)
            v = i_ref[0, sl, :]
            att = [_dot_nt(head(qp, h), head(kpb, h)) for h in heads]
            inter = [_dot_nt(head(qp, h), states[h].astype(BF16)) for h in heads]
            update = [_dot_tn(head(v, h), head(kd, h)) for h in heads]
            att = [jnp.where(tri_mask, a, 0.0).astype(BF16) for a in att]
            outs = [inter[h] + jnp.dot(att[h], head(v, h), preferred_element_type=F32) for h in heads]
            states = [states[h] * head(eb, h) + update[h] for h in heads]
            finish_all(sl, outs)
        for h in heads:
            st_ref[h] = states[h]

    def safe():
        def chunk(ci, carry):
            sl = pl.ds(pl.multiple_of(ci * HG_CHUNK, HG_CHUNK), HG_CHUNK)
            outs = []
            for h in heads:
                hs = slice(h * HG_DIM, (h + 1) * HG_DIM)
                o, st_new = _hgrn_head_safe(q_ref[0, sl, hs].astype(F32), kf_ref[sl, hs], i_ref[0, sl, hs],
                                            b_ref[sl, hs], st_ref[h])
                st_ref[h] = st_new
                outs.append(o)
            finish_all(sl, outs)
            return carry

        lax.fori_loop(0, n_chunks, chunk, 0)

    lax.cond(jnp.min(worst) >= -HG_SAFE_DECAY, fast, safe)


def _hgrn2(proj3, lb_logits, gn):
    bsz, s_len, _ = proj3.shape
    blk = lambda off: (lambda b, t: (b, t, off // HG_WIDTH))
    return pl.pallas_call(
        _hgrn_kernel,
        grid=(bsz, s_len // HG_TC),
        in_specs=[
            pl.BlockSpec((1, HG_TC, HG_WIDTH), blk(OFF_QH)),
            pl.BlockSpec((1, HG_TC, HG_WIDTH), blk(OFF_FH)),
            pl.BlockSpec((1, HG_TC, HG_WIDTH), blk(OFF_IH)),
            pl.BlockSpec((1, HG_TC, HG_WIDTH), blk(OFF_GH)),
            pl.BlockSpec((lb_logits.shape[0], HG_WIDTH), lambda b, t: (0, 0)),
            pl.BlockSpec((1, HG_DIM), lambda b, t: (0, 0)),
        ],
        out_specs=pl.BlockSpec((1, HG_TC, HG_WIDTH), lambda b, t: (b, t, 0)),
        out_shape=jax.ShapeDtypeStruct((bsz, s_len, HG_WIDTH), BF16),
        scratch_shapes=[
            pltpu.VMEM((HG_HEADS, HG_DIM, HG_DIM), F32),
            pltpu.VMEM((HG_TC, HG_WIDTH), F32),
            pltpu.VMEM((HG_TC, HG_WIDTH), F32),
        ],
        compiler_params=pltpu.CompilerParams(
            dimension_semantics=("parallel", "arbitrary"), vmem_limit_bytes=_vmem(32)),
        name="hgrn2",
    )(proj3, proj3, proj3, proj3, lb_logits, gn)


def _store_rows_as_tiles(ref, val):
    rows = val.shape[0]
    for s in range(ROW_TILE):
        ref[pl.ds(s, rows, stride=ROW_TILE), :] = val[:, s * LANES:(s + 1) * LANES]


def _load_rows_from_tiles(ref, start, rows):
    return jnp.concatenate(
        [ref[pl.ds(start * ROW_TILE + s, rows, stride=ROW_TILE), :] for s in range(ROW_TILE)], axis=1)


def _pack_bf16_pairs(x):
    c = x.shape[1] // 2
    bits = pltpu.bitcast(x.astype(BF16).astype(F32), U32)
    return jnp.bitwise_or(jnp.right_shift(bits[:, :c], jnp.uint32(16)), bits[:, c:])


def _unpack_bf16_pairs(p):
    lo = pltpu.bitcast(jnp.left_shift(p, jnp.uint32(16)), F32)
    hi = pltpu.bitcast(jnp.bitwise_and(p, jnp.uint32(0xFFFF0000)), F32)
    return lo, hi


def _mix_kernel(sink_ref, q_ref, kp_ref, kc_ref, vp_ref, vc_ref, bias_ref,
                r_ref, ga_ref, gh_ref, x_ref, wa_ref, wr_ref, wo_ref, g2_ref, wr2_ref, br_ref,
                x1_ref, h2p_ref, route_ref, cnt_out_ref, cnt_ref, *, tiles_per_seq):
    i = pl.program_id(0)

    @pl.when(i == 0)
    def _():
        cnt_ref[...] = jnp.zeros_like(cnt_ref)

    rm = jnp.dot(r_ref[...], wr_ref[...], preferred_element_type=F32)

    k_rows = jnp.concatenate([kp_ref[...], kc_ref[...]], axis=0)
    v_rows = jnp.concatenate([vp_ref[...], vc_ref[...]], axis=0)
    first_in_seq = lax.rem(i, tiles_per_seq) == 0
    blocks = []
    for blk in range(MIX_TM // ATTN_BLOCK):
        rows = slice(blk * ATTN_BLOCK, (blk + 1) * ATTN_BLOCK)
        window = slice(blk * ATTN_BLOCK, (blk + 2) * ATTN_BLOCK)
        has_prev = jnp.logical_not(first_in_seq) if blk == 0 else True
        blocks.append(_swa_block(q_ref[rows, :], k_rows[window], v_rows[window], has_prev, bias_ref, sink_ref))
    a_t = jnp.concatenate(blocks, axis=1).astype(BF16)
    am = _dot_tn(a_t, wa_ref[...])
    mixed = _sigmoid(ga_ref[...].astype(F32)) * am + _sigmoid(gh_ref[...].astype(F32)) * rm
    x1 = x_ref[...] + jnp.dot(mixed.astype(BF16), wo_ref[...], preferred_element_type=F32)
    x1_ref[...] = x1
    h2 = x1 * lax.rsqrt(jnp.mean(x1 * x1, axis=-1, keepdims=True) + RMS_EPS) * g2_ref[...]

    _store_rows_as_tiles(h2p_ref, _pack_bf16_pairs(h2))

    hi = h2.astype(BF16)
    lo = (h2 - hi.astype(F32)).astype(BF16)
    both = jnp.dot(hi, wr2_ref[...], preferred_element_type=F32)
    logits = jnp.dot(lo, wr2_ref[:, :ROUTE_LANES], preferred_element_type=F32) + both[:, ROUTE_LANES:]
    logits = logits + both[:, :ROUTE_LANES] + br_ref[...]

    lane = lax.broadcasted_iota(jnp.int32, logits.shape, 1)
    lane_f = lane.astype(F32)
    big = float(ROUTE_LANES)
    gl = jnp.where(lane < N_GROUPS, logits, NEG_INF)
    gmax = jnp.max(gl, axis=-1, keepdims=True)
    g_idx = jnp.min(jnp.where(gl == gmax, lane_f, big), axis=-1, keepdims=True)
    g_w = 1.0 / jnp.sum(jnp.exp(gl - gmax), axis=-1, keepdims=True)
    e_lane = lane - N_GROUPS
    in_group = jnp.logical_and(e_lane >= 0, e_lane < N_EXPERTS)
    lane_group = jnp.right_shift(e_lane, 3).astype(F32)
    in_group = jnp.logical_and(in_group, lane_group == g_idx)
    el = jnp.where(in_group, logits, NEG_INF)
    m1 = jnp.max(el, axis=-1, keepdims=True)
    i1 = jnp.min(jnp.where(el == m1, lane_f, big), axis=-1, keepdims=True)
    el2 = jnp.where(lane_f == i1, NEG_INF, el)
    m2 = jnp.max(el2, axis=-1, keepdims=True)
    i2 = jnp.min(jnp.where(el2 == m2, lane_f, big), axis=-1, keepdims=True)
    p2 = jnp.exp(m2 - m1)
    w1 = g_w / (1.0 + p2)
    w2 = g_w * p2 / (1.0 + p2)
    e1 = i1 - N_GROUPS
    e2 = i2 - N_GROUPS

    tm = logits.shape[0]
    rr = lax.broadcasted_iota(jnp.int32, (tm, tm), 0)
    cc = lax.broadcasted_iota(jnp.int32, (tm, tm), 1)
    ltri = jnp.where(rr > cc, 1.0, 0.0).astype(BF16)
    sel1 = lane_f == e1
    sel2 = lane_f == e2
    oh1 = jnp.where(sel1, 1.0, 0.0)
    oh2 = jnp.where(sel2, 1.0, 0.0)
    base = cnt_ref[...]
    c1 = jnp.sum(oh1, axis=0, keepdims=True)
    c2 = jnp.sum(oh2, axis=0, keepdims=True)
    pre1 = jnp.dot(ltri, oh1.astype(BF16), preferred_element_type=F32) + base
    pre2 = jnp.dot(ltri, oh2.astype(BF16), preferred_element_type=F32) + (base + c1)
    rank1 = jnp.sum(jnp.where(sel1, pre1, 0.0), axis=-1, keepdims=True)
    rank2 = jnp.sum(jnp.where(sel2, pre2, 0.0), axis=-1, keepdims=True)
    total = base + c1 + c2
    cnt_ref[...] = total
    cnt_out_ref[...] = total

    route = jnp.zeros_like(logits)
    for k, val in enumerate((w1, w2, e1, e2, rank1, rank2)):
        route = jnp.where(lane == k, val, route)
    route_ref[...] = route


def _mix(sinks, bias, r2d, proj, x2d, wa, wr, wo, g2, wr2, br, seq_len):
    n = x2d.shape[0]
    row = lambda i: (i, 0)
    const = lambda i: (0, 0)
    blocks_per_tile = MIX_TM // ATTN_BLOCK
    prev_block = lambda i: jnp.maximum(blocks_per_tile * i - 1, 0)
    return pl.pallas_call(
        functools.partial(_mix_kernel, tiles_per_seq=seq_len // MIX_TM),
        grid=(n // MIX_TM,),
        in_specs=[
            pl.BlockSpec(memory_space=pltpu.SMEM),
            pl.BlockSpec((MIX_TM, ATTN_WIDTH), lambda i: (i, OFF_QA // ATTN_WIDTH)),
            pl.BlockSpec((ATTN_BLOCK, KV_WIDTH), lambda i: (prev_block(i), OFF_KA // KV_WIDTH)),
            pl.BlockSpec((MIX_TM, KV_WIDTH), lambda i: (i, OFF_KA // KV_WIDTH)),
            pl.BlockSpec((ATTN_BLOCK, KV_WIDTH), lambda i: (prev_block(i), OFF_VA // KV_WIDTH)),
            pl.BlockSpec((MIX_TM, KV_WIDTH), lambda i: (i, OFF_VA // KV_WIDTH)),
            pl.BlockSpec((N_Q_HEADS, ATTN_BLOCK, ATTN_BLOCK), lambda i: (0, 0, 0)),
            pl.BlockSpec((MIX_TM, HG_WIDTH), row),
            pl.BlockSpec((MIX_TM, D_MODEL), lambda i: (i, OFF_GATE_A // D_MODEL)),
            pl.BlockSpec((MIX_TM, D_MODEL), lambda i: (i, OFF_GATE_H // D_MODEL)),
            pl.BlockSpec((MIX_TM, D_MODEL), row),
            pl.BlockSpec((ATTN_WIDTH, D_MODEL), const),
            pl.BlockSpec((HG_WIDTH, D_MODEL), const),
            pl.BlockSpec((D_MODEL, D_MODEL), const),
            pl.BlockSpec((1, D_MODEL), const),
            pl.BlockSpec((D_MODEL, 2 * ROUTE_LANES), const),
            pl.BlockSpec((1, ROUTE_LANES), const),
        ],
        out_specs=[
            pl.BlockSpec((MIX_TM, D_MODEL), row),
            pl.BlockSpec((MIX_TM * ROW_TILE, LANES), row),
            pl.BlockSpec((MIX_TM, ROUTE_LANES), row),
            pl.BlockSpec((1, ROUTE_LANES), const),
        ],
        out_shape=[
            jax.ShapeDtypeStruct((n, D_MODEL), F32),
            jax.ShapeDtypeStruct((n * ROW_TILE, LANES), U32),
            jax.ShapeDtypeStruct((n, ROUTE_LANES), F32),
            jax.ShapeDtypeStruct((1, ROUTE_LANES), F32),
        ],
        scratch_shapes=[pltpu.VMEM((1, ROUTE_LANES), F32)],
        compiler_params=pltpu.CompilerParams(dimension_semantics=("arbitrary",), vmem_limit_bytes=_vmem(56)),
        name="mix",
    )(sinks, proj, proj, proj, proj, proj, bias, r2d, proj, proj, x2d, wa, wr, wo, g2, wr2, br)


def _dispatch_kernel(dest_ref, src_ref, zeros_hbm, dst_hbm, sem):
    del zeros_hbm

    for a in range(DISPATCH_ROWS):
        pltpu.make_async_copy(src_ref.at[a // TOP_K], dst_hbm.at[dest_ref[0, 0, a]], sem.at[0]).start(priority=a % 2)
    for _ in range(TOP_K):
        pltpu.make_async_copy(src_ref, dst_hbm.at[pl.ds(0, DISPATCH_ROWS // TOP_K)], sem.at[0]).wait()


def _dispatch(dest_flat, h2p_tiles, n_rows):
    n_asg = dest_flat.shape[0]
    steps = n_asg // DISPATCH_ROWS
    return pl.pallas_call(
        _dispatch_kernel,
        grid=(steps,),
        in_specs=[
            pl.BlockSpec((1, 1, DISPATCH_ROWS), lambda s: (s, 0, 0), memory_space=pltpu.SMEM),
            pl.BlockSpec((DISPATCH_ROWS // TOP_K, ROW_TILE, LANES), lambda s: (s, 0, 0)),
            pl.BlockSpec(memory_space=pl.ANY),
        ],
        out_specs=pl.BlockSpec(memory_space=pl.ANY),
        out_shape=jax.ShapeDtypeStruct((n_rows, ROW_TILE, LANES), U32),
        input_output_aliases={2: 0},
        scratch_shapes=[pltpu.SemaphoreType.DMA((1,))],
        compiler_params=pltpu.CompilerParams(dimension_semantics=("arbitrary",)),
        name="dispatch",
    )(dest_flat.reshape(steps, 1, DISPATCH_ROWS), h2p_tiles, jnp.zeros((n_rows, ROW_TILE, LANES), U32))


def _moe_kernel(be_ref, nused_ref, first_ref, slot_ref, next_ref, xs_ref, wg_hbm, wu_hbm, wd_hbm, y_ref,
                wg_buf, wu_buf, wd_buf, wg_s, wu_s, wd_s, sem):
    b = pl.program_id(0)
    live = b < nused_ref[0]

    def weight_copies(e, s):
        return (pltpu.make_async_copy(wg_hbm.at[0, e], wg_buf.at[s], sem.at[s, 0]),
                pltpu.make_async_copy(wu_hbm.at[0, e], wu_buf.at[s], sem.at[s, 1]),
                pltpu.make_async_copy(wd_hbm.at[0, e], wd_buf.at[s], sem.at[s, 2]))

    @pl.when(b == 0)
    def _():
        for c in weight_copies(be_ref[0], 0):
            c.start()

    @pl.when(jnp.logical_and(live, first_ref[b] == 1))
    def _():
        s = slot_ref[b]
        for c in weight_copies(be_ref[b], s):
            c.wait()
        wg_s[...] = wg_buf[s].astype(BF16)
        wu_s[...] = wu_buf[s].astype(BF16)
        wd_s[...] = wd_buf[s].astype(BF16)

        @pl.when(next_ref[b] >= 0)
        def _():
            for c in weight_copies(next_ref[b], 1 - s):
                c.start()

    @pl.when(jnp.logical_not(live))
    def _():
        y_ref[...] = jnp.zeros_like(y_ref)

    @pl.when(live)
    def _():
        xlo, xhi = _unpack_bf16_pairs(_load_rows_from_tiles(xs_ref, 0, MOE_TM))
        xb = jnp.concatenate([xlo.astype(BF16), xhi.astype(BF16)], axis=1)
        hg = jnp.dot(xb, wg_s[...], preferred_element_type=F32)
        hu = jnp.dot(xb, wu_s[...], preferred_element_type=F32)
        hb = (hg * _sigmoid(hg) * hu).astype(BF16)
        y = jnp.dot(hb, wd_s[...], preferred_element_type=F32)
        _store_rows_as_tiles(y_ref, _pack_bf16_pairs(y))


def _moe(block_e, n_used, first, slot, next_e, xs, w_gate, w_up, w_down):
    n_rows = xs.shape[0] // ROW_TILE
    n_blocks = n_rows // MOE_TM
    grid_spec = pltpu.PrefetchScalarGridSpec(
        num_scalar_prefetch=5,
        grid=(n_blocks,),
        in_specs=[
            pl.BlockSpec((MOE_TM * ROW_TILE, LANES), lambda b, be, nu, *_: (jnp.minimum(b, nu[0] - 1), 0)),
            pl.BlockSpec(memory_space=pl.ANY),
            pl.BlockSpec(memory_space=pl.ANY),
            pl.BlockSpec(memory_space=pl.ANY),
        ],
        out_specs=pl.BlockSpec((MOE_TM * ROW_TILE, LANES), lambda b, *_: (b, 0)),
        scratch_shapes=[
            pltpu.VMEM((2, D_MODEL, EXPERT_FF), F32),
            pltpu.VMEM((2, D_MODEL, EXPERT_FF), F32),
            pltpu.VMEM((2, EXPERT_FF, D_MODEL), F32),
            pltpu.VMEM((D_MODEL, EXPERT_FF), BF16),
            pltpu.VMEM((D_MODEL, EXPERT_FF), BF16),
            pltpu.VMEM((EXPERT_FF, D_MODEL), BF16),
            pltpu.SemaphoreType.DMA((2, 3)),
        ],
    )
    return pl.pallas_call(
        _moe_kernel,
        grid_spec=grid_spec,
        out_shape=jax.ShapeDtypeStruct((n_rows * ROW_TILE, LANES), U32),
        compiler_params=pltpu.CompilerParams(dimension_semantics=("arbitrary",), vmem_limit_bytes=_vmem(52)),
        name="moe",
    )(block_e, n_used, first, slot, next_e, xs, w_gate, w_up, w_down)


def _moe_layout(route, counts_f):
    counts = counts_f[0, :N_EXPERTS].astype(jnp.int32)
    padded = (counts + MOE_TM - 1) // MOE_TM * MOE_TM
    pends = jnp.cumsum(padded)
    pstarts = pends - padded
    n_tok = route.shape[0]
    n_blocks = n_tok * TOP_K // MOE_TM + N_EXPERTS
    blk_start = jnp.arange(n_blocks, dtype=jnp.int32) * MOE_TM
    block_e = jnp.minimum(jnp.sum((blk_start[:, None] >= pends[None, :]).astype(jnp.int32), axis=1), N_EXPERTS - 1)
    n_used = (pends[-1] // MOE_TM).astype(jnp.int32).reshape(1)
    blk = jnp.arange(n_blocks, dtype=jnp.int32)
    first = jnp.logical_and(blk < n_used[0], jnp.logical_or(blk == 0, block_e != jnp.roll(block_e, 1)))
    slot = jnp.bitwise_and(jnp.cumsum(first.astype(jnp.int32)) - 1, 1)
    ex = jnp.arange(N_EXPERTS, dtype=jnp.int32)
    later = jnp.logical_and(counts[None, :] > 0, ex[None, :] > ex[:, None])
    next_of_expert = jnp.min(jnp.where(later, ex[None, :], N_EXPERTS), axis=1)
    next_of_expert = jnp.where(next_of_expert == N_EXPERTS, -1, next_of_expert)
    next_e = jnp.sum(jnp.where(block_e[:, None] == ex[None, :], next_of_expert[None, :], 0), axis=1)
    e_id = route[:, 2:2 + TOP_K].astype(jnp.int32)
    rank = route[:, 2 + TOP_K:2 + 2 * TOP_K].astype(jnp.int32)
    onehot_e = e_id[:, :, None] == jnp.arange(N_EXPERTS, dtype=jnp.int32)[None, None, :]
    dest = jnp.sum(jnp.where(onehot_e, pstarts[None, None, :], 0), axis=2) + rank
    tables = (block_e.astype(jnp.int32), n_used, first.astype(jnp.int32), slot.astype(jnp.int32),
              next_e.astype(jnp.int32))
    return tables, dest, n_blocks * MOE_TM


def _final_kernel(dest_ref, destn_ref, x1_ref, route_ref, g_ref, y_hbm, y_flat_hbm, o_ref, ybuf0, ybuf1, sem):
    i = pl.program_id(0)
    n_steps = pl.num_programs(0)
    slot = lax.rem(i, 2)
    n_rows = TOP_K * FIN_TM
    ybufs = (ybuf0, ybuf1)

    def start_gather(idx_ref, s):
        for r in range(n_rows):
            pltpu.make_async_copy(y_hbm.at[idx_ref[0, 0, r]], ybufs[s].at[pl.ds(r * ROW_TILE, ROW_TILE)],
                                  sem.at[s]).start(priority=r % 2)

    def wait_gather(s):
        pltpu.make_async_copy(y_flat_hbm.at[pl.ds(0, n_rows * ROW_TILE)], ybufs[s], sem.at[s]).wait()

    def combine(s):
        route = route_ref[...]
        x = x1_ref[...]
        for k in range(TOP_K):
            ylo, yhi = _unpack_bf16_pairs(_load_rows_from_tiles(ybufs[s], k * FIN_TM, FIN_TM))
            x = x + route[:, k:k + 1] * jnp.concatenate([ylo, yhi], axis=1)
        o_ref[...] = x * lax.rsqrt(jnp.mean(x * x, axis=-1, keepdims=True) + RMS_EPS) * g_ref[...]

    @pl.when(i == 0)
    def _():
        start_gather(dest_ref, 0)

    for s in range(2):
        @pl.when(slot == s)
        def _():
            wait_gather(s)
            start_gather(destn_ref, 1 - s)
            combine(s)

            @pl.when(i == n_steps - 1)
            def _():
                wait_gather(1 - s)


def _final(dest, x1, route, g, y_sorted):
    n = x1.shape[0]
    steps = n // FIN_TM
    dest3 = dest.reshape(steps, FIN_TM, TOP_K).transpose(0, 2, 1).reshape(steps, 1, TOP_K * FIN_TM)
    return pl.pallas_call(
        _final_kernel,
        grid=(steps,),
        in_specs=[
            pl.BlockSpec((1, 1, TOP_K * FIN_TM), lambda i: (i, 0, 0), memory_space=pltpu.SMEM),
            pl.BlockSpec((1, 1, TOP_K * FIN_TM), lambda i: (jnp.minimum(i + 1, steps - 1), 0, 0),
                         memory_space=pltpu.SMEM),
            pl.BlockSpec((FIN_TM, D_MODEL), lambda i: (i, 0)),
            pl.BlockSpec((FIN_TM, ROUTE_LANES), lambda i: (i, 0)),
            pl.BlockSpec((1, D_MODEL), lambda i: (0, 0)),
            pl.BlockSpec(memory_space=pl.ANY),
            pl.BlockSpec(memory_space=pl.ANY),
        ],
        out_specs=pl.BlockSpec((FIN_TM, D_MODEL), lambda i: (i, 0)),
        out_shape=jax.ShapeDtypeStruct((n, D_MODEL), F32),
        scratch_shapes=[
            pltpu.VMEM((TOP_K * FIN_TM * ROW_TILE, LANES), U32),
            pltpu.VMEM((TOP_K * FIN_TM * ROW_TILE, LANES), U32),
            pltpu.SemaphoreType.DMA((2,)),
        ],
        compiler_params=pltpu.CompilerParams(dimension_semantics=("arbitrary",), vmem_limit_bytes=_vmem(40)),
        name="final",
    )(dest3, dest3, x1, route, g, y_sorted.reshape(-1, ROW_TILE, LANES), y_sorted)


def kernel(x, norm1_g, w_in, attn_sinks, rel_bias, hg_lb_logits, hg_norm_g, w_attn_branch, w_hg_branch, w_out,
           norm2_g, w_group_router, b_group_router, w_expert_router, b_expert_router, w_gate, w_up, w_down, final_g):
    bsz, s_len, d = x.shape
    n_tok = bsz * s_len
    x2d = x.reshape(n_tok, d)

    w_route = jnp.concatenate([w_group_router[0], w_expert_router[0]], axis=1).astype(F32)
    w_route = jnp.pad(w_route, ((0, 0), (0, ROUTE_LANES - w_route.shape[1])))
    whi = w_route.astype(BF16)
    wlo = (w_route - whi.astype(F32)).astype(BF16)
    wr2 = jnp.concatenate([whi, wlo], axis=1)
    b_route = jnp.concatenate([b_group_router[0], b_expert_router[0]]).astype(F32)
    b_route = jnp.pad(b_route, (0, ROUTE_LANES - b_route.shape[0])).reshape(1, ROUTE_LANES)
    bias = _attn_bias_table(rel_bias)

    proj = _in_proj(x2d, norm1_g[0].reshape(1, d).astype(F32), w_in)
    proj3 = proj.reshape(bsz, s_len, IN_WIDTH)
    r = _hgrn2(proj3, hg_lb_logits.astype(F32), hg_norm_g[0].reshape(1, HG_DIM).astype(F32))
    x1, h2p, route, counts = _mix(
        attn_sinks[0].astype(F32), bias, r.reshape(n_tok, HG_WIDTH), proj, x2d,
        w_attn_branch[0].astype(BF16), w_hg_branch[0].astype(BF16), w_out[0].astype(BF16),
        norm2_g[0].reshape(1, d).astype(F32), wr2, b_route, s_len)
    tables, dest, n_rows = _moe_layout(route, counts)
    xs = _dispatch(dest.reshape(n_tok * TOP_K), h2p.reshape(n_tok, ROW_TILE, LANES), n_rows)
    y_sorted = _moe(*tables, xs.reshape(n_rows * ROW_TILE, LANES), w_gate, w_up, w_down)
    out = _final(dest, x1, route, final_g.reshape(1, d).astype(F32), y_sorted)
    return out.reshape(bsz, s_len, d)
```

```python
import functools
import math

import jax
import jax.numpy as jnp
from jax import lax
from jax.experimental import pallas as pl
from jax.experimental.pallas import tpu as pltpu

D_MODEL = 2048
N_Q_HEADS = 16
N_KV_HEADS = 4
HEAD_DIM = 64
WINDOW = 128
ATTN_BLOCK = 128
ATTN_WIDTH = N_Q_HEADS * HEAD_DIM
KV_WIDTH = N_KV_HEADS * HEAD_DIM
REL_BUCKETS = 32
REL_MAX_DIST = 128
HG_HEADS = 8
HG_DIM = 128
HG_WIDTH = HG_HEADS * HG_DIM
HG_CHUNK = 64
HG_SUB = 16
N_GROUPS = 4
EXPERTS_PER_GROUP = 8
N_EXPERTS = N_GROUPS * EXPERTS_PER_GROUP
TOP_K = 2
EXPERT_FF = 512
RMS_EPS = 1e-6

OFF_GATE_A = 0
OFF_GATE_H = D_MODEL
OFF_QA = 2 * D_MODEL
OFF_QH = OFF_QA + ATTN_WIDTH
OFF_FH = OFF_QH + HG_WIDTH
OFF_IH = OFF_FH + HG_WIDTH
OFF_GH = OFF_IH + HG_WIDTH
OFF_KA = OFF_GH + HG_WIDTH
OFF_VA = OFF_KA + KV_WIDTH
IN_WIDTH = OFF_VA + KV_WIDTH
REF_OFF_QA = 0
REF_OFF_KA = ATTN_WIDTH
REF_OFF_QH = ATTN_WIDTH + 2 * KV_WIDTH
REF_OFF_GATE_A = REF_OFF_QH + 4 * HG_WIDTH

HG_SAFE_DECAY = 60.0

ROW_TILE = 8
LANES = 128
assert D_MODEL // 2 == ROW_TILE * LANES
MXU_WIDTH = 256

IN_TM = 2048
IN_TN = 512
HG_TC = 1024
MIX_TM = 512
ROUTE_LANES = 128
DISPATCH_ROWS = 1024
MOE_TM = 256
FIN_TM = 256

F32 = jnp.float32
BF16 = jnp.bfloat16
U32 = jnp.uint32
NEG_INF = float("-inf")


def _vmem(mib):
    return mib * 1024 * 1024


def _sigmoid(x):
    return 1.0 / (1.0 + jnp.exp(-x))


def _dot_nt(a, b):
    return lax.dot_general(a, b, (((1,), (1,)), ((), ())), preferred_element_type=F32)


def _dot_tn(a, b):
    return lax.dot_general(a, b, (((0,), (0,)), ((), ())), preferred_element_type=F32)


def _in_proj_kernel(x_ref, g_ref, w_ref, o_ref, h_ref):
    @pl.when(pl.program_id(1) == 0)
    def _():
        x = x_ref[...]
        ms = jnp.mean(x * x, axis=-1, keepdims=True)
        h_ref[...] = (x * lax.rsqrt(ms + RMS_EPS) * g_ref[...]).astype(BF16)

    o_ref[...] = jnp.dot(h_ref[...], w_ref[0].astype(BF16), preferred_element_type=F32).astype(o_ref.dtype)


def _in_proj(x2d, g, w_in):
    n = x2d.shape[0]
    n_col = IN_WIDTH // IN_TN

    def out_block(j):
        gates = j - REF_OFF_GATE_A // IN_TN + OFF_GATE_A // IN_TN
        q_a = j - REF_OFF_QA // IN_TN + OFF_QA // IN_TN
        kv = j - REF_OFF_KA // IN_TN + OFF_KA // IN_TN
        hg = j - REF_OFF_QH // IN_TN + OFF_QH // IN_TN
        return jnp.where(j >= REF_OFF_GATE_A // IN_TN, gates,
                         jnp.where(j >= REF_OFF_QH // IN_TN, hg, jnp.where(j >= REF_OFF_KA // IN_TN, kv, q_a)))

    return pl.pallas_call(
        _in_proj_kernel,
        grid=(n // IN_TM, n_col),
        in_specs=[
            pl.BlockSpec((IN_TM, D_MODEL), lambda i, j: (i, 0)),
            pl.BlockSpec((1, D_MODEL), lambda i, j: (0, 0)),
            pl.BlockSpec((1, D_MODEL, IN_TN), lambda i, j: (0, 0, j)),
        ],
        out_specs=pl.BlockSpec((IN_TM, IN_TN), lambda i, j: (i, out_block(j))),
        out_shape=jax.ShapeDtypeStruct((n, IN_WIDTH), BF16),
        scratch_shapes=[pltpu.VMEM((IN_TM, D_MODEL), BF16)],
        compiler_params=pltpu.CompilerParams(
            dimension_semantics=("parallel", "arbitrary"), vmem_limit_bytes=_vmem(58)),
        name="in_proj",
    )(x2d, g, w_in)


def _swa_block(q, kk, vv, has_prev, bias_ref, sink_ref, side_work):
    q = q * jnp.asarray(HEAD_DIM ** -0.5, BF16)
    key = lax.broadcasted_iota(jnp.int32, (ATTN_BLOCK, ATTN_BLOCK), 0)
    qry = lax.broadcasted_iota(jnp.int32, (ATTN_BLOCK, ATTN_BLOCK), 1)
    cur = key <= qry
    key_ok = jnp.logical_or(cur, has_prev)
    grp = N_Q_HEADS // N_KV_HEADS
    head = lambda t, h: t[:, h * HEAD_DIM:(h + 1) * HEAD_DIM]
    lanes = lambda t, g: t[:, g * ATTN_BLOCK:(g + 1) * ATTN_BLOCK]
    scores = []
    for hk in range(N_KV_HEADS):
        q_grp = jnp.concatenate([head(q, hk * grp + g) for g in range(grp)], axis=0)
        scores.append(_dot_nt(head(kk, hk), q_grp))
    emitted = 0
    probs, denoms = [], []
    for hq in range(N_Q_HEADS):
        while emitted * N_Q_HEADS < hq * len(side_work) + len(side_work):
            side_work[emitted]()
            emitted += 1
        s2 = lanes(scores[hq // grp], hq % grp)
        s = jnp.where(cur, s2[ATTN_BLOCK:], s2[:ATTN_BLOCK]) + bias_ref[hq]
        s = jnp.where(key_ok, s, NEG_INF)
        sink = sink_ref[hq]
        m = jnp.maximum(jnp.max(s, axis=0, keepdims=True), sink)
        p = jnp.exp(s - m)
        denoms.append(jnp.sum(p, axis=0, keepdims=True) + jnp.exp(sink - m))
        probs.append(jnp.concatenate([jnp.where(cur, 0.0, p), jnp.where(cur, p, 0.0)], axis=0).astype(BF16))
    outs = []
    for hk in range(N_KV_HEADS):
        p_grp = jnp.concatenate(probs[hk * grp:(hk + 1) * grp], axis=1)
        o_grp = _dot_tn(head(vv, hk), p_grp)
        outs += [lanes(o_grp, g) / denoms[hk * grp + g] for g in range(grp)]
    return jnp.concatenate(outs, axis=0)


def _t5_causal_bucket(n):
    max_exact = REL_BUCKETS // 2
    nf = jnp.maximum(n, 1).astype(F32)
    large = max_exact + (jnp.log(nf / max_exact) / math.log(REL_MAX_DIST / max_exact)
                         * (REL_BUCKETS - max_exact)).astype(jnp.int32)
    large = jnp.minimum(large, REL_BUCKETS - 1)
    return jnp.where(n < max_exact, n, large)


def _attn_bias_table(rel_bias):
    assert WINDOW == ATTN_BLOCK
    r = jnp.arange(ATTN_BLOCK)[None, :]
    c = jnp.arange(ATTN_BLOCK)[:, None]
    dist = jnp.where(c <= r, r - c, r + ATTN_BLOCK - c)
    bucket = _t5_causal_bucket(dist)
    onehot = (bucket[None, :, :] == jnp.arange(REL_BUCKETS)[:, None, None]).astype(F32)
    return jnp.einsum("bh,bqk->hqk", rel_bias.astype(F32), onehot, precision=lax.Precision.HIGHEST)


def _cumsum_rows(tri_bf16, g):
    width = g.shape[1]
    g1 = g.astype(BF16)
    r1 = g - g1.astype(F32)
    g2 = r1.astype(BF16)
    g3 = (r1 - g2.astype(F32)).astype(BF16)
    acc = jnp.dot(tri_bf16, jnp.concatenate([g3, g2, g1], axis=1), preferred_element_type=F32)
    return (acc[:, :width] + acc[:, width:2 * width]) + acc[:, 2 * width:]


def _hgrn_head_safe(q, kf, v, b, st):
    rows = lax.broadcasted_iota(jnp.int32, (HG_SUB, 1), 0)
    outs = []
    prev_end = jnp.zeros((1, HG_DIM), F32)
    for i in range(HG_CHUNK // HG_SUB):
        sl = slice(i * HG_SUB, (i + 1) * HG_SUB)
        bl = b[sl] - prev_end
        qi, ki, vi = q[sl], kf[sl], v[sl]
        vif = vi.astype(F32)
        blast = bl[HG_SUB - 1:HG_SUB]
        o = _dot_nt((qi * jnp.exp(bl)).astype(BF16), st.astype(BF16))
        for s in range(HG_SUB):
            d = jnp.exp(jnp.minimum(bl - bl[s:s + 1], 0.0))
            a = jnp.sum(qi * ki[s:s + 1] * d, axis=-1, keepdims=True)
            o = o + jnp.where(rows >= s, a, 0.0) * vif[s:s + 1]
        kd = (ki * jnp.exp(blast - bl)).astype(BF16)
        st = st * jnp.exp(blast) + _dot_tn(vi, kd)
        prev_end = b[(i + 1) * HG_SUB - 1:(i + 1) * HG_SUB]
        outs.append(o)
    return jnp.concatenate(outs, axis=0), st


def _hgrn_kernel(q_ref, f_ref, i_ref, g_ref, lbl_ref, gn_ref, o_ref, st_ref, kf_ref, b_ref):
    @pl.when(pl.program_id(1) == 0)
    def _():
        st_ref[...] = jnp.zeros_like(st_ref)

    lg2 = lbl_ref[...].astype(F32)
    e = jnp.exp(lg2 - jnp.max(lg2, axis=0, keepdims=True))
    lb = e[0:1] / jnp.sum(e, axis=0, keepdims=True)
    gn = gn_ref[...].astype(F32)
    r = lax.broadcasted_iota(jnp.int32, (HG_CHUNK, HG_CHUNK), 0)
    c = lax.broadcasted_iota(jnp.int32, (HG_CHUNK, HG_CHUNK), 1)
    tri_mask = r >= c
    tri = jnp.where(tri_mask, 1.0, 0.0).astype(BF16)

    n_chunks = HG_TC // HG_CHUNK

    f = lb + (1.0 - lb) * _sigmoid(f_ref[0].astype(F32))
    kf_ref[...] = 1.0 - f
    logf = jnp.log(f)
    worst = None
    for ci in range(n_chunks):
        sl = slice(ci * HG_CHUNK, (ci + 1) * HG_CHUNK)
        b = _cumsum_rows(tri, logf[sl])
        b_ref[sl, :] = b
        bend = b[HG_CHUNK - 1:HG_CHUNK]
        worst = bend if worst is None else jnp.minimum(worst, bend)

    heads = range(HG_HEADS)
    head = lambda t, h: t[:, h * HG_DIM:(h + 1) * HG_DIM]

    def finish_all(sl, outs):
        scale = [lax.rsqrt(jnp.mean(o * o, axis=-1, keepdims=True) + RMS_EPS) for o in outs]
        normed = jnp.concatenate([outs[h] * scale[h] * gn for h in heads], axis=1)
        go = g_ref[0, sl, :].astype(F32)
        o_ref[0, sl, :] = (normed * (go * _sigmoid(go))).astype(o_ref.dtype)

    def fast():
        states = [st_ref[h] for h in heads]
        for ci in range(n_chunks):
            sl = slice(ci * HG_CHUNK, (ci + 1) * HG_CHUNK)
            b = b_ref[sl, :]
            qp = (q_ref[0, sl, :].astype(F32) * jnp.exp(b)).astype(BF16)
            kp = kf_ref[sl, :] * jnp.exp(-b)
            kpb = kp.astype(BF16)
            eb = jnp.exp(b[HG_CHUNK - 1:HG_CHUNK])
            kd = (kp * eb).astype(BF16)
            v = i_ref[0, sl, :]
            att = [_dot_nt(head(qp, h), head(kpb, h)) for h in heads]
            inter = [_dot_nt(head(qp, h), states[h].astype(BF16)) for h in heads]
            update = [_dot_tn(head(v, h), head(kd, h)) for h in heads]
            att = [jnp.where(tri_mask, a, 0.0).astype(BF16) for a in att]
            outs = [inter[h] + jnp.dot(att[h], head(v, h), preferred_element_type=F32) for h in heads]
            states = [states[h] * head(eb, h) + update[h] for h in heads]
            finish_all(sl, outs)
        for h in heads:
            st_ref[h] = states[h]

    def safe():
        def chunk(ci, carry):
            sl = pl.ds(pl.multiple_of(ci * HG_CHUNK, HG_CHUNK), HG_CHUNK)
            outs = []
            for h in heads:
                hs = slice(h * HG_DIM, (h + 1) * HG_DIM)
                o, st_new = _hgrn_head_safe(q_ref[0, sl, hs].astype(F32), kf_ref[sl, hs], i_ref[0, sl, hs],
                                            b_ref[sl, hs], st_ref[h])
                st_ref[h] = st_new
                outs.append(o)
            finish_all(sl, outs)
            return carry

        lax.fori_loop(0, n_chunks, chunk, 0)

    lax.cond(jnp.min(worst) >= -HG_SAFE_DECAY, fast, safe)


def _hgrn2(proj3, lb_logits, gn):
    bsz, s_len, _ = proj3.shape
    blk = lambda off: (lambda b, t: (b, t, off // HG_WIDTH))
    return pl.pallas_call(
        _hgrn_kernel,
        grid=(bsz, s_len // HG_TC),
        in_specs=[
            pl.BlockSpec((1, HG_TC, HG_WIDTH), blk(OFF_QH)),
            pl.BlockSpec((1, HG_TC, HG_WIDTH), blk(OFF_FH)),
            pl.BlockSpec((1, HG_TC, HG_WIDTH), blk(OFF_IH)),
            pl.BlockSpec((1, HG_TC, HG_WIDTH), blk(OFF_GH)),
            pl.BlockSpec((lb_logits.shape[0], HG_WIDTH), lambda b, t: (0, 0)),
            pl.BlockSpec((1, HG_DIM), lambda b, t: (0, 0)),
        ],
        out_specs=pl.BlockSpec((1, HG_TC, HG_WIDTH), lambda b, t: (b, t, 0)),
        out_shape=jax.ShapeDtypeStruct((bsz, s_len, HG_WIDTH), BF16),
        scratch_shapes=[
            pltpu.VMEM((HG_HEADS, HG_DIM, HG_DIM), F32),
            pltpu.VMEM((HG_TC, HG_WIDTH), F32),
            pltpu.VMEM((HG_TC, HG_WIDTH), F32),
        ],
        compiler_params=pltpu.CompilerParams(
            dimension_semantics=("parallel", "arbitrary"), vmem_limit_bytes=_vmem(48)),
        name="hgrn2",
    )(proj3, proj3, proj3, proj3, lb_logits, gn)


def _store_rows_as_tiles(ref, val):
    rows = val.shape[0]
    for s in range(ROW_TILE):
        ref[pl.ds(s, rows, stride=ROW_TILE), :] = val[:, s * LANES:(s + 1) * LANES]


def _load_rows_from_tiles(ref, start, rows):
    return jnp.concatenate(
        [ref[pl.ds(start * ROW_TILE + s, rows, stride=ROW_TILE), :] for s in range(ROW_TILE)], axis=1)


def _pack_bf16_pairs(x):
    c = x.shape[1] // 2
    bits = pltpu.bitcast(x.astype(BF16).astype(F32), U32)
    return jnp.bitwise_or(jnp.right_shift(bits[:, :c], jnp.uint32(16)), bits[:, c:])


def _unpack_bf16_pairs(p):
    lo = pltpu.bitcast(jnp.left_shift(p, jnp.uint32(16)), F32)
    hi = pltpu.bitcast(jnp.bitwise_and(p, jnp.uint32(0xFFFF0000)), F32)
    return lo, hi


def _mix_kernel(sink_ref, q_ref, kp_ref, kc_ref, vp_ref, vc_ref, bias_ref,
                r_ref, ga_ref, gh_ref, x_ref, wa_ref, wr_ref, wo_ref, g2_ref, wr2_ref, br_ref,
                x1_ref, h2p_ref, route_ref, cnt_out_ref, cnt_ref, *, tiles_per_seq):
    i = pl.program_id(0)

    @pl.when(i == 0)
    def _():
        cnt_ref[...] = jnp.zeros_like(cnt_ref)

    r_tile = r_ref[...]
    n_blk = MIX_TM // ATTN_BLOCK
    n_piece = D_MODEL // MXU_WIDTH
    rm_pieces = []

    def rm_piece(c):
        def emit():
            rm_pieces.append(jnp.dot(r_tile, wr_ref[:, c * MXU_WIDTH:(c + 1) * MXU_WIDTH],
                                     preferred_element_type=F32))
        return emit

    k_rows = jnp.concatenate([kp_ref[...], kc_ref[...]], axis=0)
    v_rows = jnp.concatenate([vp_ref[...], vc_ref[...]], axis=0)
    first_in_seq = lax.rem(i, tiles_per_seq) == 0
    blocks = []
    for blk in range(n_blk):
        rows = slice(blk * ATTN_BLOCK, (blk + 1) * ATTN_BLOCK)
        window = slice(blk * ATTN_BLOCK, (blk + 2) * ATTN_BLOCK)
        has_prev = jnp.logical_not(first_in_seq) if blk == 0 else True
        side = [rm_piece(c) for c in range(blk * n_piece // n_blk, (blk + 1) * n_piece // n_blk)]
        blocks.append(_swa_block(q_ref[rows, :], k_rows[window], v_rows[window], has_prev, bias_ref, sink_ref, side))
    rm = jnp.concatenate(rm_pieces, axis=1)
    a_t = jnp.concatenate(blocks, axis=1).astype(BF16)
    am = _dot_tn(a_t, wa_ref[...])
    mixed = _sigmoid(ga_ref[...].astype(F32)) * am + _sigmoid(gh_ref[...].astype(F32)) * rm
    x1 = x_ref[...] + jnp.dot(mixed.astype(BF16), wo_ref[...], preferred_element_type=F32)
    x1_ref[...] = x1
    h2 = x1 * lax.rsqrt(jnp.mean(x1 * x1, axis=-1, keepdims=True) + RMS_EPS) * g2_ref[...]

    _store_rows_as_tiles(h2p_ref, _pack_bf16_pairs(h2))

    hi = h2.astype(BF16)
    lo = (h2 - hi.astype(F32)).astype(BF16)
    both = jnp.dot(hi, wr2_ref[...], preferred_element_type=F32)
    logits = jnp.dot(lo, wr2_ref[:, :ROUTE_LANES], preferred_element_type=F32) + both[:, ROUTE_LANES:]
    logits = logits + both[:, :ROUTE_LANES] + br_ref[...]

    lane = lax.broadcasted_iota(jnp.int32, logits.shape, 1)
    lane_f = lane.astype(F32)
    big = float(ROUTE_LANES)
    gl = jnp.where(lane < N_GROUPS, logits, NEG_INF)
    gmax = jnp.max(gl, axis=-1, keepdims=True)
    g_idx = jnp.min(jnp.where(gl == gmax, lane_f, big), axis=-1, keepdims=True)
    g_w = 1.0 / jnp.sum(jnp.exp(gl - gmax), axis=-1, keepdims=True)
    e_lane = lane - N_GROUPS
    in_group = jnp.logical_and(e_lane >= 0, e_lane < N_EXPERTS)
    lane_group = jnp.right_shift(e_lane, 3).astype(F32)
    in_group = jnp.logical_and(in_group, lane_group == g_idx)
    el = jnp.where(in_group, logits, NEG_INF)
    m1 = jnp.max(el, axis=-1, keepdims=True)
    i1 = jnp.min(jnp.where(el == m1, lane_f, big), axis=-1, keepdims=True)
    el2 = jnp.where(lane_f == i1, NEG_INF, el)
    m2 = jnp.max(el2, axis=-1, keepdims=True)
    i2 = jnp.min(jnp.where(el2 == m2, lane_f, big), axis=-1, keepdims=True)
    p2 = jnp.exp(m2 - m1)
    w1 = g_w / (1.0 + p2)
    w2 = g_w * p2 / (1.0 + p2)
    e1 = i1 - N_GROUPS
    e2 = i2 - N_GROUPS

    tm = logits.shape[0]
    rr = lax.broadcasted_iota(jnp.int32, (tm, tm), 0)
    cc = lax.broadcasted_iota(jnp.int32, (tm, tm), 1)
    ltri = jnp.where(rr > cc, 1.0, 0.0).astype(BF16)
    sel1 = lane_f == e1
    sel2 = lane_f == e2
    oh1 = jnp.where(sel1, 1.0, 0.0)
    oh2 = jnp.where(sel2, 1.0, 0.0)
    base = cnt_ref[...]
    c1 = jnp.sum(oh1, axis=0, keepdims=True)
    c2 = jnp.sum(oh2, axis=0, keepdims=True)
    pre1 = jnp.dot(ltri, oh1.astype(BF16), preferred_element_type=F32) + base
    pre2 = jnp.dot(ltri, oh2.astype(BF16), preferred_element_type=F32) + (base + c1)
    rank1 = jnp.sum(jnp.where(sel1, pre1, 0.0), axis=-1, keepdims=True)
    rank2 = jnp.sum(jnp.where(sel2, pre2, 0.0), axis=-1, keepdims=True)
    total = base + c1 + c2
    cnt_ref[...] = total
    cnt_out_ref[...] = total

    route = jnp.zeros_like(logits)
    for k, val in enumerate((w1, w2, e1, e2, rank1, rank2)):
        route = jnp.where(lane == k, val, route)
    route_ref[...] = route


def _mix(sinks, bias, r2d, proj, x2d, wa, wr, wo, g2, wr2, br, seq_len):
    n = x2d.shape[0]
    row = lambda i: (i, 0)
    const = lambda i: (0, 0)
    blocks_per_tile = MIX_TM // ATTN_BLOCK
    prev_block = lambda i: jnp.maximum(blocks_per_tile * i - 1, 0)
    return pl.pallas_call(
        functools.partial(_mix_kernel, tiles_per_seq=seq_len // MIX_TM),
        grid=(n // MIX_TM,),
        in_specs=[
            pl.BlockSpec(memory_space=pltpu.SMEM),
            pl.BlockSpec((MIX_TM, ATTN_WIDTH), lambda i: (i, OFF_QA // ATTN_WIDTH)),
            pl.BlockSpec((ATTN_BLOCK, KV_WIDTH), lambda i: (prev_block(i), OFF_KA // KV_WIDTH)),
            pl.BlockSpec((MIX_TM, KV_WIDTH), lambda i: (i, OFF_KA // KV_WIDTH)),
            pl.BlockSpec((ATTN_BLOCK, KV_WIDTH), lambda i: (prev_block(i), OFF_VA // KV_WIDTH)),
            pl.BlockSpec((MIX_TM, KV_WIDTH), lambda i: (i, OFF_VA // KV_WIDTH)),
            pl.BlockSpec((N_Q_HEADS, ATTN_BLOCK, ATTN_BLOCK), lambda i: (0, 0, 0)),
            pl.BlockSpec((MIX_TM, HG_WIDTH), row),
            pl.BlockSpec((MIX_TM, D_MODEL), lambda i: (i, OFF_GATE_A // D_MODEL)),
            pl.BlockSpec((MIX_TM, D_MODEL), lambda i: (i, OFF_GATE_H // D_MODEL)),
            pl.BlockSpec((MIX_TM, D_MODEL), row),
            pl.BlockSpec((ATTN_WIDTH, D_MODEL), const, pipeline_mode=pl.Buffered(1)),
            pl.BlockSpec((HG_WIDTH, D_MODEL), const, pipeline_mode=pl.Buffered(1)),
            pl.BlockSpec((D_MODEL, D_MODEL), const, pipeline_mode=pl.Buffered(1)),
            pl.BlockSpec((1, D_MODEL), const),
            pl.BlockSpec((D_MODEL, 2 * ROUTE_LANES), const, pipeline_mode=pl.Buffered(1)),
            pl.BlockSpec((1, ROUTE_LANES), const),
        ],
        out_specs=[
            pl.BlockSpec((MIX_TM, D_MODEL), row),
            pl.BlockSpec((MIX_TM * ROW_TILE, LANES), row),
            pl.BlockSpec((MIX_TM, ROUTE_LANES), row),
            pl.BlockSpec((1, ROUTE_LANES), const),
        ],
        out_shape=[
            jax.ShapeDtypeStruct((n, D_MODEL), F32),
            jax.ShapeDtypeStruct((n * ROW_TILE, LANES), U32),
            jax.ShapeDtypeStruct((n, ROUTE_LANES), F32),
            jax.ShapeDtypeStruct((1, ROUTE_LANES), F32),
        ],
        scratch_shapes=[pltpu.VMEM((1, ROUTE_LANES), F32)],
        compiler_params=pltpu.CompilerParams(dimension_semantics=("arbitrary",), vmem_limit_bytes=_vmem(60)),
        name="mix",
    )(sinks, proj, proj, proj, proj, proj, bias, r2d, proj, proj, x2d, wa, wr, wo, g2, wr2, br)


def _dispatch_kernel(dest_ref, src_ref, zeros_hbm, dst_hbm, sem):
    del zeros_hbm

    for a in range(DISPATCH_ROWS):
        pltpu.make_async_copy(src_ref.at[a // TOP_K], dst_hbm.at[dest_ref[0, 0, a]], sem.at[0]).start(priority=a % 2)
    for _ in range(TOP_K):
        pltpu.make_async_copy(src_ref, dst_hbm.at[pl.ds(0, DISPATCH_ROWS // TOP_K)], sem.at[0]).wait()


def _dispatch(dest_flat, h2p_tiles, n_rows):
    n_asg = dest_flat.shape[0]
    steps = n_asg // DISPATCH_ROWS
    return pl.pallas_call(
        _dispatch_kernel,
        grid=(steps,),
        in_specs=[
            pl.BlockSpec((1, 1, DISPATCH_ROWS), lambda s: (s, 0, 0), memory_space=pltpu.SMEM),
            pl.BlockSpec((DISPATCH_ROWS // TOP_K, ROW_TILE, LANES), lambda s: (s, 0, 0)),
            pl.BlockSpec(memory_space=pl.ANY),
        ],
        out_specs=pl.BlockSpec(memory_space=pl.ANY),
        out_shape=jax.ShapeDtypeStruct((n_rows, ROW_TILE, LANES), U32),
        input_output_aliases={2: 0},
        scratch_shapes=[pltpu.SemaphoreType.DMA((1,))],
        compiler_params=pltpu.CompilerParams(dimension_semantics=("arbitrary",)),
        name="dispatch",
    )(dest_flat.reshape(steps, 1, DISPATCH_ROWS), h2p_tiles, jnp.zeros((n_rows, ROW_TILE, LANES), U32))


def _moe_kernel(be_ref, nused_ref, first_ref, slot_ref, next_ref, xs_ref, wg_hbm, wu_hbm, wd_hbm, y_ref,
                wg_buf, wu_buf, wd_buf, wg_s, wu_s, wd_s, sem):
    b = pl.program_id(0)
    live = b < nused_ref[0]

    def weight_copies(e, s):
        return (pltpu.make_async_copy(wg_hbm.at[0, e], wg_buf.at[s], sem.at[s, 0]),
                pltpu.make_async_copy(wu_hbm.at[0, e], wu_buf.at[s], sem.at[s, 1]),
                pltpu.make_async_copy(wd_hbm.at[0, e], wd_buf.at[s], sem.at[s, 2]))

    @pl.when(b == 0)
    def _():
        for c in weight_copies(be_ref[0], 0):
            c.start()

    for s in range(2):
        @pl.when(jnp.logical_and(jnp.logical_and(live, first_ref[b] == 1), slot_ref[b] == s))
        def _():
            for c in weight_copies(be_ref[b], s):
                c.wait()

            @pl.when(next_ref[b] >= 0)
            def _():
                for c in weight_copies(next_ref[b], 1 - s):
                    c.start()

            wg_s[...] = wg_buf[s].astype(BF16)
            wu_s[...] = wu_buf[s].astype(BF16)
            wd_s[...] = wd_buf[s].astype(BF16)

    @pl.when(jnp.logical_not(live))
    def _():
        y_ref[...] = jnp.zeros_like(y_ref)

    @pl.when(live)
    def _():
        xlo, xhi = _unpack_bf16_pairs(_load_rows_from_tiles(xs_ref, 0, MOE_TM))
        xb = jnp.concatenate([xlo.astype(BF16), xhi.astype(BF16)], axis=1)
        hb = []
        for c in range(EXPERT_FF // MXU_WIDTH):
            cols = slice(c * MXU_WIDTH, (c + 1) * MXU_WIDTH)
            hg = jnp.dot(xb, wg_s[:, cols], preferred_element_type=F32)
            hu = jnp.dot(xb, wu_s[:, cols], preferred_element_type=F32)
            hb.append((hg * _sigmoid(hg) * hu).astype(BF16))
        hb = jnp.concatenate(hb, axis=1)
        half = D_MODEL // 2
        for c in range(half // MXU_WIDTH):
            lo = jnp.dot(hb, wd_s[:, c * MXU_WIDTH:(c + 1) * MXU_WIDTH], preferred_element_type=F32)
            hi = jnp.dot(hb, wd_s[:, half + c * MXU_WIDTH:half + (c + 1) * MXU_WIDTH], preferred_element_type=F32)
            packed = _pack_bf16_pairs(jnp.concatenate([lo, hi], axis=1))
            for k in range(MXU_WIDTH // LANES):
                s = c * (MXU_WIDTH // LANES) + k
                y_ref[pl.ds(s, MOE_TM, stride=ROW_TILE), :] = packed[:, k * LANES:(k + 1) * LANES]


def _moe(block_e, n_used, first, slot, next_e, xs, w_gate, w_up, w_down):
    n_rows = xs.shape[0] // ROW_TILE
    n_blocks = n_rows // MOE_TM
    grid_spec = pltpu.PrefetchScalarGridSpec(
        num_scalar_prefetch=5,
        grid=(n_blocks,),
        in_specs=[
            pl.BlockSpec((MOE_TM * ROW_TILE, LANES), lambda b, be, nu, *_: (jnp.minimum(b, nu[0] - 1), 0)),
            pl.BlockSpec(memory_space=pl.ANY),
            pl.BlockSpec(memory_space=pl.ANY),
            pl.BlockSpec(memory_space=pl.ANY),
        ],
        out_specs=pl.BlockSpec((MOE_TM * ROW_TILE, LANES), lambda b, *_: (b, 0)),
        scratch_shapes=[
            pltpu.VMEM((2, D_MODEL, EXPERT_FF), F32),
            pltpu.VMEM((2, D_MODEL, EXPERT_FF), F32),
            pltpu.VMEM((2, EXPERT_FF, D_MODEL), F32),
            pltpu.VMEM((D_MODEL, EXPERT_FF), BF16),
            pltpu.VMEM((D_MODEL, EXPERT_FF), BF16),
            pltpu.VMEM((EXPERT_FF, D_MODEL), BF16),
            pltpu.SemaphoreType.DMA((2, 3)),
        ],
    )
    return pl.pallas_call(
        _moe_kernel,
        grid_spec=grid_spec,
        out_shape=jax.ShapeDtypeStruct((n_rows * ROW_TILE, LANES), U32),
        compiler_params=pltpu.CompilerParams(dimension_semantics=("arbitrary",), vmem_limit_bytes=_vmem(52)),
        name="moe",
    )(block_e, n_used, first, slot, next_e, xs, w_gate, w_up, w_down)


def _moe_layout(route, counts_f):
    counts = counts_f[0, :N_EXPERTS].astype(jnp.int32)
    padded = (counts + MOE_TM - 1) // MOE_TM * MOE_TM
    pends = jnp.cumsum(padded)
    pstarts = pends - padded
    n_tok = route.shape[0]
    n_blocks = n_tok * TOP_K // MOE_TM + N_EXPERTS
    blk_start = jnp.arange(n_blocks, dtype=jnp.int32) * MOE_TM
    block_e = jnp.minimum(jnp.sum((blk_start[:, None] >= pends[None, :]).astype(jnp.int32), axis=1), N_EXPERTS - 1)
    n_used = (pends[-1] // MOE_TM).astype(jnp.int32).reshape(1)
    blk = jnp.arange(n_blocks, dtype=jnp.int32)
    first = jnp.logical_and(blk < n_used[0], jnp.logical_or(blk == 0, block_e != jnp.roll(block_e, 1)))
    slot = jnp.bitwise_and(jnp.cumsum(first.astype(jnp.int32)) - 1, 1)
    ex = jnp.arange(N_EXPERTS, dtype=jnp.int32)
    later = jnp.logical_and(counts[None, :] > 0, ex[None, :] > ex[:, None])
    next_of_expert = jnp.min(jnp.where(later, ex[None, :], N_EXPERTS), axis=1)
    next_of_expert = jnp.where(next_of_expert == N_EXPERTS, -1, next_of_expert)
    next_e = jnp.sum(jnp.where(block_e[:, None] == ex[None, :], next_of_expert[None, :], 0), axis=1)
    e_id = route[:, 2:2 + TOP_K].astype(jnp.int32)
    rank = route[:, 2 + TOP_K:2 + 2 * TOP_K].astype(jnp.int32)
    onehot_e = e_id[:, :, None] == jnp.arange(N_EXPERTS, dtype=jnp.int32)[None, None, :]
    dest = jnp.sum(jnp.where(onehot_e, pstarts[None, None, :], 0), axis=2) + rank
    tables = (block_e.astype(jnp.int32), n_used, first.astype(jnp.int32), slot.astype(jnp.int32),
              next_e.astype(jnp.int32))
    return tables, dest, n_blocks * MOE_TM


def _final_kernel(dest_ref, destn_ref, x1_ref, route_ref, g_ref, y_hbm, y_flat_hbm, o_ref, ybuf, sem):
    i = pl.program_id(0)
    n_steps = pl.num_programs(0)
    slot = lax.rem(i, 2)
    n_rows = TOP_K * FIN_TM

    def start_gather(idx_ref, s):
        for r in range(n_rows):
            pltpu.make_async_copy(y_hbm.at[idx_ref[0, 0, r]], ybuf.at[pl.ds((s * n_rows + r) * ROW_TILE, ROW_TILE)],
                                  sem.at[s]).start(priority=r % 2)

    @pl.when(i == 0)
    def _():
        start_gather(dest_ref, 0)

    for s in range(2):
        @pl.when(jnp.logical_and(i + 1 < n_steps, 1 - slot == s))
        def _():
            start_gather(destn_ref, s)

    slot_rows = n_rows * ROW_TILE
    slot_start = pl.multiple_of(slot * slot_rows, slot_rows)
    pltpu.make_async_copy(y_flat_hbm.at[pl.ds(0, slot_rows)], ybuf.at[pl.ds(slot_start, slot_rows)],
                          sem.at[slot]).wait()
    route = route_ref[...]
    x = x1_ref[...]
    for k in range(TOP_K):
        ylo, yhi = _unpack_bf16_pairs(_load_rows_from_tiles(ybuf, slot * n_rows + k * FIN_TM, FIN_TM))
        x = x + route[:, k:k + 1] * jnp.concatenate([ylo, yhi], axis=1)
    o_ref[...] = x * lax.rsqrt(jnp.mean(x * x, axis=-1, keepdims=True) + RMS_EPS) * g_ref[...]


def _final(dest, x1, route, g, y_sorted):
    n = x1.shape[0]
    steps = n // FIN_TM
    dest3 = dest.reshape(steps, FIN_TM, TOP_K).transpose(0, 2, 1).reshape(steps, 1, TOP_K * FIN_TM)
    return pl.pallas_call(
        _final_kernel,
        grid=(steps,),
        in_specs=[
            pl.BlockSpec((1, 1, TOP_K * FIN_TM), lambda i: (i, 0, 0), memory_space=pltpu.SMEM),
            pl.BlockSpec((1, 1, TOP_K * FIN_TM), lambda i: (jnp.minimum(i + 1, steps - 1), 0, 0),
                         memory_space=pltpu.SMEM),
            pl.BlockSpec((FIN_TM, D_MODEL), lambda i: (i, 0)),
            pl.BlockSpec((FIN_TM, ROUTE_LANES), lambda i: (i, 0)),
            pl.BlockSpec((1, D_MODEL), lambda i: (0, 0)),
            pl.BlockSpec(memory_space=pl.ANY),
            pl.BlockSpec(memory_space=pl.ANY),
        ],
        out_specs=pl.BlockSpec((FIN_TM, D_MODEL), lambda i: (i, 0)),
        out_shape=jax.ShapeDtypeStruct((n, D_MODEL), F32),
        scratch_shapes=[
            pltpu.VMEM((2 * TOP_K * FIN_TM * ROW_TILE, LANES), U32),
            pltpu.SemaphoreType.DMA((2,)),
        ],
        compiler_params=pltpu.CompilerParams(dimension_semantics=("arbitrary",), vmem_limit_bytes=_vmem(40)),
        name="final",
    )(dest3, dest3, x1, route, g, y_sorted.reshape(-1, ROW_TILE, LANES), y_sorted)


def kernel(x, norm1_g, w_in, attn_sinks, rel_bias, hg_lb_logits, hg_norm_g, w_attn_branch, w_hg_branch, w_out,
           norm2_g, w_group_router, b_group_router, w_expert_router, b_expert_router, w_gate, w_up, w_down, final_g):
    bsz, s_len, d = x.shape
    n_tok = bsz * s_len
    x2d = x.reshape(n_tok, d)

    w_route = jnp.concatenate([w_group_router[0], w_expert_router[0]], axis=1).astype(F32)
    w_route = jnp.pad(w_route, ((0, 0), (0, ROUTE_LANES - w_route.shape[1])))
    whi = w_route.astype(BF16)
    wlo = (w_route - whi.astype(F32)).astype(BF16)
    wr2 = jnp.concatenate([whi, wlo], axis=1)
    b_route = jnp.concatenate([b_group_router[0], b_expert_router[0]]).astype(F32)
    b_route = jnp.pad(b_route, (0, ROUTE_LANES - b_route.shape[0])).reshape(1, ROUTE_LANES)
    bias = _attn_bias_table(rel_bias)

    proj = _in_proj(x2d, norm1_g[0].reshape(1, d).astype(F32), w_in)
    proj3 = proj.reshape(bsz, s_len, IN_WIDTH)
    r = _hgrn2(proj3, hg_lb_logits.astype(F32), hg_norm_g[0].reshape(1, HG_DIM).astype(F32))
    x1, h2p, route, counts = _mix(
        attn_sinks[0].astype(F32), bias, r.reshape(n_tok, HG_WIDTH), proj, x2d,
        w_attn_branch[0].astype(BF16), w_hg_branch[0].astype(BF16), w_out[0].astype(BF16),
        norm2_g[0].reshape(1, d).astype(F32), wr2, b_route, s_len)
    tables, dest, n_rows = _moe_layout(route, counts)
    xs = _dispatch(dest.reshape(n_tok * TOP_K), h2p.reshape(n_tok, ROW_TILE, LANES), n_rows)
    y_sorted = _moe(*tables, xs.reshape(n_rows * ROW_TILE, LANES), w_gate, w_up, w_down)
    out = _final(dest, x1, route, final_g.reshape(1, d).astype(F32), y_sorted)
    return out.reshape(bsz, s_len, d)
```

```python
import functools
import math

import jax
import jax.numpy as jnp
from jax import lax
from jax.experimental import pallas as pl
from jax.experimental.pallas import tpu as pltpu

D_MODEL = 2048
N_Q_HEADS = 16
N_KV_HEADS = 4
HEAD_DIM = 64
WINDOW = 128
ATTN_BLOCK = 128
ATTN_WIDTH = N_Q_HEADS * HEAD_DIM
KV_WIDTH = N_KV_HEADS * HEAD_DIM
REL_BUCKETS = 32
REL_MAX_DIST = 128
HG_HEADS = 8
HG_DIM = 128
HG_WIDTH = HG_HEADS * HG_DIM
HG_CHUNK = 64
HG_SUB = 16
N_GROUPS = 4
EXPERTS_PER_GROUP = 8
N_EXPERTS = N_GROUPS * EXPERTS_PER_GROUP
TOP_K = 2
EXPERT_FF = 512
RMS_EPS = 1e-6

OFF_GATE_A = 0
OFF_GATE_H = D_MODEL
OFF_QA = 2 * D_MODEL
OFF_QH = OFF_QA + ATTN_WIDTH
OFF_FH = OFF_QH + HG_WIDTH
OFF_IH = OFF_FH + HG_WIDTH
OFF_GH = OFF_IH + HG_WIDTH
OFF_KA = OFF_GH + HG_WIDTH
OFF_VA = OFF_KA + KV_WIDTH
IN_WIDTH = OFF_VA + KV_WIDTH
REF_OFF_QA = 0
REF_OFF_KA = ATTN_WIDTH
REF_OFF_QH = ATTN_WIDTH + 2 * KV_WIDTH
REF_OFF_GATE_A = REF_OFF_QH + 4 * HG_WIDTH

HG_SAFE_DECAY = 60.0

ROW_TILE = 8
LANES = 128
assert D_MODEL // 2 == ROW_TILE * LANES
MXU_WIDTH = 256

IN_TM = 2048
IN_TN = 512
HG_TC = 1024
MIX_TM = 512
ROUTE_LANES = 128
DISPATCH_ROWS = 1024
MOE_TM = 256
FIN_TM = 256

F32 = jnp.float32
BF16 = jnp.bfloat16
U32 = jnp.uint32
NEG_INF = float("-inf")


def _vmem(mib):
    return mib * 1024 * 1024


def _sigmoid(x):
    return 1.0 / (1.0 + jnp.exp(-x))


def _dot_nt(a, b):
    return lax.dot_general(a, b, (((1,), (1,)), ((), ())), preferred_element_type=F32)


def _dot_tn(a, b):
    return lax.dot_general(a, b, (((0,), (0,)), ((), ())), preferred_element_type=F32)


def _in_proj_kernel(x_ref, g_ref, w_ref, o_ref, h_ref):
    @pl.when(pl.program_id(1) == 0)
    def _():
        x = x_ref[...]
        ms = jnp.mean(x * x, axis=-1, keepdims=True)
        h_ref[...] = (x * lax.rsqrt(ms + RMS_EPS) * g_ref[...]).astype(BF16)

    o_ref[...] = jnp.dot(h_ref[...], w_ref[0].astype(BF16), preferred_element_type=F32).astype(o_ref.dtype)


def _in_proj(x2d, g, w_in):
    n = x2d.shape[0]
    n_col = IN_WIDTH // IN_TN

    def out_block(j):
        gates = j - REF_OFF_GATE_A // IN_TN + OFF_GATE_A // IN_TN
        q_a = j - REF_OFF_QA // IN_TN + OFF_QA // IN_TN
        kv = j - REF_OFF_KA // IN_TN + OFF_KA // IN_TN
        hg = j - REF_OFF_QH // IN_TN + OFF_QH // IN_TN
        return jnp.where(j >= REF_OFF_GATE_A // IN_TN, gates,
                         jnp.where(j >= REF_OFF_QH // IN_TN, hg, jnp.where(j >= REF_OFF_KA // IN_TN, kv, q_a)))

    return pl.pallas_call(
        _in_proj_kernel,
        grid=(n // IN_TM, n_col),
        in_specs=[
            pl.BlockSpec((IN_TM, D_MODEL), lambda i, j: (i, 0)),
            pl.BlockSpec((1, D_MODEL), lambda i, j: (0, 0)),
            pl.BlockSpec((1, D_MODEL, IN_TN), lambda i, j: (0, 0, j)),
        ],
        out_specs=pl.BlockSpec((IN_TM, IN_TN), lambda i, j: (i, out_block(j))),
        out_shape=jax.ShapeDtypeStruct((n, IN_WIDTH), BF16),
        scratch_shapes=[pltpu.VMEM((IN_TM, D_MODEL), BF16)],
        compiler_params=pltpu.CompilerParams(
            dimension_semantics=("parallel", "arbitrary"), vmem_limit_bytes=_vmem(58)),
        name="in_proj",
    )(x2d, g, w_in)


def _swa_block(q, kk, vv, has_prev, bias_ref, sink_ref, side_work):
    q = q * jnp.asarray(HEAD_DIM ** -0.5, BF16)
    key = lax.broadcasted_iota(jnp.int32, (ATTN_BLOCK, ATTN_BLOCK), 0)
    qry = lax.broadcasted_iota(jnp.int32, (ATTN_BLOCK, ATTN_BLOCK), 1)
    cur = key <= qry
    key_ok = jnp.logical_or(cur, has_prev)
    grp = N_Q_HEADS // N_KV_HEADS
    head = lambda t, h: t[:, h * HEAD_DIM:(h + 1) * HEAD_DIM]
    lanes = lambda t, g: t[:, g * ATTN_BLOCK:(g + 1) * ATTN_BLOCK]
    scores = []
    for hk in range(N_KV_HEADS):
        q_grp = jnp.concatenate([head(q, hk * grp + g) for g in range(grp)], axis=0)
        scores.append(_dot_nt(head(kk, hk), q_grp))
    emitted = 0
    probs, denoms = [], []
    for hq in range(N_Q_HEADS):
        while emitted * N_Q_HEADS < hq * len(side_work) + len(side_work):
            side_work[emitted]()
            emitted += 1
        s2 = lanes(scores[hq // grp], hq % grp)
        s = jnp.where(cur, s2[ATTN_BLOCK:], s2[:ATTN_BLOCK]) + bias_ref[hq]
        s = jnp.where(key_ok, s, NEG_INF)
        sink = sink_ref[hq]
        m = jnp.maximum(jnp.max(s, axis=0, keepdims=True), sink)
        p = jnp.exp(s - m)
        denoms.append(jnp.sum(p, axis=0, keepdims=True) + jnp.exp(sink - m))
        probs.append(jnp.concatenate([jnp.where(cur, 0.0, p), jnp.where(cur, p, 0.0)], axis=0).astype(BF16))
    outs = []
    for hk in range(N_KV_HEADS):
        p_grp = jnp.concatenate(probs[hk * grp:(hk + 1) * grp], axis=1)
        o_grp = _dot_tn(head(vv, hk), p_grp)
        outs += [lanes(o_grp, g) / denoms[hk * grp + g] for g in range(grp)]
    return jnp.concatenate(outs, axis=0)


def _t5_causal_bucket(n):
    max_exact = REL_BUCKETS // 2
    nf = jnp.maximum(n, 1).astype(F32)
    large = max_exact + (jnp.log(nf / max_exact) / math.log(REL_MAX_DIST / max_exact)
                         * (REL_BUCKETS - max_exact)).astype(jnp.int32)
    large = jnp.minimum(large, REL_BUCKETS - 1)
    return jnp.where(n < max_exact, n, large)


def _attn_bias_table(rel_bias):
    assert WINDOW == ATTN_BLOCK
    r = jnp.arange(ATTN_BLOCK)[None, :]
    c = jnp.arange(ATTN_BLOCK)[:, None]
    dist = jnp.where(c <= r, r - c, r + ATTN_BLOCK - c)
    bucket = _t5_causal_bucket(dist)
    onehot = (bucket[None, :, :] == jnp.arange(REL_BUCKETS)[:, None, None]).astype(F32)
    return jnp.einsum("bh,bqk->hqk", rel_bias.astype(F32), onehot, precision=lax.Precision.HIGHEST)


def _cumsum_rows(tri_bf16, g):
    width = g.shape[1]
    g1 = g.astype(BF16)
    r1 = g - g1.astype(F32)
    g2 = r1.astype(BF16)
    g3 = (r1 - g2.astype(F32)).astype(BF16)
    acc = jnp.dot(tri_bf16, jnp.concatenate([g3, g2, g1], axis=1), preferred_element_type=F32)
    return (acc[:, :width] + acc[:, width:2 * width]) + acc[:, 2 * width:]


def _hgrn_head_safe(q, kf, v, b, st):
    rows = lax.broadcasted_iota(jnp.int32, (HG_SUB, 1), 0)
    outs = []
    prev_end = jnp.zeros((1, HG_DIM), F32)
    for i in range(HG_CHUNK // HG_SUB):
        sl = slice(i * HG_SUB, (i + 1) * HG_SUB)
        bl = b[sl] - prev_end
        qi, ki, vi = q[sl], kf[sl], v[sl]
        vif = vi.astype(F32)
        blast = bl[HG_SUB - 1:HG_SUB]
        o = _dot_nt((qi * jnp.exp(bl)).astype(BF16), st.astype(BF16))
        for s in range(HG_SUB):
            d = jnp.exp(jnp.minimum(bl - bl[s:s + 1], 0.0))
            a = jnp.sum(qi * ki[s:s + 1] * d, axis=-1, keepdims=True)
            o = o + jnp.where(rows >= s, a, 0.0) * vif[s:s + 1]
        kd = (ki * jnp.exp(blast - bl)).astype(BF16)
        st = st * jnp.exp(blast) + _dot_tn(vi, kd)
        prev_end = b[(i + 1) * HG_SUB - 1:(i + 1) * HG_SUB]
        outs.append(o)
    return jnp.concatenate(outs, axis=0), st


def _hgrn_kernel(q_ref, f_ref, i_ref, g_ref, lbl_ref, gn_ref, o_ref, st_ref, kf_ref, b_ref):
    @pl.when(pl.program_id(1) == 0)
    def _():
        st_ref[...] = jnp.zeros_like(st_ref)

    lg2 = lbl_ref[...].astype(F32)
    e = jnp.exp(lg2 - jnp.max(lg2, axis=0, keepdims=True))
    lb = e[0:1] / jnp.sum(e, axis=0, keepdims=True)
    gn = gn_ref[...].astype(F32)
    r = lax.broadcasted_iota(jnp.int32, (HG_CHUNK, HG_CHUNK), 0)
    c = lax.broadcasted_iota(jnp.int32, (HG_CHUNK, HG_CHUNK), 1)
    tri_mask = r >= c
    tri = jnp.where(tri_mask, 1.0, 0.0).astype(BF16)

    n_chunks = HG_TC // HG_CHUNK

    f = lb + (1.0 - lb) * _sigmoid(f_ref[0].astype(F32))
    kf_ref[...] = 1.0 - f
    logf = jnp.log(f)
    worst = None
    for ci in range(n_chunks):
        sl = slice(ci * HG_CHUNK, (ci + 1) * HG_CHUNK)
        b = _cumsum_rows(tri, logf[sl])
        b_ref[sl, :] = b
        bend = b[HG_CHUNK - 1:HG_CHUNK]
        worst = bend if worst is None else jnp.minimum(worst, bend)

    heads = range(HG_HEADS)
    head = lambda t, h: t[:, h * HG_DIM:(h + 1) * HG_DIM]

    def finish_all(sl, outs):
        scale = [lax.rsqrt(jnp.mean(o * o, axis=-1, keepdims=True) + RMS_EPS) for o in outs]
        normed = jnp.concatenate([outs[h] * scale[h] * gn for h in heads], axis=1)
        go = g_ref[0, sl, :].astype(F32)
        o_ref[0, sl, :] = (normed * (go * _sigmoid(go))).astype(o_ref.dtype)

    def fast():
        states = [st_ref[h] for h in heads]
        for ci in range(n_chunks):
            sl = slice(ci * HG_CHUNK, (ci + 1) * HG_CHUNK)
            b = b_ref[sl, :]
            qp = (q_ref[0, sl, :].astype(F32) * jnp.exp(b)).astype(BF16)
            kp = kf_ref[sl, :] * jnp.exp(-b)
            kpb = kp.astype(BF16)
            eb = jnp.exp(b[HG_CHUNK - 1:HG_CHUNK])
            kd = (kp * eb).astype(BF16)
            v = i_ref[0, sl, :]
            att = [_dot_nt(head(qp, h), head(kpb, h)) for h in heads]
            inter = [_dot_nt(head(qp, h), states[h].astype(BF16)) for h in heads]
            update = [_dot_tn(head(v, h), head(kd, h)) for h in heads]
            att = [jnp.where(tri_mask, a, 0.0).astype(BF16) for a in att]
            outs = [inter[h] + jnp.dot(att[h], head(v, h), preferred_element_type=F32) for h in heads]
            states = [states[h] * head(eb, h) + update[h] for h in heads]
            finish_all(sl, outs)
        for h in heads:
            st_ref[h] = states[h]

    def safe():
        def chunk(ci, carry):
            sl = pl.ds(pl.multiple_of(ci * HG_CHUNK, HG_CHUNK), HG_CHUNK)
            outs = []
            for h in heads:
                hs = slice(h * HG_DIM, (h + 1) * HG_DIM)
                o, st_new = _hgrn_head_safe(q_ref[0, sl, hs].astype(F32), kf_ref[sl, hs], i_ref[0, sl, hs],
                                            b_ref[sl, hs], st_ref[h])
                st_ref[h] = st_new
                outs.append(o)
            finish_all(sl, outs)
            return carry

        lax.fori_loop(0, n_chunks, chunk, 0)

    lax.cond(jnp.min(worst) >= -HG_SAFE_DECAY, fast, safe)


def _hgrn2(proj3, lb_logits, gn):
    bsz, s_len, _ = proj3.shape
    blk = lambda off: (lambda b, t: (b, t, off // HG_WIDTH))
    return pl.pallas_call(
        _hgrn_kernel,
        grid=(bsz, s_len // HG_TC),
        in_specs=[
            pl.BlockSpec((1, HG_TC, HG_WIDTH), blk(OFF_QH)),
            pl.BlockSpec((1, HG_TC, HG_WIDTH), blk(OFF_FH)),
            pl.BlockSpec((1, HG_TC, HG_WIDTH), blk(OFF_IH)),
            pl.BlockSpec((1, HG_TC, HG_WIDTH), blk(OFF_GH)),
            pl.BlockSpec((lb_logits.shape[0], HG_WIDTH), lambda b, t: (0, 0)),
            pl.BlockSpec((1, HG_DIM), lambda b, t: (0, 0)),
        ],
        out_specs=pl.BlockSpec((1, HG_TC, HG_WIDTH), lambda b, t: (b, t, 0)),
        out_shape=jax.ShapeDtypeStruct((bsz, s_len, HG_WIDTH), BF16),
        scratch_shapes=[
            pltpu.VMEM((HG_HEADS, HG_DIM, HG_DIM), F32),
            pltpu.VMEM((HG_TC, HG_WIDTH), F32),
            pltpu.VMEM((HG_TC, HG_WIDTH), F32),
        ],
        compiler_params=pltpu.CompilerParams(
            dimension_semantics=("parallel", "arbitrary"), vmem_limit_bytes=_vmem(48)),
        name="hgrn2",
    )(proj3, proj3, proj3, proj3, lb_logits, gn)


def _store_rows_as_tiles(ref, val):
    rows = val.shape[0]
    for s in range(ROW_TILE):
        ref[pl.ds(s, rows, stride=ROW_TILE), :] = val[:, s * LANES:(s + 1) * LANES]


def _load_rows_from_tiles(ref, start, rows):
    return jnp.concatenate(
        [ref[pl.ds(start * ROW_TILE + s, rows, stride=ROW_TILE), :] for s in range(ROW_TILE)], axis=1)


def _pack_bf16_pairs(x):
    c = x.shape[1] // 2
    bits = pltpu.bitcast(x.astype(BF16).astype(F32), U32)
    return jnp.bitwise_or(jnp.right_shift(bits[:, :c], jnp.uint32(16)), bits[:, c:])


def _unpack_bf16_pairs(p):
    lo = pltpu.bitcast(jnp.left_shift(p, jnp.uint32(16)), F32)
    hi = pltpu.bitcast(jnp.bitwise_and(p, jnp.uint32(0xFFFF0000)), F32)
    return lo, hi


def _mix_kernel(sink_ref, q_ref, kp_ref, kc_ref, vp_ref, vc_ref, bias_ref,
                r_ref, ga_ref, gh_ref, x_ref, wa_ref, wr_ref, wo_ref, g2_ref, wr2_ref, br_ref,
                x1_ref, h2p_ref, route_ref, cnt_out_ref, cnt_ref, *, tiles_per_seq):
    i = pl.program_id(0)

    @pl.when(i == 0)
    def _():
        cnt_ref[...] = jnp.zeros_like(cnt_ref)

    r_tile = r_ref[...]
    n_blk = MIX_TM // ATTN_BLOCK
    n_piece = D_MODEL // MXU_WIDTH
    rm_pieces = []

    def rm_piece(c):
        def emit():
            rm_pieces.append(jnp.dot(r_tile, wr_ref[:, c * MXU_WIDTH:(c + 1) * MXU_WIDTH],
                                     preferred_element_type=F32))
        return emit

    k_rows = jnp.concatenate([kp_ref[...], kc_ref[...]], axis=0)
    v_rows = jnp.concatenate([vp_ref[...], vc_ref[...]], axis=0)
    first_in_seq = lax.rem(i, tiles_per_seq) == 0
    blocks = []
    for blk in range(n_blk):
        rows = slice(blk * ATTN_BLOCK, (blk + 1) * ATTN_BLOCK)
        window = slice(blk * ATTN_BLOCK, (blk + 2) * ATTN_BLOCK)
        has_prev = jnp.logical_not(first_in_seq) if blk == 0 else True
        side = [rm_piece(c) for c in range(blk * n_piece // n_blk, (blk + 1) * n_piece // n_blk)]
        blocks.append(_swa_block(q_ref[rows, :], k_rows[window], v_rows[window], has_prev, bias_ref, sink_ref, side))
    rm = jnp.concatenate(rm_pieces, axis=1)
    a_t = jnp.concatenate(blocks, axis=1).astype(BF16)
    am = _dot_tn(a_t, wa_ref[...])
    mixed = _sigmoid(ga_ref[...].astype(F32)) * am + _sigmoid(gh_ref[...].astype(F32)) * rm
    x1 = x_ref[...] + jnp.dot(mixed.astype(BF16), wo_ref[...], preferred_element_type=F32)
    x1_ref[...] = x1
    h2 = x1 * lax.rsqrt(jnp.mean(x1 * x1, axis=-1, keepdims=True) + RMS_EPS) * g2_ref[...]

    _store_rows_as_tiles(h2p_ref, _pack_bf16_pairs(h2))

    hi = h2.astype(BF16)
    lo = (h2 - hi.astype(F32)).astype(BF16)
    both = jnp.dot(hi, wr2_ref[...], preferred_element_type=F32)
    logits = jnp.dot(lo, wr2_ref[:, :ROUTE_LANES], preferred_element_type=F32) + both[:, ROUTE_LANES:]
    logits = logits + both[:, :ROUTE_LANES] + br_ref[...]

    lane = lax.broadcasted_iota(jnp.int32, logits.shape, 1)
    lane_f = lane.astype(F32)
    big = float(ROUTE_LANES)
    gl = jnp.where(lane < N_GROUPS, logits, NEG_INF)
    gmax = jnp.max(gl, axis=-1, keepdims=True)
    g_idx = jnp.min(jnp.where(gl == gmax, lane_f, big), axis=-1, keepdims=True)
    g_w = 1.0 / jnp.sum(jnp.exp(gl - gmax), axis=-1, keepdims=True)
    e_lane = lane - N_GROUPS
    in_group = jnp.logical_and(e_lane >= 0, e_lane < N_EXPERTS)
    lane_group = jnp.right_shift(e_lane, 3).astype(F32)
    in_group = jnp.logical_and(in_group, lane_group == g_idx)
    el = jnp.where(in_group, logits, NEG_INF)
    m1 = jnp.max(el, axis=-1, keepdims=True)
    i1 = jnp.min(jnp.where(el == m1, lane_f, big), axis=-1, keepdims=True)
    el2 = jnp.where(lane_f == i1, NEG_INF, el)
    m2 = jnp.max(el2, axis=-1, keepdims=True)
    i2 = jnp.min(jnp.where(el2 == m2, lane_f, big), axis=-1, keepdims=True)
    p2 = jnp.exp(m2 - m1)
    w1 = g_w / (1.0 + p2)
    w2 = g_w * p2 / (1.0 + p2)
    e1 = i1 - N_GROUPS
    e2 = i2 - N_GROUPS

    tm = logits.shape[0]
    rr = lax.broadcasted_iota(jnp.int32, (tm, tm), 0)
    cc = lax.broadcasted_iota(jnp.int32, (tm, tm), 1)
    ltri = jnp.where(rr > cc, 1.0, 0.0).astype(BF16)
    sel1 = lane_f == e1
    sel2 = lane_f == e2
    oh1 = jnp.where(sel1, 1.0, 0.0)
    oh2 = jnp.where(sel2, 1.0, 0.0)
    base = cnt_ref[...]
    c1 = jnp.sum(oh1, axis=0, keepdims=True)
    c2 = jnp.sum(oh2, axis=0, keepdims=True)
    pre1 = jnp.dot(ltri, oh1.astype(BF16), preferred_element_type=F32) + base
    pre2 = jnp.dot(ltri, oh2.astype(BF16), preferred_element_type=F32) + (base + c1)
    rank1 = jnp.sum(jnp.where(sel1, pre1, 0.0), axis=-1, keepdims=True)
    rank2 = jnp.sum(jnp.where(sel2, pre2, 0.0), axis=-1, keepdims=True)
    total = base + c1 + c2
    cnt_ref[...] = total
    cnt_out_ref[...] = total

    route = jnp.zeros_like(logits)
    for k, val in enumerate((w1, w2, e1, e2, rank1, rank2)):
        route = jnp.where(lane == k, val, route)
    route_ref[...] = route


def _mix(sinks, bias, r2d, proj, x2d, wa, wr, wo, g2, wr2, br, seq_len):
    n = x2d.shape[0]
    row = lambda i: (i, 0)
    const = lambda i: (0, 0)
    blocks_per_tile = MIX_TM // ATTN_BLOCK
    prev_block = lambda i: jnp.maximum(blocks_per_tile * i - 1, 0)
    return pl.pallas_call(
        functools.partial(_mix_kernel, tiles_per_seq=seq_len // MIX_TM),
        grid=(n // MIX_TM,),
        in_specs=[
            pl.BlockSpec(memory_space=pltpu.SMEM),
            pl.BlockSpec((MIX_TM, ATTN_WIDTH), lambda i: (i, OFF_QA // ATTN_WIDTH)),
            pl.BlockSpec((ATTN_BLOCK, KV_WIDTH), lambda i: (prev_block(i), OFF_KA // KV_WIDTH)),
            pl.BlockSpec((MIX_TM, KV_WIDTH), lambda i: (i, OFF_KA // KV_WIDTH)),
            pl.BlockSpec((ATTN_BLOCK, KV_WIDTH), lambda i: (prev_block(i), OFF_VA // KV_WIDTH)),
            pl.BlockSpec((MIX_TM, KV_WIDTH), lambda i: (i, OFF_VA // KV_WIDTH)),
            pl.BlockSpec((N_Q_HEADS, ATTN_BLOCK, ATTN_BLOCK), lambda i: (0, 0, 0)),
            pl.BlockSpec((MIX_TM, HG_WIDTH), row),
            pl.BlockSpec((MIX_TM, D_MODEL), lambda i: (i, OFF_GATE_A // D_MODEL)),
            pl.BlockSpec((MIX_TM, D_MODEL), lambda i: (i, OFF_GATE_H // D_MODEL)),
            pl.BlockSpec((MIX_TM, D_MODEL), row),
            pl.BlockSpec((ATTN_WIDTH, D_MODEL), const, pipeline_mode=pl.Buffered(1)),
            pl.BlockSpec((HG_WIDTH, D_MODEL), const, pipeline_mode=pl.Buffered(1)),
            pl.BlockSpec((D_MODEL, D_MODEL), const, pipeline_mode=pl.Buffered(1)),
            pl.BlockSpec((1, D_MODEL), const),
            pl.BlockSpec((D_MODEL, 2 * ROUTE_LANES), const, pipeline_mode=pl.Buffered(1)),
            pl.BlockSpec((1, ROUTE_LANES), const),
        ],
        out_specs=[
            pl.BlockSpec((MIX_TM, D_MODEL), row),
            pl.BlockSpec((MIX_TM * ROW_TILE, LANES), row),
            pl.BlockSpec((MIX_TM, ROUTE_LANES), row),
            pl.BlockSpec((1, ROUTE_LANES), const),
        ],
        out_shape=[
            jax.ShapeDtypeStruct((n, D_MODEL), F32),
            jax.ShapeDtypeStruct((n * ROW_TILE, LANES), U32),
            jax.ShapeDtypeStruct((n, ROUTE_LANES), F32),
            jax.ShapeDtypeStruct((1, ROUTE_LANES), F32),
        ],
        scratch_shapes=[pltpu.VMEM((1, ROUTE_LANES), F32)],
        compiler_params=pltpu.CompilerParams(dimension_semantics=("arbitrary",), vmem_limit_bytes=_vmem(60)),
        name="mix",
    )(sinks, proj, proj, proj, proj, proj, bias, r2d, proj, proj, x2d, wa, wr, wo, g2, wr2, br)


def _dispatch_kernel(dest_ref, pad_ref, src_ref, dst_hbm, zero_ref, sem):
    n_pad = pad_ref.shape[-1]

    @pl.when(pl.program_id(0) == 0)
    def _():
        zero_ref[...] = jnp.zeros_like(zero_ref)

    for a in range(DISPATCH_ROWS):
        pltpu.make_async_copy(src_ref.at[a // TOP_K], dst_hbm.at[dest_ref[0, 0, a]], sem.at[0]).start(priority=a % 2)
    for p in range(n_pad):
        pltpu.make_async_copy(zero_ref, dst_hbm.at[pad_ref[0, 0, p]], sem.at[0]).start(priority=p % 2)
    tile_rows = DISPATCH_ROWS // TOP_K
    for _ in range(TOP_K + n_pad // tile_rows):
        pltpu.make_async_copy(src_ref, dst_hbm.at[pl.ds(0, tile_rows)], sem.at[0]).wait()
    if n_pad % tile_rows:
        rest = n_pad % tile_rows
        pltpu.make_async_copy(src_ref.at[pl.ds(0, rest)], dst_hbm.at[pl.ds(0, rest)], sem.at[0]).wait()


def _dispatch(dest_flat, pad_rows, h2p_tiles, n_rows):
    n_asg = dest_flat.shape[0]
    steps = n_asg // DISPATCH_ROWS
    assert n_asg + pad_rows.shape[0] == n_rows and pad_rows.shape[0] % steps == 0
    pad_per_step = pad_rows.shape[0] // steps
    return pl.pallas_call(
        _dispatch_kernel,
        grid=(steps,),
        in_specs=[
            pl.BlockSpec((1, 1, DISPATCH_ROWS), lambda s: (s, 0, 0), memory_space=pltpu.SMEM),
            pl.BlockSpec((1, 1, pad_per_step), lambda s: (s, 0, 0), memory_space=pltpu.SMEM),
            pl.BlockSpec((DISPATCH_ROWS // TOP_K, ROW_TILE, LANES), lambda s: (s, 0, 0)),
        ],
        out_specs=pl.BlockSpec(memory_space=pl.ANY),
        out_shape=jax.ShapeDtypeStruct((n_rows, ROW_TILE, LANES), U32),
        scratch_shapes=[pltpu.VMEM((ROW_TILE, LANES), U32), pltpu.SemaphoreType.DMA((1,))],
        compiler_params=pltpu.CompilerParams(dimension_semantics=("arbitrary",)),
        name="dispatch",
    )(dest_flat.reshape(steps, 1, DISPATCH_ROWS), pad_rows.reshape(steps, 1, pad_per_step), h2p_tiles)


def _moe_kernel(be_ref, nused_ref, first_ref, slot_ref, next_ref, xs_ref, wg_hbm, wu_hbm, wd_hbm, y_ref,
                wg_buf, wu_buf, wd_buf, wg_s, wu_s, wd_s, sem):
    b = pl.program_id(0)
    live = b < nused_ref[0]

    def weight_copies(e, s):
        return (pltpu.make_async_copy(wg_hbm.at[0, e], wg_buf.at[s], sem.at[s, 0]),
                pltpu.make_async_copy(wu_hbm.at[0, e], wu_buf.at[s], sem.at[s, 1]),
                pltpu.make_async_copy(wd_hbm.at[0, e], wd_buf.at[s], sem.at[s, 2]))

    @pl.when(b == 0)
    def _():
        for c in weight_copies(be_ref[0], 0):
            c.start()

    for s in range(2):
        @pl.when(jnp.logical_and(jnp.logical_and(live, first_ref[b] == 1), slot_ref[b] == s))
        def _():
            for c in weight_copies(be_ref[b], s):
                c.wait()

            @pl.when(next_ref[b] >= 0)
            def _():
                for c in weight_copies(next_ref[b], 1 - s):
                    c.start()

            wg_s[...] = wg_buf[s].astype(BF16)
            wu_s[...] = wu_buf[s].astype(BF16)
            wd_s[...] = wd_buf[s].astype(BF16)

    @pl.when(jnp.logical_not(live))
    def _():
        y_ref[...] = jnp.zeros_like(y_ref)

    @pl.when(live)
    def _():
        xlo, xhi = _unpack_bf16_pairs(_load_rows_from_tiles(xs_ref, 0, MOE_TM))
        xb = jnp.concatenate([xlo.astype(BF16), xhi.astype(BF16)], axis=1)
        hb = []
        for c in range(EXPERT_FF // MXU_WIDTH):
            cols = slice(c * MXU_WIDTH, (c + 1) * MXU_WIDTH)
            hg = jnp.dot(xb, wg_s[:, cols], preferred_element_type=F32)
            hu = jnp.dot(xb, wu_s[:, cols], preferred_element_type=F32)
            hb.append((hg * _sigmoid(hg) * hu).astype(BF16))
        hb = jnp.concatenate(hb, axis=1)
        half = D_MODEL // 2
        for c in range(half // MXU_WIDTH):
            lo = jnp.dot(hb, wd_s[:, c * MXU_WIDTH:(c + 1) * MXU_WIDTH], preferred_element_type=F32)
            hi = jnp.dot(hb, wd_s[:, half + c * MXU_WIDTH:half + (c + 1) * MXU_WIDTH], preferred_element_type=F32)
            packed = _pack_bf16_pairs(jnp.concatenate([lo, hi], axis=1))
            for k in range(MXU_WIDTH // LANES):
                s = c * (MXU_WIDTH // LANES) + k
                y_ref[pl.ds(s, MOE_TM, stride=ROW_TILE), :] = packed[:, k * LANES:(k + 1) * LANES]


def _moe(block_e, n_used, first, slot, next_e, xs, w_gate, w_up, w_down):
    n_rows = xs.shape[0] // ROW_TILE
    n_blocks = n_rows // MOE_TM
    grid_spec = pltpu.PrefetchScalarGridSpec(
        num_scalar_prefetch=5,
        grid=(n_blocks,),
        in_specs=[
            pl.BlockSpec((MOE_TM * ROW_TILE, LANES), lambda b, be, nu, *_: (jnp.minimum(b, nu[0] - 1), 0)),
            pl.BlockSpec(memory_space=pl.ANY),
            pl.BlockSpec(memory_space=pl.ANY),
            pl.BlockSpec(memory_space=pl.ANY),
        ],
        out_specs=pl.BlockSpec((MOE_TM * ROW_TILE, LANES), lambda b, *_: (b, 0)),
        scratch_shapes=[
            pltpu.VMEM((2, D_MODEL, EXPERT_FF), F32),
            pltpu.VMEM((2, D_MODEL, EXPERT_FF), F32),
            pltpu.VMEM((2, EXPERT_FF, D_MODEL), F32),
            pltpu.VMEM((D_MODEL, EXPERT_FF), BF16),
            pltpu.VMEM((D_MODEL, EXPERT_FF), BF16),
            pltpu.VMEM((EXPERT_FF, D_MODEL), BF16),
            pltpu.SemaphoreType.DMA((2, 3)),
        ],
    )
    return pl.pallas_call(
        _moe_kernel,
        grid_spec=grid_spec,
        out_shape=jax.ShapeDtypeStruct((n_rows * ROW_TILE, LANES), U32),
        compiler_params=pltpu.CompilerParams(dimension_semantics=("arbitrary",), vmem_limit_bytes=_vmem(52)),
        name="moe",
    )(block_e, n_used, first, slot, next_e, xs, w_gate, w_up, w_down)


def _moe_layout(route, counts_f):
    counts = counts_f[0, :N_EXPERTS].astype(jnp.int32)
    padded = (counts + MOE_TM - 1) // MOE_TM * MOE_TM
    pends = jnp.cumsum(padded)
    pstarts = pends - padded
    n_tok = route.shape[0]
    n_blocks = n_tok * TOP_K // MOE_TM + N_EXPERTS
    blk_start = jnp.arange(n_blocks, dtype=jnp.int32) * MOE_TM
    block_e = jnp.minimum(jnp.sum((blk_start[:, None] >= pends[None, :]).astype(jnp.int32), axis=1), N_EXPERTS - 1)
    n_used = (pends[-1] // MOE_TM).astype(jnp.int32).reshape(1)
    blk = jnp.arange(n_blocks, dtype=jnp.int32)
    first = jnp.logical_and(blk < n_used[0], jnp.logical_or(blk == 0, block_e != jnp.roll(block_e, 1)))
    slot = jnp.bitwise_and(jnp.cumsum(first.astype(jnp.int32)) - 1, 1)
    ex = jnp.arange(N_EXPERTS, dtype=jnp.int32)
    later = jnp.logical_and(counts[None, :] > 0, ex[None, :] > ex[:, None])
    next_of_expert = jnp.min(jnp.where(later, ex[None, :], N_EXPERTS), axis=1)
    next_of_expert = jnp.where(next_of_expert == N_EXPERTS, -1, next_of_expert)
    next_e = jnp.sum(jnp.where(block_e[:, None] == ex[None, :], next_of_expert[None, :], 0), axis=1)
    e_id = route[:, 2:2 + TOP_K].astype(jnp.int32)
    rank = route[:, 2 + TOP_K:2 + 2 * TOP_K].astype(jnp.int32)
    onehot_e = e_id[:, :, None] == jnp.arange(N_EXPERTS, dtype=jnp.int32)[None, None, :]
    dest = jnp.sum(jnp.where(onehot_e, pstarts[None, None, :], 0), axis=2) + rank
    n_pad = n_blocks * MOE_TM - n_tok * TOP_K
    pad_count = padded - counts
    pad_end = jnp.cumsum(pad_count)
    k = jnp.arange(n_pad, dtype=jnp.int32)
    owner = jnp.sum((k[:, None] >= pad_end[None, :]).astype(jnp.int32), axis=1)
    own = owner[:, None] == ex[None, :]
    pick = lambda table: jnp.sum(jnp.where(own, table[None, :], 0), axis=1)
    in_expert = pick(pstarts + counts) + (k - pick(pad_end - pad_count))
    pad_rows = jnp.where(owner < N_EXPERTS, in_expert, pends[-1] + (k - pad_end[-1]))
    tables = (block_e.astype(jnp.int32), n_used, first.astype(jnp.int32), slot.astype(jnp.int32),
              next_e.astype(jnp.int32))
    return tables, dest, pad_rows.astype(jnp.int32), n_blocks * MOE_TM


def _final_kernel(dest_ref, destn_ref, x1_ref, route_ref, g_ref, y_hbm, y_flat_hbm, o_ref, ybuf, sem):
    i = pl.program_id(0)
    n_steps = pl.num_programs(0)
    slot = lax.rem(i, 2)
    n_rows = TOP_K * FIN_TM

    def start_gather(idx_ref, s):
        for r in range(n_rows):
            pltpu.make_async_copy(y_hbm.at[idx_ref[0, 0, r]], ybuf.at[pl.ds((s * n_rows + r) * ROW_TILE, ROW_TILE)],
                                  sem.at[s]).start(priority=r % 2)

    @pl.when(i == 0)
    def _():
        start_gather(dest_ref, 0)

    for s in range(2):
        @pl.when(jnp.logical_and(i + 1 < n_steps, 1 - slot == s))
        def _():
            start_gather(destn_ref, s)

    slot_rows = n_rows * ROW_TILE
    slot_start = pl.multiple_of(slot * slot_rows, slot_rows)
    pltpu.make_async_copy(y_flat_hbm.at[pl.ds(0, slot_rows)], ybuf.at[pl.ds(slot_start, slot_rows)],
                          sem.at[slot]).wait()
    route = route_ref[...]
    x = x1_ref[...]
    for k in range(TOP_K):
        ylo, yhi = _unpack_bf16_pairs(_load_rows_from_tiles(ybuf, slot * n_rows + k * FIN_TM, FIN_TM))
        x = x + route[:, k:k + 1] * jnp.concatenate([ylo, yhi], axis=1)
    o_ref[...] = x * lax.rsqrt(jnp.mean(x * x, axis=-1, keepdims=True) + RMS_EPS) * g_ref[...]


def _final(dest, x1, route, g, y_sorted):
    n = x1.shape[0]
    steps = n // FIN_TM
    dest3 = dest.reshape(steps, FIN_TM, TOP_K).transpose(0, 2, 1).reshape(steps, 1, TOP_K * FIN_TM)
    return pl.pallas_call(
        _final_kernel,
        grid=(steps,),
        in_specs=[
            pl.BlockSpec((1, 1, TOP_K * FIN_TM), lambda i: (i, 0, 0), memory_space=pltpu.SMEM),
            pl.BlockSpec((1, 1, TOP_K * FIN_TM), lambda i: (jnp.minimum(i + 1, steps - 1), 0, 0),
                         memory_space=pltpu.SMEM),
            pl.BlockSpec((FIN_TM, D_MODEL), lambda i: (i, 0)),
            pl.BlockSpec((FIN_TM, ROUTE_LANES), lambda i: (i, 0)),
            pl.BlockSpec((1, D_MODEL), lambda i: (0, 0)),
            pl.BlockSpec(memory_space=pl.ANY),
            pl.BlockSpec(memory_space=pl.ANY),
        ],
        out_specs=pl.BlockSpec((FIN_TM, D_MODEL), lambda i: (i, 0)),
        out_shape=jax.ShapeDtypeStruct((n, D_MODEL), F32),
        scratch_shapes=[
            pltpu.VMEM((2 * TOP_K * FIN_TM * ROW_TILE, LANES), U32),
            pltpu.SemaphoreType.DMA((2,)),
        ],
        compiler_params=pltpu.CompilerParams(dimension_semantics=("arbitrary",), vmem_limit_bytes=_vmem(40)),
        name="final",
    )(dest3, dest3, x1, route, g, y_sorted.reshape(-1, ROW_TILE, LANES), y_sorted)


def kernel(x, norm1_g, w_in, attn_sinks, rel_bias, hg_lb_logits, hg_norm_g, w_attn_branch, w_hg_branch, w_out,
           norm2_g, w_group_router, b_group_router, w_expert_router, b_expert_router, w_gate, w_up, w_down, final_g):
    bsz, s_len, d = x.shape
    n_tok = bsz * s_len
    x2d = x.reshape(n_tok, d)

    w_route = jnp.concatenate([w_group_router[0], w_expert_router[0]], axis=1).astype(F32)
    w_route = jnp.pad(w_route, ((0, 0), (0, ROUTE_LANES - w_route.shape[1])))
    whi = w_route.astype(BF16)
    wlo = (w_route - whi.astype(F32)).astype(BF16)
    wr2 = jnp.concatenate([whi, wlo], axis=1)
    b_route = jnp.concatenate([b_group_router[0], b_expert_router[0]]).astype(F32)
    b_route = jnp.pad(b_route, (0, ROUTE_LANES - b_route.shape[0])).reshape(1, ROUTE_LANES)
    bias = _attn_bias_table(rel_bias)

    proj = _in_proj(x2d, norm1_g[0].reshape(1, d).astype(F32), w_in)
    proj3 = proj.reshape(bsz, s_len, IN_WIDTH)
    r = _hgrn2(proj3, hg_lb_logits.astype(F32), hg_norm_g[0].reshape(1, HG_DIM).astype(F32))
    x1, h2p, route, counts = _mix(
        attn_sinks[0].astype(F32), bias, r.reshape(n_tok, HG_WIDTH), proj, x2d,
        w_attn_branch[0].astype(BF16), w_hg_branch[0].astype(BF16), w_out[0].astype(BF16),
        norm2_g[0].reshape(1, d).astype(F32), wr2, b_route, s_len)
    tables, dest, pad_rows, n_rows = _moe_layout(route, counts)
    xs = _dispatch(dest.reshape(n_tok * TOP_K), pad_rows, h2p.reshape(n_tok, ROW_TILE, LANES), n_rows)
    y_sorted = _moe(*tables, xs.reshape(n_rows * ROW_TILE, LANES), w_gate, w_up, w_down)
    out = _final(dest, x1, route, final_g.reshape(1, d).astype(F32), y_sorted)
    return out.reshape(bsz, s_len, d)
```

```python
import functools
import math

import jax
import jax.numpy as jnp
from jax import lax
from jax.experimental import pallas as pl
from jax.experimental.pallas import tpu as pltpu

D_MODEL = 2048
N_Q_HEADS = 16
N_KV_HEADS = 4
HEAD_DIM = 64
WINDOW = 128
ATTN_BLOCK = 128
ATTN_WIDTH = N_Q_HEADS * HEAD_DIM
KV_WIDTH = N_KV_HEADS * HEAD_DIM
REL_BUCKETS = 32
REL_MAX_DIST = 128
HG_HEADS = 8
HG_DIM = 128
HG_WIDTH = HG_HEADS * HG_DIM
HG_CHUNK = 64
HG_SUB = 16
N_GROUPS = 4
EXPERTS_PER_GROUP = 8
N_EXPERTS = N_GROUPS * EXPERTS_PER_GROUP
TOP_K = 2
EXPERT_FF = 512
RMS_EPS = 1e-6

OFF_GATE_A = 0
OFF_GATE_H = D_MODEL
OFF_QA = 2 * D_MODEL
OFF_QH = OFF_QA + ATTN_WIDTH
OFF_FH = OFF_QH + HG_WIDTH
OFF_IH = OFF_FH + HG_WIDTH
OFF_GH = OFF_IH + HG_WIDTH
OFF_KA = OFF_GH + HG_WIDTH
OFF_VA = OFF_KA + KV_WIDTH
IN_WIDTH = OFF_VA + KV_WIDTH
REF_OFF_QA = 0
REF_OFF_KA = ATTN_WIDTH
REF_OFF_QH = ATTN_WIDTH + 2 * KV_WIDTH
REF_OFF_GATE_A = REF_OFF_QH + 4 * HG_WIDTH

HG_SAFE_DECAY = 60.0

ROW_TILE = 8
LANES = 128
assert D_MODEL // 2 == ROW_TILE * LANES
MXU_WIDTH = 256

IN_TM = 2048
IN_TN = 512
HG_TC = 1024
MIX_TM = 512
ROUTE_LANES = 128
ROUTE_ROWS = 40
ROUTE_VALS = 16
assert N_GROUPS + N_EXPERTS <= ROUTE_ROWS <= ROUTE_LANES
DISPATCH_ROWS = 1024
MOE_TM = 256
MOE_WEIGHT_BUFFERS = 3
FIN_TM = 256

F32 = jnp.float32
BF16 = jnp.bfloat16
U32 = jnp.uint32
NEG_INF = float("-inf")


def _vmem(mib):
    return mib * 1024 * 1024


def _sigmoid(x):
    return 1.0 / (1.0 + jnp.exp(-x))


def _dot_nt(a, b):
    return lax.dot_general(a, b, (((1,), (1,)), ((), ())), preferred_element_type=F32)


def _dot_tn(a, b):
    return lax.dot_general(a, b, (((0,), (0,)), ((), ())), preferred_element_type=F32)


def _in_proj_kernel(x_ref, g_ref, w_ref, o_ref, h_ref):
    @pl.when(pl.program_id(1) == 0)
    def _():
        x = x_ref[...]
        ms = jnp.mean(x * x, axis=-1, keepdims=True)
        h_ref[...] = (x * lax.rsqrt(ms + RMS_EPS) * g_ref[...]).astype(BF16)

    o_ref[...] = jnp.dot(h_ref[...], w_ref[0].astype(BF16), preferred_element_type=F32).astype(o_ref.dtype)


def _in_proj(x2d, g, w_in):
    n = x2d.shape[0]
    n_col = IN_WIDTH // IN_TN

    def out_block(j):
        gates = j - REF_OFF_GATE_A // IN_TN + OFF_GATE_A // IN_TN
        q_a = j - REF_OFF_QA // IN_TN + OFF_QA // IN_TN
        kv = j - REF_OFF_KA // IN_TN + OFF_KA // IN_TN
        hg = j - REF_OFF_QH // IN_TN + OFF_QH // IN_TN
        return jnp.where(j >= REF_OFF_GATE_A // IN_TN, gates,
                         jnp.where(j >= REF_OFF_QH // IN_TN, hg, jnp.where(j >= REF_OFF_KA // IN_TN, kv, q_a)))

    return pl.pallas_call(
        _in_proj_kernel,
        grid=(n // IN_TM, n_col),
        in_specs=[
            pl.BlockSpec((IN_TM, D_MODEL), lambda i, j: (i, 0)),
            pl.BlockSpec((1, D_MODEL), lambda i, j: (0, 0)),
            pl.BlockSpec((1, D_MODEL, IN_TN), lambda i, j: (0, 0, j)),
        ],
        out_specs=pl.BlockSpec((IN_TM, IN_TN), lambda i, j: (i, out_block(j))),
        out_shape=jax.ShapeDtypeStruct((n, IN_WIDTH), BF16),
        scratch_shapes=[pltpu.VMEM((IN_TM, D_MODEL), BF16)],
        compiler_params=pltpu.CompilerParams(
            dimension_semantics=("parallel", "arbitrary"), vmem_limit_bytes=_vmem(58)),
        name="in_proj",
    )(x2d, g, w_in)


def _swa_block(q, kk, vv, has_prev, bias_ref, sink_ref, side_work):
    q = q * jnp.asarray(HEAD_DIM ** -0.5, BF16)
    key = lax.broadcasted_iota(jnp.int32, (ATTN_BLOCK, ATTN_BLOCK), 0)
    qry = lax.broadcasted_iota(jnp.int32, (ATTN_BLOCK, ATTN_BLOCK), 1)
    cur = key <= qry
    key_ok = jnp.logical_or(cur, has_prev)
    grp = N_Q_HEADS // N_KV_HEADS
    head = lambda t, h: t[:, h * HEAD_DIM:(h + 1) * HEAD_DIM]
    lanes = lambda t, g: t[:, g * ATTN_BLOCK:(g + 1) * ATTN_BLOCK]
    scores = []
    for hk in range(N_KV_HEADS):
        q_grp = jnp.concatenate([head(q, hk * grp + g) for g in range(grp)], axis=0)
        scores.append(_dot_nt(head(kk, hk), q_grp))
    emitted = 0
    probs, denoms = [], []
    for hq in range(N_Q_HEADS):
        while emitted * N_Q_HEADS < hq * len(side_work) + len(side_work):
            side_work[emitted]()
            emitted += 1
        s2 = lanes(scores[hq // grp], hq % grp)
        s = jnp.where(cur, s2[ATTN_BLOCK:], s2[:ATTN_BLOCK]) + bias_ref[hq]
        s = jnp.where(key_ok, s, NEG_INF)
        sink = sink_ref[hq]
        m = jnp.maximum(jnp.max(s, axis=0, keepdims=True), sink)
        p = jnp.exp(s - m)
        denoms.append(jnp.sum(p, axis=0, keepdims=True) + jnp.exp(sink - m))
        probs.append(jnp.concatenate([jnp.where(cur, 0.0, p), jnp.where(cur, p, 0.0)], axis=0).astype(BF16))
    outs = []
    for hk in range(N_KV_HEADS):
        p_grp = jnp.concatenate(probs[hk * grp:(hk + 1) * grp], axis=1)
        o_grp = _dot_tn(head(vv, hk), p_grp)
        outs += [lanes(o_grp, g) / denoms[hk * grp + g] for g in range(grp)]
    return jnp.concatenate(outs, axis=0)


def _t5_causal_bucket(n):
    max_exact = REL_BUCKETS // 2
    nf = jnp.maximum(n, 1).astype(F32)
    large = max_exact + (jnp.log(nf / max_exact) / math.log(REL_MAX_DIST / max_exact)
                         * (REL_BUCKETS - max_exact)).astype(jnp.int32)
    large = jnp.minimum(large, REL_BUCKETS - 1)
    return jnp.where(n < max_exact, n, large)


def _attn_bias_table(rel_bias):
    assert WINDOW == ATTN_BLOCK
    r = jnp.arange(ATTN_BLOCK)[None, :]
    c = jnp.arange(ATTN_BLOCK)[:, None]
    dist = jnp.where(c <= r, r - c, r + ATTN_BLOCK - c)
    bucket = _t5_causal_bucket(dist)
    onehot = (bucket[None, :, :] == jnp.arange(REL_BUCKETS)[:, None, None]).astype(F32)
    return jnp.einsum("bh,bqk->hqk", rel_bias.astype(F32), onehot, precision=lax.Precision.HIGHEST)


def _cumsum_rows(tri_bf16, g):
    width = g.shape[1]
    g1 = g.astype(BF16)
    r1 = g - g1.astype(F32)
    g2 = r1.astype(BF16)
    g3 = (r1 - g2.astype(F32)).astype(BF16)
    acc = jnp.dot(tri_bf16, jnp.concatenate([g3, g2, g1], axis=1), preferred_element_type=F32)
    return (acc[:, :width] + acc[:, width:2 * width]) + acc[:, 2 * width:]


def _hgrn_head_safe(q, kf, v, b, st):
    rows = lax.broadcasted_iota(jnp.int32, (HG_SUB, 1), 0)
    outs = []
    prev_end = jnp.zeros((1, HG_DIM), F32)
    for i in range(HG_CHUNK // HG_SUB):
        sl = slice(i * HG_SUB, (i + 1) * HG_SUB)
        bl = b[sl] - prev_end
        qi, ki, vi = q[sl], kf[sl], v[sl]
        vif = vi.astype(F32)
        blast = bl[HG_SUB - 1:HG_SUB]
        o = _dot_nt((qi * jnp.exp(bl)).astype(BF16), st.astype(BF16))
        for s in range(HG_SUB):
            d = jnp.exp(jnp.minimum(bl - bl[s:s + 1], 0.0))
            a = jnp.sum(qi * ki[s:s + 1] * d, axis=-1, keepdims=True)
            o = o + jnp.where(rows >= s, a, 0.0) * vif[s:s + 1]
        kd = (ki * jnp.exp(blast - bl)).astype(BF16)
        st = st * jnp.exp(blast) + _dot_tn(vi, kd)
        prev_end = b[(i + 1) * HG_SUB - 1:(i + 1) * HG_SUB]
        outs.append(o)
    return jnp.concatenate(outs, axis=0), st


def _hgrn_kernel(q_ref, f_ref, i_ref, g_ref, lbl_ref, gn_ref, o_ref, st_ref, kf_ref, b_ref):
    @pl.when(pl.program_id(1) == 0)
    def _():
        st_ref[...] = jnp.zeros_like(st_ref)

    lg2 = lbl_ref[...].astype(F32)
    e = jnp.exp(lg2 - jnp.max(lg2, axis=0, keepdims=True))
    lb = e[0:1] / jnp.sum(e, axis=0, keepdims=True)
    gn = gn_ref[...].astype(F32)
    r = lax.broadcasted_iota(jnp.int32, (HG_CHUNK, HG_CHUNK), 0)
    c = lax.broadcasted_iota(jnp.int32, (HG_CHUNK, HG_CHUNK), 1)
    tri_mask = r >= c
    tri = jnp.where(tri_mask, 1.0, 0.0).astype(BF16)

    n_chunks = HG_TC // HG_CHUNK

    f = lb + (1.0 - lb) * _sigmoid(f_ref[0].astype(F32))
    kf_ref[...] = 1.0 - f
    logf = jnp.log(f)
    worst = None
    for ci in range(n_chunks):
        sl = slice(ci * HG_CHUNK, (ci + 1) * HG_CHUNK)
        b = _cumsum_rows(tri, logf[sl])
        b_ref[sl, :] = b
        bend = b[HG_CHUNK - 1:HG_CHUNK]
        worst = bend if worst is None else jnp.minimum(worst, bend)

    heads = range(HG_HEADS)
    head = lambda t, h: t[:, h * HG_DIM:(h + 1) * HG_DIM]

    def finish_all(sl, outs):
        scale = [lax.rsqrt(jnp.mean(o * o, axis=-1, keepdims=True) + RMS_EPS) for o in outs]
        normed = jnp.concatenate([outs[h] * scale[h] * gn for h in heads], axis=1)
        go = g_ref[0, sl, :].astype(F32)
        o_ref[0, sl, :] = (normed * (go * _sigmoid(go))).astype(o_ref.dtype)

    def fast():
        states = [st_ref[h] for h in heads]
        for ci in range(n_chunks):
            sl = slice(ci * HG_CHUNK, (ci + 1) * HG_CHUNK)
            b = b_ref[sl, :]
            qp = (q_ref[0, sl, :].astype(F32) * jnp.exp(b)).astype(BF16)
            kp = kf_ref[sl, :] * jnp.exp(-b)
            kpb = kp.astype(BF16)
            eb = jnp.exp(b[HG_CHUNK - 1:HG_CHUNK])
            kd = (kp * eb).astype(BF16)
            v = i_ref[0, sl, :]
            att = [_dot_nt(head(qp, h), head(kpb, h)) for h in heads]
            inter = [_dot_nt(head(qp, h), states[h].astype(BF16)) for h in heads]
            update = [_dot_tn(head(v, h), head(kd, h)) for h in heads]
            att = [jnp.where(tri_mask, a, 0.0).astype(BF16) for a in att]
            outs = [inter[h] + jnp.dot(att[h], head(v, h), preferred_element_type=F32) for h in heads]
            states = [states[h] * head(eb, h) + update[h] for h in heads]
            finish_all(sl, outs)
        for h in heads:
            st_ref[h] = states[h]

    def safe():
        def chunk(ci, carry):
            sl = pl.ds(pl.multiple_of(ci * HG_CHUNK, HG_CHUNK), HG_CHUNK)
            outs = []
            for h in heads:
                hs = slice(h * HG_DIM, (h + 1) * HG_DIM)
                o, st_new = _hgrn_head_safe(q_ref[0, sl, hs].astype(F32), kf_ref[sl, hs], i_ref[0, sl, hs],
                                            b_ref[sl, hs], st_ref[h])
                st_ref[h] = st_new
                outs.append(o)
            finish_all(sl, outs)
            return carry

        lax.fori_loop(0, n_chunks, chunk, 0)

    lax.cond(jnp.min(worst) >= -HG_SAFE_DECAY, fast, safe)


def _hgrn2(proj3, lb_logits, gn):
    bsz, s_len, _ = proj3.shape
    blk = lambda off: (lambda b, t: (b, t, off // HG_WIDTH))
    return pl.pallas_call(
        _hgrn_kernel,
        grid=(bsz, s_len // HG_TC),
        in_specs=[
            pl.BlockSpec((1, HG_TC, HG_WIDTH), blk(OFF_QH)),
            pl.BlockSpec((1, HG_TC, HG_WIDTH), blk(OFF_FH)),
            pl.BlockSpec((1, HG_TC, HG_WIDTH), blk(OFF_IH)),
            pl.BlockSpec((1, HG_TC, HG_WIDTH), blk(OFF_GH)),
            pl.BlockSpec((lb_logits.shape[0], HG_WIDTH), lambda b, t: (0, 0)),
            pl.BlockSpec((1, HG_DIM), lambda b, t: (0, 0)),
        ],
        out_specs=pl.BlockSpec((1, HG_TC, HG_WIDTH), lambda b, t: (b, t, 0)),
        out_shape=jax.ShapeDtypeStruct((bsz, s_len, HG_WIDTH), BF16),
        scratch_shapes=[
            pltpu.VMEM((HG_HEADS, HG_DIM, HG_DIM), F32),
            pltpu.VMEM((HG_TC, HG_WIDTH), F32),
            pltpu.VMEM((HG_TC, HG_WIDTH), F32),
        ],
        compiler_params=pltpu.CompilerParams(
            dimension_semantics=("parallel", "arbitrary"), vmem_limit_bytes=_vmem(48)),
        name="hgrn2",
    )(proj3, proj3, proj3, proj3, lb_logits, gn)


def _store_rows_as_tiles(ref, val):
    rows = val.shape[0]
    for s in range(ROW_TILE):
        ref[pl.ds(s, rows, stride=ROW_TILE), :] = val[:, s * LANES:(s + 1) * LANES]


def _load_rows_from_tiles(ref, start, rows):
    return jnp.concatenate(
        [ref[pl.ds(start * ROW_TILE + s, rows, stride=ROW_TILE), :] for s in range(ROW_TILE)], axis=1)


def _pack_bf16_pairs(x):
    c = x.shape[1] // 2
    bits = pltpu.bitcast(x.astype(BF16).astype(F32), U32)
    return jnp.bitwise_or(jnp.right_shift(bits[:, :c], jnp.uint32(16)), bits[:, c:])


def _unpack_bf16_pairs(p):
    lo = pltpu.bitcast(jnp.left_shift(p, jnp.uint32(16)), F32)
    hi = pltpu.bitcast(jnp.bitwise_and(p, jnp.uint32(0xFFFF0000)), F32)
    return lo, hi


def _mix_kernel(sink_ref, q_ref, kp_ref, kc_ref, vp_ref, vc_ref, bias_ref,
                r_ref, ga_ref, gh_ref, x_ref, wa_ref, wr_ref, wo_ref, g2_ref, wr2t_ref, brt_ref,
                x1_ref, h2p_ref, route_ref, cnt_ref, *, tiles_per_seq):
    i = pl.program_id(0)

    @pl.when(i == 0)
    def _():
        cnt_ref[...] = jnp.zeros_like(cnt_ref)

    r_tile = r_ref[...]
    n_blk = MIX_TM // ATTN_BLOCK
    n_piece = D_MODEL // MXU_WIDTH
    rm_pieces = []

    def rm_piece(c):
        def emit():
            rm_pieces.append(jnp.dot(r_tile, wr_ref[:, c * MXU_WIDTH:(c + 1) * MXU_WIDTH],
                                     preferred_element_type=F32))
        return emit

    k_rows = jnp.concatenate([kp_ref[...], kc_ref[...]], axis=0)
    v_rows = jnp.concatenate([vp_ref[...], vc_ref[...]], axis=0)
    first_in_seq = lax.rem(i, tiles_per_seq) == 0
    blocks = []
    for blk in range(n_blk):
        rows = slice(blk * ATTN_BLOCK, (blk + 1) * ATTN_BLOCK)
        window = slice(blk * ATTN_BLOCK, (blk + 2) * ATTN_BLOCK)
        has_prev = jnp.logical_not(first_in_seq) if blk == 0 else True
        side = [rm_piece(c) for c in range(blk * n_piece // n_blk, (blk + 1) * n_piece // n_blk)]
        blocks.append(_swa_block(q_ref[rows, :], k_rows[window], v_rows[window], has_prev, bias_ref, sink_ref, side))
    rm = jnp.concatenate(rm_pieces, axis=1)
    a_t = jnp.concatenate(blocks, axis=1).astype(BF16)
    am = _dot_tn(a_t, wa_ref[...])
    mixed = _sigmoid(ga_ref[...].astype(F32)) * am + _sigmoid(gh_ref[...].astype(F32)) * rm
    x1 = x_ref[...] + jnp.dot(mixed.astype(BF16), wo_ref[...], preferred_element_type=F32)
    x1_ref[...] = x1
    h2 = x1 * lax.rsqrt(jnp.mean(x1 * x1, axis=-1, keepdims=True) + RMS_EPS) * g2_ref[...]

    _store_rows_as_tiles(h2p_ref, _pack_bf16_pairs(h2))

    hi = h2.astype(BF16)
    lo = (h2 - hi.astype(F32)).astype(BF16)
    both = _dot_nt(wr2t_ref[...], hi)
    logits = both[:ROUTE_LANES] + both[ROUTE_LANES:] + _dot_nt(wr2t_ref[:ROUTE_LANES, :], lo)
    logits = logits[:ROUTE_ROWS] + brt_ref[:ROUTE_ROWS, :]

    tm = logits.shape[1]
    row = lax.broadcasted_iota(jnp.int32, logits.shape, 0)
    row_f = row.astype(F32)
    big = float(ROUTE_ROWS)
    gl = jnp.where(row < N_GROUPS, logits, NEG_INF)
    gmax = jnp.max(gl, axis=0, keepdims=True)
    g_idx = jnp.min(jnp.where(gl == gmax, row_f, big), axis=0, keepdims=True)
    g_w = 1.0 / jnp.sum(jnp.exp(gl - gmax), axis=0, keepdims=True)
    e_row = row - N_GROUPS
    in_group = jnp.logical_and(e_row >= 0, e_row < N_EXPERTS)
    row_group = jnp.right_shift(e_row, 3).astype(F32)
    in_group = jnp.logical_and(in_group, row_group == g_idx)
    el = jnp.where(in_group, logits, NEG_INF)
    m1 = jnp.max(el, axis=0, keepdims=True)
    i1 = jnp.min(jnp.where(el == m1, row_f, big), axis=0, keepdims=True)
    el2 = jnp.where(row_f == i1, NEG_INF, el)
    m2 = jnp.max(el2, axis=0, keepdims=True)
    i2 = jnp.min(jnp.where(el2 == m2, row_f, big), axis=0, keepdims=True)
    p2 = jnp.exp(m2 - m1)
    w1 = g_w / (1.0 + p2)
    w2 = g_w * p2 / (1.0 + p2)
    e1 = i1 - N_GROUPS
    e2 = i2 - N_GROUPS

    ss = lax.broadcasted_iota(jnp.int32, (tm, tm), 0)
    tt = lax.broadcasted_iota(jnp.int32, (tm, tm), 1)
    earlier = jnp.where(ss < tt, 1.0, 0.0).astype(BF16)
    sel1 = row_f == i1
    sel2 = row_f == i2
    oh1 = jnp.where(sel1, 1.0, 0.0)
    oh2 = jnp.where(sel2, 1.0, 0.0)
    base = cnt_ref[...]
    c1 = jnp.sum(oh1, axis=1, keepdims=True)
    c2 = jnp.sum(oh2, axis=1, keepdims=True)
    pre1 = jnp.dot(oh1.astype(BF16), earlier, preferred_element_type=F32) + base
    pre2 = jnp.dot(oh2.astype(BF16), earlier, preferred_element_type=F32) + (base + c1)
    rank1 = jnp.sum(jnp.where(sel1, pre1, 0.0), axis=0, keepdims=True)
    rank2 = jnp.sum(jnp.where(sel2, pre2, 0.0), axis=0, keepdims=True)
    cnt_ref[...] = base + c1 + c2

    vals = jnp.concatenate([w1, w2, e1, e2, rank1, rank2, jnp.zeros((ROUTE_VALS - 6, tm), F32)], axis=0)
    v1 = vals.astype(BF16)
    rest = vals - v1.astype(F32)
    v2 = rest.astype(BF16)
    v3 = (rest - v2.astype(F32)).astype(BF16)
    kk = lax.broadcasted_iota(jnp.int32, (ROUTE_VALS, ROUTE_LANES), 0)
    ll = lax.broadcasted_iota(jnp.int32, (ROUTE_VALS, ROUTE_LANES), 1)
    pick = jnp.where(kk == ll, 1.0, 0.0).astype(BF16)
    route_ref[...] = (_dot_tn(v3, pick) + _dot_tn(v2, pick)) + _dot_tn(v1, pick)


def _mix(sinks, bias, r2d, proj, x2d, wa, wr, wo, g2, wr2t, brt, seq_len):
    n = x2d.shape[0]
    row = lambda i: (i, 0)
    const = lambda i: (0, 0)
    blocks_per_tile = MIX_TM // ATTN_BLOCK
    prev_block = lambda i: jnp.maximum(blocks_per_tile * i - 1, 0)
    return pl.pallas_call(
        functools.partial(_mix_kernel, tiles_per_seq=seq_len // MIX_TM),
        grid=(n // MIX_TM,),
        in_specs=[
            pl.BlockSpec(memory_space=pltpu.SMEM),
            pl.BlockSpec((MIX_TM, ATTN_WIDTH), lambda i: (i, OFF_QA // ATTN_WIDTH)),
            pl.BlockSpec((ATTN_BLOCK, KV_WIDTH), lambda i: (prev_block(i), OFF_KA // KV_WIDTH)),
            pl.BlockSpec((MIX_TM, KV_WIDTH), lambda i: (i, OFF_KA // KV_WIDTH)),
            pl.BlockSpec((ATTN_BLOCK, KV_WIDTH), lambda i: (prev_block(i), OFF_VA // KV_WIDTH)),
            pl.BlockSpec((MIX_TM, KV_WIDTH), lambda i: (i, OFF_VA // KV_WIDTH)),
            pl.BlockSpec((N_Q_HEADS, ATTN_BLOCK, ATTN_BLOCK), lambda i: (0, 0, 0)),
            pl.BlockSpec((MIX_TM, HG_WIDTH), row),
            pl.BlockSpec((MIX_TM, D_MODEL), lambda i: (i, OFF_GATE_A // D_MODEL)),
            pl.BlockSpec((MIX_TM, D_MODEL), lambda i: (i, OFF_GATE_H // D_MODEL)),
            pl.BlockSpec((MIX_TM, D_MODEL), row),
            pl.BlockSpec((ATTN_WIDTH, D_MODEL), const, pipeline_mode=pl.Buffered(1)),
            pl.BlockSpec((HG_WIDTH, D_MODEL), const, pipeline_mode=pl.Buffered(1)),
            pl.BlockSpec((D_MODEL, D_MODEL), const, pipeline_mode=pl.Buffered(1)),
            pl.BlockSpec((1, D_MODEL), const),
            pl.BlockSpec((2 * ROUTE_LANES, D_MODEL), const, pipeline_mode=pl.Buffered(1)),
            pl.BlockSpec((ROUTE_LANES, 1), const),
        ],
        out_specs=[
            pl.BlockSpec((MIX_TM, D_MODEL), row),
            pl.BlockSpec((MIX_TM * ROW_TILE, LANES), row),
            pl.BlockSpec((MIX_TM, ROUTE_LANES), row),
        ],
        out_shape=[
            jax.ShapeDtypeStruct((n, D_MODEL), F32),
            jax.ShapeDtypeStruct((n * ROW_TILE, LANES), U32),
            jax.ShapeDtypeStruct((n, ROUTE_LANES), F32),
        ],
        scratch_shapes=[pltpu.VMEM((ROUTE_ROWS, 1), F32)],
        compiler_params=pltpu.CompilerParams(dimension_semantics=("arbitrary",), vmem_limit_bytes=_vmem(60)),
        name="mix",
    )(sinks, proj, proj, proj, proj, proj, bias, r2d, proj, proj, x2d, wa, wr, wo, g2, wr2t, brt)


def _dispatch_kernel(pad_start_ref, pad_count_ref, tail_ref, dest_ref, src_ref, dst_hbm, zero_ref, sem):
    @pl.when(pl.program_id(0) == 0)
    def _():
        zero_ref[...] = jnp.zeros_like(zero_ref)
        for phase in ("start", "wait"):
            for e in range(N_EXPERTS):
                count, start = pad_count_ref[e], pad_start_ref[e]
                size = MOE_TM // 2
                while size >= 1:
                    done = jnp.bitwise_and(count, -2 * size)
                    copy = pltpu.make_async_copy(zero_ref.at[pl.ds(0, size)],
                                                 dst_hbm.at[pl.ds(start + done, size)], sem.at[1])
                    pl.when(jnp.bitwise_and(count, size) != 0)(getattr(copy, phase))
                    size //= 2
            for j in range(N_EXPERTS):
                copy = pltpu.make_async_copy(zero_ref, dst_hbm.at[pl.ds(tail_ref[0] + j * MOE_TM, MOE_TM)],
                                             sem.at[1])
                pl.when(j < tail_ref[1])(getattr(copy, phase))

    for a in range(DISPATCH_ROWS):
        pltpu.make_async_copy(src_ref.at[a // TOP_K], dst_hbm.at[dest_ref[0, 0, a]], sem.at[0]).start(priority=a % 2)
    for _ in range(TOP_K):
        pltpu.make_async_copy(src_ref, dst_hbm.at[pl.ds(0, DISPATCH_ROWS // TOP_K)], sem.at[0]).wait()


def _dispatch(dest_flat, pad_start, pad_count, tail, h2p_tiles, n_rows):
    n_asg = dest_flat.shape[0]
    steps = n_asg // DISPATCH_ROWS
    smem = pl.BlockSpec(memory_space=pltpu.SMEM)
    return pl.pallas_call(
        _dispatch_kernel,
        grid=(steps,),
        in_specs=[
            smem, smem, smem,
            pl.BlockSpec((1, 1, DISPATCH_ROWS), lambda s: (s, 0, 0), memory_space=pltpu.SMEM),
            pl.BlockSpec((DISPATCH_ROWS // TOP_K, ROW_TILE, LANES), lambda s: (s, 0, 0)),
        ],
        out_specs=pl.BlockSpec(memory_space=pl.ANY),
        out_shape=jax.ShapeDtypeStruct((n_rows, ROW_TILE, LANES), U32),
        scratch_shapes=[pltpu.VMEM((MOE_TM, ROW_TILE, LANES), U32), pltpu.SemaphoreType.DMA((2,))],
        compiler_params=pltpu.CompilerParams(dimension_semantics=("arbitrary",)),
        name="dispatch",
    )(pad_start, pad_count, tail, dest_flat.reshape(steps, 1, DISPATCH_ROWS), h2p_tiles)


def _moe_kernel(be_ref, nused_ref, first_ref, slot_ref, next_ref, next2_ref, xs_ref, wg_hbm, wu_hbm, wd_hbm, y_ref,
                wg_buf, wu_buf, wd_buf, wg_s, wu_s, wd_s, sem):
    b = pl.program_id(0)
    live = b < nused_ref[0]

    def weight_copies(e, s):
        return (pltpu.make_async_copy(wg_hbm.at[0, e], wg_buf.at[s], sem.at[s, 0]),
                pltpu.make_async_copy(wu_hbm.at[0, e], wu_buf.at[s], sem.at[s, 1]),
                pltpu.make_async_copy(wd_hbm.at[0, e], wd_buf.at[s], sem.at[s, 2]))

    @pl.when(b == 0)
    def _():
        for c in weight_copies(be_ref[0], 0):
            c.start()

        @pl.when(next_ref[0] >= 0)
        def _():
            for c in weight_copies(next_ref[0], 1):
                c.start()

    for s in range(MOE_WEIGHT_BUFFERS):
        @pl.when(jnp.logical_and(jnp.logical_and(live, first_ref[b] == 1), slot_ref[b] == s))
        def _():
            for c in weight_copies(be_ref[b], s):
                c.wait()

            @pl.when(next2_ref[b] >= 0)
            def _():
                for c in weight_copies(next2_ref[b], (s + 2) % MOE_WEIGHT_BUFFERS):
                    c.start()

            wg_s[...] = wg_buf[s].astype(BF16)
            wu_s[...] = wu_buf[s].astype(BF16)
            wd_s[...] = wd_buf[s].astype(BF16)

    @pl.when(jnp.logical_not(live))
    def _():
        y_ref[...] = jnp.zeros_like(y_ref)

    @pl.when(live)
    def _():
        xlo, xhi = _unpack_bf16_pairs(_load_rows_from_tiles(xs_ref, 0, MOE_TM))
        xb = jnp.concatenate([xlo.astype(BF16), xhi.astype(BF16)], axis=1)
        hb = []
        for c in range(EXPERT_FF // MXU_WIDTH):
            cols = slice(c * MXU_WIDTH, (c + 1) * MXU_WIDTH)
            hg = jnp.dot(xb, wg_s[:, cols], preferred_element_type=F32)
            hu = jnp.dot(xb, wu_s[:, cols], preferred_element_type=F32)
            hb.append((hg * _sigmoid(hg) * hu).astype(BF16))
        hb = jnp.concatenate(hb, axis=1)
        half = D_MODEL // 2
        for c in range(half // MXU_WIDTH):
            lo = jnp.dot(hb, wd_s[:, c * MXU_WIDTH:(c + 1) * MXU_WIDTH], preferred_element_type=F32)
            hi = jnp.dot(hb, wd_s[:, half + c * MXU_WIDTH:half + (c + 1) * MXU_WIDTH], preferred_element_type=F32)
            packed = _pack_bf16_pairs(jnp.concatenate([lo, hi], axis=1))
            for k in range(MXU_WIDTH // LANES):
                s = c * (MXU_WIDTH // LANES) + k
                y_ref[pl.ds(s, MOE_TM, stride=ROW_TILE), :] = packed[:, k * LANES:(k + 1) * LANES]


def _moe(block_e, n_used, first, slot, next_e, next2_e, xs, w_gate, w_up, w_down):
    n_rows = xs.shape[0] // ROW_TILE
    n_blocks = n_rows // MOE_TM
    grid_spec = pltpu.PrefetchScalarGridSpec(
        num_scalar_prefetch=6,
        grid=(n_blocks,),
        in_specs=[
            pl.BlockSpec((MOE_TM * ROW_TILE, LANES), lambda b, be, nu, *_: (jnp.minimum(b, nu[0] - 1), 0)),
            pl.BlockSpec(memory_space=pl.ANY),
            pl.BlockSpec(memory_space=pl.ANY),
            pl.BlockSpec(memory_space=pl.ANY),
        ],
        out_specs=pl.BlockSpec((MOE_TM * ROW_TILE, LANES), lambda b, *_: (b, 0)),
        scratch_shapes=[
            pltpu.VMEM((MOE_WEIGHT_BUFFERS, D_MODEL, EXPERT_FF), F32),
            pltpu.VMEM((MOE_WEIGHT_BUFFERS, D_MODEL, EXPERT_FF), F32),
            pltpu.VMEM((MOE_WEIGHT_BUFFERS, EXPERT_FF, D_MODEL), F32),
            pltpu.VMEM((D_MODEL, EXPERT_FF), BF16),
            pltpu.VMEM((D_MODEL, EXPERT_FF), BF16),
            pltpu.VMEM((EXPERT_FF, D_MODEL), BF16),
            pltpu.SemaphoreType.DMA((MOE_WEIGHT_BUFFERS, 3)),
        ],
    )
    return pl.pallas_call(
        _moe_kernel,
        grid_spec=grid_spec,
        out_shape=jax.ShapeDtypeStruct((n_rows * ROW_TILE, LANES), U32),
        compiler_params=pltpu.CompilerParams(dimension_semantics=("arbitrary",), vmem_limit_bytes=_vmem(52)),
        name="moe",
    )(block_e, n_used, first, slot, next_e, next2_e, xs, w_gate, w_up, w_down)


def _moe_layout(route):
    e_id = route[:, 2:2 + TOP_K].astype(jnp.int32)
    rank = route[:, 2 + TOP_K:2 + 2 * TOP_K].astype(jnp.int32)
    onehot_e = e_id[:, :, None] == jnp.arange(N_EXPERTS, dtype=jnp.int32)[None, None, :]
    counts = jnp.sum(onehot_e.astype(jnp.int32), axis=(0, 1))
    padded = (counts + MOE_TM - 1) // MOE_TM * MOE_TM
    pends = jnp.cumsum(padded)
    pstarts = pends - padded
    n_tok = route.shape[0]
    n_blocks = n_tok * TOP_K // MOE_TM + N_EXPERTS
    blk_start = jnp.arange(n_blocks, dtype=jnp.int32) * MOE_TM
    block_e = jnp.minimum(jnp.sum((blk_start[:, None] >= pends[None, :]).astype(jnp.int32), axis=1), N_EXPERTS - 1)
    n_used = (pends[-1] // MOE_TM).astype(jnp.int32).reshape(1)
    blk = jnp.arange(n_blocks, dtype=jnp.int32)
    first = jnp.logical_and(blk < n_used[0], jnp.logical_or(blk == 0, block_e != jnp.roll(block_e, 1)))
    slot = lax.rem(jnp.cumsum(first.astype(jnp.int32)) - 1, MOE_WEIGHT_BUFFERS)
    ex = jnp.arange(N_EXPERTS, dtype=jnp.int32)
    later = jnp.logical_and(counts[None, :] > 0, ex[None, :] > ex[:, None])
    next_of_expert = jnp.min(jnp.where(later, ex[None, :], N_EXPERTS), axis=1)
    next_of_expert = jnp.where(next_of_expert == N_EXPERTS, -1, next_of_expert)
    lookup = lambda keys, table: jnp.sum(jnp.where(keys[:, None] == ex[None, :], table[None, :] + 1, 0), axis=1) - 1
    next2_of_expert = lookup(next_of_expert, next_of_expert)
    next_e = lookup(block_e, next_of_expert)
    next2_e = lookup(block_e, next2_of_expert)
    dest = jnp.sum(jnp.where(onehot_e, pstarts[None, None, :], 0), axis=2) + rank
    pads = ((pstarts + counts).astype(jnp.int32), (padded - counts).astype(jnp.int32),
            jnp.stack([pends[-1], n_blocks - n_used[0]]).astype(jnp.int32))
    tables = (block_e.astype(jnp.int32), n_used, first.astype(jnp.int32), slot.astype(jnp.int32),
              next_e.astype(jnp.int32), next2_e.astype(jnp.int32))
    return tables, dest, pads, n_blocks * MOE_TM


def _final_kernel(dest_ref, destn_ref, x1_ref, route_ref, g_ref, y_hbm, y_flat_hbm, o_ref, ybuf, sem):
    i = pl.program_id(0)
    n_steps = pl.num_programs(0)
    slot = lax.rem(i, 2)
    n_rows = TOP_K * FIN_TM

    def start_gather(idx_ref, s):
        for r in range(n_rows):
            pltpu.make_async_copy(y_hbm.at[idx_ref[0, 0, r]], ybuf.at[pl.ds((s * n_rows + r) * ROW_TILE, ROW_TILE)],
                                  sem.at[s]).start(priority=r % 2)

    @pl.when(i == 0)
    def _():
        start_gather(dest_ref, 0)

    for s in range(2):
        @pl.when(jnp.logical_and(i + 1 < n_steps, 1 - slot == s))
        def _():
            start_gather(destn_ref, s)

    slot_rows = n_rows * ROW_TILE
    slot_start = pl.multiple_of(slot * slot_rows, slot_rows)
    pltpu.make_async_copy(y_flat_hbm.at[pl.ds(0, slot_rows)], ybuf.at[pl.ds(slot_start, slot_rows)],
                          sem.at[slot]).wait()
    route = route_ref[...]
    x = x1_ref[...]
    for k in range(TOP_K):
        ylo, yhi = _unpack_bf16_pairs(_load_rows_from_tiles(ybuf, slot * n_rows + k * FIN_TM, FIN_TM))
        x = x + route[:, k:k + 1] * jnp.concatenate([ylo, yhi], axis=1)
    o_ref[...] = x * lax.rsqrt(jnp.mean(x * x, axis=-1, keepdims=True) + RMS_EPS) * g_ref[...]


def _final(dest, x1, route, g, y_sorted):
    n = x1.shape[0]
    steps = n // FIN_TM
    dest3 = dest.reshape(steps, FIN_TM, TOP_K).transpose(0, 2, 1).reshape(steps, 1, TOP_K * FIN_TM)
    return pl.pallas_call(
        _final_kernel,
        grid=(steps,),
        in_specs=[
            pl.BlockSpec((1, 1, TOP_K * FIN_TM), lambda i: (i, 0, 0), memory_space=pltpu.SMEM),
            pl.BlockSpec((1, 1, TOP_K * FIN_TM), lambda i: (jnp.minimum(i + 1, steps - 1), 0, 0),
                         memory_space=pltpu.SMEM),
            pl.BlockSpec((FIN_TM, D_MODEL), lambda i: (i, 0)),
            pl.BlockSpec((FIN_TM, ROUTE_LANES), lambda i: (i, 0)),
            pl.BlockSpec((1, D_MODEL), lambda i: (0, 0)),
            pl.BlockSpec(memory_space=pl.ANY),
            pl.BlockSpec(memory_space=pl.ANY),
        ],
        out_specs=pl.BlockSpec((FIN_TM, D_MODEL), lambda i: (i, 0)),
        out_shape=jax.ShapeDtypeStruct((n, D_MODEL), F32),
        scratch_shapes=[
            pltpu.VMEM((2 * TOP_K * FIN_TM * ROW_TILE, LANES), U32),
            pltpu.SemaphoreType.DMA((2,)),
        ],
        compiler_params=pltpu.CompilerParams(dimension_semantics=("arbitrary",), vmem_limit_bytes=_vmem(40)),
        name="final",
    )(dest3, dest3, x1, route, g, y_sorted.reshape(-1, ROW_TILE, LANES), y_sorted)


def kernel(x, norm1_g, w_in, attn_sinks, rel_bias, hg_lb_logits, hg_norm_g, w_attn_branch, w_hg_branch, w_out,
           norm2_g, w_group_router, b_group_router, w_expert_router, b_expert_router, w_gate, w_up, w_down, final_g):
    bsz, s_len, d = x.shape
    n_tok = bsz * s_len
    x2d = x.reshape(n_tok, d)

    w_route = jnp.concatenate([w_group_router[0], w_expert_router[0]], axis=1).astype(F32)
    w_route = jnp.pad(w_route, ((0, 0), (0, ROUTE_LANES - w_route.shape[1])))
    whi = w_route.astype(BF16)
    wlo = (w_route - whi.astype(F32)).astype(BF16)
    wr2t = jnp.concatenate([whi.T, wlo.T], axis=0)
    b_route = jnp.concatenate([b_group_router[0], b_expert_router[0]]).astype(F32)
    b_route = jnp.pad(b_route, (0, ROUTE_LANES - b_route.shape[0])).reshape(ROUTE_LANES, 1)
    bias = _attn_bias_table(rel_bias)

    proj = _in_proj(x2d, norm1_g[0].reshape(1, d).astype(F32), w_in)
    proj3 = proj.reshape(bsz, s_len, IN_WIDTH)
    r = _hgrn2(proj3, hg_lb_logits.astype(F32), hg_norm_g[0].reshape(1, HG_DIM).astype(F32))
    x1, h2p, route = _mix(
        attn_sinks[0].astype(F32), bias, r.reshape(n_tok, HG_WIDTH), proj, x2d,
        w_attn_branch[0].astype(BF16), w_hg_branch[0].astype(BF16), w_out[0].astype(BF16),
        norm2_g[0].reshape(1, d).astype(F32), wr2t, b_route, s_len)
    tables, dest, pads, n_rows = _moe_layout(route)
    xs = _dispatch(dest.reshape(n_tok * TOP_K), *pads, h2p.reshape(n_tok, ROW_TILE, LANES), n_rows)
    y_sorted = _moe(*tables, xs.reshape(n_rows * ROW_TILE, LANES), w_gate, w_up, w_down)
    out = _final(dest, x1, route, final_g.reshape(1, d).astype(F32), y_sorted)
    return out.reshape(bsz, s_len, d)
```

```python
import functools
import math

import jax
import jax.numpy as jnp
from jax import lax
from jax.experimental import pallas as pl
from jax.experimental.pallas import tpu as pltpu

D_MODEL = 2048
N_Q_HEADS = 16
N_KV_HEADS = 4
HEAD_DIM = 64
WINDOW = 128
ATTN_BLOCK = 128
ATTN_WIDTH = N_Q_HEADS * HEAD_DIM
KV_WIDTH = N_KV_HEADS * HEAD_DIM
REL_BUCKETS = 32
REL_MAX_DIST = 128
HG_HEADS = 8
HG_DIM = 128
HG_WIDTH = HG_HEADS * HG_DIM
HG_CHUNK = 64
HG_SUB = 16
N_GROUPS = 4
EXPERTS_PER_GROUP = 8
N_EXPERTS = N_GROUPS * EXPERTS_PER_GROUP
TOP_K = 2
EXPERT_FF = 512
RMS_EPS = 1e-6

OFF_GATE_A = 0
OFF_GATE_H = D_MODEL
OFF_QA = 2 * D_MODEL
OFF_QH = OFF_QA + ATTN_WIDTH
OFF_FH = OFF_QH + HG_WIDTH
OFF_IH = OFF_FH + HG_WIDTH
OFF_GH = OFF_IH + HG_WIDTH
OFF_KA = OFF_GH + HG_WIDTH
OFF_VA = OFF_KA + KV_WIDTH
IN_WIDTH = OFF_VA + KV_WIDTH
REF_OFF_QA = 0
REF_OFF_KA = ATTN_WIDTH
REF_OFF_QH = ATTN_WIDTH + 2 * KV_WIDTH
REF_OFF_GATE_A = REF_OFF_QH + 4 * HG_WIDTH

HG_SAFE_DECAY = 60.0

ROW_TILE = 8
LANES = 128
assert D_MODEL // 2 == ROW_TILE * LANES
MXU_WIDTH = 256

IN_TM = 2048
IN_TN = 512
HG_TC = 1024
MIX_TM = 512
ROUTE_LANES = 128
ROUTE_ROWS = 40
ROUTE_VALS = 16
assert N_GROUPS + N_EXPERTS <= ROUTE_ROWS <= ROUTE_LANES
DISPATCH_ROWS = 2048
MOE_TM = 256
MOE_WEIGHT_BUFFERS = 3
FIN_TM = 512

F32 = jnp.float32
BF16 = jnp.bfloat16
U32 = jnp.uint32
NEG_INF = float("-inf")


def _vmem(mib):
    return mib * 1024 * 1024


def _sigmoid(x):
    return 1.0 / (1.0 + jnp.exp(-x))


def _dot_nt(a, b):
    return lax.dot_general(a, b, (((1,), (1,)), ((), ())), preferred_element_type=F32)


def _dot_tn(a, b):
    return lax.dot_general(a, b, (((0,), (0,)), ((), ())), preferred_element_type=F32)


def _in_proj_kernel(x_ref, g_ref, w_ref, o_ref, h_ref):
    @pl.when(pl.program_id(1) == 0)
    def _():
        x = x_ref[...]
        ms = jnp.mean(x * x, axis=-1, keepdims=True)
        h_ref[...] = (x * lax.rsqrt(ms + RMS_EPS) * g_ref[...]).astype(BF16)

    o_ref[...] = jnp.dot(h_ref[...], w_ref[0].astype(BF16), preferred_element_type=F32).astype(o_ref.dtype)


def _in_proj(x2d, g, w_in):
    n = x2d.shape[0]
    n_col = IN_WIDTH // IN_TN

    def out_block(j):
        gates = j - REF_OFF_GATE_A // IN_TN + OFF_GATE_A // IN_TN
        q_a = j - REF_OFF_QA // IN_TN + OFF_QA // IN_TN
        kv = j - REF_OFF_KA // IN_TN + OFF_KA // IN_TN
        hg = j - REF_OFF_QH // IN_TN + OFF_QH // IN_TN
        return jnp.where(j >= REF_OFF_GATE_A // IN_TN, gates,
                         jnp.where(j >= REF_OFF_QH // IN_TN, hg, jnp.where(j >= REF_OFF_KA // IN_TN, kv, q_a)))

    return pl.pallas_call(
        _in_proj_kernel,
        grid=(n // IN_TM, n_col),
        in_specs=[
            pl.BlockSpec((IN_TM, D_MODEL), lambda i, j: (i, 0)),
            pl.BlockSpec((1, D_MODEL), lambda i, j: (0, 0)),
            pl.BlockSpec((1, D_MODEL, IN_TN), lambda i, j: (0, 0, j)),
        ],
        out_specs=pl.BlockSpec((IN_TM, IN_TN), lambda i, j: (i, out_block(j))),
        out_shape=jax.ShapeDtypeStruct((n, IN_WIDTH), BF16),
        scratch_shapes=[pltpu.VMEM((IN_TM, D_MODEL), BF16)],
        compiler_params=pltpu.CompilerParams(
            dimension_semantics=("parallel", "arbitrary"), vmem_limit_bytes=_vmem(58)),
        name="in_proj",
    )(x2d, g, w_in)


def _swa_block(q, kk, vv, has_prev, bias_ref, sink_ref, side_work):
    q = q * jnp.asarray(HEAD_DIM ** -0.5, BF16)
    key = lax.broadcasted_iota(jnp.int32, (ATTN_BLOCK, ATTN_BLOCK), 0)
    qry = lax.broadcasted_iota(jnp.int32, (ATTN_BLOCK, ATTN_BLOCK), 1)
    cur = key <= qry
    key_ok = jnp.logical_or(cur, has_prev)
    grp = N_Q_HEADS // N_KV_HEADS
    head = lambda t, h: t[:, h * HEAD_DIM:(h + 1) * HEAD_DIM]
    lanes = lambda t, g: t[:, g * ATTN_BLOCK:(g + 1) * ATTN_BLOCK]
    scores = []
    for hk in range(N_KV_HEADS):
        q_grp = jnp.concatenate([head(q, hk * grp + g) for g in range(grp)], axis=0)
        scores.append(_dot_nt(head(kk, hk), q_grp))
    emitted = 0
    probs, denoms = [], []
    for hq in range(N_Q_HEADS):
        while emitted * N_Q_HEADS < hq * len(side_work) + len(side_work):
            side_work[emitted]()
            emitted += 1
        s2 = lanes(scores[hq // grp], hq % grp)
        s = jnp.where(cur, s2[ATTN_BLOCK:], s2[:ATTN_BLOCK]) + bias_ref[hq]
        s = jnp.where(key_ok, s, NEG_INF)
        sink = sink_ref[hq]
        m = jnp.maximum(jnp.max(s, axis=0, keepdims=True), sink)
        p = jnp.exp(s - m)
        denoms.append(jnp.sum(p, axis=0, keepdims=True) + jnp.exp(sink - m))
        probs.append(jnp.concatenate([jnp.where(cur, 0.0, p), jnp.where(cur, p, 0.0)], axis=0).astype(BF16))
    outs = []
    for hk in range(N_KV_HEADS):
        p_grp = jnp.concatenate(probs[hk * grp:(hk + 1) * grp], axis=1)
        o_grp = _dot_tn(head(vv, hk), p_grp)
        outs += [lanes(o_grp, g) / denoms[hk * grp + g] for g in range(grp)]
    return jnp.concatenate(outs, axis=0)


def _t5_causal_bucket(n):
    max_exact = REL_BUCKETS // 2
    nf = jnp.maximum(n, 1).astype(F32)
    large = max_exact + (jnp.log(nf / max_exact) / math.log(REL_MAX_DIST / max_exact)
                         * (REL_BUCKETS - max_exact)).astype(jnp.int32)
    large = jnp.minimum(large, REL_BUCKETS - 1)
    return jnp.where(n < max_exact, n, large)


def _attn_bias_table(rel_bias):
    assert WINDOW == ATTN_BLOCK
    r = jnp.arange(ATTN_BLOCK)[None, :]
    c = jnp.arange(ATTN_BLOCK)[:, None]
    dist = jnp.where(c <= r, r - c, r + ATTN_BLOCK - c)
    bucket = _t5_causal_bucket(dist)
    onehot = (bucket[None, :, :] == jnp.arange(REL_BUCKETS)[:, None, None]).astype(F32)
    return jnp.einsum("bh,bqk->hqk", rel_bias.astype(F32), onehot, precision=lax.Precision.HIGHEST)


def _cumsum_rows(tri_bf16, g):
    width = g.shape[1]
    g1 = g.astype(BF16)
    r1 = g - g1.astype(F32)
    g2 = r1.astype(BF16)
    g3 = (r1 - g2.astype(F32)).astype(BF16)
    acc = jnp.dot(tri_bf16, jnp.concatenate([g3, g2, g1], axis=1), preferred_element_type=F32)
    return (acc[:, :width] + acc[:, width:2 * width]) + acc[:, 2 * width:]


def _hgrn_head_safe(q, kf, v, b, st):
    rows = lax.broadcasted_iota(jnp.int32, (HG_SUB, 1), 0)
    outs = []
    prev_end = jnp.zeros((1, HG_DIM), F32)
    for i in range(HG_CHUNK // HG_SUB):
        sl = slice(i * HG_SUB, (i + 1) * HG_SUB)
        bl = b[sl] - prev_end
        qi, ki, vi = q[sl], kf[sl], v[sl]
        vif = vi.astype(F32)
        blast = bl[HG_SUB - 1:HG_SUB]
        o = _dot_nt((qi * jnp.exp(bl)).astype(BF16), st.astype(BF16))
        for s in range(HG_SUB):
            d = jnp.exp(jnp.minimum(bl - bl[s:s + 1], 0.0))
            a = jnp.sum(qi * ki[s:s + 1] * d, axis=-1, keepdims=True)
            o = o + jnp.where(rows >= s, a, 0.0) * vif[s:s + 1]
        kd = (ki * jnp.exp(blast - bl)).astype(BF16)
        st = st * jnp.exp(blast) + _dot_tn(vi, kd)
        prev_end = b[(i + 1) * HG_SUB - 1:(i + 1) * HG_SUB]
        outs.append(o)
    return jnp.concatenate(outs, axis=0), st


def _hgrn_kernel(q_ref, f_ref, i_ref, g_ref, lbl_ref, gn_ref, o_ref, st_ref, kf_ref, b_ref):
    @pl.when(pl.program_id(1) == 0)
    def _():
        st_ref[...] = jnp.zeros_like(st_ref)

    lg2 = lbl_ref[...].astype(F32)
    e = jnp.exp(lg2 - jnp.max(lg2, axis=0, keepdims=True))
    lb = e[0:1] / jnp.sum(e, axis=0, keepdims=True)
    gn = gn_ref[...].astype(F32)
    r = lax.broadcasted_iota(jnp.int32, (HG_CHUNK, HG_CHUNK), 0)
    c = lax.broadcasted_iota(jnp.int32, (HG_CHUNK, HG_CHUNK), 1)
    tri_mask = r >= c
    tri = jnp.where(tri_mask, 1.0, 0.0).astype(BF16)

    n_chunks = HG_TC // HG_CHUNK

    f = lb + (1.0 - lb) * _sigmoid(f_ref[0].astype(F32))
    kf_ref[...] = 1.0 - f
    logf = jnp.log(f)
    worst = None
    for ci in range(n_chunks):
        sl = slice(ci * HG_CHUNK, (ci + 1) * HG_CHUNK)
        b = _cumsum_rows(tri, logf[sl])
        b_ref[sl, :] = b
        bend = b[HG_CHUNK - 1:HG_CHUNK]
        worst = bend if worst is None else jnp.minimum(worst, bend)

    heads = range(HG_HEADS)
    head = lambda t, h: t[:, h * HG_DIM:(h + 1) * HG_DIM]

    def finish_all(sl, outs):
        scale = [lax.rsqrt(jnp.mean(o * o, axis=-1, keepdims=True) + RMS_EPS) for o in outs]
        normed = jnp.concatenate([outs[h] * scale[h] * gn for h in heads], axis=1)
        go = g_ref[0, sl, :].astype(F32)
        o_ref[0, sl, :] = (normed * (go * _sigmoid(go))).astype(o_ref.dtype)

    def fast():
        states = [st_ref[h] for h in heads]
        for ci in range(n_chunks):
            sl = slice(ci * HG_CHUNK, (ci + 1) * HG_CHUNK)
            b = b_ref[sl, :]
            qp = (q_ref[0, sl, :].astype(F32) * jnp.exp(b)).astype(BF16)
            kp = kf_ref[sl, :] * jnp.exp(-b)
            kpb = kp.astype(BF16)
            eb = jnp.exp(b[HG_CHUNK - 1:HG_CHUNK])
            kd = (kp * eb).astype(BF16)
            v = i_ref[0, sl, :]
            att = [_dot_nt(head(qp, h), head(kpb, h)) for h in heads]
            inter = [_dot_nt(head(qp, h), states[h].astype(BF16)) for h in heads]
            update = [_dot_tn(head(v, h), head(kd, h)) for h in heads]
            att = [jnp.where(tri_mask, a, 0.0).astype(BF16) for a in att]
            outs = [inter[h] + jnp.dot(att[h], head(v, h), preferred_element_type=F32) for h in heads]
            states = [states[h] * head(eb, h) + update[h] for h in heads]
            finish_all(sl, outs)
        for h in heads:
            st_ref[h] = states[h]

    def safe():
        def chunk(ci, carry):
            sl = pl.ds(pl.multiple_of(ci * HG_CHUNK, HG_CHUNK), HG_CHUNK)
            outs = []
            for h in heads:
                hs = slice(h * HG_DIM, (h + 1) * HG_DIM)
                o, st_new = _hgrn_head_safe(q_ref[0, sl, hs].astype(F32), kf_ref[sl, hs], i_ref[0, sl, hs],
                                            b_ref[sl, hs], st_ref[h])
                st_ref[h] = st_new
                outs.append(o)
            finish_all(sl, outs)
            return carry

        lax.fori_loop(0, n_chunks, chunk, 0)

    lax.cond(jnp.min(worst) >= -HG_SAFE_DECAY, fast, safe)


def _hgrn2(proj3, lb_logits, gn):
    bsz, s_len, _ = proj3.shape
    blk = lambda off: (lambda b, t: (b, t, off // HG_WIDTH))
    return pl.pallas_call(
        _hgrn_kernel,
        grid=(bsz, s_len // HG_TC),
        in_specs=[
            pl.BlockSpec((1, HG_TC, HG_WIDTH), blk(OFF_QH)),
            pl.BlockSpec((1, HG_TC, HG_WIDTH), blk(OFF_FH)),
            pl.BlockSpec((1, HG_TC, HG_WIDTH), blk(OFF_IH)),
            pl.BlockSpec((1, HG_TC, HG_WIDTH), blk(OFF_GH)),
            pl.BlockSpec((lb_logits.shape[0], HG_WIDTH), lambda b, t: (0, 0)),
            pl.BlockSpec((1, HG_DIM), lambda b, t: (0, 0)),
        ],
        out_specs=pl.BlockSpec((1, HG_TC, HG_WIDTH), lambda b, t: (b, t, 0)),
        out_shape=jax.ShapeDtypeStruct((bsz, s_len, HG_WIDTH), BF16),
        scratch_shapes=[
            pltpu.VMEM((HG_HEADS, HG_DIM, HG_DIM), F32),
            pltpu.VMEM((HG_TC, HG_WIDTH), F32),
            pltpu.VMEM((HG_TC, HG_WIDTH), F32),
        ],
        compiler_params=pltpu.CompilerParams(
            dimension_semantics=("parallel", "arbitrary"), vmem_limit_bytes=_vmem(48)),
        name="hgrn2",
    )(proj3, proj3, proj3, proj3, lb_logits, gn)


def _store_rows_as_tiles(ref, val):
    rows = val.shape[0]
    for s in range(ROW_TILE):
        ref[pl.ds(s, rows, stride=ROW_TILE), :] = val[:, s * LANES:(s + 1) * LANES]


def _load_rows_from_tiles(ref, start, rows):
    return jnp.concatenate(
        [ref[pl.ds(start * ROW_TILE + s, rows, stride=ROW_TILE), :] for s in range(ROW_TILE)], axis=1)


def _pack_bf16_pairs(x):
    c = x.shape[1] // 2
    bits = pltpu.bitcast(x.astype(BF16).astype(F32), U32)
    return jnp.bitwise_or(jnp.right_shift(bits[:, :c], jnp.uint32(16)), bits[:, c:])


def _unpack_bf16_pairs(p):
    lo = pltpu.bitcast(jnp.left_shift(p, jnp.uint32(16)), F32)
    hi = pltpu.bitcast(jnp.bitwise_and(p, jnp.uint32(0xFFFF0000)), F32)
    return lo, hi


def _mix_kernel(sink_ref, q_ref, kp_ref, kc_ref, vp_ref, vc_ref, bias_ref,
                r_ref, ga_ref, gh_ref, x_ref, wa_ref, wr_ref, wo_ref, g2_ref, wr2t_ref, brt_ref,
                x1_ref, h2p_ref, route_ref, cnt_ref, *, tiles_per_seq):
    i = pl.program_id(0)

    @pl.when(i == 0)
    def _():
        cnt_ref[...] = jnp.zeros_like(cnt_ref)

    r_tile = r_ref[...]
    n_blk = MIX_TM // ATTN_BLOCK
    n_piece = D_MODEL // MXU_WIDTH
    rm_pieces = []

    def rm_piece(c):
        def emit():
            rm_pieces.append(jnp.dot(r_tile, wr_ref[:, c * MXU_WIDTH:(c + 1) * MXU_WIDTH],
                                     preferred_element_type=F32))
        return emit

    k_rows = jnp.concatenate([kp_ref[...], kc_ref[...]], axis=0)
    v_rows = jnp.concatenate([vp_ref[...], vc_ref[...]], axis=0)
    first_in_seq = lax.rem(i, tiles_per_seq) == 0
    blocks = []
    for blk in range(n_blk):
        rows = slice(blk * ATTN_BLOCK, (blk + 1) * ATTN_BLOCK)
        window = slice(blk * ATTN_BLOCK, (blk + 2) * ATTN_BLOCK)
        has_prev = jnp.logical_not(first_in_seq) if blk == 0 else True
        side = [rm_piece(c) for c in range(blk * n_piece // n_blk, (blk + 1) * n_piece // n_blk)]
        blocks.append(_swa_block(q_ref[rows, :], k_rows[window], v_rows[window], has_prev, bias_ref, sink_ref, side))
    rm = jnp.concatenate(rm_pieces, axis=1)
    a_t = jnp.concatenate(blocks, axis=1).astype(BF16)
    am = _dot_tn(a_t, wa_ref[...])
    mixed = _sigmoid(ga_ref[...].astype(F32)) * am + _sigmoid(gh_ref[...].astype(F32)) * rm
    x1 = x_ref[...] + jnp.dot(mixed.astype(BF16), wo_ref[...], preferred_element_type=F32)
    x1_ref[...] = x1
    h2 = x1 * lax.rsqrt(jnp.mean(x1 * x1, axis=-1, keepdims=True) + RMS_EPS) * g2_ref[...]

    _store_rows_as_tiles(h2p_ref, _pack_bf16_pairs(h2))

    hi = h2.astype(BF16)
    lo = (h2 - hi.astype(F32)).astype(BF16)
    both = _dot_nt(wr2t_ref[...], hi)
    logits = both[:ROUTE_LANES] + both[ROUTE_LANES:] + _dot_nt(wr2t_ref[:ROUTE_LANES, :], lo)
    logits = logits[:ROUTE_ROWS] + brt_ref[:ROUTE_ROWS, :]

    tm = logits.shape[1]
    row = lax.broadcasted_iota(jnp.int32, logits.shape, 0)
    row_f = row.astype(F32)
    big = float(ROUTE_ROWS)
    gl = jnp.where(row < N_GROUPS, logits, NEG_INF)
    gmax = jnp.max(gl, axis=0, keepdims=True)
    g_idx = jnp.min(jnp.where(gl == gmax, row_f, big), axis=0, keepdims=True)
    g_w = 1.0 / jnp.sum(jnp.exp(gl - gmax), axis=0, keepdims=True)
    e_row = row - N_GROUPS
    in_group = jnp.logical_and(e_row >= 0, e_row < N_EXPERTS)
    row_group = jnp.right_shift(e_row, 3).astype(F32)
    in_group = jnp.logical_and(in_group, row_group == g_idx)
    el = jnp.where(in_group, logits, NEG_INF)
    m1 = jnp.max(el, axis=0, keepdims=True)
    i1 = jnp.min(jnp.where(el == m1, row_f, big), axis=0, keepdims=True)
    el2 = jnp.where(row_f == i1, NEG_INF, el)
    m2 = jnp.max(el2, axis=0, keepdims=True)
    i2 = jnp.min(jnp.where(el2 == m2, row_f, big), axis=0, keepdims=True)
    p2 = jnp.exp(m2 - m1)
    w1 = g_w / (1.0 + p2)
    w2 = g_w * p2 / (1.0 + p2)
    e1 = i1 - N_GROUPS
    e2 = i2 - N_GROUPS

    ss = lax.broadcasted_iota(jnp.int32, (tm, tm), 0)
    tt = lax.broadcasted_iota(jnp.int32, (tm, tm), 1)
    earlier = jnp.where(ss < tt, 1.0, 0.0).astype(BF16)
    sel1 = row_f == i1
    sel2 = row_f == i2
    oh1 = jnp.where(sel1, 1.0, 0.0)
    oh2 = jnp.where(sel2, 1.0, 0.0)
    base = cnt_ref[...]
    c1 = jnp.sum(oh1, axis=1, keepdims=True)
    c2 = jnp.sum(oh2, axis=1, keepdims=True)
    pre1 = jnp.dot(oh1.astype(BF16), earlier, preferred_element_type=F32) + base
    pre2 = jnp.dot(oh2.astype(BF16), earlier, preferred_element_type=F32) + (base + c1)
    rank1 = jnp.sum(jnp.where(sel1, pre1, 0.0), axis=0, keepdims=True)
    rank2 = jnp.sum(jnp.where(sel2, pre2, 0.0), axis=0, keepdims=True)
    cnt_ref[...] = base + c1 + c2

    vals = jnp.concatenate([w1, w2, e1, e2, rank1, rank2, jnp.zeros((ROUTE_VALS - 6, tm), F32)], axis=0)
    v1 = vals.astype(BF16)
    rest = vals - v1.astype(F32)
    v2 = rest.astype(BF16)
    v3 = (rest - v2.astype(F32)).astype(BF16)
    kk = lax.broadcasted_iota(jnp.int32, (ROUTE_VALS, ROUTE_LANES), 0)
    ll = lax.broadcasted_iota(jnp.int32, (ROUTE_VALS, ROUTE_LANES), 1)
    pick = jnp.where(kk == ll, 1.0, 0.0).astype(BF16)
    route_ref[...] = (_dot_tn(v3, pick) + _dot_tn(v2, pick)) + _dot_tn(v1, pick)


def _mix(sinks, bias, r2d, proj, x2d, wa, wr, wo, g2, wr2t, brt, seq_len):
    n = x2d.shape[0]
    row = lambda i: (i, 0)
    const = lambda i: (0, 0)
    blocks_per_tile = MIX_TM // ATTN_BLOCK
    prev_block = lambda i: jnp.maximum(blocks_per_tile * i - 1, 0)
    return pl.pallas_call(
        functools.partial(_mix_kernel, tiles_per_seq=seq_len // MIX_TM),
        grid=(n // MIX_TM,),
        in_specs=[
            pl.BlockSpec(memory_space=pltpu.SMEM),
            pl.BlockSpec((MIX_TM, ATTN_WIDTH), lambda i: (i, OFF_QA // ATTN_WIDTH)),
            pl.BlockSpec((ATTN_BLOCK, KV_WIDTH), lambda i: (prev_block(i), OFF_KA // KV_WIDTH)),
            pl.BlockSpec((MIX_TM, KV_WIDTH), lambda i: (i, OFF_KA // KV_WIDTH)),
            pl.BlockSpec((ATTN_BLOCK, KV_WIDTH), lambda i: (prev_block(i), OFF_VA // KV_WIDTH)),
            pl.BlockSpec((MIX_TM, KV_WIDTH), lambda i: (i, OFF_VA // KV_WIDTH)),
            pl.BlockSpec((N_Q_HEADS, ATTN_BLOCK, ATTN_BLOCK), lambda i: (0, 0, 0)),
            pl.BlockSpec((MIX_TM, HG_WIDTH), row),
            pl.BlockSpec((MIX_TM, D_MODEL), lambda i: (i, OFF_GATE_A // D_MODEL)),
            pl.BlockSpec((MIX_TM, D_MODEL), lambda i: (i, OFF_GATE_H // D_MODEL)),
            pl.BlockSpec((MIX_TM, D_MODEL), row),
            pl.BlockSpec((ATTN_WIDTH, D_MODEL), const, pipeline_mode=pl.Buffered(1)),
            pl.BlockSpec((HG_WIDTH, D_MODEL), const, pipeline_mode=pl.Buffered(1)),
            pl.BlockSpec((D_MODEL, D_MODEL), const, pipeline_mode=pl.Buffered(1)),
            pl.BlockSpec((1, D_MODEL), const),
            pl.BlockSpec((2 * ROUTE_LANES, D_MODEL), const, pipeline_mode=pl.Buffered(1)),
            pl.BlockSpec((ROUTE_LANES, 1), const),
        ],
        out_specs=[
            pl.BlockSpec((MIX_TM, D_MODEL), row),
            pl.BlockSpec((MIX_TM * ROW_TILE, LANES), row),
            pl.BlockSpec((MIX_TM, ROUTE_LANES), row),
        ],
        out_shape=[
            jax.ShapeDtypeStruct((n, D_MODEL), F32),
            jax.ShapeDtypeStruct((n * ROW_TILE, LANES), U32),
            jax.ShapeDtypeStruct((n, ROUTE_LANES), F32),
        ],
        scratch_shapes=[pltpu.VMEM((ROUTE_ROWS, 1), F32)],
        compiler_params=pltpu.CompilerParams(dimension_semantics=("arbitrary",), vmem_limit_bytes=_vmem(60)),
        name="mix",
    )(sinks, proj, proj, proj, proj, proj, bias, r2d, proj, proj, x2d, wa, wr, wo, g2, wr2t, brt)


def _dispatch_kernel(pad_start_ref, pad_count_ref, tail_ref, dest_ref, src_ref, dst_hbm, zero_ref, sem):
    @pl.when(pl.program_id(0) == 0)
    def _():
        zero_ref[...] = jnp.zeros_like(zero_ref)
        for phase in ("start", "wait"):
            for e in range(N_EXPERTS):
                count, start = pad_count_ref[e], pad_start_ref[e]
                size = MOE_TM // 2
                while size >= 1:
                    done = jnp.bitwise_and(count, -2 * size)
                    copy = pltpu.make_async_copy(zero_ref.at[pl.ds(0, size)],
                                                 dst_hbm.at[pl.ds(start + done, size)], sem.at[1])
                    pl.when(jnp.bitwise_and(count, size) != 0)(getattr(copy, phase))
                    size //= 2
            for j in range(N_EXPERTS):
                copy = pltpu.make_async_copy(zero_ref, dst_hbm.at[pl.ds(tail_ref[0] + j * MOE_TM, MOE_TM)],
                                             sem.at[1])
                pl.when(j < tail_ref[1])(getattr(copy, phase))

    for a in range(DISPATCH_ROWS):
        pltpu.make_async_copy(src_ref.at[a // TOP_K], dst_hbm.at[dest_ref[0, 0, a]], sem.at[0]).start(priority=a % 2)
    for _ in range(TOP_K):
        pltpu.make_async_copy(src_ref, dst_hbm.at[pl.ds(0, DISPATCH_ROWS // TOP_K)], sem.at[0]).wait()


def _dispatch(dest_flat, pad_start, pad_count, tail, h2p_tiles, n_rows):
    n_asg = dest_flat.shape[0]
    steps = n_asg // DISPATCH_ROWS
    smem = pl.BlockSpec(memory_space=pltpu.SMEM)
    return pl.pallas_call(
        _dispatch_kernel,
        grid=(steps,),
        in_specs=[
            smem, smem, smem,
            pl.BlockSpec((1, 1, DISPATCH_ROWS), lambda s: (s, 0, 0), memory_space=pltpu.SMEM),
            pl.BlockSpec((DISPATCH_ROWS // TOP_K, ROW_TILE, LANES), lambda s: (s, 0, 0)),
        ],
        out_specs=pl.BlockSpec(memory_space=pl.ANY),
        out_shape=jax.ShapeDtypeStruct((n_rows, ROW_TILE, LANES), U32),
        scratch_shapes=[pltpu.VMEM((MOE_TM, ROW_TILE, LANES), U32), pltpu.SemaphoreType.DMA((2,))],
        compiler_params=pltpu.CompilerParams(dimension_semantics=("arbitrary",)),
        name="dispatch",
    )(pad_start, pad_count, tail, dest_flat.reshape(steps, 1, DISPATCH_ROWS), h2p_tiles)


def _moe_kernel(be_ref, nused_ref, first_ref, slot_ref, next_ref, next2_ref, xs_ref, wg_hbm, wu_hbm, wd_hbm, y_ref,
                wg_buf, wu_buf, wd_buf, wg_s, wu_s, wd_s, sem):
    b = pl.program_id(0)
    live = b < nused_ref[0]

    def weight_copies(e, s):
        return (pltpu.make_async_copy(wg_hbm.at[0, e], wg_buf.at[s], sem.at[s, 0]),
                pltpu.make_async_copy(wu_hbm.at[0, e], wu_buf.at[s], sem.at[s, 1]),
                pltpu.make_async_copy(wd_hbm.at[0, e], wd_buf.at[s], sem.at[s, 2]))

    @pl.when(b == 0)
    def _():
        for c in weight_copies(be_ref[0], 0):
            c.start()

        @pl.when(next_ref[0] >= 0)
        def _():
            for c in weight_copies(next_ref[0], 1):
                c.start()

    for s in range(MOE_WEIGHT_BUFFERS):
        @pl.when(jnp.logical_and(jnp.logical_and(live, first_ref[b] == 1), slot_ref[b] == s))
        def _():
            for c in weight_copies(be_ref[b], s):
                c.wait()

            @pl.when(next2_ref[b] >= 0)
            def _():
                for c in weight_copies(next2_ref[b], (s + 2) % MOE_WEIGHT_BUFFERS):
                    c.start()

            wg_s[...] = wg_buf[s].astype(BF16)
            wu_s[...] = wu_buf[s].astype(BF16)
            wd_s[...] = wd_buf[s].astype(BF16)

    @pl.when(jnp.logical_not(live))
    def _():
        y_ref[...] = jnp.zeros_like(y_ref)

    @pl.when(live)
    def _():
        xlo, xhi = _unpack_bf16_pairs(_load_rows_from_tiles(xs_ref, 0, MOE_TM))
        xb = jnp.concatenate([xlo.astype(BF16), xhi.astype(BF16)], axis=1)
        hb = []
        for c in range(EXPERT_FF // MXU_WIDTH):
            cols = slice(c * MXU_WIDTH, (c + 1) * MXU_WIDTH)
            hg = jnp.dot(xb, wg_s[:, cols], preferred_element_type=F32)
            hu = jnp.dot(xb, wu_s[:, cols], preferred_element_type=F32)
            hb.append((hg * _sigmoid(hg) * hu).astype(BF16))
        hb = jnp.concatenate(hb, axis=1)
        half = D_MODEL // 2
        for c in range(half // MXU_WIDTH):
            lo = jnp.dot(hb, wd_s[:, c * MXU_WIDTH:(c + 1) * MXU_WIDTH], preferred_element_type=F32)
            hi = jnp.dot(hb, wd_s[:, half + c * MXU_WIDTH:half + (c + 1) * MXU_WIDTH], preferred_element_type=F32)
            packed = _pack_bf16_pairs(jnp.concatenate([lo, hi], axis=1))
            for k in range(MXU_WIDTH // LANES):
                s = c * (MXU_WIDTH // LANES) + k
                y_ref[pl.ds(s, MOE_TM, stride=ROW_TILE), :] = packed[:, k * LANES:(k + 1) * LANES]


def _moe(block_e, n_used, first, slot, next_e, next2_e, xs, w_gate, w_up, w_down):
    n_rows = xs.shape[0] // ROW_TILE
    n_blocks = n_rows // MOE_TM
    grid_spec = pltpu.PrefetchScalarGridSpec(
        num_scalar_prefetch=6,
        grid=(n_blocks,),
        in_specs=[
            pl.BlockSpec((MOE_TM * ROW_TILE, LANES), lambda b, be, nu, *_: (jnp.minimum(b, nu[0] - 1), 0)),
            pl.BlockSpec(memory_space=pl.ANY),
            pl.BlockSpec(memory_space=pl.ANY),
            pl.BlockSpec(memory_space=pl.ANY),
        ],
        out_specs=pl.BlockSpec((MOE_TM * ROW_TILE, LANES), lambda b, *_: (b, 0)),
        scratch_shapes=[
            pltpu.VMEM((MOE_WEIGHT_BUFFERS, D_MODEL, EXPERT_FF), F32),
            pltpu.VMEM((MOE_WEIGHT_BUFFERS, D_MODEL, EXPERT_FF), F32),
            pltpu.VMEM((MOE_WEIGHT_BUFFERS, EXPERT_FF, D_MODEL), F32),
            pltpu.VMEM((D_MODEL, EXPERT_FF), BF16),
            pltpu.VMEM((D_MODEL, EXPERT_FF), BF16),
            pltpu.VMEM((EXPERT_FF, D_MODEL), BF16),
            pltpu.SemaphoreType.DMA((MOE_WEIGHT_BUFFERS, 3)),
        ],
    )
    return pl.pallas_call(
        _moe_kernel,
        grid_spec=grid_spec,
        out_shape=jax.ShapeDtypeStruct((n_rows * ROW_TILE, LANES), U32),
        compiler_params=pltpu.CompilerParams(dimension_semantics=("arbitrary",), vmem_limit_bytes=_vmem(52)),
        name="moe",
    )(block_e, n_used, first, slot, next_e, next2_e, xs, w_gate, w_up, w_down)


def _moe_layout(route):
    e_id = route[:, 2:2 + TOP_K].astype(jnp.int32)
    rank = route[:, 2 + TOP_K:2 + 2 * TOP_K].astype(jnp.int32)
    onehot_e = e_id[:, :, None] == jnp.arange(N_EXPERTS, dtype=jnp.int32)[None, None, :]
    counts = jnp.sum(onehot_e.astype(jnp.int32), axis=(0, 1))
    padded = (counts + MOE_TM - 1) // MOE_TM * MOE_TM
    pends = jnp.cumsum(padded)
    pstarts = pends - padded
    n_tok = route.shape[0]
    n_blocks = n_tok * TOP_K // MOE_TM + N_EXPERTS
    blk_start = jnp.arange(n_blocks, dtype=jnp.int32) * MOE_TM
    block_e = jnp.minimum(jnp.sum((blk_start[:, None] >= pends[None, :]).astype(jnp.int32), axis=1), N_EXPERTS - 1)
    n_used = (pends[-1] // MOE_TM).astype(jnp.int32).reshape(1)
    blk = jnp.arange(n_blocks, dtype=jnp.int32)
    first = jnp.logical_and(blk < n_used[0], jnp.logical_or(blk == 0, block_e != jnp.roll(block_e, 1)))
    slot = lax.rem(jnp.cumsum(first.astype(jnp.int32)) - 1, MOE_WEIGHT_BUFFERS)
    ex = jnp.arange(N_EXPERTS, dtype=jnp.int32)
    later = jnp.logical_and(counts[None, :] > 0, ex[None, :] > ex[:, None])
    next_of_expert = jnp.min(jnp.where(later, ex[None, :], N_EXPERTS), axis=1)
    next_of_expert = jnp.where(next_of_expert == N_EXPERTS, -1, next_of_expert)
    lookup = lambda keys, table: jnp.sum(jnp.where(keys[:, None] == ex[None, :], table[None, :] + 1, 0), axis=1) - 1
    next2_of_expert = lookup(next_of_expert, next_of_expert)
    next_e = lookup(block_e, next_of_expert)
    next2_e = lookup(block_e, next2_of_expert)
    dest = jnp.sum(jnp.where(onehot_e, pstarts[None, None, :], 0), axis=2) + rank
    pads = ((pstarts + counts).astype(jnp.int32), (padded - counts).astype(jnp.int32),
            jnp.stack([pends[-1], n_blocks - n_used[0]]).astype(jnp.int32))
    tables = (block_e.astype(jnp.int32), n_used, first.astype(jnp.int32), slot.astype(jnp.int32),
              next_e.astype(jnp.int32), next2_e.astype(jnp.int32))
    return tables, dest, pads, n_blocks * MOE_TM


def _final_kernel(dest_ref, destn_ref, x1_ref, route_ref, g_ref, y_hbm, y_flat_hbm, o_ref, ybuf, sem):
    i = pl.program_id(0)
    n_steps = pl.num_programs(0)
    slot = lax.rem(i, 2)
    n_rows = TOP_K * FIN_TM

    def start_gather(idx_ref, s):
        for r in range(n_rows):
            pltpu.make_async_copy(y_hbm.at[idx_ref[0, 0, r]], ybuf.at[pl.ds((s * n_rows + r) * ROW_TILE, ROW_TILE)],
                                  sem.at[s]).start(priority=r % 2)

    @pl.when(i == 0)
    def _():
        start_gather(dest_ref, 0)

    for s in range(2):
        @pl.when(jnp.logical_and(i + 1 < n_steps, 1 - slot == s))
        def _():
            start_gather(destn_ref, s)

    slot_rows = n_rows * ROW_TILE
    slot_start = pl.multiple_of(slot * slot_rows, slot_rows)
    pltpu.make_async_copy(y_flat_hbm.at[pl.ds(0, slot_rows)], ybuf.at[pl.ds(slot_start, slot_rows)],
                          sem.at[slot]).wait()
    route = route_ref[...]
    x = x1_ref[...]
    for k in range(TOP_K):
        ylo, yhi = _unpack_bf16_pairs(_load_rows_from_tiles(ybuf, slot * n_rows + k * FIN_TM, FIN_TM))
        x = x + route[:, k:k + 1] * jnp.concatenate([ylo, yhi], axis=1)
    o_ref[...] = x * lax.rsqrt(jnp.mean(x * x, axis=-1, keepdims=True) + RMS_EPS) * g_ref[...]


def _final(dest, x1, route, g, y_sorted):
    n = x1.shape[0]
    steps = n // FIN_TM
    dest3 = dest.reshape(steps, FIN_TM, TOP_K).transpose(0, 2, 1).reshape(steps, 1, TOP_K * FIN_TM)
    return pl.pallas_call(
        _final_kernel,
        grid=(steps,),
        in_specs=[
            pl.BlockSpec((1, 1, TOP_K * FIN_TM), lambda i: (i, 0, 0), memory_space=pltpu.SMEM),
            pl.BlockSpec((1, 1, TOP_K * FIN_TM), lambda i: (jnp.minimum(i + 1, steps - 1), 0, 0),
                         memory_space=pltpu.SMEM),
            pl.BlockSpec((FIN_TM, D_MODEL), lambda i: (i, 0)),
            pl.BlockSpec((FIN_TM, ROUTE_LANES), lambda i: (i, 0)),
            pl.BlockSpec((1, D_MODEL), lambda i: (0, 0)),
            pl.BlockSpec(memory_space=pl.ANY),
            pl.BlockSpec(memory_space=pl.ANY),
        ],
        out_specs=pl.BlockSpec((FIN_TM, D_MODEL), lambda i: (i, 0)),
        out_shape=jax.ShapeDtypeStruct((n, D_MODEL), F32),
        scratch_shapes=[
            pltpu.VMEM((2 * TOP_K * FIN_TM * ROW_TILE, LANES), U32),
            pltpu.SemaphoreType.DMA((2,)),
        ],
        compiler_params=pltpu.CompilerParams(dimension_semantics=("arbitrary",), vmem_limit_bytes=_vmem(40)),
        name="final",
    )(dest3, dest3, x1, route, g, y_sorted.reshape(-1, ROW_TILE, LANES), y_sorted)


def kernel(x, norm1_g, w_in, attn_sinks, rel_bias, hg_lb_logits, hg_norm_g, w_attn_branch, w_hg_branch, w_out,
           norm2_g, w_group_router, b_group_router, w_expert_router, b_expert_router, w_gate, w_up, w_down, final_g):
    bsz, s_len, d = x.shape
    n_tok = bsz * s_len
    x2d = x.reshape(n_tok, d)

    w_route = jnp.concatenate([w_group_router[0], w_expert_router[0]], axis=1).astype(F32)
    w_route = jnp.pad(w_route, ((0, 0), (0, ROUTE_LANES - w_route.shape[1])))
    whi = w_route.astype(BF16)
    wlo = (w_route - whi.astype(F32)).astype(BF16)
    wr2t = jnp.concatenate([whi.T, wlo.T], axis=0)
    b_route = jnp.concatenate([b_group_router[0], b_expert_router[0]]).astype(F32)
    b_route = jnp.pad(b_route, (0, ROUTE_LANES - b_route.shape[0])).reshape(ROUTE_LANES, 1)
    bias = _attn_bias_table(rel_bias)

    proj = _in_proj(x2d, norm1_g[0].reshape(1, d).astype(F32), w_in)
    proj3 = proj.reshape(bsz, s_len, IN_WIDTH)
    r = _hgrn2(proj3, hg_lb_logits.astype(F32), hg_norm_g[0].reshape(1, HG_DIM).astype(F32))
    x1, h2p, route = _mix(
        attn_sinks[0].astype(F32), bias, r.reshape(n_tok, HG_WIDTH), proj, x2d,
        w_attn_branch[0].astype(BF16), w_hg_branch[0].astype(BF16), w_out[0].astype(BF16),
        norm2_g[0].reshape(1, d).astype(F32), wr2t, b_route, s_len)
    tables, dest, pads, n_rows = _moe_layout(route)
    xs = _dispatch(dest.reshape(n_tok * TOP_K), *pads, h2p.reshape(n_tok, ROW_TILE, LANES), n_rows)
    y_sorted = _moe(*tables, xs.reshape(n_rows * ROW_TILE, LANES), w_gate, w_up, w_down)
    out = _final(dest, x1, route, final_g.reshape(1, d).astype(F32), y_sorted)
    return out.reshape(bsz, s_len, d)
```

```python
import functools
import math

import jax
import jax.numpy as jnp
from jax import lax
from jax.experimental import pallas as pl
from jax.experimental.pallas import tpu as pltpu

D_MODEL = 2048
N_Q_HEADS = 16
N_KV_HEADS = 4
HEAD_DIM = 64
WINDOW = 128
ATTN_BLOCK = 128
ATTN_WIDTH = N_Q_HEADS * HEAD_DIM
KV_WIDTH = N_KV_HEADS * HEAD_DIM
REL_BUCKETS = 32
REL_MAX_DIST = 128
HG_HEADS = 8
HG_DIM = 128
HG_WIDTH = HG_HEADS * HG_DIM
HG_CHUNK = 64
HG_SUB = 16
N_GROUPS = 4
EXPERTS_PER_GROUP = 8
N_EXPERTS = N_GROUPS * EXPERTS_PER_GROUP
TOP_K = 2
EXPERT_FF = 512
RMS_EPS = 1e-6

OFF_GATE_A = 0
OFF_GATE_H = D_MODEL
OFF_QA = 2 * D_MODEL
OFF_QH = OFF_QA + ATTN_WIDTH
OFF_FH = OFF_QH + HG_WIDTH
OFF_IH = OFF_FH + HG_WIDTH
OFF_GH = OFF_IH + HG_WIDTH
OFF_KA = OFF_GH + HG_WIDTH
OFF_VA = OFF_KA + KV_WIDTH
IN_WIDTH = OFF_VA + KV_WIDTH
REF_OFF_QA = 0
REF_OFF_KA = ATTN_WIDTH
REF_OFF_QH = ATTN_WIDTH + 2 * KV_WIDTH
REF_OFF_GATE_A = REF_OFF_QH + 4 * HG_WIDTH

HG_SAFE_DECAY = 60.0

ROW_TILE = 8
LANES = 128
assert D_MODEL // 2 == ROW_TILE * LANES
MXU_WIDTH = 256

IN_TM = 2048
IN_TN = 512
HG_TC = 1024
MIX_TM = 512
ROUTE_LANES = 128
ROUTE_ROWS = 40
ROUTE_VALS = 16
assert N_GROUPS + N_EXPERTS <= ROUTE_ROWS <= ROUTE_LANES
DISPATCH_ROWS = 2048
MOE_TM = 256
MOE_WEIGHT_BUFFERS = 3
FIN_TM = 128

F32 = jnp.float32
BF16 = jnp.bfloat16
U32 = jnp.uint32
NEG_INF = float("-inf")


def _vmem(mib):
    return mib * 1024 * 1024


def _sigmoid(x):
    return 1.0 / (1.0 + jnp.exp(-x))


def _dot_nt(a, b):
    return lax.dot_general(a, b, (((1,), (1,)), ((), ())), preferred_element_type=F32)


def _dot_tn(a, b):
    return lax.dot_general(a, b, (((0,), (0,)), ((), ())), preferred_element_type=F32)


def _in_proj_kernel(x_ref, g_ref, w_ref, o_ref, h_ref):
    @pl.when(pl.program_id(1) == 0)
    def _():
        x = x_ref[...]
        ms = jnp.mean(x * x, axis=-1, keepdims=True)
        h_ref[...] = (x * lax.rsqrt(ms + RMS_EPS) * g_ref[...]).astype(BF16)

    o_ref[...] = jnp.dot(h_ref[...], w_ref[0].astype(BF16), preferred_element_type=F32).astype(o_ref.dtype)


def _in_proj(x2d, g, w_in):
    n = x2d.shape[0]
    n_col = IN_WIDTH // IN_TN

    def out_block(j):
        gates = j - REF_OFF_GATE_A // IN_TN + OFF_GATE_A // IN_TN
        q_a = j - REF_OFF_QA // IN_TN + OFF_QA // IN_TN
        kv = j - REF_OFF_KA // IN_TN + OFF_KA // IN_TN
        hg = j - REF_OFF_QH // IN_TN + OFF_QH // IN_TN
        return jnp.where(j >= REF_OFF_GATE_A // IN_TN, gates,
                         jnp.where(j >= REF_OFF_QH // IN_TN, hg, jnp.where(j >= REF_OFF_KA // IN_TN, kv, q_a)))

    return pl.pallas_call(
        _in_proj_kernel,
        grid=(n // IN_TM, n_col),
        in_specs=[
            pl.BlockSpec((IN_TM, D_MODEL), lambda i, j: (i, 0)),
            pl.BlockSpec((1, D_MODEL), lambda i, j: (0, 0)),
            pl.BlockSpec((1, D_MODEL, IN_TN), lambda i, j: (0, 0, j)),
        ],
        out_specs=pl.BlockSpec((IN_TM, IN_TN), lambda i, j: (i, out_block(j))),
        out_shape=jax.ShapeDtypeStruct((n, IN_WIDTH), BF16),
        scratch_shapes=[pltpu.VMEM((IN_TM, D_MODEL), BF16)],
        compiler_params=pltpu.CompilerParams(
            dimension_semantics=("parallel", "arbitrary"), vmem_limit_bytes=_vmem(58)),
        name="in_proj",
    )(x2d, g, w_in)


def _swa_block(q, kk, vv, has_prev, bias_ref, sink_ref, side_work):
    q = q * jnp.asarray(HEAD_DIM ** -0.5, BF16)
    key = lax.broadcasted_iota(jnp.int32, (ATTN_BLOCK, ATTN_BLOCK), 0)
    qry = lax.broadcasted_iota(jnp.int32, (ATTN_BLOCK, ATTN_BLOCK), 1)
    cur = key <= qry
    key_ok = jnp.logical_or(cur, has_prev)
    grp = N_Q_HEADS // N_KV_HEADS
    head = lambda t, h: t[:, h * HEAD_DIM:(h + 1) * HEAD_DIM]
    lanes = lambda t, g: t[:, g * ATTN_BLOCK:(g + 1) * ATTN_BLOCK]
    scores = []
    for hk in range(N_KV_HEADS):
        q_grp = jnp.concatenate([head(q, hk * grp + g) for g in range(grp)], axis=0)
        scores.append(_dot_nt(head(kk, hk), q_grp))
    emitted = 0
    probs, denoms = [], []
    for hq in range(N_Q_HEADS):
        while emitted * N_Q_HEADS < hq * len(side_work) + len(side_work):
            side_work[emitted]()
            emitted += 1
        s2 = lanes(scores[hq // grp], hq % grp)
        s = jnp.where(cur, s2[ATTN_BLOCK:], s2[:ATTN_BLOCK]) + bias_ref[hq]
        s = jnp.where(key_ok, s, NEG_INF)
        sink = sink_ref[hq]
        m = jnp.maximum(jnp.max(s, axis=0, keepdims=True), sink)
        p = jnp.exp(s - m)
        denoms.append(jnp.sum(p, axis=0, keepdims=True) + jnp.exp(sink - m))
        probs.append(jnp.concatenate([jnp.where(cur, 0.0, p), jnp.where(cur, p, 0.0)], axis=0).astype(BF16))
    outs = []
    for hk in range(N_KV_HEADS):
        p_grp = jnp.concatenate(probs[hk * grp:(hk + 1) * grp], axis=1)
        o_grp = _dot_tn(head(vv, hk), p_grp)
        outs += [lanes(o_grp, g) / denoms[hk * grp + g] for g in range(grp)]
    return jnp.concatenate(outs, axis=0)


def _t5_causal_bucket(n):
    max_exact = REL_BUCKETS // 2
    nf = jnp.maximum(n, 1).astype(F32)
    large = max_exact + (jnp.log(nf / max_exact) / math.log(REL_MAX_DIST / max_exact)
                         * (REL_BUCKETS - max_exact)).astype(jnp.int32)
    large = jnp.minimum(large, REL_BUCKETS - 1)
    return jnp.where(n < max_exact, n, large)


def _attn_bias_table(rel_bias):
    assert WINDOW == ATTN_BLOCK
    r = jnp.arange(ATTN_BLOCK)[None, :]
    c = jnp.arange(ATTN_BLOCK)[:, None]
    dist = jnp.where(c <= r, r - c, r + ATTN_BLOCK - c)
    bucket = _t5_causal_bucket(dist)
    onehot = (bucket[None, :, :] == jnp.arange(REL_BUCKETS)[:, None, None]).astype(F32)
    return jnp.einsum("bh,bqk->hqk", rel_bias.astype(F32), onehot, precision=lax.Precision.HIGHEST)


def _cumsum_rows(tri_bf16, g):
    width = g.shape[1]
    g1 = g.astype(BF16)
    r1 = g - g1.astype(F32)
    g2 = r1.astype(BF16)
    g3 = (r1 - g2.astype(F32)).astype(BF16)
    acc = jnp.dot(tri_bf16, jnp.concatenate([g3, g2, g1], axis=1), preferred_element_type=F32)
    return (acc[:, :width] + acc[:, width:2 * width]) + acc[:, 2 * width:]


def _hgrn_head_safe(q, kf, v, b, st):
    rows = lax.broadcasted_iota(jnp.int32, (HG_SUB, 1), 0)
    outs = []
    prev_end = jnp.zeros((1, HG_DIM), F32)
    for i in range(HG_CHUNK // HG_SUB):
        sl = slice(i * HG_SUB, (i + 1) * HG_SUB)
        bl = b[sl] - prev_end
        qi, ki, vi = q[sl], kf[sl], v[sl]
        vif = vi.astype(F32)
        blast = bl[HG_SUB - 1:HG_SUB]
        o = _dot_nt((qi * jnp.exp(bl)).astype(BF16), st.astype(BF16))
        for s in range(HG_SUB):
            d = jnp.exp(jnp.minimum(bl - bl[s:s + 1], 0.0))
            a = jnp.sum(qi * ki[s:s + 1] * d, axis=-1, keepdims=True)
            o = o + jnp.where(rows >= s, a, 0.0) * vif[s:s + 1]
        kd = (ki * jnp.exp(blast - bl)).astype(BF16)
        st = st * jnp.exp(blast) + _dot_tn(vi, kd)
        prev_end = b[(i + 1) * HG_SUB - 1:(i + 1) * HG_SUB]
        outs.append(o)
    return jnp.concatenate(outs, axis=0), st


def _hgrn_kernel(q_ref, f_ref, i_ref, g_ref, lbl_ref, gn_ref, o_ref, st_ref, kf_ref, b_ref):
    @pl.when(pl.program_id(1) == 0)
    def _():
        st_ref[...] = jnp.zeros_like(st_ref)

    lg2 = lbl_ref[...].astype(F32)
    e = jnp.exp(lg2 - jnp.max(lg2, axis=0, keepdims=True))
    lb = e[0:1] / jnp.sum(e, axis=0, keepdims=True)
    gn = gn_ref[...].astype(F32)
    r = lax.broadcasted_iota(jnp.int32, (HG_CHUNK, HG_CHUNK), 0)
    c = lax.broadcasted_iota(jnp.int32, (HG_CHUNK, HG_CHUNK), 1)
    tri_mask = r >= c
    tri = jnp.where(tri_mask, 1.0, 0.0).astype(BF16)

    n_chunks = HG_TC // HG_CHUNK

    f = lb + (1.0 - lb) * _sigmoid(f_ref[0].astype(F32))
    kf_ref[...] = 1.0 - f
    logf = jnp.log(f)
    worst = None
    for ci in range(n_chunks):
        sl = slice(ci * HG_CHUNK, (ci + 1) * HG_CHUNK)
        b = _cumsum_rows(tri, logf[sl])
        b_ref[sl, :] = b
        bend = b[HG_CHUNK - 1:HG_CHUNK]
        worst = bend if worst is None else jnp.minimum(worst, bend)

    heads = range(HG_HEADS)
    head = lambda t, h: t[:, h * HG_DIM:(h + 1) * HG_DIM]

    def finish_all(sl, outs):
        scale = [lax.rsqrt(jnp.mean(o * o, axis=-1, keepdims=True) + RMS_EPS) for o in outs]
        normed = jnp.concatenate([outs[h] * scale[h] * gn for h in heads], axis=1)
        go = g_ref[0, sl, :].astype(F32)
        o_ref[0, sl, :] = (normed * (go * _sigmoid(go))).astype(o_ref.dtype)

    def fast():
        states = [st_ref[h] for h in heads]
        for ci in range(n_chunks):
            sl = slice(ci * HG_CHUNK, (ci + 1) * HG_CHUNK)
            b = b_ref[sl, :]
            qp = (q_ref[0, sl, :].astype(F32) * jnp.exp(b)).astype(BF16)
            kp = kf_ref[sl, :] * jnp.exp(-b)
            kpb = kp.astype(BF16)
            eb = jnp.exp(b[HG_CHUNK - 1:HG_CHUNK])
            kd = (kp * eb).astype(BF16)
            v = i_ref[0, sl, :]
            att = [_dot_nt(head(qp, h), head(kpb, h)) for h in heads]
            inter = [_dot_nt(head(qp, h), states[h].astype(BF16)) for h in heads]
            update = [_dot_tn(head(v, h), head(kd, h)) for h in heads]
            att = [jnp.where(tri_mask, a, 0.0).astype(BF16) for a in att]
            outs = [inter[h] + jnp.dot(att[h], head(v, h), preferred_element_type=F32) for h in heads]
            states = [states[h] * head(eb, h) + update[h] for h in heads]
            finish_all(sl, outs)
        for h in heads:
            st_ref[h] = states[h]

    def safe():
        def chunk(ci, carry):
            sl = pl.ds(pl.multiple_of(ci * HG_CHUNK, HG_CHUNK), HG_CHUNK)
            outs = []
            for h in heads:
                hs = slice(h * HG_DIM, (h + 1) * HG_DIM)
                o, st_new = _hgrn_head_safe(q_ref[0, sl, hs].astype(F32), kf_ref[sl, hs], i_ref[0, sl, hs],
                                            b_ref[sl, hs], st_ref[h])
                st_ref[h] = st_new
                outs.append(o)
            finish_all(sl, outs)
            return carry

        lax.fori_loop(0, n_chunks, chunk, 0)

    lax.cond(jnp.min(worst) >= -HG_SAFE_DECAY, fast, safe)


def _hgrn2(proj3, lb_logits, gn):
    bsz, s_len, _ = proj3.shape
    blk = lambda off: (lambda b, t: (b, t, off // HG_WIDTH))
    return pl.pallas_call(
        _hgrn_kernel,
        grid=(bsz, s_len // HG_TC),
        in_specs=[
            pl.BlockSpec((1, HG_TC, HG_WIDTH), blk(OFF_QH)),
            pl.BlockSpec((1, HG_TC, HG_WIDTH), blk(OFF_FH)),
            pl.BlockSpec((1, HG_TC, HG_WIDTH), blk(OFF_IH)),
            pl.BlockSpec((1, HG_TC, HG_WIDTH), blk(OFF_GH)),
            pl.BlockSpec((lb_logits.shape[0], HG_WIDTH), lambda b, t: (0, 0)),
            pl.BlockSpec((1, HG_DIM), lambda b, t: (0, 0)),
        ],
        out_specs=pl.BlockSpec((1, HG_TC, HG_WIDTH), lambda b, t: (b, t, 0)),
        out_shape=jax.ShapeDtypeStruct((bsz, s_len, HG_WIDTH), BF16),
        scratch_shapes=[
            pltpu.VMEM((HG_HEADS, HG_DIM, HG_DIM), F32),
            pltpu.VMEM((HG_TC, HG_WIDTH), F32),
            pltpu.VMEM((HG_TC, HG_WIDTH), F32),
        ],
        compiler_params=pltpu.CompilerParams(
            dimension_semantics=("parallel", "arbitrary"), vmem_limit_bytes=_vmem(48)),
        name="hgrn2",
    )(proj3, proj3, proj3, proj3, lb_logits, gn)


def _store_rows_as_tiles(ref, val):
    rows = val.shape[0]
    for s in range(ROW_TILE):
        ref[pl.ds(s, rows, stride=ROW_TILE), :] = val[:, s * LANES:(s + 1) * LANES]


def _load_rows_from_tiles(ref, start, rows):
    return jnp.concatenate(
        [ref[pl.ds(start * ROW_TILE + s, rows, stride=ROW_TILE), :] for s in range(ROW_TILE)], axis=1)


def _pack_bf16_pairs(x):
    c = x.shape[1] // 2
    bits = pltpu.bitcast(x.astype(BF16).astype(F32), U32)
    return jnp.bitwise_or(jnp.right_shift(bits[:, :c], jnp.uint32(16)), bits[:, c:])


def _unpack_bf16_pairs(p):
    lo = pltpu.bitcast(jnp.left_shift(p, jnp.uint32(16)), F32)
    hi = pltpu.bitcast(jnp.bitwise_and(p, jnp.uint32(0xFFFF0000)), F32)
    return lo, hi


def _mix_kernel(sink_ref, q_ref, kp_ref, kc_ref, vp_ref, vc_ref, bias_ref,
                r_ref, ga_ref, gh_ref, x_ref, wa_ref, wr_ref, wo_ref, g2_ref, wr2t_ref, brt_ref,
                x1_ref, h2p_ref, route_ref, cnt_ref, *, tiles_per_seq):
    i = pl.program_id(0)

    @pl.when(i == 0)
    def _():
        cnt_ref[...] = jnp.zeros_like(cnt_ref)

    r_tile = r_ref[...]
    n_blk = MIX_TM // ATTN_BLOCK
    n_piece = D_MODEL // MXU_WIDTH
    rm_pieces = []

    def rm_piece(c):
        def emit():
            rm_pieces.append(jnp.dot(r_tile, wr_ref[:, c * MXU_WIDTH:(c + 1) * MXU_WIDTH],
                                     preferred_element_type=F32))
        return emit

    k_rows = jnp.concatenate([kp_ref[...], kc_ref[...]], axis=0)
    v_rows = jnp.concatenate([vp_ref[...], vc_ref[...]], axis=0)
    first_in_seq = lax.rem(i, tiles_per_seq) == 0
    blocks = []
    for blk in range(n_blk):
        rows = slice(blk * ATTN_BLOCK, (blk + 1) * ATTN_BLOCK)
        window = slice(blk * ATTN_BLOCK, (blk + 2) * ATTN_BLOCK)
        has_prev = jnp.logical_not(first_in_seq) if blk == 0 else True
        side = [rm_piece(c) for c in range(blk * n_piece // n_blk, (blk + 1) * n_piece // n_blk)]
        blocks.append(_swa_block(q_ref[rows, :], k_rows[window], v_rows[window], has_prev, bias_ref, sink_ref, side))
    rm = jnp.concatenate(rm_pieces, axis=1)
    a_t = jnp.concatenate(blocks, axis=1).astype(BF16)
    am = _dot_tn(a_t, wa_ref[...])
    mixed = _sigmoid(ga_ref[...].astype(F32)) * am + _sigmoid(gh_ref[...].astype(F32)) * rm
    x1 = x_ref[...] + jnp.dot(mixed.astype(BF16), wo_ref[...], preferred_element_type=F32)
    x1_ref[...] = x1
    h2 = x1 * lax.rsqrt(jnp.mean(x1 * x1, axis=-1, keepdims=True) + RMS_EPS) * g2_ref[...]

    _store_rows_as_tiles(h2p_ref, _pack_bf16_pairs(h2))

    hi = h2.astype(BF16)
    lo = (h2 - hi.astype(F32)).astype(BF16)
    both = _dot_nt(wr2t_ref[...], hi)
    logits = both[:ROUTE_LANES] + both[ROUTE_LANES:] + _dot_nt(wr2t_ref[:ROUTE_LANES, :], lo)
    logits = logits[:ROUTE_ROWS] + brt_ref[:ROUTE_ROWS, :]

    tm = logits.shape[1]
    row = lax.broadcasted_iota(jnp.int32, logits.shape, 0)
    row_f = row.astype(F32)
    big = float(ROUTE_ROWS)
    gl = jnp.where(row < N_GROUPS, logits, NEG_INF)
    gmax = jnp.max(gl, axis=0, keepdims=True)
    g_idx = jnp.min(jnp.where(gl == gmax, row_f, big), axis=0, keepdims=True)
    g_w = 1.0 / jnp.sum(jnp.exp(gl - gmax), axis=0, keepdims=True)
    e_row = row - N_GROUPS
    in_group = jnp.logical_and(e_row >= 0, e_row < N_EXPERTS)
    row_group = jnp.right_shift(e_row, 3).astype(F32)
    in_group = jnp.logical_and(in_group, row_group == g_idx)
    el = jnp.where(in_group, logits, NEG_INF)
    m1 = jnp.max(el, axis=0, keepdims=True)
    i1 = jnp.min(jnp.where(el == m1, row_f, big), axis=0, keepdims=True)
    el2 = jnp.where(row_f == i1, NEG_INF, el)
    m2 = jnp.max(el2, axis=0, keepdims=True)
    i2 = jnp.min(jnp.where(el2 == m2, row_f, big), axis=0, keepdims=True)
    p2 = jnp.exp(m2 - m1)
    w1 = g_w / (1.0 + p2)
    w2 = g_w * p2 / (1.0 + p2)
    e1 = i1 - N_GROUPS
    e2 = i2 - N_GROUPS

    ss = lax.broadcasted_iota(jnp.int32, (tm, tm), 0)
    tt = lax.broadcasted_iota(jnp.int32, (tm, tm), 1)
    earlier = jnp.where(ss < tt, 1.0, 0.0).astype(BF16)
    sel1 = row_f == i1
    sel2 = row_f == i2
    oh1 = jnp.where(sel1, 1.0, 0.0)
    oh2 = jnp.where(sel2, 1.0, 0.0)
    base = cnt_ref[...]
    c1 = jnp.sum(oh1, axis=1, keepdims=True)
    c2 = jnp.sum(oh2, axis=1, keepdims=True)
    pre1 = jnp.dot(oh1.astype(BF16), earlier, preferred_element_type=F32) + base
    pre2 = jnp.dot(oh2.astype(BF16), earlier, preferred_element_type=F32) + (base + c1)
    rank1 = jnp.sum(jnp.where(sel1, pre1, 0.0), axis=0, keepdims=True)
    rank2 = jnp.sum(jnp.where(sel2, pre2, 0.0), axis=0, keepdims=True)
    cnt_ref[...] = base + c1 + c2

    vals = jnp.concatenate([w1, w2, e1, e2, rank1, rank2, jnp.zeros((ROUTE_VALS - 6, tm), F32)], axis=0)
    v1 = vals.astype(BF16)
    rest = vals - v1.astype(F32)
    v2 = rest.astype(BF16)
    v3 = (rest - v2.astype(F32)).astype(BF16)
    kk = lax.broadcasted_iota(jnp.int32, (ROUTE_VALS, ROUTE_LANES), 0)
    ll = lax.broadcasted_iota(jnp.int32, (ROUTE_VALS, ROUTE_LANES), 1)
    pick = jnp.where(kk == ll, 1.0, 0.0).astype(BF16)
    route_ref[...] = (_dot_tn(v3, pick) + _dot_tn(v2, pick)) + _dot_tn(v1, pick)


def _mix(sinks, bias, r2d, proj, x2d, wa, wr, wo, g2, wr2t, brt, seq_len):
    n = x2d.shape[0]
    row = lambda i: (i, 0)
    const = lambda i: (0, 0)
    blocks_per_tile = MIX_TM // ATTN_BLOCK
    prev_block = lambda i: jnp.maximum(blocks_per_tile * i - 1, 0)
    return pl.pallas_call(
        functools.partial(_mix_kernel, tiles_per_seq=seq_len // MIX_TM),
        grid=(n // MIX_TM,),
        in_specs=[
            pl.BlockSpec(memory_space=pltpu.SMEM),
            pl.BlockSpec((MIX_TM, ATTN_WIDTH), lambda i: (i, OFF_QA // ATTN_WIDTH)),
            pl.BlockSpec((ATTN_BLOCK, KV_WIDTH), lambda i: (prev_block(i), OFF_KA // KV_WIDTH)),
            pl.BlockSpec((MIX_TM, KV_WIDTH), lambda i: (i, OFF_KA // KV_WIDTH)),
            pl.BlockSpec((ATTN_BLOCK, KV_WIDTH), lambda i: (prev_block(i), OFF_VA // KV_WIDTH)),
            pl.BlockSpec((MIX_TM, KV_WIDTH), lambda i: (i, OFF_VA // KV_WIDTH)),
            pl.BlockSpec((N_Q_HEADS, ATTN_BLOCK, ATTN_BLOCK), lambda i: (0, 0, 0)),
            pl.BlockSpec((MIX_TM, HG_WIDTH), row),
            pl.BlockSpec((MIX_TM, D_MODEL), lambda i: (i, OFF_GATE_A // D_MODEL)),
            pl.BlockSpec((MIX_TM, D_MODEL), lambda i: (i, OFF_GATE_H // D_MODEL)),
            pl.BlockSpec((MIX_TM, D_MODEL), row),
            pl.BlockSpec((ATTN_WIDTH, D_MODEL), const, pipeline_mode=pl.Buffered(1)),
            pl.BlockSpec((HG_WIDTH, D_MODEL), const, pipeline_mode=pl.Buffered(1)),
            pl.BlockSpec((D_MODEL, D_MODEL), const, pipeline_mode=pl.Buffered(1)),
            pl.BlockSpec((1, D_MODEL), const),
            pl.BlockSpec((2 * ROUTE_LANES, D_MODEL), const, pipeline_mode=pl.Buffered(1)),
            pl.BlockSpec((ROUTE_LANES, 1), const),
        ],
        out_specs=[
            pl.BlockSpec((MIX_TM, D_MODEL), row),
            pl.BlockSpec((MIX_TM * ROW_TILE, LANES), row),
            pl.BlockSpec((MIX_TM, ROUTE_LANES), row),
        ],
        out_shape=[
            jax.ShapeDtypeStruct((n, D_MODEL), F32),
            jax.ShapeDtypeStruct((n * ROW_TILE, LANES), U32),
            jax.ShapeDtypeStruct((n, ROUTE_LANES), F32),
        ],
        scratch_shapes=[pltpu.VMEM((ROUTE_ROWS, 1), F32)],
        compiler_params=pltpu.CompilerParams(dimension_semantics=("arbitrary",), vmem_limit_bytes=_vmem(60)),
        name="mix",
    )(sinks, proj, proj, proj, proj, proj, bias, r2d, proj, proj, x2d, wa, wr, wo, g2, wr2t, brt)


def _dispatch_kernel(pad_start_ref, pad_count_ref, tail_ref, dest_ref, src_ref, dst_hbm, zero_ref, sem):
    @pl.when(pl.program_id(0) == 0)
    def _():
        zero_ref[...] = jnp.zeros_like(zero_ref)
        for phase in ("start", "wait"):
            for e in range(N_EXPERTS):
                count, start = pad_count_ref[e], pad_start_ref[e]
                size = MOE_TM // 2
                while size >= 1:
                    done = jnp.bitwise_and(count, -2 * size)
                    copy = pltpu.make_async_copy(zero_ref.at[pl.ds(0, size)],
                                                 dst_hbm.at[pl.ds(start + done, size)], sem.at[1])
                    pl.when(jnp.bitwise_and(count, size) != 0)(getattr(copy, phase))
                    size //= 2
            for j in range(N_EXPERTS):
                copy = pltpu.make_async_copy(zero_ref, dst_hbm.at[pl.ds(tail_ref[0] + j * MOE_TM, MOE_TM)],
                                             sem.at[1])
                pl.when(j < tail_ref[1])(getattr(copy, phase))

    for a in range(DISPATCH_ROWS):
        pltpu.make_async_copy(src_ref.at[a // TOP_K], dst_hbm.at[dest_ref[0, 0, a]], sem.at[0]).start(priority=a % 2)
    for _ in range(TOP_K):
        pltpu.make_async_copy(src_ref, dst_hbm.at[pl.ds(0, DISPATCH_ROWS // TOP_K)], sem.at[0]).wait()


def _dispatch(dest_flat, pad_start, pad_count, tail, h2p_tiles, n_rows):
    n_asg = dest_flat.shape[0]
    steps = n_asg // DISPATCH_ROWS
    smem = pl.BlockSpec(memory_space=pltpu.SMEM)
    return pl.pallas_call(
        _dispatch_kernel,
        grid=(steps,),
        in_specs=[
            smem, smem, smem,
            pl.BlockSpec((1, 1, DISPATCH_ROWS), lambda s: (s, 0, 0), memory_space=pltpu.SMEM),
            pl.BlockSpec((DISPATCH_ROWS // TOP_K, ROW_TILE, LANES), lambda s: (s, 0, 0)),
        ],
        out_specs=pl.BlockSpec(memory_space=pl.ANY),
        out_shape=jax.ShapeDtypeStruct((n_rows, ROW_TILE, LANES), U32),
        scratch_shapes=[pltpu.VMEM((MOE_TM, ROW_TILE, LANES), U32), pltpu.SemaphoreType.DMA((2,))],
        compiler_params=pltpu.CompilerParams(dimension_semantics=("arbitrary",)),
        name="dispatch",
    )(pad_start, pad_count, tail, dest_flat.reshape(steps, 1, DISPATCH_ROWS), h2p_tiles)


def _moe_kernel(be_ref, nused_ref, first_ref, slot_ref, next_ref, next2_ref, xs_ref, wg_hbm, wu_hbm, wd_hbm, y_ref,
                wg_buf, wu_buf, wd_buf, wg_s, wu_s, wd_s, sem):
    b = pl.program_id(0)
    live = b < nused_ref[0]

    def weight_copies(e, s):
        return (pltpu.make_async_copy(wg_hbm.at[0, e], wg_buf.at[s], sem.at[s, 0]),
                pltpu.make_async_copy(wu_hbm.at[0, e], wu_buf.at[s], sem.at[s, 1]),
                pltpu.make_async_copy(wd_hbm.at[0, e], wd_buf.at[s], sem.at[s, 2]))

    @pl.when(b == 0)
    def _():
        for c in weight_copies(be_ref[0], 0):
            c.start()

        @pl.when(next_ref[0] >= 0)
        def _():
            for c in weight_copies(next_ref[0], 1):
                c.start()

    for s in range(MOE_WEIGHT_BUFFERS):
        @pl.when(jnp.logical_and(jnp.logical_and(live, first_ref[b] == 1), slot_ref[b] == s))
        def _():
            for c in weight_copies(be_ref[b], s):
                c.wait()

            @pl.when(next2_ref[b] >= 0)
            def _():
                for c in weight_copies(next2_ref[b], (s + 2) % MOE_WEIGHT_BUFFERS):
                    c.start()

            wg_s[...] = wg_buf[s].astype(BF16)
            wu_s[...] = wu_buf[s].astype(BF16)
            wd_s[...] = wd_buf[s].astype(BF16)

    @pl.when(jnp.logical_not(live))
    def _():
        y_ref[...] = jnp.zeros_like(y_ref)

    @pl.when(live)
    def _():
        xlo, xhi = _unpack_bf16_pairs(_load_rows_from_tiles(xs_ref, 0, MOE_TM))
        xb = jnp.concatenate([xlo.astype(BF16), xhi.astype(BF16)], axis=1)
        hb = []
        for c in range(EXPERT_FF // MXU_WIDTH):
            cols = slice(c * MXU_WIDTH, (c + 1) * MXU_WIDTH)
            hg = jnp.dot(xb, wg_s[:, cols], preferred_element_type=F32)
            hu = jnp.dot(xb, wu_s[:, cols], preferred_element_type=F32)
            hb.append((hg * _sigmoid(hg) * hu).astype(BF16))
        hb = jnp.concatenate(hb, axis=1)
        half = D_MODEL // 2
        for c in range(half // MXU_WIDTH):
            lo = jnp.dot(hb, wd_s[:, c * MXU_WIDTH:(c + 1) * MXU_WIDTH], preferred_element_type=F32)
            hi = jnp.dot(hb, wd_s[:, half + c * MXU_WIDTH:half + (c + 1) * MXU_WIDTH], preferred_element_type=F32)
            packed = _pack_bf16_pairs(jnp.concatenate([lo, hi], axis=1))
            for k in range(MXU_WIDTH // LANES):
                s = c * (MXU_WIDTH // LANES) + k
                y_ref[pl.ds(s, MOE_TM, stride=ROW_TILE), :] = packed[:, k * LANES:(k + 1) * LANES]


def _moe(block_e, n_used, first, slot, next_e, next2_e, xs, w_gate, w_up, w_down):
    n_rows = xs.shape[0] // ROW_TILE
    n_blocks = n_rows // MOE_TM
    grid_spec = pltpu.PrefetchScalarGridSpec(
        num_scalar_prefetch=6,
        grid=(n_blocks,),
        in_specs=[
            pl.BlockSpec((MOE_TM * ROW_TILE, LANES), lambda b, be, nu, *_: (jnp.minimum(b, nu[0] - 1), 0)),
            pl.BlockSpec(memory_space=pl.ANY),
            pl.BlockSpec(memory_space=pl.ANY),
            pl.BlockSpec(memory_space=pl.ANY),
        ],
        out_specs=pl.BlockSpec((MOE_TM * ROW_TILE, LANES), lambda b, *_: (b, 0)),
        scratch_shapes=[
            pltpu.VMEM((MOE_WEIGHT_BUFFERS, D_MODEL, EXPERT_FF), F32),
            pltpu.VMEM((MOE_WEIGHT_BUFFERS, D_MODEL, EXPERT_FF), F32),
            pltpu.VMEM((MOE_WEIGHT_BUFFERS, EXPERT_FF, D_MODEL), F32),
            pltpu.VMEM((D_MODEL, EXPERT_FF), BF16),
            pltpu.VMEM((D_MODEL, EXPERT_FF), BF16),
            pltpu.VMEM((EXPERT_FF, D_MODEL), BF16),
            pltpu.SemaphoreType.DMA((MOE_WEIGHT_BUFFERS, 3)),
        ],
    )
    return pl.pallas_call(
        _moe_kernel,
        grid_spec=grid_spec,
        out_shape=jax.ShapeDtypeStruct((n_rows * ROW_TILE, LANES), U32),
        compiler_params=pltpu.CompilerParams(dimension_semantics=("arbitrary",), vmem_limit_bytes=_vmem(52)),
        name="moe",
    )(block_e, n_used, first, slot, next_e, next2_e, xs, w_gate, w_up, w_down)


def _moe_layout(route):
    e_id = route[:, 2:2 + TOP_K].astype(jnp.int32)
    rank = route[:, 2 + TOP_K:2 + 2 * TOP_K].astype(jnp.int32)
    onehot_e = e_id[:, :, None] == jnp.arange(N_EXPERTS, dtype=jnp.int32)[None, None, :]
    counts = jnp.sum(onehot_e.astype(jnp.int32), axis=(0, 1))
    padded = (counts + MOE_TM - 1) // MOE_TM * MOE_TM
    pends = jnp.cumsum(padded)
    pstarts = pends - padded
    n_tok = route.shape[0]
    n_blocks = n_tok * TOP_K // MOE_TM + N_EXPERTS
    blk_start = jnp.arange(n_blocks, dtype=jnp.int32) * MOE_TM
    block_e = jnp.minimum(jnp.sum((blk_start[:, None] >= pends[None, :]).astype(jnp.int32), axis=1), N_EXPERTS - 1)
    n_used = (pends[-1] // MOE_TM).astype(jnp.int32).reshape(1)
    blk = jnp.arange(n_blocks, dtype=jnp.int32)
    first = jnp.logical_and(blk < n_used[0], jnp.logical_or(blk == 0, block_e != jnp.roll(block_e, 1)))
    slot = lax.rem(jnp.cumsum(first.astype(jnp.int32)) - 1, MOE_WEIGHT_BUFFERS)
    ex = jnp.arange(N_EXPERTS, dtype=jnp.int32)
    later = jnp.logical_and(counts[None, :] > 0, ex[None, :] > ex[:, None])
    next_of_expert = jnp.min(jnp.where(later, ex[None, :], N_EXPERTS), axis=1)
    next_of_expert = jnp.where(next_of_expert == N_EXPERTS, -1, next_of_expert)
    lookup = lambda keys, table: jnp.sum(jnp.where(keys[:, None] == ex[None, :], table[None, :] + 1, 0), axis=1) - 1
    next2_of_expert = lookup(next_of_expert, next_of_expert)
    next_e = lookup(block_e, next_of_expert)
    next2_e = lookup(block_e, next2_of_expert)
    dest = jnp.sum(jnp.where(onehot_e, pstarts[None, None, :], 0), axis=2) + rank
    pads = ((pstarts + counts).astype(jnp.int32), (padded - counts).astype(jnp.int32),
            jnp.stack([pends[-1], n_blocks - n_used[0]]).astype(jnp.int32))
    tables = (block_e.astype(jnp.int32), n_used, first.astype(jnp.int32), slot.astype(jnp.int32),
              next_e.astype(jnp.int32), next2_e.astype(jnp.int32))
    return tables, dest, pads, n_blocks * MOE_TM


def _final_kernel(dest_ref, destn_ref, x1_ref, route_ref, g_ref, y_hbm, y_flat_hbm, o_ref, ybuf, sem):
    i = pl.program_id(0)
    n_steps = pl.num_programs(0)
    slot = lax.rem(i, 2)
    n_rows = TOP_K * FIN_TM

    def start_gather(idx_ref, s):
        for r in range(n_rows):
            pltpu.make_async_copy(y_hbm.at[idx_ref[0, 0, r]], ybuf.at[pl.ds((s * n_rows + r) * ROW_TILE, ROW_TILE)],
                                  sem.at[s]).start(priority=r % 2)

    @pl.when(i == 0)
    def _():
        start_gather(dest_ref, 0)

    for s in range(2):
        @pl.when(jnp.logical_and(i + 1 < n_steps, 1 - slot == s))
        def _():
            start_gather(destn_ref, s)

    slot_rows = n_rows * ROW_TILE
    slot_start = pl.multiple_of(slot * slot_rows, slot_rows)
    pltpu.make_async_copy(y_flat_hbm.at[pl.ds(0, slot_rows)], ybuf.at[pl.ds(slot_start, slot_rows)],
                          sem.at[slot]).wait()
    route = route_ref[...]
    x = x1_ref[...]
    for k in range(TOP_K):
        ylo, yhi = _unpack_bf16_pairs(_load_rows_from_tiles(ybuf, slot * n_rows + k * FIN_TM, FIN_TM))
        x = x + route[:, k:k + 1] * jnp.concatenate([ylo, yhi], axis=1)
    o_ref[...] = x * lax.rsqrt(jnp.mean(x * x, axis=-1, keepdims=True) + RMS_EPS) * g_ref[...]


def _final(dest, x1, route, g, y_sorted):
    n = x1.shape[0]
    steps = n // FIN_TM
    dest3 = dest.reshape(steps, FIN_TM, TOP_K).transpose(0, 2, 1).reshape(steps, 1, TOP_K * FIN_TM)
    return pl.pallas_call(
        _final_kernel,
        grid=(steps,),
        in_specs=[
            pl.BlockSpec((1, 1, TOP_K * FIN_TM), lambda i: (i, 0, 0), memory_space=pltpu.SMEM),
            pl.BlockSpec((1, 1, TOP_K * FIN_TM), lambda i: (jnp.minimum(i + 1, steps - 1), 0, 0),
                         memory_space=pltpu.SMEM),
            pl.BlockSpec((FIN_TM, D_MODEL), lambda i: (i, 0)),
            pl.BlockSpec((FIN_TM, ROUTE_LANES), lambda i: (i, 0)),
            pl.BlockSpec((1, D_MODEL), lambda i: (0, 0)),
            pl.BlockSpec(memory_space=pl.ANY),
            pl.BlockSpec(memory_space=pl.ANY),
        ],
        out_specs=pl.BlockSpec((FIN_TM, D_MODEL), lambda i: (i, 0)),
        out_shape=jax.ShapeDtypeStruct((n, D_MODEL), F32),
        scratch_shapes=[
            pltpu.VMEM((2 * TOP_K * FIN_TM * ROW_TILE, LANES), U32),
            pltpu.SemaphoreType.DMA((2,)),
        ],
        compiler_params=pltpu.CompilerParams(dimension_semantics=("arbitrary",), vmem_limit_bytes=_vmem(40)),
        name="final",
    )(dest3, dest3, x1, route, g, y_sorted.reshape(-1, ROW_TILE, LANES), y_sorted)


def kernel(x, norm1_g, w_in, attn_sinks, rel_bias, hg_lb_logits, hg_norm_g, w_attn_branch, w_hg_branch, w_out,
           norm2_g, w_group_router, b_group_router, w_expert_router, b_expert_router, w_gate, w_up, w_down, final_g):
    bsz, s_len, d = x.shape
    n_tok = bsz * s_len
    x2d = x.reshape(n_tok, d)

    w_route = jnp.concatenate([w_group_router[0], w_expert_router[0]], axis=1).astype(F32)
    w_route = jnp.pad(w_route, ((0, 0), (0, ROUTE_LANES - w_route.shape[1])))
    whi = w_route.astype(BF16)
    wlo = (w_route - whi.astype(F32)).astype(BF16)
    wr2t = jnp.concatenate([whi.T, wlo.T], axis=0)
    b_route = jnp.concatenate([b_group_router[0], b_expert_router[0]]).astype(F32)
    b_route = jnp.pad(b_route, (0, ROUTE_LANES - b_route.shape[0])).reshape(ROUTE_LANES, 1)
    bias = _attn_bias_table(rel_bias)

    proj = _in_proj(x2d, norm1_g[0].reshape(1, d).astype(F32), w_in)
    proj3 = proj.reshape(bsz, s_len, IN_WIDTH)
    r = _hgrn2(proj3, hg_lb_logits.astype(F32), hg_norm_g[0].reshape(1, HG_DIM).astype(F32))
    x1, h2p, route = _mix(
        attn_sinks[0].astype(F32), bias, r.reshape(n_tok, HG_WIDTH), proj, x2d,
        w_attn_branch[0].astype(BF16), w_hg_branch[0].astype(BF16), w_out[0].astype(BF16),
        norm2_g[0].reshape(1, d).astype(F32), wr2t, b_route, s_len)
    tables, dest, pads, n_rows = _moe_layout(route)
    xs = _dispatch(dest.reshape(n_tok * TOP_K), *pads, h2p.reshape(n_tok, ROW_TILE, LANES), n_rows)
    y_sorted = _moe(*tables, xs.reshape(n_rows * ROW_TILE, LANES), w_gate, w_up, w_down)
    out = _final(dest, x1, route, final_g.reshape(1, d).astype(F32), y_sorted)
    return out.reshape(bsz, s_len, d)
```

```python
import functools
import math

import jax
import jax.numpy as jnp
from jax import lax
from jax.experimental import pallas as pl
from jax.experimental.pallas import tpu as pltpu

D_MODEL = 2048
N_Q_HEADS = 16
N_KV_HEADS = 4
HEAD_DIM = 64
WINDOW = 128
ATTN_BLOCK = 128
ATTN_WIDTH = N_Q_HEADS * HEAD_DIM
KV_WIDTH = N_KV_HEADS * HEAD_DIM
REL_BUCKETS = 32
REL_MAX_DIST = 128
HG_HEADS = 8
HG_DIM = 128
HG_WIDTH = HG_HEADS * HG_DIM
HG_CHUNK = 64
HG_SUB = 16
N_GROUPS = 4
EXPERTS_PER_GROUP = 8
N_EXPERTS = N_GROUPS * EXPERTS_PER_GROUP
TOP_K = 2
EXPERT_FF = 512
RMS_EPS = 1e-6

OFF_GATE_A = 0
OFF_GATE_H = D_MODEL
OFF_QA = 2 * D_MODEL
OFF_QH = OFF_QA + ATTN_WIDTH
OFF_FH = OFF_QH + HG_WIDTH
OFF_IH = OFF_FH + HG_WIDTH
OFF_GH = OFF_IH + HG_WIDTH
OFF_KA = OFF_GH + HG_WIDTH
OFF_VA = OFF_KA + KV_WIDTH
IN_WIDTH = OFF_VA + KV_WIDTH
REF_OFF_QA = 0
REF_OFF_KA = ATTN_WIDTH
REF_OFF_QH = ATTN_WIDTH + 2 * KV_WIDTH
REF_OFF_GATE_A = REF_OFF_QH + 4 * HG_WIDTH

HG_SAFE_DECAY = 60.0

ROW_TILE = 8
LANES = 128
assert D_MODEL // 2 == ROW_TILE * LANES
MXU_WIDTH = 256

IN_TM = 2048
IN_TN = 512
HG_TC = 1024
MIX_TM = 512
ROUTE_LANES = 128
ROUTE_ROWS = 40
ROUTE_VALS = 16
assert N_GROUPS + N_EXPERTS <= ROUTE_ROWS <= ROUTE_LANES
DISPATCH_ROWS = 2048
MOE_TM = 256
MOE_WEIGHT_BUFFERS = 3
FIN_TM = 256

F32 = jnp.float32
BF16 = jnp.bfloat16
U32 = jnp.uint32
NEG_INF = float("-inf")


def _vmem(mib):
    return mib * 1024 * 1024


def _sigmoid(x):
    return 1.0 / (1.0 + jnp.exp(-x))


def _dot_nt(a, b):
    return lax.dot_general(a, b, (((1,), (1,)), ((), ())), preferred_element_type=F32)


def _dot_tn(a, b):
    return lax.dot_general(a, b, (((0,), (0,)), ((), ())), preferred_element_type=F32)


def _in_proj_kernel(x_ref, g_ref, w_ref, o_ref, h_ref):
    @pl.when(pl.program_id(1) == 0)
    def _():
        x = x_ref[...]
        ms = jnp.mean(x * x, axis=-1, keepdims=True)
        h_ref[...] = (x * lax.rsqrt(ms + RMS_EPS) * g_ref[...]).astype(BF16)

    o_ref[...] = jnp.dot(h_ref[...], w_ref[0].astype(BF16), preferred_element_type=F32).astype(o_ref.dtype)


def _in_proj(x2d, g, w_in):
    n = x2d.shape[0]
    n_col = IN_WIDTH // IN_TN

    def out_block(j):
        gates = j - REF_OFF_GATE_A // IN_TN + OFF_GATE_A // IN_TN
        q_a = j - REF_OFF_QA // IN_TN + OFF_QA // IN_TN
        kv = j - REF_OFF_KA // IN_TN + OFF_KA // IN_TN
        hg = j - REF_OFF_QH // IN_TN + OFF_QH // IN_TN
        return jnp.where(j >= REF_OFF_GATE_A // IN_TN, gates,
                         jnp.where(j >= REF_OFF_QH // IN_TN, hg, jnp.where(j >= REF_OFF_KA // IN_TN, kv, q_a)))

    return pl.pallas_call(
        _in_proj_kernel,
        grid=(n // IN_TM, n_col),
        in_specs=[
            pl.BlockSpec((IN_TM, D_MODEL), lambda i, j: (i, 0)),
            pl.BlockSpec((1, D_MODEL), lambda i, j: (0, 0)),
            pl.BlockSpec((1, D_MODEL, IN_TN), lambda i, j: (0, 0, j)),
        ],
        out_specs=pl.BlockSpec((IN_TM, IN_TN), lambda i, j: (i, out_block(j))),
        out_shape=jax.ShapeDtypeStruct((n, IN_WIDTH), BF16),
        scratch_shapes=[pltpu.VMEM((IN_TM, D_MODEL), BF16)],
        compiler_params=pltpu.CompilerParams(
            dimension_semantics=("parallel", "arbitrary"), vmem_limit_bytes=_vmem(58)),
        name="in_proj",
    )(x2d, g, w_in)


def _swa_block(q, kk, vv, has_prev, bias_ref, sink_ref, side_work):
    q = q * jnp.asarray(HEAD_DIM ** -0.5, BF16)
    key = lax.broadcasted_iota(jnp.int32, (ATTN_BLOCK, ATTN_BLOCK), 0)
    qry = lax.broadcasted_iota(jnp.int32, (ATTN_BLOCK, ATTN_BLOCK), 1)
    cur = key <= qry
    key_ok = jnp.logical_or(cur, has_prev)
    grp = N_Q_HEADS // N_KV_HEADS
    head = lambda t, h: t[:, h * HEAD_DIM:(h + 1) * HEAD_DIM]
    lanes = lambda t, g: t[:, g * ATTN_BLOCK:(g + 1) * ATTN_BLOCK]
    scores = []
    for hk in range(N_KV_HEADS):
        q_grp = jnp.concatenate([head(q, hk * grp + g) for g in range(grp)], axis=0)
        scores.append(_dot_nt(head(kk, hk), q_grp))
    emitted = 0
    probs, denoms = [], []
    for hq in range(N_Q_HEADS):
        while emitted * N_Q_HEADS < hq * len(side_work) + len(side_work):
            side_work[emitted]()
            emitted += 1
        s2 = lanes(scores[hq // grp], hq % grp)
        s = jnp.where(cur, s2[ATTN_BLOCK:], s2[:ATTN_BLOCK]) + bias_ref[hq]
        s = jnp.where(key_ok, s, NEG_INF)
        sink = sink_ref[hq]
        m = jnp.maximum(jnp.max(s, axis=0, keepdims=True), sink)
        p = jnp.exp(s - m)
        denoms.append(jnp.sum(p, axis=0, keepdims=True) + jnp.exp(sink - m))
        probs.append(jnp.concatenate([jnp.where(cur, 0.0, p), jnp.where(cur, p, 0.0)], axis=0).astype(BF16))
    outs = []
    for hk in range(N_KV_HEADS):
        p_grp = jnp.concatenate(probs[hk * grp:(hk + 1) * grp], axis=1)
        o_grp = _dot_tn(head(vv, hk), p_grp)
        outs += [lanes(o_grp, g) / denoms[hk * grp + g] for g in range(grp)]
    return jnp.concatenate(outs, axis=0)


def _t5_causal_bucket(n):
    max_exact = REL_BUCKETS // 2
    nf = jnp.maximum(n, 1).astype(F32)
    large = max_exact + (jnp.log(nf / max_exact) / math.log(REL_MAX_DIST / max_exact)
                         * (REL_BUCKETS - max_exact)).astype(jnp.int32)
    large = jnp.minimum(large, REL_BUCKETS - 1)
    return jnp.where(n < max_exact, n, large)


def _attn_bias_table(rel_bias):
    assert WINDOW == ATTN_BLOCK
    r = jnp.arange(ATTN_BLOCK)[None, :]
    c = jnp.arange(ATTN_BLOCK)[:, None]
    dist = jnp.where(c <= r, r - c, r + ATTN_BLOCK - c)
    bucket = _t5_causal_bucket(dist)
    onehot = (bucket[None, :, :] == jnp.arange(REL_BUCKETS)[:, None, None]).astype(F32)
    return jnp.einsum("bh,bqk->hqk", rel_bias.astype(F32), onehot, precision=lax.Precision.HIGHEST)


def _cumsum_rows(tri_bf16, g):
    width = g.shape[1]
    g1 = g.astype(BF16)
    r1 = g - g1.astype(F32)
    g2 = r1.astype(BF16)
    g3 = (r1 - g2.astype(F32)).astype(BF16)
    acc = jnp.dot(tri_bf16, jnp.concatenate([g3, g2, g1], axis=1), preferred_element_type=F32)
    return (acc[:, :width] + acc[:, width:2 * width]) + acc[:, 2 * width:]


def _hgrn_head_safe(q, kf, v, b, st):
    rows = lax.broadcasted_iota(jnp.int32, (HG_SUB, 1), 0)
    outs = []
    prev_end = jnp.zeros((1, HG_DIM), F32)
    for i in range(HG_CHUNK // HG_SUB):
        sl = slice(i * HG_SUB, (i + 1) * HG_SUB)
        bl = b[sl] - prev_end
        qi, ki, vi = q[sl], kf[sl], v[sl]
        vif = vi.astype(F32)
        blast = bl[HG_SUB - 1:HG_SUB]
        o = _dot_nt((qi * jnp.exp(bl)).astype(BF16), st.astype(BF16))
        for s in range(HG_SUB):
            d = jnp.exp(jnp.minimum(bl - bl[s:s + 1], 0.0))
            a = jnp.sum(qi * ki[s:s + 1] * d, axis=-1, keepdims=True)
            o = o + jnp.where(rows >= s, a, 0.0) * vif[s:s + 1]
        kd = (ki * jnp.exp(blast - bl)).astype(BF16)
        st = st * jnp.exp(blast) + _dot_tn(vi, kd)
        prev_end = b[(i + 1) * HG_SUB - 1:(i + 1) * HG_SUB]
        outs.append(o)
    return jnp.concatenate(outs, axis=0), st


def _hgrn_kernel(q_ref, f_ref, i_ref, g_ref, lbl_ref, gn_ref, o_ref, st_ref, kf_ref, b_ref):
    @pl.when(pl.program_id(1) == 0)
    def _():
        st_ref[...] = jnp.zeros_like(st_ref)

    lg2 = lbl_ref[...].astype(F32)
    e = jnp.exp(lg2 - jnp.max(lg2, axis=0, keepdims=True))
    lb = e[0:1] / jnp.sum(e, axis=0, keepdims=True)
    gn = gn_ref[...].astype(F32)
    r = lax.broadcasted_iota(jnp.int32, (HG_CHUNK, HG_CHUNK), 0)
    c = lax.broadcasted_iota(jnp.int32, (HG_CHUNK, HG_CHUNK), 1)
    tri_mask = r >= c
    tri = jnp.where(tri_mask, 1.0, 0.0).astype(BF16)

    n_chunks = HG_TC // HG_CHUNK

    f = lb + (1.0 - lb) * _sigmoid(f_ref[0].astype(F32))
    kf_ref[...] = 1.0 - f
    logf = jnp.log(f)
    worst = None
    for ci in range(n_chunks):
        sl = slice(ci * HG_CHUNK, (ci + 1) * HG_CHUNK)
        b = _cumsum_rows(tri, logf[sl])
        b_ref[sl, :] = b
        bend = b[HG_CHUNK - 1:HG_CHUNK]
        worst = bend if worst is None else jnp.minimum(worst, bend)

    heads = range(HG_HEADS)
    head = lambda t, h: t[:, h * HG_DIM:(h + 1) * HG_DIM]

    def finish_all(sl, outs):
        scale = [lax.rsqrt(jnp.mean(o * o, axis=-1, keepdims=True) + RMS_EPS) for o in outs]
        normed = jnp.concatenate([outs[h] * scale[h] * gn for h in heads], axis=1)
        go = g_ref[0, sl, :].astype(F32)
        o_ref[0, sl, :] = (normed * (go * _sigmoid(go))).astype(o_ref.dtype)

    def fast():
        states = [st_ref[h] for h in heads]
        for ci in range(n_chunks):
            sl = slice(ci * HG_CHUNK, (ci + 1) * HG_CHUNK)
            b = b_ref[sl, :]
            qp = (q_ref[0, sl, :].astype(F32) * jnp.exp(b)).astype(BF16)
            kp = kf_ref[sl, :] * jnp.exp(-b)
            kpb = kp.astype(BF16)
            eb = jnp.exp(b[HG_CHUNK - 1:HG_CHUNK])
            kd = (kp * eb).astype(BF16)
            v = i_ref[0, sl, :]
            att = [_dot_nt(head(qp, h), head(kpb, h)) for h in heads]
            inter = [_dot_nt(head(qp, h), states[h].astype(BF16)) for h in heads]
            update = [_dot_tn(head(v, h), head(kd, h)) for h in heads]
            att = [jnp.where(tri_mask, a, 0.0).astype(BF16) for a in att]
            outs = [inter[h] + jnp.dot(att[h], head(v, h), preferred_element_type=F32) for h in heads]
            states = [states[h] * head(eb, h) + update[h] for h in heads]
            finish_all(sl, outs)
        for h in heads:
            st_ref[h] = states[h]

    def safe():
        def chunk(ci, carry):
            sl = pl.ds(pl.multiple_of(ci * HG_CHUNK, HG_CHUNK), HG_CHUNK)
            outs = []
            for h in heads:
                hs = slice(h * HG_DIM, (h + 1) * HG_DIM)
                o, st_new = _hgrn_head_safe(q_ref[0, sl, hs].astype(F32), kf_ref[sl, hs], i_ref[0, sl, hs],
                                            b_ref[sl, hs], st_ref[h])
                st_ref[h] = st_new
                outs.append(o)
            finish_all(sl, outs)
            return carry

        lax.fori_loop(0, n_chunks, chunk, 0)

    lax.cond(jnp.min(worst) >= -HG_SAFE_DECAY, fast, safe)


def _hgrn2(proj3, lb_logits, gn):
    bsz, s_len, _ = proj3.shape
    blk = lambda off: (lambda b, t: (b, t, off // HG_WIDTH))
    return pl.pallas_call(
        _hgrn_kernel,
        grid=(bsz, s_len // HG_TC),
        in_specs=[
            pl.BlockSpec((1, HG_TC, HG_WIDTH), blk(OFF_QH)),
            pl.BlockSpec((1, HG_TC, HG_WIDTH), blk(OFF_FH)),
            pl.BlockSpec((1, HG_TC, HG_WIDTH), blk(OFF_IH)),
            pl.BlockSpec((1, HG_TC, HG_WIDTH), blk(OFF_GH)),
            pl.BlockSpec((lb_logits.shape[0], HG_WIDTH), lambda b, t: (0, 0)),
            pl.BlockSpec((1, HG_DIM), lambda b, t: (0, 0)),
        ],
        out_specs=pl.BlockSpec((1, HG_TC, HG_WIDTH), lambda b, t: (b, t, 0)),
        out_shape=jax.ShapeDtypeStruct((bsz, s_len, HG_WIDTH), BF16),
        scratch_shapes=[
            pltpu.VMEM((HG_HEADS, HG_DIM, HG_DIM), F32),
            pltpu.VMEM((HG_TC, HG_WIDTH), F32),
            pltpu.VMEM((HG_TC, HG_WIDTH), F32),
        ],
        compiler_params=pltpu.CompilerParams(
            dimension_semantics=("parallel", "arbitrary"), vmem_limit_bytes=_vmem(48)),
        name="hgrn2",
    )(proj3, proj3, proj3, proj3, lb_logits, gn)


def _store_rows_as_tiles(ref, val):
    rows = val.shape[0]
    for s in range(ROW_TILE):
        ref[pl.ds(s, rows, stride=ROW_TILE), :] = val[:, s * LANES:(s + 1) * LANES]


def _load_rows_from_tiles(ref, start, rows):
    return jnp.concatenate(
        [ref[pl.ds(start * ROW_TILE + s, rows, stride=ROW_TILE), :] for s in range(ROW_TILE)], axis=1)


def _pack_bf16_pairs(x):
    c = x.shape[1] // 2
    bits = pltpu.bitcast(x.astype(BF16).astype(F32), U32)
    return jnp.bitwise_or(jnp.right_shift(bits[:, :c], jnp.uint32(16)), bits[:, c:])


def _unpack_bf16_pairs(p):
    lo = pltpu.bitcast(jnp.left_shift(p, jnp.uint32(16)), F32)
    hi = pltpu.bitcast(jnp.bitwise_and(p, jnp.uint32(0xFFFF0000)), F32)
    return lo, hi


def _mix_kernel(sink_ref, q_ref, kp_ref, kc_ref, vp_ref, vc_ref, bias_ref,
                r_ref, ga_ref, gh_ref, x_ref, wa_ref, wr_ref, wo_ref, g2_ref, wr2t_ref, brt_ref,
                x1_ref, h2p_ref, route_ref, cnt_ref, *, tiles_per_seq):
    i = pl.program_id(0)

    @pl.when(i == 0)
    def _():
        cnt_ref[...] = jnp.zeros_like(cnt_ref)

    r_tile = r_ref[...]
    n_blk = MIX_TM // ATTN_BLOCK
    n_piece = D_MODEL // MXU_WIDTH
    rm_pieces = []

    def rm_piece(c):
        def emit():
            rm_pieces.append(jnp.dot(r_tile, wr_ref[:, c * MXU_WIDTH:(c + 1) * MXU_WIDTH],
                                     preferred_element_type=F32))
        return emit

    k_rows = jnp.concatenate([kp_ref[...], kc_ref[...]], axis=0)
    v_rows = jnp.concatenate([vp_ref[...], vc_ref[...]], axis=0)
    first_in_seq = lax.rem(i, tiles_per_seq) == 0
    blocks = []
    for blk in range(n_blk):
        rows = slice(blk * ATTN_BLOCK, (blk + 1) * ATTN_BLOCK)
        window = slice(blk * ATTN_BLOCK, (blk + 2) * ATTN_BLOCK)
        has_prev = jnp.logical_not(first_in_seq) if blk == 0 else True
        side = [rm_piece(c) for c in range(blk * n_piece // n_blk, (blk + 1) * n_piece // n_blk)]
        blocks.append(_swa_block(q_ref[rows, :], k_rows[window], v_rows[window], has_prev, bias_ref, sink_ref, side))
    rm = jnp.concatenate(rm_pieces, axis=1)
    a_t = jnp.concatenate(blocks, axis=1).astype(BF16)
    am = _dot_tn(a_t, wa_ref[...])
    mixed = _sigmoid(ga_ref[...].astype(F32)) * am + _sigmoid(gh_ref[...].astype(F32)) * rm
    x1 = x_ref[...] + jnp.dot(mixed.astype(BF16), wo_ref[...], preferred_element_type=F32)
    x1_ref[...] = x1
    h2 = x1 * lax.rsqrt(jnp.mean(x1 * x1, axis=-1, keepdims=True) + RMS_EPS) * g2_ref[...]

    _store_rows_as_tiles(h2p_ref, _pack_bf16_pairs(h2))

    hi = h2.astype(BF16)
    lo = (h2 - hi.astype(F32)).astype(BF16)
    both = _dot_nt(wr2t_ref[...], hi)
    logits = both[:ROUTE_LANES] + both[ROUTE_LANES:] + _dot_nt(wr2t_ref[:ROUTE_LANES, :], lo)
    logits = logits[:ROUTE_ROWS] + brt_ref[:ROUTE_ROWS, :]

    tm = logits.shape[1]
    row = lax.broadcasted_iota(jnp.int32, logits.shape, 0)
    row_f = row.astype(F32)
    big = float(ROUTE_ROWS)
    gl = jnp.where(row < N_GROUPS, logits, NEG_INF)
    gmax = jnp.max(gl, axis=0, keepdims=True)
    g_idx = jnp.min(jnp.where(gl == gmax, row_f, big), axis=0, keepdims=True)
    g_w = 1.0 / jnp.sum(jnp.exp(gl - gmax), axis=0, keepdims=True)
    e_row = row - N_GROUPS
    in_group = jnp.logical_and(e_row >= 0, e_row < N_EXPERTS)
    row_group = jnp.right_shift(e_row, 3).astype(F32)
    in_group = jnp.logical_and(in_group, row_group == g_idx)
    el = jnp.where(in_group, logits, NEG_INF)
    m1 = jnp.max(el, axis=0, keepdims=True)
    i1 = jnp.min(jnp.where(el == m1, row_f, big), axis=0, keepdims=True)
    el2 = jnp.where(row_f == i1, NEG_INF, el)
    m2 = jnp.max(el2, axis=0, keepdims=True)
    i2 = jnp.min(jnp.where(el2 == m2, row_f, big), axis=0, keepdims=True)
    p2 = jnp.exp(m2 - m1)
    w1 = g_w / (1.0 + p2)
    w2 = g_w * p2 / (1.0 + p2)
    e1 = i1 - N_GROUPS
    e2 = i2 - N_GROUPS

    ss = lax.broadcasted_iota(jnp.int32, (tm, tm), 0)
    tt = lax.broadcasted_iota(jnp.int32, (tm, tm), 1)
    earlier = jnp.where(ss < tt, 1.0, 0.0).astype(BF16)
    sel1 = row_f == i1
    sel2 = row_f == i2
    oh1 = jnp.where(sel1, 1.0, 0.0)
    oh2 = jnp.where(sel2, 1.0, 0.0)
    base = cnt_ref[...]
    c1 = jnp.sum(oh1, axis=1, keepdims=True)
    c2 = jnp.sum(oh2, axis=1, keepdims=True)
    pre1 = jnp.dot(oh1.astype(BF16), earlier, preferred_element_type=F32) + base
    pre2 = jnp.dot(oh2.astype(BF16), earlier, preferred_element_type=F32) + (base + c1)
    rank1 = jnp.sum(jnp.where(sel1, pre1, 0.0), axis=0, keepdims=True)
    rank2 = jnp.sum(jnp.where(sel2, pre2, 0.0), axis=0, keepdims=True)
    cnt_ref[...] = base + c1 + c2

    vals = jnp.concatenate([w1, w2, e1, e2, rank1, rank2, jnp.zeros((ROUTE_VALS - 6, tm), F32)], axis=0)
    v1 = vals.astype(BF16)
    rest = vals - v1.astype(F32)
    v2 = rest.astype(BF16)
    v3 = (rest - v2.astype(F32)).astype(BF16)
    kk = lax.broadcasted_iota(jnp.int32, (ROUTE_VALS, ROUTE_LANES), 0)
    ll = lax.broadcasted_iota(jnp.int32, (ROUTE_VALS, ROUTE_LANES), 1)
    pick = jnp.where(kk == ll, 1.0, 0.0).astype(BF16)
    route_ref[...] = (_dot_tn(v3, pick) + _dot_tn(v2, pick)) + _dot_tn(v1, pick)


def _mix(sinks, bias, r2d, proj, x2d, wa, wr, wo, g2, wr2t, brt, seq_len):
    n = x2d.shape[0]
    row = lambda i: (i, 0)
    const = lambda i: (0, 0)
    blocks_per_tile = MIX_TM // ATTN_BLOCK
    prev_block = lambda i: jnp.maximum(blocks_per_tile * i - 1, 0)
    return pl.pallas_call(
        functools.partial(_mix_kernel, tiles_per_seq=seq_len // MIX_TM),
        grid=(n // MIX_TM,),
        in_specs=[
            pl.BlockSpec(memory_space=pltpu.SMEM),
            pl.BlockSpec((MIX_TM, ATTN_WIDTH), lambda i: (i, OFF_QA // ATTN_WIDTH)),
            pl.BlockSpec((ATTN_BLOCK, KV_WIDTH), lambda i: (prev_block(i), OFF_KA // KV_WIDTH)),
            pl.BlockSpec((MIX_TM, KV_WIDTH), lambda i: (i, OFF_KA // KV_WIDTH)),
            pl.BlockSpec((ATTN_BLOCK, KV_WIDTH), lambda i: (prev_block(i), OFF_VA // KV_WIDTH)),
            pl.BlockSpec((MIX_TM, KV_WIDTH), lambda i: (i, OFF_VA // KV_WIDTH)),
            pl.BlockSpec((N_Q_HEADS, ATTN_BLOCK, ATTN_BLOCK), lambda i: (0, 0, 0)),
            pl.BlockSpec((MIX_TM, HG_WIDTH), row),
            pl.BlockSpec((MIX_TM, D_MODEL), lambda i: (i, OFF_GATE_A // D_MODEL)),
            pl.BlockSpec((MIX_TM, D_MODEL), lambda i: (i, OFF_GATE_H // D_MODEL)),
            pl.BlockSpec((MIX_TM, D_MODEL), row),
            pl.BlockSpec((ATTN_WIDTH, D_MODEL), const, pipeline_mode=pl.Buffered(1)),
            pl.BlockSpec((HG_WIDTH, D_MODEL), const, pipeline_mode=pl.Buffered(1)),
            pl.BlockSpec((D_MODEL, D_MODEL), const, pipeline_mode=pl.Buffered(1)),
            pl.BlockSpec((1, D_MODEL), const),
            pl.BlockSpec((2 * ROUTE_LANES, D_MODEL), const, pipeline_mode=pl.Buffered(1)),
            pl.BlockSpec((ROUTE_LANES, 1), const),
        ],
        out_specs=[
            pl.BlockSpec((MIX_TM, D_MODEL), row),
            pl.BlockSpec((MIX_TM * ROW_TILE, LANES), row),
            pl.BlockSpec((MIX_TM, ROUTE_LANES), row),
        ],
        out_shape=[
            jax.ShapeDtypeStruct((n, D_MODEL), F32),
            jax.ShapeDtypeStruct((n * ROW_TILE, LANES), U32),
            jax.ShapeDtypeStruct((n, ROUTE_LANES), F32),
        ],
        scratch_shapes=[pltpu.VMEM((ROUTE_ROWS, 1), F32)],
        compiler_params=pltpu.CompilerParams(dimension_semantics=("arbitrary",), vmem_limit_bytes=_vmem(60)),
        name="mix",
    )(sinks, proj, proj, proj, proj, proj, bias, r2d, proj, proj, x2d, wa, wr, wo, g2, wr2t, brt)


def _dispatch_kernel(pad_start_ref, pad_count_ref, tail_ref, dest_ref, src_ref, dst_hbm, zero_ref, sem):
    @pl.when(pl.program_id(0) == 0)
    def _():
        zero_ref[...] = jnp.zeros_like(zero_ref)
        for phase in ("start", "wait"):
            for e in range(N_EXPERTS):
                count, start = pad_count_ref[e], pad_start_ref[e]
                size = MOE_TM // 2
                while size >= 1:
                    done = jnp.bitwise_and(count, -2 * size)
                    copy = pltpu.make_async_copy(zero_ref.at[pl.ds(0, size)],
                                                 dst_hbm.at[pl.ds(start + done, size)], sem.at[1])
                    pl.when(jnp.bitwise_and(count, size) != 0)(getattr(copy, phase))
                    size //= 2
            for j in range(N_EXPERTS):
                copy = pltpu.make_async_copy(zero_ref, dst_hbm.at[pl.ds(tail_ref[0] + j * MOE_TM, MOE_TM)],
                                             sem.at[1])
                pl.when(j < tail_ref[1])(getattr(copy, phase))

    for a in range(DISPATCH_ROWS):
        pltpu.make_async_copy(src_ref.at[a // TOP_K], dst_hbm.at[dest_ref[0, 0, a]], sem.at[0]).start(priority=a % 2)
    for _ in range(TOP_K):
        pltpu.make_async_copy(src_ref, dst_hbm.at[pl.ds(0, DISPATCH_ROWS // TOP_K)], sem.at[0]).wait()


def _dispatch(dest_flat, pad_start, pad_count, tail, h2p_tiles, n_rows):
    n_asg = dest_flat.shape[0]
    steps = n_asg // DISPATCH_ROWS
    smem = pl.BlockSpec(memory_space=pltpu.SMEM)
    return pl.pallas_call(
        _dispatch_kernel,
        grid=(steps,),
        in_specs=[
            smem, smem, smem,
            pl.BlockSpec((1, 1, DISPATCH_ROWS), lambda s: (s, 0, 0), memory_space=pltpu.SMEM),
            pl.BlockSpec((DISPATCH_ROWS // TOP_K, ROW_TILE, LANES), lambda s: (s, 0, 0)),
        ],
        out_specs=pl.BlockSpec(memory_space=pl.ANY),
        out_shape=jax.ShapeDtypeStruct((n_rows, ROW_TILE, LANES), U32),
        scratch_shapes=[pltpu.VMEM((MOE_TM, ROW_TILE, LANES), U32), pltpu.SemaphoreType.DMA((2,))],
        compiler_params=pltpu.CompilerParams(dimension_semantics=("arbitrary",)),
        name="dispatch",
    )(pad_start, pad_count, tail, dest_flat.reshape(steps, 1, DISPATCH_ROWS), h2p_tiles)


def _moe_kernel(be_ref, nused_ref, first_ref, slot_ref, next_ref, next2_ref, xs_ref, wg_hbm, wu_hbm, wd_hbm, y_ref,
                wg_buf, wu_buf, wd_buf, wg_s, wu_s, wd_s, sem):
    b = pl.program_id(0)
    live = b < nused_ref[0]

    def weight_copies(e, s):
        return (pltpu.make_async_copy(wg_hbm.at[0, e], wg_buf.at[s], sem.at[s, 0]),
                pltpu.make_async_copy(wu_hbm.at[0, e], wu_buf.at[s], sem.at[s, 1]),
                pltpu.make_async_copy(wd_hbm.at[0, e], wd_buf.at[s], sem.at[s, 2]))

    @pl.when(b == 0)
    def _():
        for c in weight_copies(be_ref[0], 0):
            c.start()

        @pl.when(next_ref[0] >= 0)
        def _():
            for c in weight_copies(next_ref[0], 1):
                c.start()

    for s in range(MOE_WEIGHT_BUFFERS):
        @pl.when(jnp.logical_and(jnp.logical_and(live, first_ref[b] == 1), slot_ref[b] == s))
        def _():
            for c in weight_copies(be_ref[b], s):
                c.wait()

            @pl.when(next2_ref[b] >= 0)
            def _():
                for c in weight_copies(next2_ref[b], (s + 2) % MOE_WEIGHT_BUFFERS):
                    c.start()

            wg_s[...] = wg_buf[s].astype(BF16)
            wu_s[...] = wu_buf[s].astype(BF16)
            wd_s[...] = wd_buf[s].astype(BF16)

    @pl.when(jnp.logical_not(live))
    def _():
        y_ref[...] = jnp.zeros_like(y_ref)

    @pl.when(live)
    def _():
        xlo, xhi = _unpack_bf16_pairs(_load_rows_from_tiles(xs_ref, 0, MOE_TM))
        xb = jnp.concatenate([xlo.astype(BF16), xhi.astype(BF16)], axis=1)
        hb = []
        for c in range(EXPERT_FF // MXU_WIDTH):
            cols = slice(c * MXU_WIDTH, (c + 1) * MXU_WIDTH)
            hg = jnp.dot(xb, wg_s[:, cols], preferred_element_type=F32)
            hu = jnp.dot(xb, wu_s[:, cols], preferred_element_type=F32)
            hb.append((hg * _sigmoid(hg) * hu).astype(BF16))
        hb = jnp.concatenate(hb, axis=1)
        half = D_MODEL // 2
        for c in range(half // MXU_WIDTH):
            lo = jnp.dot(hb, wd_s[:, c * MXU_WIDTH:(c + 1) * MXU_WIDTH], preferred_element_type=F32)
            hi = jnp.dot(hb, wd_s[:, half + c * MXU_WIDTH:half + (c + 1) * MXU_WIDTH], preferred_element_type=F32)
            packed = _pack_bf16_pairs(jnp.concatenate([lo, hi], axis=1))
            for k in range(MXU_WIDTH // LANES):
                s = c * (MXU_WIDTH // LANES) + k
                y_ref[pl.ds(s, MOE_TM, stride=ROW_TILE), :] = packed[:, k * LANES:(k + 1) * LANES]


def _moe(block_e, n_used, first, slot, next_e, next2_e, xs, w_gate, w_up, w_down):
    n_rows = xs.shape[0] // ROW_TILE
    n_blocks = n_rows // MOE_TM
    grid_spec = pltpu.PrefetchScalarGridSpec(
        num_scalar_prefetch=6,
        grid=(n_blocks,),
        in_specs=[
            pl.BlockSpec((MOE_TM * ROW_TILE, LANES), lambda b, be, nu, *_: (jnp.minimum(b, nu[0] - 1), 0)),
            pl.BlockSpec(memory_space=pl.ANY),
            pl.BlockSpec(memory_space=pl.ANY),
            pl.BlockSpec(memory_space=pl.ANY),
        ],
        out_specs=pl.BlockSpec((MOE_TM * ROW_TILE, LANES), lambda b, *_: (b, 0)),
        scratch_shapes=[
            pltpu.VMEM((MOE_WEIGHT_BUFFERS, D_MODEL, EXPERT_FF), F32),
            pltpu.VMEM((MOE_WEIGHT_BUFFERS, D_MODEL, EXPERT_FF), F32),
            pltpu.VMEM((MOE_WEIGHT_BUFFERS, EXPERT_FF, D_MODEL), F32),
            pltpu.VMEM((D_MODEL, EXPERT_FF), BF16),
            pltpu.VMEM((D_MODEL, EXPERT_FF), BF16),
            pltpu.VMEM((EXPERT_FF, D_MODEL), BF16),
            pltpu.SemaphoreType.DMA((MOE_WEIGHT_BUFFERS, 3)),
        ],
    )
    return pl.pallas_call(
        _moe_kernel,
        grid_spec=grid_spec,
        out_shape=jax.ShapeDtypeStruct((n_rows * ROW_TILE, LANES), U32),
        compiler_params=pltpu.CompilerParams(dimension_semantics=("arbitrary",), vmem_limit_bytes=_vmem(52)),
        name="moe",
    )(block_e, n_used, first, slot, next_e, next2_e, xs, w_gate, w_up, w_down)


def _moe_layout(route):
    e_id = route[:, 2:2 + TOP_K].astype(jnp.int32)
    rank = route[:, 2 + TOP_K:2 + 2 * TOP_K].astype(jnp.int32)
    onehot_e = e_id[:, :, None] == jnp.arange(N_EXPERTS, dtype=jnp.int32)[None, None, :]
    counts = jnp.sum(onehot_e.astype(jnp.int32), axis=(0, 1))
    padded = (counts + MOE_TM - 1) // MOE_TM * MOE_TM
    pends = jnp.cumsum(padded)
    pstarts = pends - padded
    n_tok = route.shape[0]
    n_blocks = n_tok * TOP_K // MOE_TM + N_EXPERTS
    blk_start = jnp.arange(n_blocks, dtype=jnp.int32) * MOE_TM
    block_e = jnp.minimum(jnp.sum((blk_start[:, None] >= pends[None, :]).astype(jnp.int32), axis=1), N_EXPERTS - 1)
    n_used = (pends[-1] // MOE_TM).astype(jnp.int32).reshape(1)
    blk = jnp.arange(n_blocks, dtype=jnp.int32)
    first = jnp.logical_and(blk < n_used[0], jnp.logical_or(blk == 0, block_e != jnp.roll(block_e, 1)))
    slot = lax.rem(jnp.cumsum(first.astype(jnp.int32)) - 1, MOE_WEIGHT_BUFFERS)
    ex = jnp.arange(N_EXPERTS, dtype=jnp.int32)
    later = jnp.logical_and(counts[None, :] > 0, ex[None, :] > ex[:, None])
    next_of_expert = jnp.min(jnp.where(later, ex[None, :], N_EXPERTS), axis=1)
    next_of_expert = jnp.where(next_of_expert == N_EXPERTS, -1, next_of_expert)
    lookup = lambda keys, table: jnp.sum(jnp.where(keys[:, None] == ex[None, :], table[None, :] + 1, 0), axis=1) - 1
    next2_of_expert = lookup(next_of_expert, next_of_expert)
    next_e = lookup(block_e, next_of_expert)
    next2_e = lookup(block_e, next2_of_expert)
    dest = jnp.sum(jnp.where(onehot_e, pstarts[None, None, :], 0), axis=2) + rank
    pads = ((pstarts + counts).astype(jnp.int32), (padded - counts).astype(jnp.int32),
            jnp.stack([pends[-1], n_blocks - n_used[0]]).astype(jnp.int32))
    tables = (block_e.astype(jnp.int32), n_used, first.astype(jnp.int32), slot.astype(jnp.int32),
              next_e.astype(jnp.int32), next2_e.astype(jnp.int32))
    return tables, dest, pads, n_blocks * MOE_TM


def _final_kernel(dest_ref, destn_ref, x1_ref, route_ref, g_ref, y_hbm, y_flat_hbm, o_ref, ybuf, sem):
    i = pl.program_id(0)
    n_steps = pl.num_programs(0)
    slot = lax.rem(i, 2)
    n_rows = TOP_K * FIN_TM

    def start_gather(idx_ref, s):
        for r in range(n_rows):
            pltpu.make_async_copy(y_hbm.at[idx_ref[0, 0, r]], ybuf.at[pl.ds((s * n_rows + r) * ROW_TILE, ROW_TILE)],
                                  sem.at[s]).start(priority=r % 2)

    @pl.when(i == 0)
    def _():
        start_gather(dest_ref, 0)

    for s in range(2):
        @pl.when(jnp.logical_and(i + 1 < n_steps, 1 - slot == s))
        def _():
            start_gather(destn_ref, s)

    slot_rows = n_rows * ROW_TILE
    slot_start = pl.multiple_of(slot * slot_rows, slot_rows)
    pltpu.make_async_copy(y_flat_hbm.at[pl.ds(0, slot_rows)], ybuf.at[pl.ds(slot_start, slot_rows)],
                          sem.at[slot]).wait()
    route = route_ref[...]
    x = x1_ref[...]
    for k in range(TOP_K):
        ylo, yhi = _unpack_bf16_pairs(_load_rows_from_tiles(ybuf, slot * n_rows + k * FIN_TM, FIN_TM))
        x = x + route[:, k:k + 1] * jnp.concatenate([ylo, yhi], axis=1)
    o_ref[...] = x * lax.rsqrt(jnp.mean(x * x, axis=-1, keepdims=True) + RMS_EPS) * g_ref[...]


def _final(dest, x1, route, g, y_sorted):
    n = x1.shape[0]
    steps = n // FIN_TM
    dest3 = dest.reshape(steps, FIN_TM, TOP_K).transpose(0, 2, 1).reshape(steps, 1, TOP_K * FIN_TM)
    return pl.pallas_call(
        _final_kernel,
        grid=(steps,),
        in_specs=[
            pl.BlockSpec((1, 1, TOP_K * FIN_TM), lambda i: (i, 0, 0), memory_space=pltpu.SMEM),
            pl.BlockSpec((1, 1, TOP_K * FIN_TM), lambda i: (jnp.minimum(i + 1, steps - 1), 0, 0),
                         memory_space=pltpu.SMEM),
            pl.BlockSpec((FIN_TM, D_MODEL), lambda i: (i, 0)),
            pl.BlockSpec((FIN_TM, ROUTE_LANES), lambda i: (i, 0)),
            pl.BlockSpec((1, D_MODEL), lambda i: (0, 0)),
            pl.BlockSpec(memory_space=pl.ANY),
            pl.BlockSpec(memory_space=pl.ANY),
        ],
        out_specs=pl.BlockSpec((FIN_TM, D_MODEL), lambda i: (i, 0)),
        out_shape=jax.ShapeDtypeStruct((n, D_MODEL), F32),
        scratch_shapes=[
            pltpu.VMEM((2 * TOP_K * FIN_TM * ROW_TILE, LANES), U32),
            pltpu.SemaphoreType.DMA((2,)),
        ],
        compiler_params=pltpu.CompilerParams(dimension_semantics=("arbitrary",), vmem_limit_bytes=_vmem(40)),
        name="final",
    )(dest3, dest3, x1, route, g, y_sorted.reshape(-1, ROW_TILE, LANES), y_sorted)


def kernel(x, norm1_g, w_in, attn_sinks, rel_bias, hg_lb_logits, hg_norm_g, w_attn_branch, w_hg_branch, w_out,
           norm2_g, w_group_router, b_group_router, w_expert_router, b_expert_router, w_gate, w_up, w_down, final_g):
    bsz, s_len, d = x.shape
    n_tok = bsz * s_len
    x2d = x.reshape(n_tok, d)

    w_route = jnp.concatenate([w_group_router[0], w_expert_router[0]], axis=1).astype(F32)
    w_route = jnp.pad(w_route, ((0, 0), (0, ROUTE_LANES - w_route.shape[1])))
    whi = w_route.astype(BF16)
    wlo = (w_route - whi.astype(F32)).astype(BF16)
    wr2t = jnp.concatenate([whi.T, wlo.T], axis=0)
    b_route = jnp.concatenate([b_group_router[0], b_expert_router[0]]).astype(F32)
    b_route = jnp.pad(b_route, (0, ROUTE_LANES - b_route.shape[0])).reshape(ROUTE_LANES, 1)
    bias = _attn_bias_table(rel_bias)

    proj = _in_proj(x2d, norm1_g[0].reshape(1, d).astype(F32), w_in)
    proj3 = proj.reshape(bsz, s_len, IN_WIDTH)
    r = _hgrn2(proj3, hg_lb_logits.astype(F32), hg_norm_g[0].reshape(1, HG_DIM).astype(F32))
    x1, h2p, route = _mix(
        attn_sinks[0].astype(F32), bias, r.reshape(n_tok, HG_WIDTH), proj, x2d,
        w_attn_branch[0].astype(BF16), w_hg_branch[0].astype(BF16), w_out[0].astype(BF16),
        norm2_g[0].reshape(1, d).astype(F32), wr2t, b_route, s_len)
    tables, dest, pads, n_rows = _moe_layout(route)
    xs = _dispatch(dest.reshape(n_tok * TOP_K), *pads, h2p.reshape(n_tok, ROW_TILE, LANES), n_rows)
    y_sorted = _moe(*tables, xs.reshape(n_rows * ROW_TILE, LANES), w_gate, w_up, w_down)
    out = _final(dest, x1, route, final_g.reshape(1, d).astype(F32), y_sorted)
    return out.reshape(bsz, s_len, d)
```

```python
import functools
import math

import jax
import jax.numpy as jnp
from jax import lax
from jax.experimental import pallas as pl
from jax.experimental.pallas import tpu as pltpu

D_MODEL = 2048
N_Q_HEADS = 16
N_KV_HEADS = 4
HEAD_DIM = 64
WINDOW = 128
ATTN_BLOCK = 128
ATTN_WIDTH = N_Q_HEADS * HEAD_DIM
KV_WIDTH = N_KV_HEADS * HEAD_DIM
REL_BUCKETS = 32
REL_MAX_DIST = 128
HG_HEADS = 8
HG_DIM = 128
HG_WIDTH = HG_HEADS * HG_DIM
HG_CHUNK = 64
HG_SUB = 16
N_GROUPS = 4
EXPERTS_PER_GROUP = 8
N_EXPERTS = N_GROUPS * EXPERTS_PER_GROUP
TOP_K = 2
EXPERT_FF = 512
RMS_EPS = 1e-6

OFF_GATE_A = 0
OFF_GATE_H = D_MODEL
OFF_QA = 2 * D_MODEL
OFF_QH = OFF_QA + ATTN_WIDTH
OFF_FH = OFF_QH + HG_WIDTH
OFF_IH = OFF_FH + HG_WIDTH
OFF_GH = OFF_IH + HG_WIDTH
OFF_KA = OFF_GH + HG_WIDTH
OFF_VA = OFF_KA + KV_WIDTH
IN_WIDTH = OFF_VA + KV_WIDTH
REF_OFF_QA = 0
REF_OFF_KA = ATTN_WIDTH
REF_OFF_QH = ATTN_WIDTH + 2 * KV_WIDTH
REF_OFF_GATE_A = REF_OFF_QH + 4 * HG_WIDTH

HG_SAFE_DECAY = 60.0

ROW_TILE = 8
LANES = 128
assert D_MODEL // 2 == ROW_TILE * LANES
MXU_WIDTH = 256

IN_TM = 2048
IN_TN = 512
HG_TC = 1024
MIX_TM = 512
ROUTE_LANES = 128
ROUTE_ROWS = 40
ROUTE_VALS = 16
assert N_GROUPS + N_EXPERTS <= ROUTE_ROWS <= ROUTE_LANES
DISPATCH_ROWS = 2048
MOE_TM = 256
MOE_WEIGHT_BUFFERS = 3
FIN_TM = 256

F32 = jnp.float32
BF16 = jnp.bfloat16
U32 = jnp.uint32
NEG_INF = float("-inf")


def _vmem(mib):
    return mib * 1024 * 1024


def _sigmoid(x):
    return 1.0 / (1.0 + jnp.exp(-x))


def _dot_nt(a, b):
    return lax.dot_general(a, b, (((1,), (1,)), ((), ())), preferred_element_type=F32)


def _dot_tn(a, b):
    return lax.dot_general(a, b, (((0,), (0,)), ((), ())), preferred_element_type=F32)


def _in_proj_kernel(x_ref, g_ref, w_ref, o_ref, h_ref):
    @pl.when(pl.program_id(1) == 0)
    def _():
        x = x_ref[...]
        ms = jnp.mean(x * x, axis=-1, keepdims=True)
        h_ref[...] = (x * lax.rsqrt(ms + RMS_EPS) * g_ref[...]).astype(BF16)

    o_ref[...] = jnp.dot(h_ref[...], w_ref[0].astype(BF16), preferred_element_type=F32).astype(o_ref.dtype)


def _in_proj(x2d, g, w_in):
    n = x2d.shape[0]
    n_col = IN_WIDTH // IN_TN

    def out_block(j):
        gates = j - REF_OFF_GATE_A // IN_TN + OFF_GATE_A // IN_TN
        q_a = j - REF_OFF_QA // IN_TN + OFF_QA // IN_TN
        kv = j - REF_OFF_KA // IN_TN + OFF_KA // IN_TN
        hg = j - REF_OFF_QH // IN_TN + OFF_QH // IN_TN
        return jnp.where(j >= REF_OFF_GATE_A // IN_TN, gates,
                         jnp.where(j >= REF_OFF_QH // IN_TN, hg, jnp.where(j >= REF_OFF_KA // IN_TN, kv, q_a)))

    return pl.pallas_call(
        _in_proj_kernel,
        grid=(n // IN_TM, n_col),
        in_specs=[
            pl.BlockSpec((IN_TM, D_MODEL), lambda i, j: (i, 0)),
            pl.BlockSpec((1, D_MODEL), lambda i, j: (0, 0)),
            pl.BlockSpec((1, D_MODEL, IN_TN), lambda i, j: (0, 0, j)),
        ],
        out_specs=pl.BlockSpec((IN_TM, IN_TN), lambda i, j: (i, out_block(j))),
        out_shape=jax.ShapeDtypeStruct((n, IN_WIDTH), BF16),
        scratch_shapes=[pltpu.VMEM((IN_TM, D_MODEL), BF16)],
        compiler_params=pltpu.CompilerParams(
            dimension_semantics=("parallel", "arbitrary"), vmem_limit_bytes=_vmem(58)),
        name="in_proj",
    )(x2d, g, w_in)


def _swa_block(q, kk, vv, has_prev, bias_ref, sink_ref, side_work):
    q = q * jnp.asarray(HEAD_DIM ** -0.5, BF16)
    key = lax.broadcasted_iota(jnp.int32, (ATTN_BLOCK, ATTN_BLOCK), 0)
    qry = lax.broadcasted_iota(jnp.int32, (ATTN_BLOCK, ATTN_BLOCK), 1)
    cur = key <= qry
    key_ok = jnp.logical_or(cur, has_prev)
    grp = N_Q_HEADS // N_KV_HEADS
    head = lambda t, h: t[:, h * HEAD_DIM:(h + 1) * HEAD_DIM]
    lanes = lambda t, g: t[:, g * ATTN_BLOCK:(g + 1) * ATTN_BLOCK]
    scores = []
    for hk in range(N_KV_HEADS):
        q_grp = jnp.concatenate([head(q, hk * grp + g) for g in range(grp)], axis=0)
        scores.append(_dot_nt(head(kk, hk), q_grp))
    emitted = 0
    probs, denoms = [], []
    for hq in range(N_Q_HEADS):
        while emitted * N_Q_HEADS < hq * len(side_work) + len(side_work):
            side_work[emitted]()
            emitted += 1
        s2 = lanes(scores[hq // grp], hq % grp)
        s = jnp.where(cur, s2[ATTN_BLOCK:], s2[:ATTN_BLOCK]) + bias_ref[hq]
        s = jnp.where(key_ok, s, NEG_INF)
        sink = sink_ref[hq]
        m = jnp.maximum(jnp.max(s, axis=0, keepdims=True), sink)
        p = jnp.exp(s - m)
        denoms.append(jnp.sum(p, axis=0, keepdims=True) + jnp.exp(sink - m))
        probs.append(jnp.concatenate([jnp.where(cur, 0.0, p), jnp.where(cur, p, 0.0)], axis=0).astype(BF16))
    outs = []
    for hk in range(N_KV_HEADS):
        p_grp = jnp.concatenate(probs[hk * grp:(hk + 1) * grp], axis=1)
        o_grp = _dot_tn(head(vv, hk), p_grp)
        outs += [lanes(o_grp, g) / denoms[hk * grp + g] for g in range(grp)]
    return jnp.concatenate(outs, axis=0)


def _t5_causal_bucket(n):
    max_exact = REL_BUCKETS // 2
    nf = jnp.maximum(n, 1).astype(F32)
    large = max_exact + (jnp.log(nf / max_exact) / math.log(REL_MAX_DIST / max_exact)
                         * (REL_BUCKETS - max_exact)).astype(jnp.int32)
    large = jnp.minimum(large, REL_BUCKETS - 1)
    return jnp.where(n < max_exact, n, large)


def _attn_bias_table(rel_bias):
    assert WINDOW == ATTN_BLOCK
    r = jnp.arange(ATTN_BLOCK)[None, :]
    c = jnp.arange(ATTN_BLOCK)[:, None]
    dist = jnp.where(c <= r, r - c, r + ATTN_BLOCK - c)
    bucket = _t5_causal_bucket(dist)
    onehot = (bucket[None, :, :] == jnp.arange(REL_BUCKETS)[:, None, None]).astype(F32)
    return jnp.einsum("bh,bqk->hqk", rel_bias.astype(F32), onehot, precision=lax.Precision.HIGHEST)


def _cumsum_rows(tri_bf16, g):
    width = g.shape[1]
    g1 = g.astype(BF16)
    r1 = g - g1.astype(F32)
    g2 = r1.astype(BF16)
    g3 = (r1 - g2.astype(F32)).astype(BF16)
    acc = jnp.dot(tri_bf16, jnp.concatenate([g3, g2, g1], axis=1), preferred_element_type=F32)
    return (acc[:, :width] + acc[:, width:2 * width]) + acc[:, 2 * width:]


def _hgrn_head_safe(q, kf, v, b, st):
    rows = lax.broadcasted_iota(jnp.int32, (HG_SUB, 1), 0)
    outs = []
    prev_end = jnp.zeros((1, HG_DIM), F32)
    for i in range(HG_CHUNK // HG_SUB):
        sl = slice(i * HG_SUB, (i + 1) * HG_SUB)
        bl = b[sl] - prev_end
        qi, ki, vi = q[sl], kf[sl], v[sl]
        vif = vi.astype(F32)
        blast = bl[HG_SUB - 1:HG_SUB]
        o = _dot_nt((qi * jnp.exp(bl)).astype(BF16), st.astype(BF16))
        for s in range(HG_SUB):
            d = jnp.exp(jnp.minimum(bl - bl[s:s + 1], 0.0))
            a = jnp.sum(qi * ki[s:s + 1] * d, axis=-1, keepdims=True)
            o = o + jnp.where(rows >= s, a, 0.0) * vif[s:s + 1]
        kd = (ki * jnp.exp(blast - bl)).astype(BF16)
        st = st * jnp.exp(blast) + _dot_tn(vi, kd)
        prev_end = b[(i + 1) * HG_SUB - 1:(i + 1) * HG_SUB]
        outs.append(o)
    return jnp.concatenate(outs, axis=0), st


def _hgrn_kernel(q_ref, f_ref, i_ref, g_ref, lbl_ref, gn_ref, o_ref, st_ref, kf_ref, b_ref):
    @pl.when(pl.program_id(1) == 0)
    def _():
        st_ref[...] = jnp.zeros_like(st_ref)

    lg2 = lbl_ref[...].astype(F32)
    e = jnp.exp(lg2 - jnp.max(lg2, axis=0, keepdims=True))
    lb = e[0:1] / jnp.sum(e, axis=0, keepdims=True)
    gn = gn_ref[...].astype(F32)
    r = lax.broadcasted_iota(jnp.int32, (HG_CHUNK, HG_CHUNK), 0)
    c = lax.broadcasted_iota(jnp.int32, (HG_CHUNK, HG_CHUNK), 1)
    tri_mask = r >= c
    tri = jnp.where(tri_mask, 1.0, 0.0).astype(BF16)

    n_chunks = HG_TC // HG_CHUNK

    f = lb + (1.0 - lb) * _sigmoid(f_ref[0].astype(F32))
    kf_ref[...] = 1.0 - f
    logf = jnp.log(f)
    worst = None
    for ci in range(n_chunks):
        sl = slice(ci * HG_CHUNK, (ci + 1) * HG_CHUNK)
        b = _cumsum_rows(tri, logf[sl])
        b_ref[sl, :] = b
        bend = b[HG_CHUNK - 1:HG_CHUNK]
        worst = bend if worst is None else jnp.minimum(worst, bend)

    heads = range(HG_HEADS)
    head = lambda t, h: t[:, h * HG_DIM:(h + 1) * HG_DIM]

    def finish_all(sl, outs):
        scale = [lax.rsqrt(jnp.mean(o * o, axis=-1, keepdims=True) + RMS_EPS) for o in outs]
        normed = jnp.concatenate([outs[h] * scale[h] * gn for h in heads], axis=1)
        go = g_ref[0, sl, :].astype(F32)
        o_ref[0, sl, :] = (normed * (go * _sigmoid(go))).astype(o_ref.dtype)

    def fast():
        states = [st_ref[h] for h in heads]
        for ci in range(n_chunks):
            sl = slice(ci * HG_CHUNK, (ci + 1) * HG_CHUNK)
            b = b_ref[sl, :]
            qp = (q_ref[0, sl, :].astype(F32) * jnp.exp(b)).astype(BF16)
            kp = kf_ref[sl, :] * jnp.exp(-b)
            kpb = kp.astype(BF16)
            eb = jnp.exp(b[HG_CHUNK - 1:HG_CHUNK])
            kd = (kp * eb).astype(BF16)
            v = i_ref[0, sl, :]
            att = [_dot_nt(head(qp, h), head(kpb, h)) for h in heads]
            inter = [_dot_nt(head(qp, h), states[h].astype(BF16)) for h in heads]
            update = [_dot_tn(head(v, h), head(kd, h)) for h in heads]
            att = [jnp.where(tri_mask, a, 0.0).astype(BF16) for a in att]
            outs = [inter[h] + jnp.dot(att[h], head(v, h), preferred_element_type=F32) for h in heads]
            states = [states[h] * head(eb, h) + update[h] for h in heads]
            finish_all(sl, outs)
        for h in heads:
            st_ref[h] = states[h]

    def safe():
        def chunk(ci, carry):
            sl = pl.ds(pl.multiple_of(ci * HG_CHUNK, HG_CHUNK), HG_CHUNK)
            outs = []
            for h in heads:
                hs = slice(h * HG_DIM, (h + 1) * HG_DIM)
                o, st_new = _hgrn_head_safe(q_ref[0, sl, hs].astype(F32), kf_ref[sl, hs], i_ref[0, sl, hs],
                                            b_ref[sl, hs], st_ref[h])
                st_ref[h] = st_new
                outs.append(o)
            finish_all(sl, outs)
            return carry

        lax.fori_loop(0, n_chunks, chunk, 0)

    lax.cond(jnp.min(worst) >= -HG_SAFE_DECAY, fast, safe)


def _hgrn2(proj3, lb_logits, gn):
    bsz, s_len, _ = proj3.shape
    blk = lambda off: (lambda b, t: (b, t, off // HG_WIDTH))
    return pl.pallas_call(
        _hgrn_kernel,
        grid=(bsz, s_len // HG_TC),
        in_specs=[
            pl.BlockSpec((1, HG_TC, HG_WIDTH), blk(OFF_QH)),
            pl.BlockSpec((1, HG_TC, HG_WIDTH), blk(OFF_FH)),
            pl.BlockSpec((1, HG_TC, HG_WIDTH), blk(OFF_IH)),
            pl.BlockSpec((1, HG_TC, HG_WIDTH), blk(OFF_GH)),
            pl.BlockSpec((lb_logits.shape[0], HG_WIDTH), lambda b, t: (0, 0)),
            pl.BlockSpec((1, HG_DIM), lambda b, t: (0, 0)),
        ],
        out_specs=pl.BlockSpec((1, HG_TC, HG_WIDTH), lambda b, t: (b, t, 0)),
        out_shape=jax.ShapeDtypeStruct((bsz, s_len, HG_WIDTH), BF16),
        scratch_shapes=[
            pltpu.VMEM((HG_HEADS, HG_DIM, HG_DIM), F32),
            pltpu.VMEM((HG_TC, HG_WIDTH), F32),
            pltpu.VMEM((HG_TC, HG_WIDTH), F32),
        ],
        compiler_params=pltpu.CompilerParams(
            dimension_semantics=("parallel", "arbitrary"), vmem_limit_bytes=_vmem(48)),
        name="hgrn2",
    )(proj3, proj3, proj3, proj3, lb_logits, gn)


def _store_rows_as_tiles(ref, val):
    rows = val.shape[0]
    for s in range(ROW_TILE):
        ref[pl.ds(s, rows, stride=ROW_TILE), :] = val[:, s * LANES:(s + 1) * LANES]


def _load_rows_from_tiles(ref, start, rows):
    return jnp.concatenate(
        [ref[pl.ds(start * ROW_TILE + s, rows, stride=ROW_TILE), :] for s in range(ROW_TILE)], axis=1)


def _pack_bf16_pairs(x):
    c = x.shape[1] // 2
    bits = pltpu.bitcast(x.astype(BF16).astype(F32), U32)
    return jnp.bitwise_or(jnp.right_shift(bits[:, :c], jnp.uint32(16)), bits[:, c:])


def _unpack_bf16_pairs(p):
    lo = pltpu.bitcast(jnp.left_shift(p, jnp.uint32(16)), F32)
    hi = pltpu.bitcast(jnp.bitwise_and(p, jnp.uint32(0xFFFF0000)), F32)
    return lo, hi


def _mix_kernel(sink_ref, q_ref, kp_ref, kc_ref, vp_ref, vc_ref, bias_ref,
                r_ref, ga_ref, gh_ref, x_ref, wa_ref, wr_ref, wo_ref, g2_ref, wr2t_ref, brt_ref,
                x1_ref, h2p_ref, route_ref, cnt_ref, *, tiles_per_seq):
    i = pl.program_id(0)

    @pl.when(i == 0)
    def _():
        cnt_ref[...] = jnp.zeros_like(cnt_ref)

    r_tile = r_ref[...]
    n_blk = MIX_TM // ATTN_BLOCK
    n_piece = D_MODEL // MXU_WIDTH
    rm_pieces = []

    def rm_piece(c):
        def emit():
            rm_pieces.append(jnp.dot(r_tile, wr_ref[:, c * MXU_WIDTH:(c + 1) * MXU_WIDTH],
                                     preferred_element_type=F32))
        return emit

    k_rows = jnp.concatenate([kp_ref[...], kc_ref[...]], axis=0)
    v_rows = jnp.concatenate([vp_ref[...], vc_ref[...]], axis=0)
    first_in_seq = lax.rem(i, tiles_per_seq) == 0
    blocks = []
    for blk in range(n_blk):
        rows = slice(blk * ATTN_BLOCK, (blk + 1) * ATTN_BLOCK)
        window = slice(blk * ATTN_BLOCK, (blk + 2) * ATTN_BLOCK)
        has_prev = jnp.logical_not(first_in_seq) if blk == 0 else True
        side = [rm_piece(c) for c in range(blk * n_piece // n_blk, (blk + 1) * n_piece // n_blk)]
        blocks.append(_swa_block(q_ref[rows, :], k_rows[window], v_rows[window], has_prev, bias_ref, sink_ref, side))
    rm = jnp.concatenate(rm_pieces, axis=1)
    a_t = jnp.concatenate(blocks, axis=1).astype(BF16)
    am = _dot_tn(a_t, wa_ref[...])
    mixed = _sigmoid(ga_ref[...].astype(F32)) * am + _sigmoid(gh_ref[...].astype(F32)) * rm
    x1 = x_ref[...] + jnp.dot(mixed.astype(BF16), wo_ref[...], preferred_element_type=F32)
    x1_ref[...] = x1
    h2 = x1 * lax.rsqrt(jnp.mean(x1 * x1, axis=-1, keepdims=True) + RMS_EPS) * g2_ref[...]

    _store_rows_as_tiles(h2p_ref, _pack_bf16_pairs(h2))

    hi = h2.astype(BF16)
    lo = (h2 - hi.astype(F32)).astype(BF16)
    both = _dot_nt(wr2t_ref[...], hi)
    logits = both[:ROUTE_LANES] + both[ROUTE_LANES:] + _dot_nt(wr2t_ref[:ROUTE_LANES, :], lo)
    logits = logits[:ROUTE_ROWS] + brt_ref[:ROUTE_ROWS, :]

    tm = logits.shape[1]
    row = lax.broadcasted_iota(jnp.int32, logits.shape, 0)
    row_f = row.astype(F32)
    big = float(ROUTE_ROWS)
    gl = jnp.where(row < N_GROUPS, logits, NEG_INF)
    gmax = jnp.max(gl, axis=0, keepdims=True)
    g_idx = jnp.min(jnp.where(gl == gmax, row_f, big), axis=0, keepdims=True)
    g_w = 1.0 / jnp.sum(jnp.exp(gl - gmax), axis=0, keepdims=True)
    e_row = row - N_GROUPS
    in_group = jnp.logical_and(e_row >= 0, e_row < N_EXPERTS)
    row_group = jnp.right_shift(e_row, 3).astype(F32)
    in_group = jnp.logical_and(in_group, row_group == g_idx)
    el = jnp.where(in_group, logits, NEG_INF)
    m1 = jnp.max(el, axis=0, keepdims=True)
    i1 = jnp.min(jnp.where(el == m1, row_f, big), axis=0, keepdims=True)
    el2 = jnp.where(row_f == i1, NEG_INF, el)
    m2 = jnp.max(el2, axis=0, keepdims=True)
    i2 = jnp.min(jnp.where(el2 == m2, row_f, big), axis=0, keepdims=True)
    p2 = jnp.exp(m2 - m1)
    w1 = g_w / (1.0 + p2)
    w2 = g_w * p2 / (1.0 + p2)
    e1 = i1 - N_GROUPS
    e2 = i2 - N_GROUPS

    ss = lax.broadcasted_iota(jnp.int32, (tm, tm), 0)
    tt = lax.broadcasted_iota(jnp.int32, (tm, tm), 1)
    earlier = jnp.where(ss < tt, 1.0, 0.0).astype(BF16)
    sel1 = row_f == i1
    sel2 = row_f == i2
    oh1 = jnp.where(sel1, 1.0, 0.0)
    oh2 = jnp.where(sel2, 1.0, 0.0)
    base = cnt_ref[...]
    c1 = jnp.sum(oh1, axis=1, keepdims=True)
    c2 = jnp.sum(oh2, axis=1, keepdims=True)
    pre1 = jnp.dot(oh1.astype(BF16), earlier, preferred_element_type=F32) + base
    pre2 = jnp.dot(oh2.astype(BF16), earlier, preferred_element_type=F32) + (base + c1)
    rank1 = jnp.sum(jnp.where(sel1, pre1, 0.0), axis=0, keepdims=True)
    rank2 = jnp.sum(jnp.where(sel2, pre2, 0.0), axis=0, keepdims=True)
    cnt_ref[...] = base + c1 + c2

    vals = jnp.concatenate([w1, w2, e1, e2, rank1, rank2, jnp.zeros((ROUTE_VALS - 6, tm), F32)], axis=0)
    v1 = vals.astype(BF16)
    rest = vals - v1.astype(F32)
    v2 = rest.astype(BF16)
    v3 = (rest - v2.astype(F32)).astype(BF16)
    kk = lax.broadcasted_iota(jnp.int32, (ROUTE_VALS, ROUTE_LANES), 0)
    ll = lax.broadcasted_iota(jnp.int32, (ROUTE_VALS, ROUTE_LANES), 1)
    pick = jnp.where(kk == ll, 1.0, 0.0).astype(BF16)
    route_ref[...] = (_dot_tn(v3, pick) + _dot_tn(v2, pick)) + _dot_tn(v1, pick)


def _mix(sinks, bias, r2d, proj, x2d, wa, wr, wo, g2, wr2t, brt, seq_len):
    n = x2d.shape[0]
    row = lambda i: (i, 0)
    const = lambda i: (0, 0)
    blocks_per_tile = MIX_TM // ATTN_BLOCK
    prev_block = lambda i: jnp.maximum(blocks_per_tile * i - 1, 0)
    return pl.pallas_call(
        functools.partial(_mix_kernel, tiles_per_seq=seq_len // MIX_TM),
        grid=(n // MIX_TM,),
        in_specs=[
            pl.BlockSpec(memory_space=pltpu.SMEM),
            pl.BlockSpec((MIX_TM, ATTN_WIDTH), lambda i: (i, OFF_QA // ATTN_WIDTH)),
            pl.BlockSpec((ATTN_BLOCK, KV_WIDTH), lambda i: (prev_block(i), OFF_KA // KV_WIDTH)),
            pl.BlockSpec((MIX_TM, KV_WIDTH), lambda i: (i, OFF_KA // KV_WIDTH)),
            pl.BlockSpec((ATTN_BLOCK, KV_WIDTH), lambda i: (prev_block(i), OFF_VA // KV_WIDTH)),
            pl.BlockSpec((MIX_TM, KV_WIDTH), lambda i: (i, OFF_VA // KV_WIDTH)),
            pl.BlockSpec((N_Q_HEADS, ATTN_BLOCK, ATTN_BLOCK), lambda i: (0, 0, 0)),
            pl.BlockSpec((MIX_TM, HG_WIDTH), row),
            pl.BlockSpec((MIX_TM, D_MODEL), lambda i: (i, OFF_GATE_A // D_MODEL)),
            pl.BlockSpec((MIX_TM, D_MODEL), lambda i: (i, OFF_GATE_H // D_MODEL)),
            pl.BlockSpec((MIX_TM, D_MODEL), row),
            pl.BlockSpec((ATTN_WIDTH, D_MODEL), const, pipeline_mode=pl.Buffered(1)),
            pl.BlockSpec((HG_WIDTH, D_MODEL), const, pipeline_mode=pl.Buffered(1)),
            pl.BlockSpec((D_MODEL, D_MODEL), const, pipeline_mode=pl.Buffered(1)),
            pl.BlockSpec((1, D_MODEL), const),
            pl.BlockSpec((2 * ROUTE_LANES, D_MODEL), const, pipeline_mode=pl.Buffered(1)),
            pl.BlockSpec((ROUTE_LANES, 1), const),
        ],
        out_specs=[
            pl.BlockSpec((MIX_TM, D_MODEL), row),
            pl.BlockSpec((MIX_TM * ROW_TILE, LANES), row),
            pl.BlockSpec((MIX_TM, ROUTE_LANES), row),
        ],
        out_shape=[
            jax.ShapeDtypeStruct((n, D_MODEL), F32),
            jax.ShapeDtypeStruct((n * ROW_TILE, LANES), U32),
            jax.ShapeDtypeStruct((n, ROUTE_LANES), F32),
        ],
        scratch_shapes=[pltpu.VMEM((ROUTE_ROWS, 1), F32)],
        compiler_params=pltpu.CompilerParams(dimension_semantics=("arbitrary",), vmem_limit_bytes=_vmem(60)),
        name="mix",
    )(sinks, proj, proj, proj, proj, proj, bias, r2d, proj, proj, x2d, wa, wr, wo, g2, wr2t, brt)


def _dispatch_kernel(pad_start_ref, pad_count_ref, tail_ref, dest_ref, src_ref, dst_hbm, zero_ref, sem):
    @pl.when(pl.program_id(0) == 0)
    def _():
        zero_ref[...] = jnp.zeros_like(zero_ref)
        for phase in ("start", "wait"):
            for e in range(N_EXPERTS):
                count, start = pad_count_ref[e], pad_start_ref[e]
                size = MOE_TM // 2
                while size >= 1:
                    done = jnp.bitwise_and(count, -2 * size)
                    copy = pltpu.make_async_copy(zero_ref.at[pl.ds(0, size)],
                                                 dst_hbm.at[pl.ds(start + done, size)], sem.at[1])
                    pl.when(jnp.bitwise_and(count, size) != 0)(getattr(copy, phase))
                    size //= 2
            for j in range(N_EXPERTS):
                copy = pltpu.make_async_copy(zero_ref, dst_hbm.at[pl.ds(tail_ref[0] + j * MOE_TM, MOE_TM)],
                                             sem.at[1])
                pl.when(j < tail_ref[1])(getattr(copy, phase))

    for a in range(DISPATCH_ROWS):
        pltpu.make_async_copy(src_ref.at[a // TOP_K], dst_hbm.at[dest_ref[0, 0, a]], sem.at[0]).start(priority=a % 2)
    for _ in range(TOP_K):
        pltpu.make_async_copy(src_ref, dst_hbm.at[pl.ds(0, DISPATCH_ROWS // TOP_K)], sem.at[0]).wait()


def _dispatch(dest_flat, pad_start, pad_count, tail, h2p_tiles, n_rows):
    n_asg = dest_flat.shape[0]
    steps = n_asg // DISPATCH_ROWS
    smem = pl.BlockSpec(memory_space=pltpu.SMEM)
    return pl.pallas_call(
        _dispatch_kernel,
        grid=(steps,),
        in_specs=[
            smem, smem, smem,
            pl.BlockSpec((1, 1, DISPATCH_ROWS), lambda s: (s, 0, 0), memory_space=pltpu.SMEM),
            pl.BlockSpec((DISPATCH_ROWS // TOP_K, ROW_TILE, LANES), lambda s: (s, 0, 0)),
        ],
        out_specs=pl.BlockSpec(memory_space=pl.ANY),
        out_shape=jax.ShapeDtypeStruct((n_rows, ROW_TILE, LANES), U32),
        scratch_shapes=[pltpu.VMEM((MOE_TM, ROW_TILE, LANES), U32), pltpu.SemaphoreType.DMA((2,))],
        compiler_params=pltpu.CompilerParams(dimension_semantics=("arbitrary",)),
        name="dispatch",
    )(pad_start, pad_count, tail, dest_flat.reshape(steps, 1, DISPATCH_ROWS), h2p_tiles)


def _moe_kernel(be_ref, nused_ref, first_ref, slot_ref, next_ref, next2_ref, xs_ref, wg_hbm, wu_hbm, wd_hbm, y_ref,
                wg_buf, wu_buf, wd_buf, wg_s, wu_s, wd_s, sem):
    b = pl.program_id(0)
    live = b < nused_ref[0]

    def weight_copies(e, s):
        return (pltpu.make_async_copy(wg_hbm.at[0, e], wg_buf.at[s], sem.at[s, 0]),
                pltpu.make_async_copy(wu_hbm.at[0, e], wu_buf.at[s], sem.at[s, 1]),
                pltpu.make_async_copy(wd_hbm.at[0, e], wd_buf.at[s], sem.at[s, 2]))

    @pl.when(b == 0)
    def _():
        for c in weight_copies(be_ref[0], 0):
            c.start()

        @pl.when(next_ref[0] >= 0)
        def _():
            for c in weight_copies(next_ref[0], 1):
                c.start()

    for s in range(MOE_WEIGHT_BUFFERS):
        @pl.when(jnp.logical_and(jnp.logical_and(live, first_ref[b] == 1), slot_ref[b] == s))
        def _():
            for c in weight_copies(be_ref[b], s):
                c.wait()

            @pl.when(next2_ref[b] >= 0)
            def _():
                for c in weight_copies(next2_ref[b], (s + 2) % MOE_WEIGHT_BUFFERS):
                    c.start()

            wg_s[...] = wg_buf[s].astype(BF16)
            wu_s[...] = wu_buf[s].astype(BF16)
            wd_s[...] = wd_buf[s].astype(BF16)

    @pl.when(jnp.logical_not(live))
    def _():
        y_ref[...] = jnp.zeros_like(y_ref)

    @pl.when(live)
    def _():
        xlo, xhi = _unpack_bf16_pairs(_load_rows_from_tiles(xs_ref, 0, MOE_TM))
        xb = jnp.concatenate([xlo.astype(BF16), xhi.astype(BF16)], axis=1)
        hb = []
        for c in range(EXPERT_FF // MXU_WIDTH):
            cols = slice(c * MXU_WIDTH, (c + 1) * MXU_WIDTH)
            hg = jnp.dot(xb, wg_s[:, cols], preferred_element_type=F32)
            hu = jnp.dot(xb, wu_s[:, cols], preferred_element_type=F32)
            hb.append((hg * _sigmoid(hg) * hu).astype(BF16))
        hb = jnp.concatenate(hb, axis=1)
        half = D_MODEL // 2
        for c in range(half // MXU_WIDTH):
            lo = jnp.dot(hb, wd_s[:, c * MXU_WIDTH:(c + 1) * MXU_WIDTH], preferred_element_type=F32)
            hi = jnp.dot(hb, wd_s[:, half + c * MXU_WIDTH:half + (c + 1) * MXU_WIDTH], preferred_element_type=F32)
            packed = _pack_bf16_pairs(jnp.concatenate([lo, hi], axis=1))
            for k in range(MXU_WIDTH // LANES):
                s = c * (MXU_WIDTH // LANES) + k
                y_ref[pl.ds(s, MOE_TM, stride=ROW_TILE), :] = packed[:, k * LANES:(k + 1) * LANES]


def _moe(block_e, n_used, first, slot, next_e, next2_e, xs, w_gate, w_up, w_down):
    n_rows = xs.shape[0] // ROW_TILE
    n_blocks = n_rows // MOE_TM
    grid_spec = pltpu.PrefetchScalarGridSpec(
        num_scalar_prefetch=6,
        grid=(n_blocks,),
        in_specs=[
            pl.BlockSpec((MOE_TM * ROW_TILE, LANES), lambda b, be, nu, *_: (jnp.minimum(b, nu[0] - 1), 0)),
            pl.BlockSpec(memory_space=pl.ANY),
            pl.BlockSpec(memory_space=pl.ANY),
            pl.BlockSpec(memory_space=pl.ANY),
        ],
        out_specs=pl.BlockSpec((MOE_TM * ROW_TILE, LANES), lambda b, *_: (b, 0)),
        scratch_shapes=[
            pltpu.VMEM((MOE_WEIGHT_BUFFERS, D_MODEL, EXPERT_FF), F32),
            pltpu.VMEM((MOE_WEIGHT_BUFFERS, D_MODEL, EXPERT_FF), F32),
            pltpu.VMEM((MOE_WEIGHT_BUFFERS, EXPERT_FF, D_MODEL), F32),
            pltpu.VMEM((D_MODEL, EXPERT_FF), BF16),
            pltpu.VMEM((D_MODEL, EXPERT_FF), BF16),
            pltpu.VMEM((EXPERT_FF, D_MODEL), BF16),
            pltpu.SemaphoreType.DMA((MOE_WEIGHT_BUFFERS, 3)),
        ],
    )
    return pl.pallas_call(
        _moe_kernel,
        grid_spec=grid_spec,
        out_shape=jax.ShapeDtypeStruct((n_rows * ROW_TILE, LANES), U32),
        compiler_params=pltpu.CompilerParams(dimension_semantics=("arbitrary",), vmem_limit_bytes=_vmem(52)),
        name="moe",
    )(block_e, n_used, first, slot, next_e, next2_e, xs, w_gate, w_up, w_down)


def _moe_layout(route):
    e_id = route[:, 2:2 + TOP_K].astype(jnp.int32)
    rank = route[:, 2 + TOP_K:2 + 2 * TOP_K].astype(jnp.int32)
    onehot_e = e_id[:, :, None] == jnp.arange(N_EXPERTS, dtype=jnp.int32)[None, None, :]
    counts = jnp.sum(onehot_e.astype(jnp.int32), axis=(0, 1))
    padded = (counts + MOE_TM - 1) // MOE_TM * MOE_TM
    pends = jnp.cumsum(padded)
    pstarts = pends - padded
    n_tok = route.shape[0]
    n_blocks = n_tok * TOP_K // MOE_TM + N_EXPERTS
    blk_start = jnp.arange(n_blocks, dtype=jnp.int32) * MOE_TM
    block_e = jnp.minimum(jnp.sum((blk_start[:, None] >= pends[None, :]).astype(jnp.int32), axis=1), N_EXPERTS - 1)
    n_used = (pends[-1] // MOE_TM).astype(jnp.int32).reshape(1)
    blk = jnp.arange(n_blocks, dtype=jnp.int32)
    first = jnp.logical_and(blk < n_used[0], jnp.logical_or(blk == 0, block_e != jnp.roll(block_e, 1)))
    slot = lax.rem(jnp.cumsum(first.astype(jnp.int32)) - 1, MOE_WEIGHT_BUFFERS)
    ex = jnp.arange(N_EXPERTS, dtype=jnp.int32)
    later = jnp.logical_and(counts[None, :] > 0, ex[None, :] > ex[:, None])
    next_of_expert = jnp.min(jnp.where(later, ex[None, :], N_EXPERTS), axis=1)
    next_of_expert = jnp.where(next_of_expert == N_EXPERTS, -1, next_of_expert)
    lookup = lambda keys, table: jnp.sum(jnp.where(keys[:, None] == ex[None, :], table[None, :] + 1, 0), axis=1) - 1
    next2_of_expert = lookup(next_of_expert, next_of_expert)
    next_e = lookup(block_e, next_of_expert)
    next2_e = lookup(block_e, next2_of_expert)
    dest = jnp.sum(jnp.where(onehot_e, pstarts[None, None, :], 0), axis=2) + rank
    pads = ((pstarts + counts).astype(jnp.int32), (padded - counts).astype(jnp.int32),
            jnp.stack([pends[-1], n_blocks - n_used[0]]).astype(jnp.int32))
    tables = (block_e.astype(jnp.int32), n_used, first.astype(jnp.int32), slot.astype(jnp.int32),
              next_e.astype(jnp.int32), next2_e.astype(jnp.int32))
    return tables, dest, pads, n_blocks * MOE_TM


def _final_kernel(dest_ref, destn_ref, x1_ref, route_ref, g_ref, y_hbm, y_flat_hbm, o_ref, ybuf, sem):
    i = pl.program_id(0)
    n_steps = pl.num_programs(0)
    slot = lax.rem(i, 2)
    n_rows = TOP_K * FIN_TM

    def start_gather(idx_ref, s):
        for r in range(n_rows):
            pltpu.make_async_copy(y_hbm.at[idx_ref[0, 0, r]], ybuf.at[pl.ds((s * n_rows + r) * ROW_TILE, ROW_TILE)],
                                  sem.at[s]).start(priority=1)

    @pl.when(i == 0)
    def _():
        start_gather(dest_ref, 0)

    for s in range(2):
        @pl.when(jnp.logical_and(i + 1 < n_steps, 1 - slot == s))
        def _():
            start_gather(destn_ref, s)

    slot_rows = n_rows * ROW_TILE
    slot_start = pl.multiple_of(slot * slot_rows, slot_rows)
    pltpu.make_async_copy(y_flat_hbm.at[pl.ds(0, slot_rows)], ybuf.at[pl.ds(slot_start, slot_rows)],
                          sem.at[slot]).wait()
    route = route_ref[...]
    x = x1_ref[...]
    for k in range(TOP_K):
        ylo, yhi = _unpack_bf16_pairs(_load_rows_from_tiles(ybuf, slot * n_rows + k * FIN_TM, FIN_TM))
        x = x + route[:, k:k + 1] * jnp.concatenate([ylo, yhi], axis=1)
    o_ref[...] = x * lax.rsqrt(jnp.mean(x * x, axis=-1, keepdims=True) + RMS_EPS) * g_ref[...]


def _final(dest, x1, route, g, y_sorted):
    n = x1.shape[0]
    steps = n // FIN_TM
    dest3 = dest.reshape(steps, FIN_TM, TOP_K).transpose(0, 2, 1).reshape(steps, 1, TOP_K * FIN_TM)
    return pl.pallas_call(
        _final_kernel,
        grid=(steps,),
        in_specs=[
            pl.BlockSpec((1, 1, TOP_K * FIN_TM), lambda i: (i, 0, 0), memory_space=pltpu.SMEM),
            pl.BlockSpec((1, 1, TOP_K * FIN_TM), lambda i: (jnp.minimum(i + 1, steps - 1), 0, 0),
                         memory_space=pltpu.SMEM),
            pl.BlockSpec((FIN_TM, D_MODEL), lambda i: (i, 0)),
            pl.BlockSpec((FIN_TM, ROUTE_LANES), lambda i: (i, 0)),
            pl.BlockSpec((1, D_MODEL), lambda i: (0, 0)),
            pl.BlockSpec(memory_space=pl.ANY),
            pl.BlockSpec(memory_space=pl.ANY),
        ],
        out_specs=pl.BlockSpec((FIN_TM, D_MODEL), lambda i: (i, 0)),
        out_shape=jax.ShapeDtypeStruct((n, D_MODEL), F32),
        scratch_shapes=[
            pltpu.VMEM((2 * TOP_K * FIN_TM * ROW_TILE, LANES), U32),
            pltpu.SemaphoreType.DMA((2,)),
        ],
        compiler_params=pltpu.CompilerParams(dimension_semantics=("arbitrary",), vmem_limit_bytes=_vmem(40)),
        name="final",
    )(dest3, dest3, x1, route, g, y_sorted.reshape(-1, ROW_TILE, LANES), y_sorted)


def kernel(x, norm1_g, w_in, attn_sinks, rel_bias, hg_lb_logits, hg_norm_g, w_attn_branch, w_hg_branch, w_out,
           norm2_g, w_group_router, b_group_router, w_expert_router, b_expert_router, w_gate, w_up, w_down, final_g):
    bsz, s_len, d = x.shape
    n_tok = bsz * s_len
    x2d = x.reshape(n_tok, d)

    w_route = jnp.concatenate([w_group_router[0], w_expert_router[0]], axis=1).astype(F32)
    w_route = jnp.pad(w_route, ((0, 0), (0, ROUTE_LANES - w_route.shape[1])))
    whi = w_route.astype(BF16)
    wlo = (w_route - whi.astype(F32)).astype(BF16)
    wr2t = jnp.concatenate([whi.T, wlo.T], axis=0)
    b_route = jnp.concatenate([b_group_router[0], b_expert_router[0]]).astype(F32)
    b_route = jnp.pad(b_route, (0, ROUTE_LANES - b_route.shape[0])).reshape(ROUTE_LANES, 1)
    bias = _attn_bias_table(rel_bias)

    proj = _in_proj(x2d, norm1_g[0].reshape(1, d).astype(F32), w_in)
    proj3 = proj.reshape(bsz, s_len, IN_WIDTH)
    r = _hgrn2(proj3, hg_lb_logits.astype(F32), hg_norm_g[0].reshape(1, HG_DIM).astype(F32))
    x1, h2p, route = _mix(
        attn_sinks[0].astype(F32), bias, r.reshape(n_tok, HG_WIDTH), proj, x2d,
        w_attn_branch[0].astype(BF16), w_hg_branch[0].astype(BF16), w_out[0].astype(BF16),
        norm2_g[0].reshape(1, d).astype(F32), wr2t, b_route, s_len)
    tables, dest, pads, n_rows = _moe_layout(route)
    xs = _dispatch(dest.reshape(n_tok * TOP_K), *pads, h2p.reshape(n_tok, ROW_TILE, LANES), n_rows)
    y_sorted = _moe(*tables, xs.reshape(n_rows * ROW_TILE, LANES), w_gate, w_up, w_down)
    out = _final(dest, x1, route, final_g.reshape(1, d).astype(F32), y_sorted)
    return out.reshape(bsz, s_len, d)
```

```python
import functools
import math

import jax
import jax.numpy as jnp
from jax import lax
from jax.experimental import pallas as pl
from jax.experimental.pallas import tpu as pltpu

D_MODEL = 2048
N_Q_HEADS = 16
N_KV_HEADS = 4
HEAD_DIM = 64
WINDOW = 128
ATTN_BLOCK = 128
ATTN_WIDTH = N_Q_HEADS * HEAD_DIM
KV_WIDTH = N_KV_HEADS * HEAD_DIM
REL_BUCKETS = 32
REL_MAX_DIST = 128
HG_HEADS = 8
HG_DIM = 128
HG_WIDTH = HG_HEADS * HG_DIM
HG_CHUNK = 64
HG_SUB = 16
N_GROUPS = 4
EXPERTS_PER_GROUP = 8
N_EXPERTS = N_GROUPS * EXPERTS_PER_GROUP
TOP_K = 2
EXPERT_FF = 512
RMS_EPS = 1e-6

OFF_GATE_A = 0
OFF_GATE_H = D_MODEL
OFF_QA = 2 * D_MODEL
OFF_QH = OFF_QA + ATTN_WIDTH
OFF_FH = OFF_QH + HG_WIDTH
OFF_IH = OFF_FH + HG_WIDTH
OFF_GH = OFF_IH + HG_WIDTH
OFF_KA = OFF_GH + HG_WIDTH
OFF_VA = OFF_KA + KV_WIDTH
IN_WIDTH = OFF_VA + KV_WIDTH
REF_OFF_QA = 0
REF_OFF_KA = ATTN_WIDTH
REF_OFF_QH = ATTN_WIDTH + 2 * KV_WIDTH
REF_OFF_GATE_A = REF_OFF_QH + 4 * HG_WIDTH

HG_SAFE_DECAY = 60.0

ROW_TILE = 8
LANES = 128
assert D_MODEL // 2 == ROW_TILE * LANES
MXU_WIDTH = 256

IN_TM = 2048
IN_TN = 512
HG_TC = 1024
MIX_TM = 512
ROUTE_LANES = 128
ROUTE_ROWS = 40
ROUTE_VALS = 16
assert N_GROUPS + N_EXPERTS <= ROUTE_ROWS <= ROUTE_LANES
DISPATCH_ROWS = 4096
MOE_TM = 256
MOE_WEIGHT_BUFFERS = 3
FIN_TM = 256

F32 = jnp.float32
BF16 = jnp.bfloat16
U32 = jnp.uint32
NEG_INF = float("-inf")


def _vmem(mib):
    return mib * 1024 * 1024


def _sigmoid(x):
    return 1.0 / (1.0 + jnp.exp(-x))


def _dot_nt(a, b):
    return lax.dot_general(a, b, (((1,), (1,)), ((), ())), preferred_element_type=F32)


def _dot_tn(a, b):
    return lax.dot_general(a, b, (((0,), (0,)), ((), ())), preferred_element_type=F32)


def _in_proj_kernel(x_ref, g_ref, w_ref, o_ref, h_ref):
    @pl.when(pl.program_id(1) == 0)
    def _():
        x = x_ref[...]
        ms = jnp.mean(x * x, axis=-1, keepdims=True)
        h_ref[...] = (x * lax.rsqrt(ms + RMS_EPS) * g_ref[...]).astype(BF16)

    o_ref[...] = jnp.dot(h_ref[...], w_ref[0].astype(BF16), preferred_element_type=F32).astype(o_ref.dtype)


def _in_proj(x2d, g, w_in):
    n = x2d.shape[0]
    n_col = IN_WIDTH // IN_TN

    def out_block(j):
        gates = j - REF_OFF_GATE_A // IN_TN + OFF_GATE_A // IN_TN
        q_a = j - REF_OFF_QA // IN_TN + OFF_QA // IN_TN
        kv = j - REF_OFF_KA // IN_TN + OFF_KA // IN_TN
        hg = j - REF_OFF_QH // IN_TN + OFF_QH // IN_TN
        return jnp.where(j >= REF_OFF_GATE_A // IN_TN, gates,
                         jnp.where(j >= REF_OFF_QH // IN_TN, hg, jnp.where(j >= REF_OFF_KA // IN_TN, kv, q_a)))

    return pl.pallas_call(
        _in_proj_kernel,
        grid=(n // IN_TM, n_col),
        in_specs=[
            pl.BlockSpec((IN_TM, D_MODEL), lambda i, j: (i, 0)),
            pl.BlockSpec((1, D_MODEL), lambda i, j: (0, 0)),
            pl.BlockSpec((1, D_MODEL, IN_TN), lambda i, j: (0, 0, j)),
        ],
        out_specs=pl.BlockSpec((IN_TM, IN_TN), lambda i, j: (i, out_block(j))),
        out_shape=jax.ShapeDtypeStruct((n, IN_WIDTH), BF16),
        scratch_shapes=[pltpu.VMEM((IN_TM, D_MODEL), BF16)],
        compiler_params=pltpu.CompilerParams(
            dimension_semantics=("parallel", "arbitrary"), vmem_limit_bytes=_vmem(58)),
        name="in_proj",
    )(x2d, g, w_in)


def _swa_block(q, kk, vv, has_prev, bias_ref, sink_ref, side_work):
    q = q * jnp.asarray(HEAD_DIM ** -0.5, BF16)
    key = lax.broadcasted_iota(jnp.int32, (ATTN_BLOCK, ATTN_BLOCK), 0)
    qry = lax.broadcasted_iota(jnp.int32, (ATTN_BLOCK, ATTN_BLOCK), 1)
    cur = key <= qry
    key_ok = jnp.logical_or(cur, has_prev)
    grp = N_Q_HEADS // N_KV_HEADS
    head = lambda t, h: t[:, h * HEAD_DIM:(h + 1) * HEAD_DIM]
    lanes = lambda t, g: t[:, g * ATTN_BLOCK:(g + 1) * ATTN_BLOCK]
    scores = []
    for hk in range(N_KV_HEADS):
        q_grp = jnp.concatenate([head(q, hk * grp + g) for g in range(grp)], axis=0)
        scores.append(_dot_nt(head(kk, hk), q_grp))
    emitted = 0
    probs, denoms = [], []
    for hq in range(N_Q_HEADS):
        while emitted * N_Q_HEADS < hq * len(side_work) + len(side_work):
            side_work[emitted]()
            emitted += 1
        s2 = lanes(scores[hq // grp], hq % grp)
        s = jnp.where(cur, s2[ATTN_BLOCK:], s2[:ATTN_BLOCK]) + bias_ref[hq]
        s = jnp.where(key_ok, s, NEG_INF)
        sink = sink_ref[hq]
        m = jnp.maximum(jnp.max(s, axis=0, keepdims=True), sink)
        p = jnp.exp(s - m)
        denoms.append(jnp.sum(p, axis=0, keepdims=True) + jnp.exp(sink - m))
        probs.append(jnp.concatenate([jnp.where(cur, 0.0, p), jnp.where(cur, p, 0.0)], axis=0).astype(BF16))
    outs = []
    for hk in range(N_KV_HEADS):
        p_grp = jnp.concatenate(probs[hk * grp:(hk + 1) * grp], axis=1)
        o_grp = _dot_tn(head(vv, hk), p_grp)
        outs += [lanes(o_grp, g) / denoms[hk * grp + g] for g in range(grp)]
    return jnp.concatenate(outs, axis=0)


def _t5_causal_bucket(n):
    max_exact = REL_BUCKETS // 2
    nf = jnp.maximum(n, 1).astype(F32)
    large = max_exact + (jnp.log(nf / max_exact) / math.log(REL_MAX_DIST / max_exact)
                         * (REL_BUCKETS - max_exact)).astype(jnp.int32)
    large = jnp.minimum(large, REL_BUCKETS - 1)
    return jnp.where(n < max_exact, n, large)


def _attn_bias_table(rel_bias):
    assert WINDOW == ATTN_BLOCK
    r = jnp.arange(ATTN_BLOCK)[None, :]
    c = jnp.arange(ATTN_BLOCK)[:, None]
    dist = jnp.where(c <= r, r - c, r + ATTN_BLOCK - c)
    bucket = _t5_causal_bucket(dist)
    onehot = (bucket[None, :, :] == jnp.arange(REL_BUCKETS)[:, None, None]).astype(F32)
    return jnp.einsum("bh,bqk->hqk", rel_bias.astype(F32), onehot, precision=lax.Precision.HIGHEST)


def _cumsum_rows(tri_bf16, g):
    width = g.shape[1]
    g1 = g.astype(BF16)
    r1 = g - g1.astype(F32)
    g2 = r1.astype(BF16)
    g3 = (r1 - g2.astype(F32)).astype(BF16)
    acc = jnp.dot(tri_bf16, jnp.concatenate([g3, g2, g1], axis=1), preferred_element_type=F32)
    return (acc[:, :width] + acc[:, width:2 * width]) + acc[:, 2 * width:]


def _hgrn_head_safe(q, kf, v, b, st):
    rows = lax.broadcasted_iota(jnp.int32, (HG_SUB, 1), 0)
    outs = []
    prev_end = jnp.zeros((1, HG_DIM), F32)
    for i in range(HG_CHUNK // HG_SUB):
        sl = slice(i * HG_SUB, (i + 1) * HG_SUB)
        bl = b[sl] - prev_end
        qi, ki, vi = q[sl], kf[sl], v[sl]
        vif = vi.astype(F32)
        blast = bl[HG_SUB - 1:HG_SUB]
        o = _dot_nt((qi * jnp.exp(bl)).astype(BF16), st.astype(BF16))
        for s in range(HG_SUB):
            d = jnp.exp(jnp.minimum(bl - bl[s:s + 1], 0.0))
            a = jnp.sum(qi * ki[s:s + 1] * d, axis=-1, keepdims=True)
            o = o + jnp.where(rows >= s, a, 0.0) * vif[s:s + 1]
        kd = (ki * jnp.exp(blast - bl)).astype(BF16)
        st = st * jnp.exp(blast) + _dot_tn(vi, kd)
        prev_end = b[(i + 1) * HG_SUB - 1:(i + 1) * HG_SUB]
        outs.append(o)
    return jnp.concatenate(outs, axis=0), st


def _hgrn_kernel(q_ref, f_ref, i_ref, g_ref, lbl_ref, gn_ref, o_ref, st_ref, kf_ref, b_ref):
    @pl.when(pl.program_id(1) == 0)
    def _():
        st_ref[...] = jnp.zeros_like(st_ref)

    lg2 = lbl_ref[...].astype(F32)
    e = jnp.exp(lg2 - jnp.max(lg2, axis=0, keepdims=True))
    lb = e[0:1] / jnp.sum(e, axis=0, keepdims=True)
    gn = gn_ref[...].astype(F32)
    r = lax.broadcasted_iota(jnp.int32, (HG_CHUNK, HG_CHUNK), 0)
    c = lax.broadcasted_iota(jnp.int32, (HG_CHUNK, HG_CHUNK), 1)
    tri_mask = r >= c
    tri = jnp.where(tri_mask, 1.0, 0.0).astype(BF16)

    n_chunks = HG_TC // HG_CHUNK

    f = lb + (1.0 - lb) * _sigmoid(f_ref[0].astype(F32))
    kf_ref[...] = 1.0 - f
    logf = jnp.log(f)
    worst = None
    for ci in range(n_chunks):
        sl = slice(ci * HG_CHUNK, (ci + 1) * HG_CHUNK)
        b = _cumsum_rows(tri, logf[sl])
        b_ref[sl, :] = b
        bend = b[HG_CHUNK - 1:HG_CHUNK]
        worst = bend if worst is None else jnp.minimum(worst, bend)

    heads = range(HG_HEADS)
    head = lambda t, h: t[:, h * HG_DIM:(h + 1) * HG_DIM]

    def finish_all(sl, outs):
        scale = [lax.rsqrt(jnp.mean(o * o, axis=-1, keepdims=True) + RMS_EPS) for o in outs]
        normed = jnp.concatenate([outs[h] * scale[h] * gn for h in heads], axis=1)
        go = g_ref[0, sl, :].astype(F32)
        o_ref[0, sl, :] = (normed * (go * _sigmoid(go))).astype(o_ref.dtype)

    def fast():
        states = [st_ref[h] for h in heads]
        for ci in range(n_chunks):
            sl = slice(ci * HG_CHUNK, (ci + 1) * HG_CHUNK)
            b = b_ref[sl, :]
            qp = (q_ref[0, sl, :].astype(F32) * jnp.exp(b)).astype(BF16)
            kp = kf_ref[sl, :] * jnp.exp(-b)
            kpb = kp.astype(BF16)
            eb = jnp.exp(b[HG_CHUNK - 1:HG_CHUNK])
            kd = (kp * eb).astype(BF16)
            v = i_ref[0, sl, :]
            att = [_dot_nt(head(qp, h), head(kpb, h)) for h in heads]
            inter = [_dot_nt(head(qp, h), states[h].astype(BF16)) for h in heads]
            update = [_dot_tn(head(v, h), head(kd, h)) for h in heads]
            att = [jnp.where(tri_mask, a, 0.0).astype(BF16) for a in att]
            outs = [inter[h] + jnp.dot(att[h], head(v, h), preferred_element_type=F32) for h in heads]
            states = [states[h] * head(eb, h) + update[h] for h in heads]
            finish_all(sl, outs)
        for h in heads:
            st_ref[h] = states[h]

    def safe():
        def chunk(ci, carry):
            sl = pl.ds(pl.multiple_of(ci * HG_CHUNK, HG_CHUNK), HG_CHUNK)
            outs = []
            for h in heads:
                hs = slice(h * HG_DIM, (h + 1) * HG_DIM)
                o, st_new = _hgrn_head_safe(q_ref[0, sl, hs].astype(F32), kf_ref[sl, hs], i_ref[0, sl, hs],
                                            b_ref[sl, hs], st_ref[h])
                st_ref[h] = st_new
                outs.append(o)
            finish_all(sl, outs)
            return carry

        lax.fori_loop(0, n_chunks, chunk, 0)

    lax.cond(jnp.min(worst) >= -HG_SAFE_DECAY, fast, safe)


def _hgrn2(proj3, lb_logits, gn):
    bsz, s_len, _ = proj3.shape
    blk = lambda off: (lambda b, t: (b, t, off // HG_WIDTH))
    return pl.pallas_call(
        _hgrn_kernel,
        grid=(bsz, s_len // HG_TC),
        in_specs=[
            pl.BlockSpec((1, HG_TC, HG_WIDTH), blk(OFF_QH)),
            pl.BlockSpec((1, HG_TC, HG_WIDTH), blk(OFF_FH)),
            pl.BlockSpec((1, HG_TC, HG_WIDTH), blk(OFF_IH)),
            pl.BlockSpec((1, HG_TC, HG_WIDTH), blk(OFF_GH)),
            pl.BlockSpec((lb_logits.shape[0], HG_WIDTH), lambda b, t: (0, 0)),
            pl.BlockSpec((1, HG_DIM), lambda b, t: (0, 0)),
        ],
        out_specs=pl.BlockSpec((1, HG_TC, HG_WIDTH), lambda b, t: (b, t, 0)),
        out_shape=jax.ShapeDtypeStruct((bsz, s_len, HG_WIDTH), BF16),
        scratch_shapes=[
            pltpu.VMEM((HG_HEADS, HG_DIM, HG_DIM), F32),
            pltpu.VMEM((HG_TC, HG_WIDTH), F32),
            pltpu.VMEM((HG_TC, HG_WIDTH), F32),
        ],
        compiler_params=pltpu.CompilerParams(
            dimension_semantics=("parallel", "arbitrary"), vmem_limit_bytes=_vmem(48)),
        name="hgrn2",
    )(proj3, proj3, proj3, proj3, lb_logits, gn)


def _store_rows_as_tiles(ref, val):
    rows = val.shape[0]
    for s in range(ROW_TILE):
        ref[pl.ds(s, rows, stride=ROW_TILE), :] = val[:, s * LANES:(s + 1) * LANES]


def _load_rows_from_tiles(ref, start, rows):
    return jnp.concatenate(
        [ref[pl.ds(start * ROW_TILE + s, rows, stride=ROW_TILE), :] for s in range(ROW_TILE)], axis=1)


def _pack_bf16_pairs(x):
    c = x.shape[1] // 2
    bits = pltpu.bitcast(x.astype(BF16).astype(F32), U32)
    return jnp.bitwise_or(jnp.right_shift(bits[:, :c], jnp.uint32(16)), bits[:, c:])


def _unpack_bf16_pairs(p):
    lo = pltpu.bitcast(jnp.left_shift(p, jnp.uint32(16)), F32)
    hi = pltpu.bitcast(jnp.bitwise_and(p, jnp.uint32(0xFFFF0000)), F32)
    return lo, hi


def _mix_kernel(sink_ref, q_ref, kp_ref, kc_ref, vp_ref, vc_ref, bias_ref,
                r_ref, ga_ref, gh_ref, x_ref, wa_ref, wr_ref, wo_ref, g2_ref, wr2t_ref, brt_ref,
                x1_ref, h2p_ref, route_ref, cnt_ref, *, tiles_per_seq):
    i = pl.program_id(0)

    @pl.when(i == 0)
    def _():
        cnt_ref[...] = jnp.zeros_like(cnt_ref)

    r_tile = r_ref[...]
    n_blk = MIX_TM // ATTN_BLOCK
    n_piece = D_MODEL // MXU_WIDTH
    rm_pieces = []

    def rm_piece(c):
        def emit():
            rm_pieces.append(jnp.dot(r_tile, wr_ref[:, c * MXU_WIDTH:(c + 1) * MXU_WIDTH],
                                     preferred_element_type=F32))
        return emit

    k_rows = jnp.concatenate([kp_ref[...], kc_ref[...]], axis=0)
    v_rows = jnp.concatenate([vp_ref[...], vc_ref[...]], axis=0)
    first_in_seq = lax.rem(i, tiles_per_seq) == 0
    blocks = []
    for blk in range(n_blk):
        rows = slice(blk * ATTN_BLOCK, (blk + 1) * ATTN_BLOCK)
        window = slice(blk * ATTN_BLOCK, (blk + 2) * ATTN_BLOCK)
        has_prev = jnp.logical_not(first_in_seq) if blk == 0 else True
        side = [rm_piece(c) for c in range(blk * n_piece // n_blk, (blk + 1) * n_piece // n_blk)]
        blocks.append(_swa_block(q_ref[rows, :], k_rows[window], v_rows[window], has_prev, bias_ref, sink_ref, side))
    rm = jnp.concatenate(rm_pieces, axis=1)
    a_t = jnp.concatenate(blocks, axis=1).astype(BF16)
    am = _dot_tn(a_t, wa_ref[...])
    mixed = _sigmoid(ga_ref[...].astype(F32)) * am + _sigmoid(gh_ref[...].astype(F32)) * rm
    x1 = x_ref[...] + jnp.dot(mixed.astype(BF16), wo_ref[...], preferred_element_type=F32)
    x1_ref[...] = x1
    h2 = x1 * lax.rsqrt(jnp.mean(x1 * x1, axis=-1, keepdims=True) + RMS_EPS) * g2_ref[...]

    _store_rows_as_tiles(h2p_ref, _pack_bf16_pairs(h2))

    hi = h2.astype(BF16)
    lo = (h2 - hi.astype(F32)).astype(BF16)
    both = _dot_nt(wr2t_ref[...], hi)
    logits = both[:ROUTE_LANES] + both[ROUTE_LANES:] + _dot_nt(wr2t_ref[:ROUTE_LANES, :], lo)
    logits = logits[:ROUTE_ROWS] + brt_ref[:ROUTE_ROWS, :]

    tm = logits.shape[1]
    row = lax.broadcasted_iota(jnp.int32, logits.shape, 0)
    row_f = row.astype(F32)
    big = float(ROUTE_ROWS)
    gl = jnp.where(row < N_GROUPS, logits, NEG_INF)
    gmax = jnp.max(gl, axis=0, keepdims=True)
    g_idx = jnp.min(jnp.where(gl == gmax, row_f, big), axis=0, keepdims=True)
    g_w = 1.0 / jnp.sum(jnp.exp(gl - gmax), axis=0, keepdims=True)
    e_row = row - N_GROUPS
    in_group = jnp.logical_and(e_row >= 0, e_row < N_EXPERTS)
    row_group = jnp.right_shift(e_row, 3).astype(F32)
    in_group = jnp.logical_and(in_group, row_group == g_idx)
    el = jnp.where(in_group, logits, NEG_INF)
    m1 = jnp.max(el, axis=0, keepdims=True)
    i1 = jnp.min(jnp.where(el == m1, row_f, big), axis=0, keepdims=True)
    el2 = jnp.where(row_f == i1, NEG_INF, el)
    m2 = jnp.max(el2, axis=0, keepdims=True)
    i2 = jnp.min(jnp.where(el2 == m2, row_f, big), axis=0, keepdims=True)
    p2 = jnp.exp(m2 - m1)
    w1 = g_w / (1.0 + p2)
    w2 = g_w * p2 / (1.0 + p2)
    e1 = i1 - N_GROUPS
    e2 = i2 - N_GROUPS

    ss = lax.broadcasted_iota(jnp.int32, (tm, tm), 0)
    tt = lax.broadcasted_iota(jnp.int32, (tm, tm), 1)
    earlier = jnp.where(ss < tt, 1.0, 0.0).astype(BF16)
    sel1 = row_f == i1
    sel2 = row_f == i2
    oh1 = jnp.where(sel1, 1.0, 0.0)
    oh2 = jnp.where(sel2, 1.0, 0.0)
    base = cnt_ref[...]
    c1 = jnp.sum(oh1, axis=1, keepdims=True)
    c2 = jnp.sum(oh2, axis=1, keepdims=True)
    pre1 = jnp.dot(oh1.astype(BF16), earlier, preferred_element_type=F32) + base
    pre2 = jnp.dot(oh2.astype(BF16), earlier, preferred_element_type=F32) + (base + c1)
    rank1 = jnp.sum(jnp.where(sel1, pre1, 0.0), axis=0, keepdims=True)
    rank2 = jnp.sum(jnp.where(sel2, pre2, 0.0), axis=0, keepdims=True)
    cnt_ref[...] = base + c1 + c2

    vals = jnp.concatenate([w1, w2, e1, e2, rank1, rank2, jnp.zeros((ROUTE_VALS - 6, tm), F32)], axis=0)
    v1 = vals.astype(BF16)
    rest = vals - v1.astype(F32)
    v2 = rest.astype(BF16)
    v3 = (rest - v2.astype(F32)).astype(BF16)
    kk = lax.broadcasted_iota(jnp.int32, (ROUTE_VALS, ROUTE_LANES), 0)
    ll = lax.broadcasted_iota(jnp.int32, (ROUTE_VALS, ROUTE_LANES), 1)
    pick = jnp.where(kk == ll, 1.0, 0.0).astype(BF16)
    route_ref[...] = (_dot_tn(v3, pick) + _dot_tn(v2, pick)) + _dot_tn(v1, pick)


def _mix(sinks, bias, r2d, proj, x2d, wa, wr, wo, g2, wr2t, brt, seq_len):
    n = x2d.shape[0]
    row = lambda i: (i, 0)
    const = lambda i: (0, 0)
    blocks_per_tile = MIX_TM // ATTN_BLOCK
    prev_block = lambda i: jnp.maximum(blocks_per_tile * i - 1, 0)
    return pl.pallas_call(
        functools.partial(_mix_kernel, tiles_per_seq=seq_len // MIX_TM),
        grid=(n // MIX_TM,),
        in_specs=[
            pl.BlockSpec(memory_space=pltpu.SMEM),
            pl.BlockSpec((MIX_TM, ATTN_WIDTH), lambda i: (i, OFF_QA // ATTN_WIDTH)),
            pl.BlockSpec((ATTN_BLOCK, KV_WIDTH), lambda i: (prev_block(i), OFF_KA // KV_WIDTH)),
            pl.BlockSpec((MIX_TM, KV_WIDTH), lambda i: (i, OFF_KA // KV_WIDTH)),
            pl.BlockSpec((ATTN_BLOCK, KV_WIDTH), lambda i: (prev_block(i), OFF_VA // KV_WIDTH)),
            pl.BlockSpec((MIX_TM, KV_WIDTH), lambda i: (i, OFF_VA // KV_WIDTH)),
            pl.BlockSpec((N_Q_HEADS, ATTN_BLOCK, ATTN_BLOCK), lambda i: (0, 0, 0)),
            pl.BlockSpec((MIX_TM, HG_WIDTH), row),
            pl.BlockSpec((MIX_TM, D_MODEL), lambda i: (i, OFF_GATE_A // D_MODEL)),
            pl.BlockSpec((MIX_TM, D_MODEL), lambda i: (i, OFF_GATE_H // D_MODEL)),
            pl.BlockSpec((MIX_TM, D_MODEL), row),
            pl.BlockSpec((ATTN_WIDTH, D_MODEL), const, pipeline_mode=pl.Buffered(1)),
            pl.BlockSpec((HG_WIDTH, D_MODEL), const, pipeline_mode=pl.Buffered(1)),
            pl.BlockSpec((D_MODEL, D_MODEL), const, pipeline_mode=pl.Buffered(1)),
            pl.BlockSpec((1, D_MODEL), const),
            pl.BlockSpec((2 * ROUTE_LANES, D_MODEL), const, pipeline_mode=pl.Buffered(1)),
            pl.BlockSpec((ROUTE_LANES, 1), const),
        ],
        out_specs=[
            pl.BlockSpec((MIX_TM, D_MODEL), row),
            pl.BlockSpec((MIX_TM * ROW_TILE, LANES), row),
            pl.BlockSpec((MIX_TM, ROUTE_LANES), row),
        ],
        out_shape=[
            jax.ShapeDtypeStruct((n, D_MODEL), F32),
            jax.ShapeDtypeStruct((n * ROW_TILE, LANES), U32),
            jax.ShapeDtypeStruct((n, ROUTE_LANES), F32),
        ],
        scratch_shapes=[pltpu.VMEM((ROUTE_ROWS, 1), F32)],
        compiler_params=pltpu.CompilerParams(dimension_semantics=("arbitrary",), vmem_limit_bytes=_vmem(60)),
        name="mix",
    )(sinks, proj, proj, proj, proj, proj, bias, r2d, proj, proj, x2d, wa, wr, wo, g2, wr2t, brt)


def _dispatch_kernel(pad_start_ref, pad_count_ref, tail_ref, dest_ref, src_ref, dst_hbm, zero_ref, sem):
    @pl.when(pl.program_id(0) == 0)
    def _():
        zero_ref[...] = jnp.zeros_like(zero_ref)
        for phase in ("start", "wait"):
            for e in range(N_EXPERTS):
                count, start = pad_count_ref[e], pad_start_ref[e]
                size = MOE_TM // 2
                while size >= 1:
                    done = jnp.bitwise_and(count, -2 * size)
                    copy = pltpu.make_async_copy(zero_ref.at[pl.ds(0, size)],
                                                 dst_hbm.at[pl.ds(start + done, size)], sem.at[1])
                    pl.when(jnp.bitwise_and(count, size) != 0)(getattr(copy, phase))
                    size //= 2
            for j in range(N_EXPERTS):
                copy = pltpu.make_async_copy(zero_ref, dst_hbm.at[pl.ds(tail_ref[0] + j * MOE_TM, MOE_TM)],
                                             sem.at[1])
                pl.when(j < tail_ref[1])(getattr(copy, phase))

    for a in range(DISPATCH_ROWS):
        pltpu.make_async_copy(src_ref.at[a // TOP_K], dst_hbm.at[dest_ref[0, 0, a]], sem.at[0]).start(priority=a % 2)
    for _ in range(TOP_K):
        pltpu.make_async_copy(src_ref, dst_hbm.at[pl.ds(0, DISPATCH_ROWS // TOP_K)], sem.at[0]).wait()


def _dispatch(dest_flat, pad_start, pad_count, tail, h2p_tiles, n_rows):
    n_asg = dest_flat.shape[0]
    steps = n_asg // DISPATCH_ROWS
    smem = pl.BlockSpec(memory_space=pltpu.SMEM)
    return pl.pallas_call(
        _dispatch_kernel,
        grid=(steps,),
        in_specs=[
            smem, smem, smem,
            pl.BlockSpec((1, 1, DISPATCH_ROWS), lambda s: (s, 0, 0), memory_space=pltpu.SMEM),
            pl.BlockSpec((DISPATCH_ROWS // TOP_K, ROW_TILE, LANES), lambda s: (s, 0, 0)),
        ],
        out_specs=pl.BlockSpec(memory_space=pl.ANY),
        out_shape=jax.ShapeDtypeStruct((n_rows, ROW_TILE, LANES), U32),
        scratch_shapes=[pltpu.VMEM((MOE_TM, ROW_TILE, LANES), U32), pltpu.SemaphoreType.DMA((2,))],
        compiler_params=pltpu.CompilerParams(dimension_semantics=("arbitrary",), vmem_limit_bytes=_vmem(32)),
        name="dispatch",
    )(pad_start, pad_count, tail, dest_flat.reshape(steps, 1, DISPATCH_ROWS), h2p_tiles)


def _moe_kernel(be_ref, nused_ref, first_ref, slot_ref, next_ref, next2_ref, xs_ref, wg_hbm, wu_hbm, wd_hbm, y_ref,
                wg_buf, wu_buf, wd_buf, wg_s, wu_s, wd_s, sem):
    b = pl.program_id(0)
    live = b < nused_ref[0]

    def weight_copies(e, s):
        return (pltpu.make_async_copy(wg_hbm.at[0, e], wg_buf.at[s], sem.at[s, 0]),
                pltpu.make_async_copy(wu_hbm.at[0, e], wu_buf.at[s], sem.at[s, 1]),
                pltpu.make_async_copy(wd_hbm.at[0, e], wd_buf.at[s], sem.at[s, 2]))

    @pl.when(b == 0)
    def _():
        for c in weight_copies(be_ref[0], 0):
            c.start()

        @pl.when(next_ref[0] >= 0)
        def _():
            for c in weight_copies(next_ref[0], 1):
                c.start()

    for s in range(MOE_WEIGHT_BUFFERS):
        @pl.when(jnp.logical_and(jnp.logical_and(live, first_ref[b] == 1), slot_ref[b] == s))
        def _():
            for c in weight_copies(be_ref[b], s):
                c.wait()

            @pl.when(next2_ref[b] >= 0)
            def _():
                for c in weight_copies(next2_ref[b], (s + 2) % MOE_WEIGHT_BUFFERS):
                    c.start()

            wg_s[...] = wg_buf[s].astype(BF16)
            wu_s[...] = wu_buf[s].astype(BF16)
            wd_s[...] = wd_buf[s].astype(BF16)

    @pl.when(jnp.logical_not(live))
    def _():
        y_ref[...] = jnp.zeros_like(y_ref)

    @pl.when(live)
    def _():
        xlo, xhi = _unpack_bf16_pairs(_load_rows_from_tiles(xs_ref, 0, MOE_TM))
        xb = jnp.concatenate([xlo.astype(BF16), xhi.astype(BF16)], axis=1)
        hb = []
        for c in range(EXPERT_FF // MXU_WIDTH):
            cols = slice(c * MXU_WIDTH, (c + 1) * MXU_WIDTH)
            hg = jnp.dot(xb, wg_s[:, cols], preferred_element_type=F32)
            hu = jnp.dot(xb, wu_s[:, cols], preferred_element_type=F32)
            hb.append((hg * _sigmoid(hg) * hu).astype(BF16))
        hb = jnp.concatenate(hb, axis=1)
        half = D_MODEL // 2
        for c in range(half // MXU_WIDTH):
            lo = jnp.dot(hb, wd_s[:, c * MXU_WIDTH:(c + 1) * MXU_WIDTH], preferred_element_type=F32)
            hi = jnp.dot(hb, wd_s[:, half + c * MXU_WIDTH:half + (c + 1) * MXU_WIDTH], preferred_element_type=F32)
            packed = _pack_bf16_pairs(jnp.concatenate([lo, hi], axis=1))
            for k in range(MXU_WIDTH // LANES):
                s = c * (MXU_WIDTH // LANES) + k
                y_ref[pl.ds(s, MOE_TM, stride=ROW_TILE), :] = packed[:, k * LANES:(k + 1) * LANES]


def _moe(block_e, n_used, first, slot, next_e, next2_e, xs, w_gate, w_up, w_down):
    n_rows = xs.shape[0] // ROW_TILE
    n_blocks = n_rows // MOE_TM
    grid_spec = pltpu.PrefetchScalarGridSpec(
        num_scalar_prefetch=6,
        grid=(n_blocks,),
        in_specs=[
            pl.BlockSpec((MOE_TM * ROW_TILE, LANES), lambda b, be, nu, *_: (jnp.minimum(b, nu[0] - 1), 0)),
            pl.BlockSpec(memory_space=pl.ANY),
            pl.BlockSpec(memory_space=pl.ANY),
            pl.BlockSpec(memory_space=pl.ANY),
        ],
        out_specs=pl.BlockSpec((MOE_TM * ROW_TILE, LANES), lambda b, *_: (b, 0)),
        scratch_shapes=[
            pltpu.VMEM((MOE_WEIGHT_BUFFERS, D_MODEL, EXPERT_FF), F32),
            pltpu.VMEM((MOE_WEIGHT_BUFFERS, D_MODEL, EXPERT_FF), F32),
            pltpu.VMEM((MOE_WEIGHT_BUFFERS, EXPERT_FF, D_MODEL), F32),
            pltpu.VMEM((D_MODEL, EXPERT_FF), BF16),
            pltpu.VMEM((D_MODEL, EXPERT_FF), BF16),
            pltpu.VMEM((EXPERT_FF, D_MODEL), BF16),
            pltpu.SemaphoreType.DMA((MOE_WEIGHT_BUFFERS, 3)),
        ],
    )
    return pl.pallas_call(
        _moe_kernel,
        grid_spec=grid_spec,
        out_shape=jax.ShapeDtypeStruct((n_rows * ROW_TILE, LANES), U32),
        compiler_params=pltpu.CompilerParams(dimension_semantics=("arbitrary",), vmem_limit_bytes=_vmem(52)),
        name="moe",
    )(block_e, n_used, first, slot, next_e, next2_e, xs, w_gate, w_up, w_down)


def _moe_layout(route):
    e_id = route[:, 2:2 + TOP_K].astype(jnp.int32)
    rank = route[:, 2 + TOP_K:2 + 2 * TOP_K].astype(jnp.int32)
    onehot_e = e_id[:, :, None] == jnp.arange(N_EXPERTS, dtype=jnp.int32)[None, None, :]
    counts = jnp.sum(onehot_e.astype(jnp.int32), axis=(0, 1))
    padded = (counts + MOE_TM - 1) // MOE_TM * MOE_TM
    pends = jnp.cumsum(padded)
    pstarts = pends - padded
    n_tok = route.shape[0]
    n_blocks = n_tok * TOP_K // MOE_TM + N_EXPERTS
    blk_start = jnp.arange(n_blocks, dtype=jnp.int32) * MOE_TM
    block_e = jnp.minimum(jnp.sum((blk_start[:, None] >= pends[None, :]).astype(jnp.int32), axis=1), N_EXPERTS - 1)
    n_used = (pends[-1] // MOE_TM).astype(jnp.int32).reshape(1)
    blk = jnp.arange(n_blocks, dtype=jnp.int32)
    first = jnp.logical_and(blk < n_used[0], jnp.logical_or(blk == 0, block_e != jnp.roll(block_e, 1)))
    slot = lax.rem(jnp.cumsum(first.astype(jnp.int32)) - 1, MOE_WEIGHT_BUFFERS)
    ex = jnp.arange(N_EXPERTS, dtype=jnp.int32)
    later = jnp.logical_and(counts[None, :] > 0, ex[None, :] > ex[:, None])
    next_of_expert = jnp.min(jnp.where(later, ex[None, :], N_EXPERTS), axis=1)
    next_of_expert = jnp.where(next_of_expert == N_EXPERTS, -1, next_of_expert)
    lookup = lambda keys, table: jnp.sum(jnp.where(keys[:, None] == ex[None, :], table[None, :] + 1, 0), axis=1) - 1
    next2_of_expert = lookup(next_of_expert, next_of_expert)
    next_e = lookup(block_e, next_of_expert)
    next2_e = lookup(block_e, next2_of_expert)
    dest = jnp.sum(jnp.where(onehot_e, pstarts[None, None, :], 0), axis=2) + rank
    pads = ((pstarts + counts).astype(jnp.int32), (padded - counts).astype(jnp.int32),
            jnp.stack([pends[-1], n_blocks - n_used[0]]).astype(jnp.int32))
    tables = (block_e.astype(jnp.int32), n_used, first.astype(jnp.int32), slot.astype(jnp.int32),
              next_e.astype(jnp.int32), next2_e.astype(jnp.int32))
    return tables, dest, pads, n_blocks * MOE_TM


def _final_kernel(dest_ref, destn_ref, x1_ref, route_ref, g_ref, y_hbm, y_flat_hbm, o_ref, ybuf, sem):
    i = pl.program_id(0)
    n_steps = pl.num_programs(0)
    slot = lax.rem(i, 2)
    n_rows = TOP_K * FIN_TM

    def start_gather(idx_ref, s):
        for r in range(n_rows):
            pltpu.make_async_copy(y_hbm.at[idx_ref[0, 0, r]], ybuf.at[pl.ds((s * n_rows + r) * ROW_TILE, ROW_TILE)],
                                  sem.at[s]).start(priority=r % 2)

    @pl.when(i == 0)
    def _():
        start_gather(dest_ref, 0)

    for s in range(2):
        @pl.when(jnp.logical_and(i + 1 < n_steps, 1 - slot == s))
        def _():
            start_gather(destn_ref, s)

    slot_rows = n_rows * ROW_TILE
    slot_start = pl.multiple_of(slot * slot_rows, slot_rows)
    pltpu.make_async_copy(y_flat_hbm.at[pl.ds(0, slot_rows)], ybuf.at[pl.ds(slot_start, slot_rows)],
                          sem.at[slot]).wait()
    route = route_ref[...]
    x = x1_ref[...]
    for k in range(TOP_K):
        ylo, yhi = _unpack_bf16_pairs(_load_rows_from_tiles(ybuf, slot * n_rows + k * FIN_TM, FIN_TM))
        x = x + route[:, k:k + 1] * jnp.concatenate([ylo, yhi], axis=1)
    o_ref[...] = x * lax.rsqrt(jnp.mean(x * x, axis=-1, keepdims=True) + RMS_EPS) * g_ref[...]


def _final(dest, x1, route, g, y_sorted):
    n = x1.shape[0]
    steps = n // FIN_TM
    dest3 = dest.reshape(steps, FIN_TM, TOP_K).transpose(0, 2, 1).reshape(steps, 1, TOP_K * FIN_TM)
    return pl.pallas_call(
        _final_kernel,
        grid=(steps,),
        in_specs=[
            pl.BlockSpec((1, 1, TOP_K * FIN_TM), lambda i: (i, 0, 0), memory_space=pltpu.SMEM),
            pl.BlockSpec((1, 1, TOP_K * FIN_TM), lambda i: (jnp.minimum(i + 1, steps - 1), 0, 0),
                         memory_space=pltpu.SMEM),
            pl.BlockSpec((FIN_TM, D_MODEL), lambda i: (i, 0)),
            pl.BlockSpec((FIN_TM, ROUTE_LANES), lambda i: (i, 0)),
            pl.BlockSpec((1, D_MODEL), lambda i: (0, 0)),
            pl.BlockSpec(memory_space=pl.ANY),
            pl.BlockSpec(memory_space=pl.ANY),
        ],
        out_specs=pl.BlockSpec((FIN_TM, D_MODEL), lambda i: (i, 0)),
        out_shape=jax.ShapeDtypeStruct((n, D_MODEL), F32),
        scratch_shapes=[
            pltpu.VMEM((2 * TOP_K * FIN_TM * ROW_TILE, LANES), U32),
            pltpu.SemaphoreType.DMA((2,)),
        ],
        compiler_params=pltpu.CompilerParams(dimension_semantics=("arbitrary",), vmem_limit_bytes=_vmem(40)),
        name="final",
    )(dest3, dest3, x1, route, g, y_sorted.reshape(-1, ROW_TILE, LANES), y_sorted)


def kernel(x, norm1_g, w_in, attn_sinks, rel_bias, hg_lb_logits, hg_norm_g, w_attn_branch, w_hg_branch, w_out,
           norm2_g, w_group_router, b_group_router, w_expert_router, b_expert_router, w_gate, w_up, w_down, final_g):
    bsz, s_len, d = x.shape
    n_tok = bsz * s_len
    x2d = x.reshape(n_tok, d)

    w_route = jnp.concatenate([w_group_router[0], w_expert_router[0]], axis=1).astype(F32)
    w_route = jnp.pad(w_route, ((0, 0), (0, ROUTE_LANES - w_route.shape[1])))
    whi = w_route.astype(BF16)
    wlo = (w_route - whi.astype(F32)).astype(BF16)
    wr2t = jnp.concatenate([whi.T, wlo.T], axis=0)
    b_route = jnp.concatenate([b_group_router[0], b_expert_router[0]]).astype(F32)
    b_route = jnp.pad(b_route, (0, ROUTE_LANES - b_route.shape[0])).reshape(ROUTE_LANES, 1)
    bias = _attn_bias_table(rel_bias)

    proj = _in_proj(x2d, norm1_g[0].reshape(1, d).astype(F32), w_in)
    proj3 = proj.reshape(bsz, s_len, IN_WIDTH)
    r = _hgrn2(proj3, hg_lb_logits.astype(F32), hg_norm_g[0].reshape(1, HG_DIM).astype(F32))
    x1, h2p, route = _mix(
        attn_sinks[0].astype(F32), bias, r.reshape(n_tok, HG_WIDTH), proj, x2d,
        w_attn_branch[0].astype(BF16), w_hg_branch[0].astype(BF16), w_out[0].astype(BF16),
        norm2_g[0].reshape(1, d).astype(F32), wr2t, b_route, s_len)
    tables, dest, pads, n_rows = _moe_layout(route)
    xs = _dispatch(dest.reshape(n_tok * TOP_K), *pads, h2p.reshape(n_tok, ROW_TILE, LANES), n_rows)
    y_sorted = _moe(*tables, xs.reshape(n_rows * ROW_TILE, LANES), w_gate, w_up, w_down)
    out = _final(dest, x1, route, final_g.reshape(1, d).astype(F32), y_sorted)
    return out.reshape(bsz, s_len, d)
```

```python
import functools
import math

import jax
import jax.numpy as jnp
from jax import lax
from jax.experimental import pallas as pl
from jax.experimental.pallas import tpu as pltpu

D_MODEL = 2048
N_Q_HEADS = 16
N_KV_HEADS = 4
HEAD_DIM = 64
WINDOW = 128
ATTN_BLOCK = 128
ATTN_WIDTH = N_Q_HEADS * HEAD_DIM
KV_WIDTH = N_KV_HEADS * HEAD_DIM
REL_BUCKETS = 32
REL_MAX_DIST = 128
HG_HEADS = 8
HG_DIM = 128
HG_WIDTH = HG_HEADS * HG_DIM
HG_CHUNK = 64
HG_SUB = 16
N_GROUPS = 4
EXPERTS_PER_GROUP = 8
N_EXPERTS = N_GROUPS * EXPERTS_PER_GROUP
TOP_K = 2
EXPERT_FF = 512
RMS_EPS = 1e-6

OFF_GATE_A = 0
OFF_GATE_H = D_MODEL
OFF_QA = 2 * D_MODEL
OFF_QH = OFF_QA + ATTN_WIDTH
OFF_FH = OFF_QH + HG_WIDTH
OFF_IH = OFF_FH + HG_WIDTH
OFF_GH = OFF_IH + HG_WIDTH
OFF_KA = OFF_GH + HG_WIDTH
OFF_VA = OFF_KA + KV_WIDTH
IN_WIDTH = OFF_VA + KV_WIDTH
REF_OFF_QA = 0
REF_OFF_KA = ATTN_WIDTH
REF_OFF_QH = ATTN_WIDTH + 2 * KV_WIDTH
REF_OFF_GATE_A = REF_OFF_QH + 4 * HG_WIDTH

HG_SAFE_DECAY = 60.0

ROW_TILE = 8
LANES = 128
assert D_MODEL // 2 == ROW_TILE * LANES
MXU_WIDTH = 256

IN_TM = 2048
IN_TN = 512
HG_TC = 1024
MIX_TM = 512
ROUTE_LANES = 128
ROUTE_ROWS = 40
ROUTE_VALS = 16
assert N_GROUPS + N_EXPERTS <= ROUTE_ROWS <= ROUTE_LANES
DISPATCH_ROWS = 4096
MOE_TM = 256
MOE_WEIGHT_BUFFERS = 3
FIN_TM = 256

F32 = jnp.float32
BF16 = jnp.bfloat16
U32 = jnp.uint32
NEG_INF = float("-inf")


def _vmem(mib):
    return mib * 1024 * 1024


def _sigmoid(x):
    return 1.0 / (1.0 + jnp.exp(-x))


def _dot_nt(a, b):
    return lax.dot_general(a, b, (((1,), (1,)), ((), ())), preferred_element_type=F32)


def _dot_tn(a, b):
    return lax.dot_general(a, b, (((0,), (0,)), ((), ())), preferred_element_type=F32)


def _in_proj_kernel(x_ref, g_ref, w_ref, o_ref, h_ref):
    @pl.when(pl.program_id(1) == 0)
    def _():
        x = x_ref[...]
        ms = jnp.mean(x * x, axis=-1, keepdims=True)
        h_ref[...] = (x * lax.rsqrt(ms + RMS_EPS) * g_ref[...]).astype(BF16)

    o_ref[...] = jnp.dot(h_ref[...], w_ref[0].astype(BF16), preferred_element_type=F32).astype(o_ref.dtype)


def _in_proj(x2d, g, w_in):
    n = x2d.shape[0]
    n_col = IN_WIDTH // IN_TN

    def out_block(j):
        gates = j - REF_OFF_GATE_A // IN_TN + OFF_GATE_A // IN_TN
        q_a = j - REF_OFF_QA // IN_TN + OFF_QA // IN_TN
        kv = j - REF_OFF_KA // IN_TN + OFF_KA // IN_TN
        hg = j - REF_OFF_QH // IN_TN + OFF_QH // IN_TN
        return jnp.where(j >= REF_OFF_GATE_A // IN_TN, gates,
                         jnp.where(j >= REF_OFF_QH // IN_TN, hg, jnp.where(j >= REF_OFF_KA // IN_TN, kv, q_a)))

    return pl.pallas_call(
        _in_proj_kernel,
        grid=(n // IN_TM, n_col),
        in_specs=[
            pl.BlockSpec((IN_TM, D_MODEL), lambda i, j: (i, 0)),
            pl.BlockSpec((1, D_MODEL), lambda i, j: (0, 0)),
            pl.BlockSpec((1, D_MODEL, IN_TN), lambda i, j: (0, 0, j)),
        ],
        out_specs=pl.BlockSpec((IN_TM, IN_TN), lambda i, j: (i, out_block(j))),
        out_shape=jax.ShapeDtypeStruct((n, IN_WIDTH), BF16),
        scratch_shapes=[pltpu.VMEM((IN_TM, D_MODEL), BF16)],
        compiler_params=pltpu.CompilerParams(
            dimension_semantics=("parallel", "arbitrary"), vmem_limit_bytes=_vmem(58)),
        name="in_proj",
    )(x2d, g, w_in)


def _swa_block(q, kk, vv, has_prev, bias_ref, sink_ref, side_work):
    q = q * jnp.asarray(HEAD_DIM ** -0.5, BF16)
    key = lax.broadcasted_iota(jnp.int32, (ATTN_BLOCK, ATTN_BLOCK), 0)
    qry = lax.broadcasted_iota(jnp.int32, (ATTN_BLOCK, ATTN_BLOCK), 1)
    cur = key <= qry
    key_ok = jnp.logical_or(cur, has_prev)
    grp = N_Q_HEADS // N_KV_HEADS
    head = lambda t, h: t[:, h * HEAD_DIM:(h + 1) * HEAD_DIM]
    lanes = lambda t, g: t[:, g * ATTN_BLOCK:(g + 1) * ATTN_BLOCK]
    scores = []
    for hk in range(N_KV_HEADS):
        q_grp = jnp.concatenate([head(q, hk * grp + g) for g in range(grp)], axis=0)
        scores.append(_dot_nt(head(kk, hk), q_grp))
    emitted = 0
    probs, denoms = [], []
    for hq in range(N_Q_HEADS):
        while emitted * N_Q_HEADS < hq * len(side_work) + len(side_work):
            side_work[emitted]()
            emitted += 1
        s2 = lanes(scores[hq // grp], hq % grp)
        s = jnp.where(cur, s2[ATTN_BLOCK:], s2[:ATTN_BLOCK]) + bias_ref[hq]
        s = jnp.where(key_ok, s, NEG_INF)
        sink = sink_ref[hq]
        m = jnp.maximum(jnp.max(s, axis=0, keepdims=True), sink)
        p = jnp.exp(s - m)
        denoms.append(jnp.sum(p, axis=0, keepdims=True) + jnp.exp(sink - m))
        probs.append(jnp.concatenate([jnp.where(cur, 0.0, p), jnp.where(cur, p, 0.0)], axis=0).astype(BF16))
    outs = []
    for hk in range(N_KV_HEADS):
        p_grp = jnp.concatenate(probs[hk * grp:(hk + 1) * grp], axis=1)
        o_grp = _dot_tn(head(vv, hk), p_grp)
        outs += [lanes(o_grp, g) / denoms[hk * grp + g] for g in range(grp)]
    return jnp.concatenate(outs, axis=0)


def _t5_causal_bucket(n):
    max_exact = REL_BUCKETS // 2
    nf = jnp.maximum(n, 1).astype(F32)
    large = max_exact + (jnp.log(nf / max_exact) / math.log(REL_MAX_DIST / max_exact)
                         * (REL_BUCKETS - max_exact)).astype(jnp.int32)
    large = jnp.minimum(large, REL_BUCKETS - 1)
    return jnp.where(n < max_exact, n, large)


def _attn_bias_table(rel_bias):
    assert WINDOW == ATTN_BLOCK
    r = jnp.arange(ATTN_BLOCK)[None, :]
    c = jnp.arange(ATTN_BLOCK)[:, None]
    dist = jnp.where(c <= r, r - c, r + ATTN_BLOCK - c)
    bucket = _t5_causal_bucket(dist)
    onehot = (bucket[None, :, :] == jnp.arange(REL_BUCKETS)[:, None, None]).astype(F32)
    return jnp.einsum("bh,bqk->hqk", rel_bias.astype(F32), onehot, precision=lax.Precision.HIGHEST)


def _cumsum_rows(tri_bf16, g):
    width = g.shape[1]
    g1 = g.astype(BF16)
    r1 = g - g1.astype(F32)
    g2 = r1.astype(BF16)
    g3 = (r1 - g2.astype(F32)).astype(BF16)
    acc = jnp.dot(tri_bf16, jnp.concatenate([g3, g2, g1], axis=1), preferred_element_type=F32)
    return (acc[:, :width] + acc[:, width:2 * width]) + acc[:, 2 * width:]


def _hgrn_head_safe(q, kf, v, b, st):
    rows = lax.broadcasted_iota(jnp.int32, (HG_SUB, 1), 0)
    outs = []
    prev_end = jnp.zeros((1, HG_DIM), F32)
    for i in range(HG_CHUNK // HG_SUB):
        sl = slice(i * HG_SUB, (i + 1) * HG_SUB)
        bl = b[sl] - prev_end
        qi, ki, vi = q[sl], kf[sl], v[sl]
        vif = vi.astype(F32)
        blast = bl[HG_SUB - 1:HG_SUB]
        o = _dot_nt((qi * jnp.exp(bl)).astype(BF16), st.astype(BF16))
        for s in range(HG_SUB):
            d = jnp.exp(jnp.minimum(bl - bl[s:s + 1], 0.0))
            a = jnp.sum(qi * ki[s:s + 1] * d, axis=-1, keepdims=True)
            o = o + jnp.where(rows >= s, a, 0.0) * vif[s:s + 1]
        kd = (ki * jnp.exp(blast - bl)).astype(BF16)
        st = st * jnp.exp(blast) + _dot_tn(vi, kd)
        prev_end = b[(i + 1) * HG_SUB - 1:(i + 1) * HG_SUB]
        outs.append(o)
    return jnp.concatenate(outs, axis=0), st


def _hgrn_kernel(q_ref, f_ref, i_ref, g_ref, lbl_ref, gn_ref, o_ref, st_ref, kf_ref, b_ref):
    @pl.when(pl.program_id(1) == 0)
    def _():
        st_ref[...] = jnp.zeros_like(st_ref)

    lg2 = lbl_ref[...].astype(F32)
    e = jnp.exp(lg2 - jnp.max(lg2, axis=0, keepdims=True))
    lb = e[0:1] / jnp.sum(e, axis=0, keepdims=True)
    gn = gn_ref[...].astype(F32)
    r = lax.broadcasted_iota(jnp.int32, (HG_CHUNK, HG_CHUNK), 0)
    c = lax.broadcasted_iota(jnp.int32, (HG_CHUNK, HG_CHUNK), 1)
    tri_mask = r >= c
    tri = jnp.where(tri_mask, 1.0, 0.0).astype(BF16)

    n_chunks = HG_TC // HG_CHUNK

    f = lb + (1.0 - lb) * _sigmoid(f_ref[0].astype(F32))
    kf_ref[...] = 1.0 - f
    logf = jnp.log(f)
    worst = None
    for ci in range(n_chunks):
        sl = slice(ci * HG_CHUNK, (ci + 1) * HG_CHUNK)
        b = _cumsum_rows(tri, logf[sl])
        b_ref[sl, :] = b
        bend = b[HG_CHUNK - 1:HG_CHUNK]
        worst = bend if worst is None else jnp.minimum(worst, bend)

    heads = range(HG_HEADS)
    head = lambda t, h: t[:, h * HG_DIM:(h + 1) * HG_DIM]

    def finish_all(sl, outs):
        scale = [lax.rsqrt(jnp.mean(o * o, axis=-1, keepdims=True) + RMS_EPS) for o in outs]
        normed = jnp.concatenate([outs[h] * scale[h] * gn for h in heads], axis=1)
        go = g_ref[0, sl, :].astype(F32)
        o_ref[0, sl, :] = (normed * (go * _sigmoid(go))).astype(o_ref.dtype)

    def fast():
        states = [st_ref[h] for h in heads]
        for ci in range(n_chunks):
            sl = slice(ci * HG_CHUNK, (ci + 1) * HG_CHUNK)
            b = b_ref[sl, :]
            qp = (q_ref[0, sl, :].astype(F32) * jnp.exp(b)).astype(BF16)
            kp = kf_ref[sl, :] * jnp.exp(-b)
            kpb = kp.astype(BF16)
            eb = jnp.exp(b[HG_CHUNK - 1:HG_CHUNK])
            kd = (kp * eb).astype(BF16)
            v = i_ref[0, sl, :]
            att = [_dot_nt(head(qp, h), head(kpb, h)) for h in heads]
            inter = [_dot_nt(head(qp, h), states[h].astype(BF16)) for h in heads]
            update = [_dot_tn(head(v, h), head(kd, h)) for h in heads]
            att = [jnp.where(tri_mask, a, 0.0).astype(BF16) for a in att]
            outs = [inter[h] + jnp.dot(att[h], head(v, h), preferred_element_type=F32) for h in heads]
            states = [states[h] * head(eb, h) + update[h] for h in heads]
            finish_all(sl, outs)
        for h in heads:
            st_ref[h] = states[h]

    def safe():
        def chunk(ci, carry):
            sl = pl.ds(pl.multiple_of(ci * HG_CHUNK, HG_CHUNK), HG_CHUNK)
            outs = []
            for h in heads:
                hs = slice(h * HG_DIM, (h + 1) * HG_DIM)
                o, st_new = _hgrn_head_safe(q_ref[0, sl, hs].astype(F32), kf_ref[sl, hs], i_ref[0, sl, hs],
                                            b_ref[sl, hs], st_ref[h])
                st_ref[h] = st_new
                outs.append(o)
            finish_all(sl, outs)
            return carry

        lax.fori_loop(0, n_chunks, chunk, 0)

    lax.cond(jnp.min(worst) >= -HG_SAFE_DECAY, fast, safe)


def _hgrn2(proj3, lb_logits, gn):
    bsz, s_len, _ = proj3.shape
    blk = lambda off: (lambda b, t: (b, t, off // HG_WIDTH))
    return pl.pallas_call(
        _hgrn_kernel,
        grid=(bsz, s_len // HG_TC),
        in_specs=[
            pl.BlockSpec((1, HG_TC, HG_WIDTH), blk(OFF_QH)),
            pl.BlockSpec((1, HG_TC, HG_WIDTH), blk(OFF_FH)),
            pl.BlockSpec((1, HG_TC, HG_WIDTH), blk(OFF_IH)),
            pl.BlockSpec((1, HG_TC, HG_WIDTH), blk(OFF_GH)),
            pl.BlockSpec((lb_logits.shape[0], HG_WIDTH), lambda b, t: (0, 0)),
            pl.BlockSpec((1, HG_DIM), lambda b, t: (0, 0)),
        ],
        out_specs=pl.BlockSpec((1, HG_TC, HG_WIDTH), lambda b, t: (b, t, 0)),
        out_shape=jax.ShapeDtypeStruct((bsz, s_len, HG_WIDTH), BF16),
        scratch_shapes=[
            pltpu.VMEM((HG_HEADS, HG_DIM, HG_DIM), F32),
            pltpu.VMEM((HG_TC, HG_WIDTH), F32),
            pltpu.VMEM((HG_TC, HG_WIDTH), F32),
        ],
        compiler_params=pltpu.CompilerParams(
            dimension_semantics=("parallel", "arbitrary"), vmem_limit_bytes=_vmem(48)),
        name="hgrn2",
    )(proj3, proj3, proj3, proj3, lb_logits, gn)


def _store_rows_as_tiles(ref, val):
    rows = val.shape[0]
    for s in range(ROW_TILE):
        ref[pl.ds(s, rows, stride=ROW_TILE), :] = val[:, s * LANES:(s + 1) * LANES]


def _load_rows_from_tiles(ref, start, rows):
    return jnp.concatenate(
        [ref[pl.ds(start * ROW_TILE + s, rows, stride=ROW_TILE), :] for s in range(ROW_TILE)], axis=1)


def _pack_bf16_pairs(x):
    c = x.shape[1] // 2
    bits = pltpu.bitcast(x.astype(BF16).astype(F32), U32)
    return jnp.bitwise_or(jnp.right_shift(bits[:, :c], jnp.uint32(16)), bits[:, c:])


def _unpack_bf16_pairs(p):
    lo = pltpu.bitcast(jnp.left_shift(p, jnp.uint32(16)), F32)
    hi = pltpu.bitcast(jnp.bitwise_and(p, jnp.uint32(0xFFFF0000)), F32)
    return lo, hi


def _mix_kernel(sink_ref, q_ref, kp_ref, kc_ref, vp_ref, vc_ref, bias_ref,
                r_ref, ga_ref, gh_ref, x_ref, wa_ref, wr_ref, wo_ref, g2_ref, wr2t_ref, brt_ref,
                x1_ref, h2p_ref, route_ref, cnt_ref, *, tiles_per_seq):
    i = pl.program_id(0)

    @pl.when(i == 0)
    def _():
        cnt_ref[...] = jnp.zeros_like(cnt_ref)

    r_tile = r_ref[...]
    n_blk = MIX_TM // ATTN_BLOCK
    n_piece = D_MODEL // MXU_WIDTH
    rm_pieces = []

    def rm_piece(c):
        def emit():
            rm_pieces.append(jnp.dot(r_tile, wr_ref[:, c * MXU_WIDTH:(c + 1) * MXU_WIDTH],
                                     preferred_element_type=F32))
        return emit

    k_rows = jnp.concatenate([kp_ref[...], kc_ref[...]], axis=0)
    v_rows = jnp.concatenate([vp_ref[...], vc_ref[...]], axis=0)
    first_in_seq = lax.rem(i, tiles_per_seq) == 0
    blocks = []
    for blk in range(n_blk):
        rows = slice(blk * ATTN_BLOCK, (blk + 1) * ATTN_BLOCK)
        window = slice(blk * ATTN_BLOCK, (blk + 2) * ATTN_BLOCK)
        has_prev = jnp.logical_not(first_in_seq) if blk == 0 else True
        side = [rm_piece(c) for c in range(blk * n_piece // n_blk, (blk + 1) * n_piece // n_blk)]
        blocks.append(_swa_block(q_ref[rows, :], k_rows[window], v_rows[window], has_prev, bias_ref, sink_ref, side))
    rm = jnp.concatenate(rm_pieces, axis=1)
    a_t = jnp.concatenate(blocks, axis=1).astype(BF16)
    am = _dot_tn(a_t, wa_ref[...])
    mixed = _sigmoid(ga_ref[...].astype(F32)) * am + _sigmoid(gh_ref[...].astype(F32)) * rm
    x1 = x_ref[...] + jnp.dot(mixed.astype(BF16), wo_ref[...], preferred_element_type=F32)
    x1_ref[...] = x1
    h2 = x1 * lax.rsqrt(jnp.mean(x1 * x1, axis=-1, keepdims=True) + RMS_EPS) * g2_ref[...]

    _store_rows_as_tiles(h2p_ref, _pack_bf16_pairs(h2))

    hi = h2.astype(BF16)
    lo = (h2 - hi.astype(F32)).astype(BF16)
    both = _dot_nt(wr2t_ref[...], hi)
    logits = both[:ROUTE_LANES] + both[ROUTE_LANES:] + _dot_nt(wr2t_ref[:ROUTE_LANES, :], lo)
    logits = logits[:ROUTE_ROWS] + brt_ref[:ROUTE_ROWS, :]

    tm = logits.shape[1]
    row = lax.broadcasted_iota(jnp.int32, logits.shape, 0)
    row_f = row.astype(F32)
    big = float(ROUTE_ROWS)
    gl = jnp.where(row < N_GROUPS, logits, NEG_INF)
    gmax = jnp.max(gl, axis=0, keepdims=True)
    g_idx = jnp.min(jnp.where(gl == gmax, row_f, big), axis=0, keepdims=True)
    g_w = 1.0 / jnp.sum(jnp.exp(gl - gmax), axis=0, keepdims=True)
    e_row = row - N_GROUPS
    in_group = jnp.logical_and(e_row >= 0, e_row < N_EXPERTS)
    row_group = jnp.right_shift(e_row, 3).astype(F32)
    in_group = jnp.logical_and(in_group, row_group == g_idx)
    el = jnp.where(in_group, logits, NEG_INF)
    m1 = jnp.max(el, axis=0, keepdims=True)
    i1 = jnp.min(jnp.where(el == m1, row_f, big), axis=0, keepdims=True)
    el2 = jnp.where(row_f == i1, NEG_INF, el)
    m2 = jnp.max(el2, axis=0, keepdims=True)
    i2 = jnp.min(jnp.where(el2 == m2, row_f, big), axis=0, keepdims=True)
    p2 = jnp.exp(m2 - m1)
    w1 = g_w / (1.0 + p2)
    w2 = g_w * p2 / (1.0 + p2)
    e1 = i1 - N_GROUPS
    e2 = i2 - N_GROUPS

    ss = lax.broadcasted_iota(jnp.int32, (tm, tm), 0)
    tt = lax.broadcasted_iota(jnp.int32, (tm, tm), 1)
    earlier = jnp.where(ss < tt, 1.0, 0.0).astype(BF16)
    sel1 = row_f == i1
    sel2 = row_f == i2
    oh1 = jnp.where(sel1, 1.0, 0.0)
    oh2 = jnp.where(sel2, 1.0, 0.0)
    base = cnt_ref[...]
    c1 = jnp.sum(oh1, axis=1, keepdims=True)
    c2 = jnp.sum(oh2, axis=1, keepdims=True)
    pre1 = jnp.dot(oh1.astype(BF16), earlier, preferred_element_type=F32) + base
    pre2 = jnp.dot(oh2.astype(BF16), earlier, preferred_element_type=F32) + (base + c1)
    rank1 = jnp.sum(jnp.where(sel1, pre1, 0.0), axis=0, keepdims=True)
    rank2 = jnp.sum(jnp.where(sel2, pre2, 0.0), axis=0, keepdims=True)
    cnt_ref[...] = base + c1 + c2

    vals = jnp.concatenate([w1, w2, e1, e2, rank1, rank2, jnp.zeros((ROUTE_VALS - 6, tm), F32)], axis=0)
    v1 = vals.astype(BF16)
    rest = vals - v1.astype(F32)
    v2 = rest.astype(BF16)
    v3 = (rest - v2.astype(F32)).astype(BF16)
    kk = lax.broadcasted_iota(jnp.int32, (ROUTE_VALS, ROUTE_LANES), 0)
    ll = lax.broadcasted_iota(jnp.int32, (ROUTE_VALS, ROUTE_LANES), 1)
    pick = jnp.where(kk == ll, 1.0, 0.0).astype(BF16)
    route_ref[...] = (_dot_tn(v3, pick) + _dot_tn(v2, pick)) + _dot_tn(v1, pick)


def _mix(sinks, bias, r2d, proj, x2d, wa, wr, wo, g2, wr2t, brt, seq_len):
    n = x2d.shape[0]
    row = lambda i: (i, 0)
    const = lambda i: (0, 0)
    blocks_per_tile = MIX_TM // ATTN_BLOCK
    prev_block = lambda i: jnp.maximum(blocks_per_tile * i - 1, 0)
    return pl.pallas_call(
        functools.partial(_mix_kernel, tiles_per_seq=seq_len // MIX_TM),
        grid=(n // MIX_TM,),
        in_specs=[
            pl.BlockSpec(memory_space=pltpu.SMEM),
            pl.BlockSpec((MIX_TM, ATTN_WIDTH), lambda i: (i, OFF_QA // ATTN_WIDTH)),
            pl.BlockSpec((ATTN_BLOCK, KV_WIDTH), lambda i: (prev_block(i), OFF_KA // KV_WIDTH)),
            pl.BlockSpec((MIX_TM, KV_WIDTH), lambda i: (i, OFF_KA // KV_WIDTH)),
            pl.BlockSpec((ATTN_BLOCK, KV_WIDTH), lambda i: (prev_block(i), OFF_VA // KV_WIDTH)),
            pl.BlockSpec((MIX_TM, KV_WIDTH), lambda i: (i, OFF_VA // KV_WIDTH)),
            pl.BlockSpec((N_Q_HEADS, ATTN_BLOCK, ATTN_BLOCK), lambda i: (0, 0, 0)),
            pl.BlockSpec((MIX_TM, HG_WIDTH), row),
            pl.BlockSpec((MIX_TM, D_MODEL), lambda i: (i, OFF_GATE_A // D_MODEL)),
            pl.BlockSpec((MIX_TM, D_MODEL), lambda i: (i, OFF_GATE_H // D_MODEL)),
            pl.BlockSpec((MIX_TM, D_MODEL), row),
            pl.BlockSpec((ATTN_WIDTH, D_MODEL), const, pipeline_mode=pl.Buffered(1)),
            pl.BlockSpec((HG_WIDTH, D_MODEL), const, pipeline_mode=pl.Buffered(1)),
            pl.BlockSpec((D_MODEL, D_MODEL), const, pipeline_mode=pl.Buffered(1)),
            pl.BlockSpec((1, D_MODEL), const),
            pl.BlockSpec((2 * ROUTE_LANES, D_MODEL), const, pipeline_mode=pl.Buffered(1)),
            pl.BlockSpec((ROUTE_LANES, 1), const),
        ],
        out_specs=[
            pl.BlockSpec((MIX_TM, D_MODEL), row),
            pl.BlockSpec((MIX_TM * ROW_TILE, LANES), row),
            pl.BlockSpec((MIX_TM, ROUTE_LANES), row),
        ],
        out_shape=[
            jax.ShapeDtypeStruct((n, D_MODEL), F32),
            jax.ShapeDtypeStruct((n * ROW_TILE, LANES), U32),
            jax.ShapeDtypeStruct((n, ROUTE_LANES), F32),
        ],
        scratch_shapes=[pltpu.VMEM((ROUTE_ROWS, 1), F32)],
        compiler_params=pltpu.CompilerParams(dimension_semantics=("arbitrary",), vmem_limit_bytes=_vmem(60)),
        name="mix",
    )(sinks, proj, proj, proj, proj, proj, bias, r2d, proj, proj, x2d, wa, wr, wo, g2, wr2t, brt)


def _dispatch_kernel(pad_start_ref, pad_count_ref, tail_ref, dest_ref, src_ref, dst_hbm, zero_ref, sem):
    @pl.when(pl.program_id(0) == 0)
    def _():
        zero_ref[...] = jnp.zeros_like(zero_ref)
        for phase in ("start", "wait"):
            for e in range(N_EXPERTS):
                count, start = pad_count_ref[e], pad_start_ref[e]
                size = MOE_TM // 2
                while size >= 1:
                    done = jnp.bitwise_and(count, -2 * size)
                    copy = pltpu.make_async_copy(zero_ref.at[pl.ds(0, size)],
                                                 dst_hbm.at[pl.ds(start + done, size)], sem.at[1])
                    pl.when(jnp.bitwise_and(count, size) != 0)(getattr(copy, phase))
                    size //= 2
            for j in range(N_EXPERTS):
                copy = pltpu.make_async_copy(zero_ref, dst_hbm.at[pl.ds(tail_ref[0] + j * MOE_TM, MOE_TM)],
                                             sem.at[1])
                pl.when(j < tail_ref[1])(getattr(copy, phase))

    for a in range(DISPATCH_ROWS):
        pltpu.make_async_copy(src_ref.at[a // TOP_K], dst_hbm.at[dest_ref[0, 0, a]], sem.at[0]).start(priority=a % 2)
    for _ in range(TOP_K):
        pltpu.make_async_copy(src_ref, dst_hbm.at[pl.ds(0, DISPATCH_ROWS // TOP_K)], sem.at[0]).wait()


def _dispatch(dest_flat, pad_start, pad_count, tail, h2p_tiles, n_rows):
    n_asg = dest_flat.shape[0]
    steps = n_asg // DISPATCH_ROWS
    smem = pl.BlockSpec(memory_space=pltpu.SMEM)
    return pl.pallas_call(
        _dispatch_kernel,
        grid=(steps,),
        in_specs=[
            smem, smem, smem,
            pl.BlockSpec((1, 1, DISPATCH_ROWS), lambda s: (s, 0, 0), memory_space=pltpu.SMEM),
            pl.BlockSpec((DISPATCH_ROWS // TOP_K, ROW_TILE, LANES), lambda s: (s, 0, 0)),
        ],
        out_specs=pl.BlockSpec(memory_space=pl.ANY),
        out_shape=jax.ShapeDtypeStruct((n_rows, ROW_TILE, LANES), U32),
        scratch_shapes=[pltpu.VMEM((MOE_TM, ROW_TILE, LANES), U32), pltpu.SemaphoreType.DMA((2,))],
        compiler_params=pltpu.CompilerParams(dimension_semantics=("arbitrary",), vmem_limit_bytes=_vmem(32)),
        name="dispatch",
    )(pad_start, pad_count, tail, dest_flat.reshape(steps, 1, DISPATCH_ROWS), h2p_tiles)


def _moe_kernel(be_ref, nused_ref, first_ref, slot_ref, next_ref, next2_ref, xs_ref, wg_hbm, wu_hbm, wd_hbm, y_ref,
                wg_buf, wu_buf, wd_buf, wg_s, wu_s, wd_s, sem, xbuf, xsem):
    b = pl.program_id(0)
    live = b < nused_ref[0]

    def weight_copies(e, s):
        return (pltpu.make_async_copy(wg_hbm.at[0, e], wg_buf.at[s], sem.at[s, 0]),
                pltpu.make_async_copy(wu_hbm.at[0, e], wu_buf.at[s], sem.at[s, 1]),
                pltpu.make_async_copy(wd_hbm.at[0, e], wd_buf.at[s], sem.at[s, 2]))

    @pl.when(b == 0)
    def _():
        for c in weight_copies(be_ref[0], 0):
            c.start()

        @pl.when(next_ref[0] >= 0)
        def _():
            for c in weight_copies(next_ref[0], 1):
                c.start()

    for s in range(MOE_WEIGHT_BUFFERS):
        @pl.when(jnp.logical_and(jnp.logical_and(live, first_ref[b] == 1), slot_ref[b] == s))
        def _():
            for c in weight_copies(be_ref[b], s):
                c.wait()

            @pl.when(next2_ref[b] >= 0)
            def _():
                for c in weight_copies(next2_ref[b], (s + 2) % MOE_WEIGHT_BUFFERS):
                    c.start()

            wg_s[...] = wg_buf[s].astype(BF16)
            wu_s[...] = wu_buf[s].astype(BF16)
            wd_s[...] = wd_buf[s].astype(BF16)

    @pl.when(jnp.logical_not(live))
    def _():
        y_ref[...] = jnp.zeros_like(y_ref)

    blk_rows = MOE_TM * ROW_TILE

    def row_block_copy(blk, s):
        return pltpu.make_async_copy(xs_ref.at[pl.ds(pl.multiple_of(blk * blk_rows, blk_rows), blk_rows)],
                                     xbuf.at[pl.ds(s * blk_rows, blk_rows)], xsem.at[s])

    @pl.when(b == 0)
    def _():
        row_block_copy(0, 0).start()

        @pl.when(nused_ref[0] > 1)
        def _():
            row_block_copy(1, 1).start()

    for s in range(3):
        @pl.when(jnp.logical_and(b + 2 < nused_ref[0], lax.rem(b + 2, 3) == s))
        def _():
            row_block_copy(b + 2, s).start()

    for s in range(3):
        @pl.when(jnp.logical_and(live, lax.rem(b, 3) == s))
        def _():
            row_block_copy(b, s).wait()

    @pl.when(live)
    def _():
        xlo, xhi = _unpack_bf16_pairs(_load_rows_from_tiles(xbuf, lax.rem(b, 3) * MOE_TM, MOE_TM))
        xb = jnp.concatenate([xlo.astype(BF16), xhi.astype(BF16)], axis=1)
        hb = []
        for c in range(EXPERT_FF // MXU_WIDTH):
            cols = slice(c * MXU_WIDTH, (c + 1) * MXU_WIDTH)
            hg = jnp.dot(xb, wg_s[:, cols], preferred_element_type=F32)
            hu = jnp.dot(xb, wu_s[:, cols], preferred_element_type=F32)
            hb.append((hg * _sigmoid(hg) * hu).astype(BF16))
        hb = jnp.concatenate(hb, axis=1)
        half = D_MODEL // 2
        for c in range(half // MXU_WIDTH):
            lo = jnp.dot(hb, wd_s[:, c * MXU_WIDTH:(c + 1) * MXU_WIDTH], preferred_element_type=F32)
            hi = jnp.dot(hb, wd_s[:, half + c * MXU_WIDTH:half + (c + 1) * MXU_WIDTH], preferred_element_type=F32)
            packed = _pack_bf16_pairs(jnp.concatenate([lo, hi], axis=1))
            for k in range(MXU_WIDTH // LANES):
                s = c * (MXU_WIDTH // LANES) + k
                y_ref[pl.ds(s, MOE_TM, stride=ROW_TILE), :] = packed[:, k * LANES:(k + 1) * LANES]


def _moe(block_e, n_used, first, slot, next_e, next2_e, xs, w_gate, w_up, w_down):
    n_rows = xs.shape[0] // ROW_TILE
    n_blocks = n_rows // MOE_TM
    grid_spec = pltpu.PrefetchScalarGridSpec(
        num_scalar_prefetch=6,
        grid=(n_blocks,),
        in_specs=[
            pl.BlockSpec(memory_space=pl.ANY),
            pl.BlockSpec(memory_space=pl.ANY),
            pl.BlockSpec(memory_space=pl.ANY),
            pl.BlockSpec(memory_space=pl.ANY),
        ],
        out_specs=pl.BlockSpec((MOE_TM * ROW_TILE, LANES), lambda b, *_: (b, 0)),
        scratch_shapes=[
            pltpu.VMEM((MOE_WEIGHT_BUFFERS, D_MODEL, EXPERT_FF), F32),
            pltpu.VMEM((MOE_WEIGHT_BUFFERS, D_MODEL, EXPERT_FF), F32),
            pltpu.VMEM((MOE_WEIGHT_BUFFERS, EXPERT_FF, D_MODEL), F32),
            pltpu.VMEM((D_MODEL, EXPERT_FF), BF16),
            pltpu.VMEM((D_MODEL, EXPERT_FF), BF16),
            pltpu.VMEM((EXPERT_FF, D_MODEL), BF16),
            pltpu.SemaphoreType.DMA((MOE_WEIGHT_BUFFERS, 3)),
            pltpu.VMEM((3 * MOE_TM * ROW_TILE, LANES), U32),
            pltpu.SemaphoreType.DMA((3,)),
        ],
    )
    return pl.pallas_call(
        _moe_kernel,
        grid_spec=grid_spec,
        out_shape=jax.ShapeDtypeStruct((n_rows * ROW_TILE, LANES), U32),
        compiler_params=pltpu.CompilerParams(dimension_semantics=("arbitrary",), vmem_limit_bytes=_vmem(52)),
        name="moe",
    )(block_e, n_used, first, slot, next_e, next2_e, xs, w_gate, w_up, w_down)


def _moe_layout(route):
    e_id = route[:, 2:2 + TOP_K].astype(jnp.int32)
    rank = route[:, 2 + TOP_K:2 + 2 * TOP_K].astype(jnp.int32)
    onehot_e = e_id[:, :, None] == jnp.arange(N_EXPERTS, dtype=jnp.int32)[None, None, :]
    counts = jnp.sum(onehot_e.astype(jnp.int32), axis=(0, 1))
    padded = (counts + MOE_TM - 1) // MOE_TM * MOE_TM
    pends = jnp.cumsum(padded)
    pstarts = pends - padded
    n_tok = route.shape[0]
    n_blocks = n_tok * TOP_K // MOE_TM + N_EXPERTS
    blk_start = jnp.arange(n_blocks, dtype=jnp.int32) * MOE_TM
    block_e = jnp.minimum(jnp.sum((blk_start[:, None] >= pends[None, :]).astype(jnp.int32), axis=1), N_EXPERTS - 1)
    n_used = (pends[-1] // MOE_TM).astype(jnp.int32).reshape(1)
    blk = jnp.arange(n_blocks, dtype=jnp.int32)
    first = jnp.logical_and(blk < n_used[0], jnp.logical_or(blk == 0, block_e != jnp.roll(block_e, 1)))
    slot = lax.rem(jnp.cumsum(first.astype(jnp.int32)) - 1, MOE_WEIGHT_BUFFERS)
    ex = jnp.arange(N_EXPERTS, dtype=jnp.int32)
    later = jnp.logical_and(counts[None, :] > 0, ex[None, :] > ex[:, None])
    next_of_expert = jnp.min(jnp.where(later, ex[None, :], N_EXPERTS), axis=1)
    next_of_expert = jnp.where(next_of_expert == N_EXPERTS, -1, next_of_expert)
    lookup = lambda keys, table: jnp.sum(jnp.where(keys[:, None] == ex[None, :], table[None, :] + 1, 0), axis=1) - 1
    next2_of_expert = lookup(next_of_expert, next_of_expert)
    next_e = lookup(block_e, next_of_expert)
    next2_e = lookup(block_e, next2_of_expert)
    dest = jnp.sum(jnp.where(onehot_e, pstarts[None, None, :], 0), axis=2) + rank
    pads = ((pstarts + counts).astype(jnp.int32), (padded - counts).astype(jnp.int32),
            jnp.stack([pends[-1], n_blocks - n_used[0]]).astype(jnp.int32))
    tables = (block_e.astype(jnp.int32), n_used, first.astype(jnp.int32), slot.astype(jnp.int32),
              next_e.astype(jnp.int32), next2_e.astype(jnp.int32))
    return tables, dest, pads, n_blocks * MOE_TM


def _final_kernel(dest_ref, destn_ref, x1_ref, route_ref, g_ref, y_hbm, y_flat_hbm, o_ref, ybuf, sem):
    i = pl.program_id(0)
    n_steps = pl.num_programs(0)
    slot = lax.rem(i, 2)
    n_rows = TOP_K * FIN_TM

    def start_gather(idx_ref, s):
        for r in range(n_rows):
            pltpu.make_async_copy(y_hbm.at[idx_ref[0, 0, r]], ybuf.at[pl.ds((s * n_rows + r) * ROW_TILE, ROW_TILE)],
                                  sem.at[s]).start(priority=r % 2)

    @pl.when(i == 0)
    def _():
        start_gather(dest_ref, 0)

    for s in range(2):
        @pl.when(jnp.logical_and(i + 1 < n_steps, 1 - slot == s))
        def _():
            start_gather(destn_ref, s)

    slot_rows = n_rows * ROW_TILE
    slot_start = pl.multiple_of(slot * slot_rows, slot_rows)
    pltpu.make_async_copy(y_flat_hbm.at[pl.ds(0, slot_rows)], ybuf.at[pl.ds(slot_start, slot_rows)],
                          sem.at[slot]).wait()
    route = route_ref[...]
    x = x1_ref[...]
    for k in range(TOP_K):
        ylo, yhi = _unpack_bf16_pairs(_load_rows_from_tiles(ybuf, slot * n_rows + k * FIN_TM, FIN_TM))
        x = x + route[:, k:k + 1] * jnp.concatenate([ylo, yhi], axis=1)
    o_ref[...] = x * lax.rsqrt(jnp.mean(x * x, axis=-1, keepdims=True) + RMS_EPS) * g_ref[...]


def _final(dest, x1, route, g, y_sorted):
    n = x1.shape[0]
    steps = n // FIN_TM
    dest3 = dest.reshape(steps, FIN_TM, TOP_K).transpose(0, 2, 1).reshape(steps, 1, TOP_K * FIN_TM)
    return pl.pallas_call(
        _final_kernel,
        grid=(steps,),
        in_specs=[
            pl.BlockSpec((1, 1, TOP_K * FIN_TM), lambda i: (i, 0, 0), memory_space=pltpu.SMEM),
            pl.BlockSpec((1, 1, TOP_K * FIN_TM), lambda i: (jnp.minimum(i + 1, steps - 1), 0, 0),
                         memory_space=pltpu.SMEM),
            pl.BlockSpec((FIN_TM, D_MODEL), lambda i: (i, 0)),
            pl.BlockSpec((FIN_TM, ROUTE_LANES), lambda i: (i, 0)),
            pl.BlockSpec((1, D_MODEL), lambda i: (0, 0)),
            pl.BlockSpec(memory_space=pl.ANY),
            pl.BlockSpec(memory_space=pl.ANY),
        ],
        out_specs=pl.BlockSpec((FIN_TM, D_MODEL), lambda i: (i, 0)),
        out_shape=jax.ShapeDtypeStruct((n, D_MODEL), F32),
        scratch_shapes=[
            pltpu.VMEM((2 * TOP_K * FIN_TM * ROW_TILE, LANES), U32),
            pltpu.SemaphoreType.DMA((2,)),
        ],
        compiler_params=pltpu.CompilerParams(dimension_semantics=("arbitrary",), vmem_limit_bytes=_vmem(40)),
        name="final",
    )(dest3, dest3, x1, route, g, y_sorted.reshape(-1, ROW_TILE, LANES), y_sorted)


def kernel(x, norm1_g, w_in, attn_sinks, rel_bias, hg_lb_logits, hg_norm_g, w_attn_branch, w_hg_branch, w_out,
           norm2_g, w_group_router, b_group_router, w_expert_router, b_expert_router, w_gate, w_up, w_down, final_g):
    bsz, s_len, d = x.shape
    n_tok = bsz * s_len
    x2d = x.reshape(n_tok, d)

    w_route = jnp.concatenate([w_group_router[0], w_expert_router[0]], axis=1).astype(F32)
    w_route = jnp.pad(w_route, ((0, 0), (0, ROUTE_LANES - w_route.shape[1])))
    whi = w_route.astype(BF16)
    wlo = (w_route - whi.astype(F32)).astype(BF16)
    wr2t = jnp.concatenate([whi.T, wlo.T], axis=0)
    b_route = jnp.concatenate([b_group_router[0], b_expert_router[0]]).astype(F32)
    b_route = jnp.pad(b_route, (0, ROUTE_LANES - b_route.shape[0])).reshape(ROUTE_LANES, 1)
    bias = _attn_bias_table(rel_bias)

    proj = _in_proj(x2d, norm1_g[0].reshape(1, d).astype(F32), w_in)
    proj3 = proj.reshape(bsz, s_len, IN_WIDTH)
    r = _hgrn2(proj3, hg_lb_logits.astype(F32), hg_norm_g[0].reshape(1, HG_DIM).astype(F32))
    x1, h2p, route = _mix(
        attn_sinks[0].astype(F32), bias, r.reshape(n_tok, HG_WIDTH), proj, x2d,
        w_attn_branch[0].astype(BF16), w_hg_branch[0].astype(BF16), w_out[0].astype(BF16),
        norm2_g[0].reshape(1, d).astype(F32), wr2t, b_route, s_len)
    tables, dest, pads, n_rows = _moe_layout(route)
    xs = _dispatch(dest.reshape(n_tok * TOP_K), *pads, h2p.reshape(n_tok, ROW_TILE, LANES), n_rows)
    y_sorted = _moe(*tables, xs.reshape(n_rows * ROW_TILE, LANES), w_gate, w_up, w_down)
    out = _final(dest, x1, route, final_g.reshape(1, d).astype(F32), y_sorted)
    return out.reshape(bsz, s_len, d)
```
